```python
import math
import jax, jax.numpy as jnp
from jax import lax
import numpy as np

D_MODEL = 1024
BATCH = 1
SEQ = 16384
DEPTH = 1

N_META = 16
HEAD_DIM = 64
N_HEADS = 8
N_KV_HEADS = 2
Q_PER_KV = N_HEADS // N_KV_HEADS
ATTN_W = N_HEADS * HEAD_DIM
KV_W = N_KV_HEADS * HEAD_DIM
F_GROUPS = 8
F_GROUP_DIM = 64
F_W = F_GROUPS * F_GROUP_DIM
MIX_W = ATTN_W + F_W
IN_W = ATTN_W + 2 * KV_W + F_W
WINDOW = 128
BLOCK = 128
D_FF = 2816
ALIBI_MAX = 8.0
EPS = 1e-6
NEG = -1e30

kernel_name = "hybrid_swa_fnet_macaron_encoder"


def rms_norm(x, g):
    xf = x.astype(jnp.float32)
    y = xf * lax.rsqrt(jnp.mean(xf * xf, axis=-1, keepdims=True) + EPS)
    return (y * g.astype(jnp.float32)).astype(x.dtype)


def swiglu(x, w_gate, w_up, w_down):
    return (jax.nn.silu(x @ w_gate) * (x @ w_up)) @ w_down


def alibi_slopes():
    h = jnp.arange(1, N_HEADS + 1, dtype=jnp.float32)
    return jnp.exp2(-ALIBI_MAX * h / N_HEADS)


def windowed_gqa(q, k, v, sink):
    B = q.shape[0]
    slopes = alibi_slopes().reshape(N_KV_HEADS, Q_PER_KV)
    sink_f = sink.astype(jnp.float32).reshape(N_KV_HEADS, Q_PER_KV)
    scale = HEAD_DIM ** -0.5

    qm, qr = q[:, :N_META], q[:, N_META:]
    km, kr = k[:, :N_META], k[:, N_META:]
    vm, vr = v[:, :N_META], v[:, N_META:]
    S = qr.shape[1]
    nb = S // BLOCK

    qb = qr.reshape(B, nb, BLOCK, N_KV_HEADS, Q_PER_KV, HEAD_DIM)
    pad = ((0, 0), (BLOCK, BLOCK), (0, 0), (0, 0))
    kp = jnp.pad(kr, pad).reshape(B, nb + 2, BLOCK, N_KV_HEADS, HEAD_DIM)
    vp = jnp.pad(vr, pad).reshape(B, nb + 2, BLOCK, N_KV_HEADS, HEAD_DIM)
    kb = jnp.concatenate([kp[:, :-2], kp[:, 1:-1], kp[:, 2:]], axis=2)
    vb = jnp.concatenate([vp[:, :-2], vp[:, 1:-1], vp[:, 2:]], axis=2)

    tpos = N_META + jnp.arange(S).reshape(nb, BLOCK)
    kpos = N_META + (jnp.arange(nb)[:, None] - 1) * BLOCK + jnp.arange(3 * BLOCK)[None]
    dist = jnp.abs(tpos[:, :, None] - kpos[:, None, :])
    valid = (dist <= WINDOW) & (kpos[:, None, :] >= N_META) & (kpos[:, None, :] < N_META + S)
    bias_band = jnp.where(valid[:, None, None],
                          -slopes[None, :, :, None, None] * dist[:, None, None].astype(jnp.float32),
                          NEG)
    mpos = jnp.arange(N_META)
    dist_m = jnp.minimum(jnp.abs(tpos[:, :, None] - mpos[None, None, :]), WINDOW)
    bias_meta = -slopes[None, :, :, None, None] * dist_m[:, None, None].astype(jnp.float32)

    s_band = jnp.einsum('bnqkgd,bnskd->bnkgqs', qb, kb).astype(jnp.float32) * scale + bias_band[None]
    s_meta = jnp.einsum('bnqkgd,bmkd->bnkgqm', qb, km).astype(jnp.float32) * scale + bias_meta[None]
    s_sink = jnp.broadcast_to(sink_f[None, None, :, :, None, None], s_meta.shape[:-1] + (1,))
    p = jax.nn.softmax(jnp.concatenate([s_meta, s_band, s_sink], axis=-1), axis=-1)
    p_meta = p[..., :N_META].astype(v.dtype)
    p_band = p[..., N_META:N_META + 3 * BLOCK].astype(v.dtype)
    out_r = (jnp.einsum('bnkgqm,bmkd->bnqkgd', p_meta, vm)
             + jnp.einsum('bnkgqs,bnskd->bnqkgd', p_band, vb))
    out_r = out_r.reshape(B, S, ATTN_W)

    qmm = qm.reshape(B, N_META, N_KV_HEADS, Q_PER_KV, HEAD_DIM)
    kr0, vr0 = kr[:, :WINDOW], vr[:, :WINDOW]
    rpos = N_META + jnp.arange(WINDOW)
    d_mm = jnp.abs(mpos[:, None] - mpos[None, :]).astype(jnp.float32)
    d_mr = jnp.abs(mpos[:, None] - rpos[None, :])
    b_mm = -slopes[:, :, None, None] * d_mm[None, None]
    b_mr = jnp.where((d_mr <= WINDOW)[None, None],
                     -slopes[:, :, None, None] * d_mr[None, None].astype(jnp.float32), NEG)
    s_mm = jnp.einsum('bqkgd,bmkd->bkgqm', qmm, km).astype(jnp.float32) * scale + b_mm[None]
    s_mr = jnp.einsum('bqkgd,bskd->bkgqs', qmm, kr0).astype(jnp.float32) * scale + b_mr[None]
    s_ms = jnp.broadcast_to(sink_f[None, :, :, None, None], s_mm.shape[:-1] + (1,))
    pm = jax.nn.softmax(jnp.concatenate([s_mm, s_mr, s_ms], axis=-1), axis=-1)
    out_m = (jnp.einsum('bkgqm,bmkd->bqkgd', pm[..., :N_META].astype(v.dtype), vm)
             + jnp.einsum('bkgqs,bskd->bqkgd', pm[..., N_META:N_META + WINDOW].astype(v.dtype), vr0))
    out_m = out_m.reshape(B, N_META, ATTN_W)
    return jnp.concatenate([out_m, out_r], axis=1)


def fourier_mix(f):
    B, L, _ = f.shape
    ff = f.astype(jnp.float32).reshape(B, L, F_GROUPS, F_GROUP_DIM)
    y = jnp.fft.fft2(ff, axes=(1, 3), norm="ortho").real
    return y.reshape(B, L, F_W).astype(f.dtype)


def setup_inputs(seed: int = 0) -> dict:
    key = jax.random.key(seed)
    ks = jax.random.split(key, 24)
    f32 = jnp.float32

    def nrm(k, shape, scale):
        return jax.random.normal(k, shape, f32) * scale

    def gain(k, n):
        return 1.0 + 0.05 * jax.random.normal(k, (n,), f32)

    return {
        "x": jax.random.normal(ks[0], (BATCH, SEQ, D_MODEL), f32),
        "meta_tokens": nrm(ks[1], (N_META, D_MODEL), 1.0),
        "ffn1_norm": gain(ks[2], D_MODEL),
        "ffn1_w_gate": nrm(ks[3], (D_MODEL, D_FF), D_MODEL ** -0.5),
        "ffn1_w_up": nrm(ks[4], (D_MODEL, D_FF), D_MODEL ** -0.5),
        "ffn1_w_down": nrm(ks[5], (D_FF, D_MODEL), D_FF ** -0.5),
        "mix_norm": gain(ks[6], D_MODEL),
        "w_in": nrm(ks[7], (D_MODEL, IN_W), D_MODEL ** -0.5),
        "q_norm": gain(ks[8], HEAD_DIM),
        "k_norm": gain(ks[9], HEAD_DIM),
        "sink": nrm(ks[10], (N_HEADS,), 0.5),
        "attn_out_norm": gain(ks[11], ATTN_W),
        "fourier_out_norm": gain(ks[12], F_W),
        "w_out": nrm(ks[13], (MIX_W, D_MODEL), MIX_W ** -0.5),
        "ffn2_norm": gain(ks[14], D_MODEL),
        "ffn2_w_gate": nrm(ks[15], (D_MODEL, D_FF), D_MODEL ** -0.5),
        "ffn2_w_up": nrm(ks[16], (D_MODEL, D_FF), D_MODEL ** -0.5),
        "ffn2_w_down": nrm(ks[17], (D_FF, D_MODEL), D_FF ** -0.5),
        "final_norm": gain(ks[18], D_MODEL),
    }


def reference(x, meta_tokens, ffn1_norm, ffn1_w_gate, ffn1_w_up, ffn1_w_down,
              mix_norm, w_in, q_norm, k_norm, sink, attn_out_norm, fourier_out_norm,
              w_out, ffn2_norm, ffn2_w_gate, ffn2_w_up, ffn2_w_down, final_norm):
    B = x.shape[0]
    meta = jnp.broadcast_to(meta_tokens[None].astype(x.dtype), (B, N_META, D_MODEL))
    h = jnp.concatenate([meta, x], axis=1)
    L = h.shape[1]

    for _ in range(DEPTH):
        h = h + 0.5 * swiglu(rms_norm(h, ffn1_norm), ffn1_w_gate, ffn1_w_up, ffn1_w_down)

        u = rms_norm(h, mix_norm) @ w_in
        q = u[..., :ATTN_W].reshape(B, L, N_HEADS, HEAD_DIM)
        k = u[..., ATTN_W:ATTN_W + KV_W].reshape(B, L, N_KV_HEADS, HEAD_DIM)
        v = u[..., ATTN_W + KV_W:ATTN_W + 2 * KV_W].reshape(B, L, N_KV_HEADS, HEAD_DIM)
        f = u[..., ATTN_W + 2 * KV_W:]
        q = rms_norm(q, q_norm)
        k = rms_norm(k, k_norm)

        a_out = windowed_gqa(q, k, v, sink)
        f_out = fourier_mix(f)
        mixed = jnp.concatenate([rms_norm(a_out, attn_out_norm),
                                 rms_norm(f_out, fourier_out_norm)], axis=-1)
        h = h + mixed @ w_out

        h = h + 0.5 * swiglu(rms_norm(h, ffn2_norm), ffn2_w_gate, ffn2_w_up, ffn2_w_down)
        h = rms_norm(h, final_norm)

    return h[:, N_META:]
```

```python
import functools

import numpy as np
import jax
import jax.numpy as jnp
from jax import lax
from jax.experimental import pallas as pl
from jax.experimental.pallas import tpu as pltpu

F32 = jnp.float32
BF16 = jnp.bfloat16

D_MODEL = 1024
SEQ = 16384
N_META = 16
L = SEQ + N_META
HEAD_DIM = 64
N_HEADS = 8
N_KV = 2
Q_PER_KV = N_HEADS // N_KV
ATTN_W = N_HEADS * HEAD_DIM
KV_W = N_KV * HEAD_DIM
F_W = 512
F_GROUP = 64
IN_W = ATTN_W + 2 * KV_W + F_W
QK_W = ATTN_W + KV_W
WINDOW = 128
D_FF = 2816
EPS = 1e-6
NEG = -1e30

TM = 400
N_TILES = L // TM
META_ROW0 = SEQ - (N_TILES - 1) * TM

BQ = 128
N_QBLK = SEQ // BQ
BAND = BQ + 2 * WINDOW
N_KEYS = BAND + N_META + 8
SINK_COL = BAND + N_META

N1 = 400
N2 = 41
N2P = 48
TC = 8

VMEM_LIMIT = 56 * 1024 * 1024


def _ones_blockdiag():
    m = np.kron(np.eye(4), np.ones((HEAD_DIM, HEAD_DIM)))
    return jnp.asarray(m, BF16)


def _channel_dft():
    c = np.arange(F_GROUP)
    ang = 2.0 * np.pi * np.outer(c, c) / F_GROUP
    cos = np.kron(np.eye(4), np.cos(ang) / 8.0)
    sin = np.kron(np.eye(4), np.sin(ang) / 8.0)
    return np.concatenate([cos, sin], axis=1).astype(np.float32)


def _stage_a_dft():
    a = np.arange(N1, dtype=np.int64)
    m = (N2 * np.outer(a + N_META, a)) % L
    ang = 2.0 * np.pi * m / L
    return np.concatenate([np.cos(ang), -np.sin(ang)], axis=0).astype(np.float32)


def _stage_c_dft():
    c = np.arange(N1, dtype=np.int64)[:, None, None]
    d = np.arange(N2, dtype=np.int64)[None, :, None]
    b = np.arange(N2, dtype=np.int64)[None, None, :]
    m = ((c + N1 * d + N_META) * (b + N_META)) % L
    ang = 2.0 * np.pi * m / L
    ec = np.zeros((N1, N2P, N2P), np.float32)
    es = np.zeros((N1, N2P, N2P), np.float32)
    ec[:, :N2, :N2] = np.cos(ang) / np.sqrt(L)
    es[:, :N2, :N2] = np.sin(ang) / np.sqrt(L)
    return ec, es


def _attn_bias_const():
    slopes = 2.0 ** -(np.arange(N_HEADS) + 1.0)
    i = np.arange(BQ)[:, None]
    j = np.arange(BAND)[None, :]
    m = np.arange(N_META)[None, :]
    out = np.full((4, N_KV, Q_PER_KV * BQ, N_KEYS), NEG, np.float64)
    for t in range(4):
        if t < 3:
            dist = np.abs(t * WINDOW + i - j)
            band_ok = dist <= WINDOW
            dist_m = np.minimum(t * WINDOW + N_META + i - m, WINDOW) if t == 0 else np.full((BQ, N_META), WINDOW)
        else:
            dist = N_META + j - i
            band_ok = (dist <= WINDOW) & (i < N_META)
            dist_m = np.abs(i - m)
        for h in range(N_HEADS):
            g, hl = divmod(h, Q_PER_KV)
            rows = slice(hl * BQ, (hl + 1) * BQ)
            band = np.where(band_ok, -slopes[h] * dist, NEG)
            if t == 3:
                band = np.where(i < N_META, band, 0.0)
            out[t, g, rows, :BAND] = band
            out[t, g, rows, BAND:BAND + N_META] = -slopes[h] * dist_m
    return out.astype(np.float32)


def _rms(x, g):
    return x * lax.rsqrt(jnp.mean(x * x, axis=-1, keepdims=True) + EPS) * g


def _swiglu_half(x, g_ref, wg_ref, wu_ref, wd_ref):
    xn = _rms(x, g_ref[...]).astype(BF16)
    gate = jnp.dot(xn, wg_ref[...], preferred_element_type=F32)
    up = jnp.dot(xn, wu_ref[...], preferred_element_type=F32)
    act = (gate * jax.nn.sigmoid(gate) * up).astype(BF16)
    return x + 0.5 * jnp.dot(act, wd_ref[...], preferred_element_type=F32)


def _pre_kernel(x_ref, meta_ref, g1_ref, wg_ref, wu_ref, wd_ref, gm_ref, win_ref, gqk_ref,
                ones_ref, cdft_ref, h1_ref, q_ref, kv_ref, z_ref):
    i = pl.program_id(0)
    row = lax.broadcasted_iota(jnp.int32, (TM, 1), 0)
    is_meta = jnp.logical_and(i == N_TILES - 1, row >= META_ROW0)
    x = jnp.where(is_meta, meta_ref[...], x_ref[...])

    h1 = _swiglu_half(x, g1_ref, wg_ref, wu_ref, wd_ref)
    h1_ref[...] = h1

    u = jnp.dot(_rms(h1, gm_ref[...]).astype(BF16), win_ref[...], preferred_element_type=F32)

    qk = u[:, :QK_W]
    sq = (qk * qk).astype(BF16)
    ones = ones_ref[...]
    ss = jnp.concatenate([
        jnp.dot(sq[:, 0:256], ones, preferred_element_type=F32),
        jnp.dot(sq[:, 256:512], ones, preferred_element_type=F32),
        jnp.dot(sq[:, 512:640], ones[:KV_W, :KV_W], preferred_element_type=F32)], axis=1)
    qkn = qk * lax.rsqrt(ss * (1.0 / HEAD_DIM) + EPS) * gqk_ref[...]
    q_ref[...] = qkn[:, :ATTN_W].astype(BF16)
    kv_ref[...] = jnp.concatenate([qkn[:, ATTN_W:], u[:, QK_W:QK_W + KV_W]], axis=1).astype(BF16)

    f = u[:, QK_W + KV_W:].astype(BF16)
    cdft = cdft_ref[...]
    lo = jnp.dot(f[:, :256], cdft, preferred_element_type=F32)
    hi = jnp.dot(f[:, 256:], cdft, preferred_element_type=F32)
    z_ref[...] = jnp.concatenate([lo[:, :256], hi[:, :256], lo[:, 256:], hi[:, 256:]], axis=1).astype(BF16)


def _attn_kernel(q_ref, kv_ref, bias_ref, o_ref):
    n = pl.program_id(0)
    start = jnp.where(n == N_QBLK, 0, jnp.clip((n - 1) * BQ, 0, SEQ - BAND))
    start = pl.multiple_of(start, BQ)
    q = q_ref[...]
    kv_band = kv_ref[pl.ds(start, BAND), :]
    kv_meta = kv_ref[SEQ:L, :]
    kv_cat = jnp.concatenate([kv_band, kv_meta, jnp.zeros((8, 2 * KV_W), BF16)], axis=0)
    heads = []
    for g in range(N_KV):
        k = kv_cat[:, g * HEAD_DIM:(g + 1) * HEAD_DIM]
        v = kv_cat[:, KV_W + g * HEAD_DIM:KV_W + (g + 1) * HEAD_DIM]
        qg = jnp.concatenate(
            [q[:, (Q_PER_KV * g + hl) * HEAD_DIM:(Q_PER_KV * g + hl + 1) * HEAD_DIM] for hl in range(Q_PER_KV)],
            axis=0)
        s = lax.dot_general(qg, k, (((1,), (1,)), ((), ())), preferred_element_type=F32)
        s = s + bias_ref[0, g]
        m = jnp.max(s, axis=-1, keepdims=True)
        p = jnp.exp(s - m)
        denom = jnp.sum(p, axis=-1, keepdims=True)
        o = jnp.dot(p.astype(BF16), v, preferred_element_type=F32) / denom
        heads.extend(o[hl * BQ:(hl + 1) * BQ] for hl in range(Q_PER_KV))
    o_ref[...] = jnp.concatenate(heads, axis=1)


def _dft_a_kernel(z_ref, wa_ref, g_ref):
    b = pl.program_id(0)

    @pl.when(b < N2)
    def _():
        t = jnp.dot(wa_ref[...], z_ref[...], preferred_element_type=F32)
        gr = t[:N1, :F_W] + t[N1:, F_W:]
        gi = t[N1:, :F_W] - t[:N1, F_W:]
        g_ref[...] = jnp.concatenate([gr, gi], axis=1).astype(BF16)

    @pl.when(b >= N2)
    def _():
        g_ref[...] = jnp.zeros_like(g_ref)


def _dft_c_kernel(g_ref, ec_ref, es_ref, y_ref):
    for c in range(TC):
        g = g_ref[c]
        y = (jnp.dot(ec_ref[c], g[:, :F_W], preferred_element_type=F32)
             + jnp.dot(es_ref[c], g[:, F_W:], preferred_element_type=F32))
        y_ref[:, c * F_W:(c + 1) * F_W] = y[:N2]


def _post_kernel(h1_ref, a_ref, y_ref, ga_ref, gf_ref, wout_ref, g2_ref, wg_ref, wu_ref, wd_ref,
                 gfin_ref, o_ref):
    mixed = jnp.concatenate([_rms(a_ref[...], ga_ref[...]), _rms(y_ref[...], gf_ref[...])], axis=1)
    h2 = h1_ref[...] + jnp.dot(mixed.astype(BF16), wout_ref[...], preferred_element_type=F32)
    h3 = _swiglu_half(h2, g2_ref, wg_ref, wu_ref, wd_ref)
    o_ref[...] = _rms(h3, gfin_ref[...])


def _resident(shape):
    zeros = (0,) * len(shape)
    return pl.BlockSpec(shape, lambda *_: zeros, pipeline_mode=pl.Buffered(1))


def _params():
    return pltpu.CompilerParams(dimension_semantics=("arbitrary",), vmem_limit_bytes=VMEM_LIMIT)


def _row(g):
    return g.astype(F32).reshape(1, -1)


def kernel(x, meta_tokens, ffn1_norm, ffn1_w_gate, ffn1_w_up, ffn1_w_down, mix_norm, w_in, q_norm, k_norm, sink, attn_out_norm, fourier_out_norm, w_out, ffn2_norm, ffn2_w_gate, ffn2_w_up, ffn2_w_down, final_norm):
    assert x.shape == (1, SEQ, D_MODEL) and x.dtype == F32
    x2 = x.reshape(SEQ, D_MODEL)
    meta_pad = jnp.pad(meta_tokens.astype(F32), ((META_ROW0, 0), (0, 0)))
    gqk = jnp.concatenate([jnp.tile(q_norm.astype(F32), N_HEADS) * (HEAD_DIM ** -0.5),
                           jnp.tile(k_norm.astype(F32), N_KV)]).reshape(1, QK_W)
    bf = lambda w: w.astype(BF16)

    row_tile = lambda w: pl.BlockSpec((TM, w), lambda i: (i, 0))
    h1, q, kv, z = pl.pallas_call(
        _pre_kernel,
        grid=(N_TILES,),
        in_specs=[row_tile(D_MODEL), _resident((TM, D_MODEL)), _resident((1, D_MODEL)),
                  _resident((D_MODEL, D_FF)), _resident((D_MODEL, D_FF)), _resident((D_FF, D_MODEL)),
                  _resident((1, D_MODEL)), _resident((D_MODEL, IN_W)), _resident((1, QK_W)),
                  _resident((256, 256)), _resident((256, 512))],
        out_specs=[row_tile(D_MODEL), row_tile(ATTN_W), row_tile(2 * KV_W), row_tile(2 * F_W)],
        out_shape=[jax.ShapeDtypeStruct((L, D_MODEL), F32), jax.ShapeDtypeStruct((L, ATTN_W), BF16),
                   jax.ShapeDtypeStruct((L, 2 * KV_W), BF16), jax.ShapeDtypeStruct((L, 2 * F_W), BF16)],
        compiler_params=_params(),
        name="pre",
    )(x2, meta_pad, _row(ffn1_norm), bf(ffn1_w_gate), bf(ffn1_w_up), bf(ffn1_w_down), _row(mix_norm),
      bf(w_in), gqk, _ones_blockdiag(), bf(jnp.asarray(_channel_dft())))

    sink_rows = jnp.repeat(sink.astype(F32).reshape(N_KV, Q_PER_KV), BQ, axis=1)
    bias_c = jnp.asarray(_attn_bias_const())
    bias = jnp.concatenate(
        [bias_c[..., :SINK_COL],
         jnp.broadcast_to(sink_rows[None, :, :, None], (4, N_KV, Q_PER_KV * BQ, 1)),
         bias_c[..., SINK_COL + 1:]], axis=-1)

    def bias_map(n):
        t = jnp.where(n == 0, 0, jnp.where(n == N_QBLK - 1, 2, jnp.where(n == N_QBLK, 3, 1)))
        return (t, 0, 0, 0)

    a_out = pl.pallas_call(
        _attn_kernel,
        grid=(N_QBLK + 1,),
        in_specs=[pl.BlockSpec((BQ, ATTN_W), lambda n: (n, 0)), _resident((L, 2 * KV_W)),
                  pl.BlockSpec((1, N_KV, Q_PER_KV * BQ, N_KEYS), bias_map)],
        out_specs=pl.BlockSpec((BQ, ATTN_W), lambda n: (n, 0)),
        out_shape=jax.ShapeDtypeStruct((L, ATTN_W), F32),
        compiler_params=_params(),
        name="attn",
    )(q, kv, bias)

    g = pl.pallas_call(
        _dft_a_kernel,
        grid=(N2P,),
        in_specs=[pl.BlockSpec((N1, 2 * F_W), lambda b: (0, jnp.minimum(b, N2 - 1))),
                  _resident((2 * N1, N1))],
        out_specs=pl.BlockSpec((N1, 2 * F_W), lambda b: (0, b)),
        out_shape=jax.ShapeDtypeStruct((N1, N2P * 2 * F_W), BF16),
        compiler_params=_params(),
        name="dft_a",
    )(z.reshape(N1, N2 * 2 * F_W), bf(jnp.asarray(_stage_a_dft())))

    ec, es = _stage_c_dft()
    y = pl.pallas_call(
        _dft_c_kernel,
        grid=(N1 // TC,),
        in_specs=[pl.BlockSpec((TC, N2P, 2 * F_W), lambda i: (i, 0, 0)),
                  pl.BlockSpec((TC, N2P, N2P), lambda i: (i, 0, 0)),
                  pl.BlockSpec((TC, N2P, N2P), lambda i: (i, 0, 0))],
        out_specs=pl.BlockSpec((N2, TC * F_W), lambda i: (0, i)),
        out_shape=jax.ShapeDtypeStruct((N2, N1 * F_W), F32),
        compiler_params=_params(),
        name="dft_c",
    )(g.reshape(N1, N2P, 2 * F_W), bf(jnp.asarray(ec)), bf(jnp.asarray(es)))

    out = pl.pallas_call(
        _post_kernel,
        grid=(N_TILES,),
        in_specs=[row_tile(D_MODEL), row_tile(ATTN_W), row_tile(F_W), _resident((1, ATTN_W)),
                  _resident((1, F_W)), _resident((D_MODEL, D_MODEL)), _resident((1, D_MODEL)),
                  _resident((D_MODEL, D_FF)), _resident((D_MODEL, D_FF)), _resident((D_FF, D_MODEL)),
                  _resident((1, D_MODEL))],
        out_specs=row_tile(D_MODEL),
        out_shape=jax.ShapeDtypeStruct((SEQ, D_MODEL), F32),
        compiler_params=_params(),
        name="post",
    )(h1, a_out, y.reshape(L, F_W), _row(attn_out_norm), _row(fourier_out_norm), bf(w_out), _row(ffn2_norm),
      bf(ffn2_w_gate), bf(ffn2_w_up), bf(ffn2_w_down), _row(final_norm))
    return out.reshape(1, SEQ, D_MODEL)
```

```python
import functools

import numpy as np
import jax
import jax.numpy as jnp
from jax import lax
from jax.experimental import pallas as pl
from jax.experimental.pallas import tpu as pltpu

F32 = jnp.float32
BF16 = jnp.bfloat16

D_MODEL = 1024
SEQ = 16384
N_META = 16
L = SEQ + N_META
HEAD_DIM = 64
N_HEADS = 8
N_KV = 2
Q_PER_KV = N_HEADS // N_KV
ATTN_W = N_HEADS * HEAD_DIM
KV_W = N_KV * HEAD_DIM
F_W = 512
F_GROUP = 64
IN_W = ATTN_W + 2 * KV_W + F_W
QK_W = ATTN_W + KV_W
WINDOW = 128
D_FF = 2816
EPS = 1e-6
NEG = -1e30

TM = 400
N_TILES = L // TM
META_ROW0 = SEQ - (N_TILES - 1) * TM

BQ = 128
N_QBLK = SEQ // BQ
BAND = BQ + 2 * WINDOW
N_KEYS = BAND + N_META + 8
SINK_COL = BAND + N_META

N1 = 400
N2 = 41
N2P = 48
TC = 25

VMEM_LIMIT = 56 * 1024 * 1024


def _ones_blockdiag():
    m = np.kron(np.eye(4), np.ones((HEAD_DIM, HEAD_DIM)))
    return jnp.asarray(m, BF16)


def _channel_dft():
    c = np.arange(F_GROUP)
    ang = 2.0 * np.pi * np.outer(c, c) / F_GROUP
    cos = np.kron(np.eye(4), np.cos(ang) / 8.0)
    sin = np.kron(np.eye(4), np.sin(ang) / 8.0)
    return np.concatenate([cos, sin], axis=1).astype(np.float32)


def _stage_a_dft():
    a = np.arange(N1, dtype=np.int64)
    m = (N2 * np.outer(a + N_META, a)) % L
    ang = 2.0 * np.pi * m / L
    return np.concatenate([np.cos(ang), -np.sin(ang)], axis=0).astype(np.float32)


def _stage_c_dft():
    c = np.arange(N1, dtype=np.int64)[:, None, None]
    d = np.arange(N2, dtype=np.int64)[None, :, None]
    b = np.arange(N2, dtype=np.int64)[None, None, :]
    m = ((c + N1 * d + N_META) * (b + N_META)) % L
    ang = 2.0 * np.pi * m / L
    ec = np.zeros((N1, N2P, N2P), np.float32)
    es = np.zeros((N1, N2P, N2P), np.float32)
    ec[:, :N2, :N2] = np.cos(ang) / np.sqrt(L)
    es[:, :N2, :N2] = np.sin(ang) / np.sqrt(L)
    return ec, es


def _attn_bias_const():
    slopes = 2.0 ** -(np.arange(N_HEADS) + 1.0)
    i = np.arange(BQ)[:, None]
    j = np.arange(BAND)[None, :]
    m = np.arange(N_META)[None, :]
    out = np.full((4, N_KV, Q_PER_KV * BQ, N_KEYS), NEG, np.float64)
    for t in range(4):
        if t < 3:
            dist = np.abs(t * WINDOW + i - j)
            band_ok = dist <= WINDOW
            dist_m = np.minimum(t * WINDOW + N_META + i - m, WINDOW) if t == 0 else np.full((BQ, N_META), WINDOW)
        else:
            dist = N_META + j - i
            band_ok = (dist <= WINDOW) & (i < N_META)
            dist_m = np.abs(i - m)
        for h in range(N_HEADS):
            g, hl = divmod(h, Q_PER_KV)
            rows = slice(hl * BQ, (hl + 1) * BQ)
            band = np.where(band_ok, -slopes[h] * dist, NEG)
            if t == 3:
                band = np.where(i < N_META, band, 0.0)
            out[t, g, rows, :BAND] = band
            out[t, g, rows, BAND:BAND + N_META] = -slopes[h] * dist_m
    return out.astype(np.float32)


def _rms(x, g):
    return x * lax.rsqrt(jnp.mean(x * x, axis=-1, keepdims=True) + EPS) * g


def _swiglu_half(x, g_ref, wg_ref, wu_ref, wd_ref):
    xn = _rms(x, g_ref[...]).astype(BF16)
    gate = jnp.dot(xn, wg_ref[...], preferred_element_type=F32)
    up = jnp.dot(xn, wu_ref[...], preferred_element_type=F32)
    act = (gate * jax.nn.sigmoid(gate) * up).astype(BF16)
    return x + 0.5 * jnp.dot(act, wd_ref[...], preferred_element_type=F32)


def _pre_kernel(x_ref, meta_ref, g1_ref, wg_ref, wu_ref, wd_ref, gm_ref, win_ref, gqk_ref,
                ones_ref, cdft_ref, h1_ref, q_ref, kv_ref, z_ref):
    i = pl.program_id(0)
    row = lax.broadcasted_iota(jnp.int32, (TM, 1), 0)
    is_meta = jnp.logical_and(i == N_TILES - 1, row >= META_ROW0)
    x = jnp.where(is_meta, meta_ref[...], x_ref[...])

    h1 = _swiglu_half(x, g1_ref, wg_ref, wu_ref, wd_ref)
    h1_ref[...] = h1

    u = jnp.dot(_rms(h1, gm_ref[...]).astype(BF16), win_ref[...], preferred_element_type=F32)

    qk = u[:, :QK_W]
    sq = (qk * qk).astype(BF16)
    ones = ones_ref[...]
    ss = jnp.concatenate([
        jnp.dot(sq[:, 0:256], ones, preferred_element_type=F32),
        jnp.dot(sq[:, 256:512], ones, preferred_element_type=F32),
        jnp.dot(sq[:, 512:640], ones[:KV_W, :KV_W], preferred_element_type=F32)], axis=1)
    qkn = qk * lax.rsqrt(ss * (1.0 / HEAD_DIM) + EPS) * gqk_ref[...]
    q_ref[...] = qkn[:, :ATTN_W].astype(BF16)
    kv_ref[...] = jnp.concatenate([qkn[:, ATTN_W:], u[:, QK_W:QK_W + KV_W]], axis=1).astype(BF16)

    f = u[:, QK_W + KV_W:].astype(BF16)
    cdft = cdft_ref[...]
    lo = jnp.dot(f[:, :256], cdft, preferred_element_type=F32)
    hi = jnp.dot(f[:, 256:], cdft, preferred_element_type=F32)
    z_ref[...] = jnp.concatenate([lo[:, :256], hi[:, :256], lo[:, 256:], hi[:, 256:]], axis=1)


def _attn_kernel(q_ref, kv_ref, bias_ref, o_ref):
    n = pl.program_id(0)
    start = jnp.where(n == N_QBLK, 0, jnp.clip((n - 1) * BQ, 0, SEQ - BAND))
    start = pl.multiple_of(start, BQ)
    q = q_ref[...]
    kv_band = kv_ref[pl.ds(start, BAND), :]
    kv_meta = kv_ref[SEQ:L, :]
    kv_cat = jnp.concatenate([kv_band, kv_meta, jnp.zeros((8, 2 * KV_W), BF16)], axis=0)
    heads = []
    for g in range(N_KV):
        k = kv_cat[:, g * HEAD_DIM:(g + 1) * HEAD_DIM]
        v = kv_cat[:, KV_W + g * HEAD_DIM:KV_W + (g + 1) * HEAD_DIM]
        qg = jnp.concatenate(
            [q[:, (Q_PER_KV * g + hl) * HEAD_DIM:(Q_PER_KV * g + hl + 1) * HEAD_DIM] for hl in range(Q_PER_KV)],
            axis=0)
        s = lax.dot_general(qg, k, (((1,), (1,)), ((), ())), preferred_element_type=F32)
        s = s + bias_ref[0, g]
        m = jnp.max(s, axis=-1, keepdims=True)
        p = jnp.exp(s - m)
        denom = jnp.sum(p, axis=-1, keepdims=True)
        o = jnp.dot(p.astype(BF16), v, preferred_element_type=F32) / denom
        heads.extend(o[hl * BQ:(hl + 1) * BQ] for hl in range(Q_PER_KV))
    o_ref[...] = jnp.concatenate(heads, axis=1)


def _dft_a_kernel(z_ref, wa_ref, g_ref, zs_ref, gs_ref):
    zs_ref[...] = z_ref[...]
    t = jnp.dot(wa_ref[...], zs_ref[...].astype(BF16), preferred_element_type=F32)
    gs_ref[:, :F_W] = t[:N1, :F_W] + t[N1:, F_W:]
    gs_ref[:, F_W:] = t[N1:, :F_W] - t[:N1, F_W:]
    g_ref[...] = gs_ref[...]


def _dft_c_kernel(g_ref, ec_ref, es_ref, y_ref, gs_ref):
    gs_ref[:, N2:, :] = jnp.zeros((TC, N2P - N2, 2 * F_W), F32)
    for c in range(TC):
        gs_ref[c, :N2, :] = g_ref[c]
        g = gs_ref[c].astype(BF16)
        y = (jnp.dot(ec_ref[c], g[:, :F_W], preferred_element_type=F32)
             + jnp.dot(es_ref[c], g[:, F_W:], preferred_element_type=F32))
        y_ref[:, c, 0, :] = y[:N2]


def _post_kernel(h1_ref, a_ref, y_ref, ga_ref, gf_ref, wout_ref, g2_ref, wg_ref, wu_ref, wd_ref,
                 gfin_ref, o_ref):
    mixed = jnp.concatenate([_rms(a_ref[...], ga_ref[...]), _rms(y_ref[...], gf_ref[...])], axis=1)
    h2 = h1_ref[...] + jnp.dot(mixed.astype(BF16), wout_ref[...], preferred_element_type=F32)
    h3 = _swiglu_half(h2, g2_ref, wg_ref, wu_ref, wd_ref)
    o_ref[...] = _rms(h3, gfin_ref[...])


def _resident(shape):
    zeros = (0,) * len(shape)
    return pl.BlockSpec(shape, lambda *_: zeros, pipeline_mode=pl.Buffered(1))


def _params():
    return pltpu.CompilerParams(dimension_semantics=("arbitrary",), vmem_limit_bytes=VMEM_LIMIT)


def _row(g):
    return g.astype(F32).reshape(1, -1)


def kernel(x, meta_tokens, ffn1_norm, ffn1_w_gate, ffn1_w_up, ffn1_w_down, mix_norm, w_in, q_norm, k_norm, sink, attn_out_norm, fourier_out_norm, w_out, ffn2_norm, ffn2_w_gate, ffn2_w_up, ffn2_w_down, final_norm):
    assert x.shape == (1, SEQ, D_MODEL) and x.dtype == F32
    x2 = x.reshape(SEQ, D_MODEL)
    meta_pad = jnp.pad(meta_tokens.astype(F32), ((META_ROW0, 0), (0, 0)))
    gqk = jnp.concatenate([jnp.tile(q_norm.astype(F32), N_HEADS) * (HEAD_DIM ** -0.5),
                           jnp.tile(k_norm.astype(F32), N_KV)]).reshape(1, QK_W)
    bf = lambda w: w.astype(BF16)

    row_tile = lambda w: pl.BlockSpec((TM, w), lambda i: (i, 0))
    flat_rows = lambda w: pl.BlockSpec((TM, None, w), lambda i: (i, 0, 0))
    h1, q, kv, z = pl.pallas_call(
        _pre_kernel,
        grid=(N_TILES,),
        in_specs=[row_tile(D_MODEL), _resident((TM, D_MODEL)), _resident((1, D_MODEL)),
                  _resident((D_MODEL, D_FF)), _resident((D_MODEL, D_FF)), _resident((D_FF, D_MODEL)),
                  _resident((1, D_MODEL)), _resident((D_MODEL, IN_W)), _resident((1, QK_W)),
                  _resident((256, 256)), _resident((256, 512))],
        out_specs=[row_tile(D_MODEL), row_tile(ATTN_W), row_tile(2 * KV_W), flat_rows(2 * F_W)],
        out_shape=[jax.ShapeDtypeStruct((L, D_MODEL), F32), jax.ShapeDtypeStruct((L, ATTN_W), BF16),
                   jax.ShapeDtypeStruct((L, 2 * KV_W), BF16), jax.ShapeDtypeStruct((L, 1, 2 * F_W), F32)],
        compiler_params=_params(),
        name="pre",
    )(x2, meta_pad, _row(ffn1_norm), bf(ffn1_w_gate), bf(ffn1_w_up), bf(ffn1_w_down), _row(mix_norm),
      bf(w_in), gqk, _ones_blockdiag(), bf(jnp.asarray(_channel_dft())))

    sink_rows = jnp.repeat(sink.astype(F32).reshape(N_KV, Q_PER_KV), BQ, axis=1)
    bias_c = jnp.asarray(_attn_bias_const())
    bias = jnp.concatenate(
        [bias_c[..., :SINK_COL],
         jnp.broadcast_to(sink_rows[None, :, :, None], (4, N_KV, Q_PER_KV * BQ, 1)),
         bias_c[..., SINK_COL + 1:]], axis=-1)

    def bias_map(n):
        t = jnp.where(n == 0, 0, jnp.where(n == N_QBLK - 1, 2, jnp.where(n == N_QBLK, 3, 1)))
        return (t, 0, 0, 0)

    a_out = pl.pallas_call(
        _attn_kernel,
        grid=(N_QBLK + 1,),
        in_specs=[pl.BlockSpec((BQ, ATTN_W), lambda n: (n, 0)), _resident((L, 2 * KV_W)),
                  pl.BlockSpec((1, N_KV, Q_PER_KV * BQ, N_KEYS), bias_map)],
        out_specs=pl.BlockSpec((BQ, ATTN_W), lambda n: (n, 0)),
        out_shape=jax.ShapeDtypeStruct((L, ATTN_W), F32),
        compiler_params=_params(),
        name="attn",
    )(q, kv, bias)

    row_set = pl.BlockSpec((N1, None, None, 2 * F_W), lambda b: (0, b, 0, 0))
    g = pl.pallas_call(
        _dft_a_kernel,
        grid=(N2,),
        in_specs=[row_set, _resident((2 * N1, N1))],
        out_specs=row_set,
        out_shape=jax.ShapeDtypeStruct((N1, N2, 1, 2 * F_W), F32),
        scratch_shapes=[pltpu.VMEM((N1, 2 * F_W), F32), pltpu.VMEM((N1, 2 * F_W), F32)],
        compiler_params=_params(),
        name="dft_a",
    )(z.reshape(N1, N2, 1, 2 * F_W), bf(jnp.asarray(_stage_a_dft())))

    ec, es = _stage_c_dft()
    y = pl.pallas_call(
        _dft_c_kernel,
        grid=(N1 // TC,),
        in_specs=[pl.BlockSpec((TC, N2, None, 2 * F_W), lambda i: (i, 0, 0, 0)),
                  pl.BlockSpec((TC, N2P, N2P), lambda i: (i, 0, 0)),
                  pl.BlockSpec((TC, N2P, N2P), lambda i: (i, 0, 0))],
        out_specs=pl.BlockSpec((N2, TC, 1, F_W), lambda i: (0, i, 0, 0)),
        out_shape=jax.ShapeDtypeStruct((N2, N1, 1, F_W), F32),
        scratch_shapes=[pltpu.VMEM((TC, N2P, 2 * F_W), F32)],
        compiler_params=_params(),
        name="dft_c",
    )(g, bf(jnp.asarray(ec)), bf(jnp.asarray(es)))

    out = pl.pallas_call(
        _post_kernel,
        grid=(N_TILES,),
        in_specs=[row_tile(D_MODEL), row_tile(ATTN_W), flat_rows(F_W), _resident((1, ATTN_W)),
                  _resident((1, F_W)), _resident((D_MODEL, D_MODEL)), _resident((1, D_MODEL)),
                  _resident((D_MODEL, D_FF)), _resident((D_MODEL, D_FF)), _resident((D_FF, D_MODEL)),
                  _resident((1, D_MODEL))],
        out_specs=row_tile(D_MODEL),
        out_shape=jax.ShapeDtypeStruct((SEQ, D_MODEL), F32),
        compiler_params=_params(),
        name="post",
    )(h1, a_out, y.reshape(L, 1, F_W), _row(attn_out_norm), _row(fourier_out_norm), bf(w_out), _row(ffn2_norm),
      bf(ffn2_w_gate), bf(ffn2_w_up), bf(ffn2_w_down), _row(final_norm))
    return out.reshape(1, SEQ, D_MODEL)
```

```python
import functools

import numpy as np
import jax
import jax.numpy as jnp
from jax import lax
from jax.experimental import pallas as pl
from jax.experimental.pallas import tpu as pltpu

F32 = jnp.float32
BF16 = jnp.bfloat16

D_MODEL = 1024
SEQ = 16384
N_META = 16
L = SEQ + N_META
HEAD_DIM = 64
N_HEADS = 8
N_KV = 2
Q_PER_KV = N_HEADS // N_KV
ATTN_W = N_HEADS * HEAD_DIM
KV_W = N_KV * HEAD_DIM
F_W = 512
F_GROUP = 64
IN_W = ATTN_W + 2 * KV_W + F_W
QK_W = ATTN_W + KV_W
WINDOW = 128
D_FF = 2816
EPS = 1e-6
NEG = -1e30

TM = 400
N_TILES = L // TM
META_ROW0 = SEQ - (N_TILES - 1) * TM

BQ = 128
N_QBLK = SEQ // BQ
QBLK_PER_STEP = 3
BAND = BQ + 2 * WINDOW
N_KEYS = BAND + N_META + 8
SINK_COL = BAND + N_META

N1 = 400
N2 = 41
N2P = 48
TC = 25

VMEM_LIMIT = 56 * 1024 * 1024


def _ones_blockdiag():
    m = np.kron(np.eye(4), np.ones((HEAD_DIM, HEAD_DIM)))
    return jnp.asarray(m, BF16)


def _channel_dft():
    c = np.arange(F_GROUP)
    ang = 2.0 * np.pi * np.outer(c, c) / F_GROUP
    cos = np.kron(np.eye(4), np.cos(ang) / 8.0)
    sin = np.kron(np.eye(4), np.sin(ang) / 8.0)
    return np.concatenate([cos, sin], axis=1).astype(np.float32)


def _stage_a_dft():
    a = np.arange(N1, dtype=np.int64)
    m = (N2 * np.outer(a + N_META, a)) % L
    ang = 2.0 * np.pi * m / L
    return np.concatenate([np.cos(ang), -np.sin(ang)], axis=0).astype(np.float32)


def _stage_c_dft():
    c = np.arange(N1, dtype=np.int64)[:, None, None]
    d = np.arange(N2, dtype=np.int64)[None, :, None]
    b = np.arange(N2, dtype=np.int64)[None, None, :]
    m = ((c + N1 * d + N_META) * (b + N_META)) % L
    ang = 2.0 * np.pi * m / L
    ec = np.zeros((N1, N2P, N2P), np.float32)
    es = np.zeros((N1, N2P, N2P), np.float32)
    ec[:, :N2, :N2] = np.cos(ang) / np.sqrt(L)
    es[:, :N2, :N2] = np.sin(ang) / np.sqrt(L)
    return ec, es


def _attn_bias_const():
    slopes = 2.0 ** -(np.arange(N_HEADS) + 1.0)
    i = np.arange(BQ)[:, None]
    j = np.arange(BAND)[None, :]
    m = np.arange(N_META)[None, :]
    out = np.full((4, N_KV, Q_PER_KV * BQ, N_KEYS), NEG, np.float64)
    for t in range(4):
        if t < 3:
            dist = np.abs(t * WINDOW + i - j)
            band_ok = dist <= WINDOW
            dist_m = np.minimum(t * WINDOW + N_META + i - m, WINDOW) if t == 0 else np.full((BQ, N_META), WINDOW)
        else:
            dist = N_META + j - i
            band_ok = (dist <= WINDOW) & (i < N_META)
            dist_m = np.abs(i - m)
        for h in range(N_HEADS):
            g, hl = divmod(h, Q_PER_KV)
            rows = slice(hl * BQ, (hl + 1) * BQ)
            band = np.where(band_ok, -slopes[h] * dist, NEG)
            if t == 3:
                band = np.where(i < N_META, band, 0.0)
            out[t, g, rows, :BAND] = band
            out[t, g, rows, BAND:BAND + N_META] = -slopes[h] * dist_m
    return np.ascontiguousarray(out.transpose(0, 1, 3, 2)).astype(np.float32)


def _rms(x, g):
    return x * lax.rsqrt(jnp.mean(x * x, axis=-1, keepdims=True) + EPS) * g


def _swiglu_half(x, g_ref, wg_ref, wu_ref, wd_ref):
    xn = _rms(x, g_ref[...]).astype(BF16)
    gate = jnp.dot(xn, wg_ref[...], preferred_element_type=F32)
    up = jnp.dot(xn, wu_ref[...], preferred_element_type=F32)
    act = (gate * jax.nn.sigmoid(gate) * up).astype(BF16)
    return x + 0.5 * jnp.dot(act, wd_ref[...], preferred_element_type=F32)


def _pre_kernel(x_ref, meta_ref, g1_ref, wg_ref, wu_ref, wd_ref, gm_ref, win_ref, gqk_ref,
                ones_ref, cdft_ref, h1_ref, q_ref, kv_ref, z_ref):
    i = pl.program_id(0)
    row = lax.broadcasted_iota(jnp.int32, (TM, 1), 0)
    is_meta = jnp.logical_and(i == N_TILES - 1, row >= META_ROW0)
    x = jnp.where(is_meta, meta_ref[...], x_ref[...])

    h1 = _swiglu_half(x, g1_ref, wg_ref, wu_ref, wd_ref)
    h1_ref[...] = h1

    u = jnp.dot(_rms(h1, gm_ref[...]).astype(BF16), win_ref[...], preferred_element_type=F32)

    qk = u[:, :QK_W]
    sq = (qk * qk).astype(BF16)
    ones = ones_ref[...]
    ss = jnp.concatenate([
        jnp.dot(sq[:, 0:256], ones, preferred_element_type=F32),
        jnp.dot(sq[:, 256:512], ones, preferred_element_type=F32),
        jnp.dot(sq[:, 512:640], ones[:KV_W, :KV_W], preferred_element_type=F32)], axis=1)
    qkn = qk * lax.rsqrt(ss * (1.0 / HEAD_DIM) + EPS) * gqk_ref[...]
    q_ref[...] = qkn[:, :ATTN_W].astype(BF16)
    kv_ref[...] = jnp.concatenate([qkn[:, ATTN_W:], u[:, QK_W:QK_W + KV_W]], axis=1).astype(BF16)

    f = u[:, QK_W + KV_W:].astype(BF16)
    cdft = cdft_ref[...]
    lo = jnp.dot(f[:, :256], cdft, preferred_element_type=F32)
    hi = jnp.dot(f[:, 256:], cdft, preferred_element_type=F32)
    z_ref[...] = jnp.concatenate([lo[:, :256], hi[:, :256], lo[:, 256:], hi[:, 256:]], axis=1)


def _attn_kernel(q_ref, kv_ref, bias_ref, o_ref):
    step = pl.program_id(0)
    kv_meta = kv_ref[SEQ:L, :]
    zero_keys = jnp.zeros((8, 2 * KV_W), BF16)
    for j in range(QBLK_PER_STEP):
        n = step * QBLK_PER_STEP + j
        btype = jnp.where(n == 0, 0, jnp.where(n == N_QBLK - 1, 2, jnp.where(n == N_QBLK, 3, 1)))
        start = jnp.where(n == N_QBLK, 0, jnp.clip((n - 1) * BQ, 0, SEQ - BAND))
        start = pl.multiple_of(start, BQ)
        kv_cat = jnp.concatenate([kv_ref[pl.ds(start, BAND), :], kv_meta, zero_keys], axis=0)
        q = q_ref[j * BQ:(j + 1) * BQ, :]
        for g in range(N_KV):
            k = kv_cat[:, g * HEAD_DIM:(g + 1) * HEAD_DIM]
            v = kv_cat[:, KV_W + g * HEAD_DIM:KV_W + (g + 1) * HEAD_DIM]
            qg = jnp.concatenate(
                [q[:, (Q_PER_KV * g + hl) * HEAD_DIM:(Q_PER_KV * g + hl + 1) * HEAD_DIM]
                 for hl in range(Q_PER_KV)], axis=0)
            st = lax.dot_general(k, qg, (((1,), (1,)), ((), ())), preferred_element_type=F32)
            st = st + bias_ref[btype, g]
            m = jnp.max(st, axis=0, keepdims=True)
            p = jnp.exp(st - m)
            denom = jnp.sum(p, axis=0, keepdims=True)
            ot = lax.dot_general(v, p.astype(BF16), (((0,), (0,)), ((), ())), preferred_element_type=F32)
            ot = ot / denom
            for hp in range(Q_PER_KV // 2):
                h0 = Q_PER_KV * g + 2 * hp
                two = jnp.concatenate([ot[:, 2 * hp * BQ:(2 * hp + 1) * BQ],
                                       ot[:, (2 * hp + 1) * BQ:(2 * hp + 2) * BQ]], axis=0)
                o_ref[j * BQ:(j + 1) * BQ, h0 * HEAD_DIM:(h0 + 2) * HEAD_DIM] = two.T


def _dft_a_kernel(z_ref, wa_ref, g_ref, zs_ref, gs_ref):
    zs_ref[...] = z_ref[...]
    t = jnp.dot(wa_ref[...], zs_ref[...].astype(BF16), preferred_element_type=F32)
    gs_ref[:, :F_W] = t[:N1, :F_W] + t[N1:, F_W:]
    gs_ref[:, F_W:] = t[N1:, :F_W] - t[:N1, F_W:]
    g_ref[...] = gs_ref[...]


def _dft_c_kernel(g_ref, ec_ref, es_ref, y_ref, gs_ref):
    gs_ref[:, N2:, :] = jnp.zeros((TC, N2P - N2, 2 * F_W), F32)
    for c in range(TC):
        gs_ref[c, :N2, :] = g_ref[c]
        g = gs_ref[c].astype(BF16)
        y = (jnp.dot(ec_ref[c], g[:, :F_W], preferred_element_type=F32)
             + jnp.dot(es_ref[c], g[:, F_W:], preferred_element_type=F32))
        y_ref[:, c, 0, :] = y[:N2]


def _post_kernel(h1_ref, a_ref, y_ref, ga_ref, gf_ref, wout_ref, g2_ref, wg_ref, wu_ref, wd_ref,
                 gfin_ref, o_ref):
    mixed = jnp.concatenate([_rms(a_ref[...], ga_ref[...]), _rms(y_ref[...], gf_ref[...])], axis=1)
    h2 = h1_ref[...] + jnp.dot(mixed.astype(BF16), wout_ref[...], preferred_element_type=F32)
    h3 = _swiglu_half(h2, g2_ref, wg_ref, wu_ref, wd_ref)
    o_ref[...] = _rms(h3, gfin_ref[...])


def _resident(shape):
    zeros = (0,) * len(shape)
    return pl.BlockSpec(shape, lambda *_: zeros, pipeline_mode=pl.Buffered(1))


def _params():
    return pltpu.CompilerParams(dimension_semantics=("arbitrary",), vmem_limit_bytes=VMEM_LIMIT)


def _row(g):
    return g.astype(F32).reshape(1, -1)


def kernel(x, meta_tokens, ffn1_norm, ffn1_w_gate, ffn1_w_up, ffn1_w_down, mix_norm, w_in, q_norm, k_norm, sink, attn_out_norm, fourier_out_norm, w_out, ffn2_norm, ffn2_w_gate, ffn2_w_up, ffn2_w_down, final_norm):
    assert x.shape == (1, SEQ, D_MODEL) and x.dtype == F32
    x2 = x.reshape(SEQ, D_MODEL)
    meta_pad = jnp.pad(meta_tokens.astype(F32), ((META_ROW0, 0), (0, 0)))
    gqk = jnp.concatenate([jnp.tile(q_norm.astype(F32), N_HEADS) * (HEAD_DIM ** -0.5),
                           jnp.tile(k_norm.astype(F32), N_KV)]).reshape(1, QK_W)
    bf = lambda w: w.astype(BF16)

    row_tile = lambda w: pl.BlockSpec((TM, w), lambda i: (i, 0))
    flat_rows = lambda w: pl.BlockSpec((TM, None, w), lambda i: (i, 0, 0))
    h1, q, kv, z = pl.pallas_call(
        _pre_kernel,
        grid=(N_TILES,),
        in_specs=[row_tile(D_MODEL), _resident((TM, D_MODEL)), _resident((1, D_MODEL)),
                  _resident((D_MODEL, D_FF)), _resident((D_MODEL, D_FF)), _resident((D_FF, D_MODEL)),
                  _resident((1, D_MODEL)), _resident((D_MODEL, IN_W)), _resident((1, QK_W)),
                  _resident((256, 256)), _resident((256, 512))],
        out_specs=[row_tile(D_MODEL), row_tile(ATTN_W), row_tile(2 * KV_W), flat_rows(2 * F_W)],
        out_shape=[jax.ShapeDtypeStruct((L, D_MODEL), F32), jax.ShapeDtypeStruct((L, ATTN_W), BF16),
                   jax.ShapeDtypeStruct((L, 2 * KV_W), BF16), jax.ShapeDtypeStruct((L, 1, 2 * F_W), F32)],
        compiler_params=_params(),
        name="pre",
    )(x2, meta_pad, _row(ffn1_norm), bf(ffn1_w_gate), bf(ffn1_w_up), bf(ffn1_w_down), _row(mix_norm),
      bf(w_in), gqk, _ones_blockdiag(), bf(jnp.asarray(_channel_dft())))

    sink_rows = jnp.repeat(sink.astype(F32).reshape(N_KV, Q_PER_KV), BQ, axis=1)
    bias_c = jnp.asarray(_attn_bias_const())
    bias = jnp.concatenate(
        [bias_c[:, :, :SINK_COL],
         jnp.broadcast_to(sink_rows[None, :, None, :], (4, N_KV, 1, Q_PER_KV * BQ)),
         bias_c[:, :, SINK_COL + 1:]], axis=2)

    q_rows = pl.BlockSpec((QBLK_PER_STEP * BQ, ATTN_W), lambda s: (s, 0))
    a_out = pl.pallas_call(
        _attn_kernel,
        grid=((N_QBLK + 1) // QBLK_PER_STEP,),
        in_specs=[q_rows, _resident((L, 2 * KV_W)), _resident((4, N_KV, N_KEYS, Q_PER_KV * BQ))],
        out_specs=q_rows,
        out_shape=jax.ShapeDtypeStruct((L, ATTN_W), F32),
        compiler_params=_params(),
        name="attn",
    )(q, kv, bias)

    row_set = pl.BlockSpec((N1, None, None, 2 * F_W), lambda b: (0, b, 0, 0))
    g = pl.pallas_call(
        _dft_a_kernel,
        grid=(N2,),
        in_specs=[row_set, _resident((2 * N1, N1))],
        out_specs=row_set,
        out_shape=jax.ShapeDtypeStruct((N1, N2, 1, 2 * F_W), F32),
        scratch_shapes=[pltpu.VMEM((N1, 2 * F_W), F32), pltpu.VMEM((N1, 2 * F_W), F32)],
        compiler_params=_params(),
        name="dft_a",
    )(z.reshape(N1, N2, 1, 2 * F_W), bf(jnp.asarray(_stage_a_dft())))

    ec, es = _stage_c_dft()
    y = pl.pallas_call(
        _dft_c_kernel,
        grid=(N1 // TC,),
        in_specs=[pl.BlockSpec((TC, N2, None, 2 * F_W), lambda i: (i, 0, 0, 0)),
                  pl.BlockSpec((TC, N2P, N2P), lambda i: (i, 0, 0)),
                  pl.BlockSpec((TC, N2P, N2P), lambda i: (i, 0, 0))],
        out_specs=pl.BlockSpec((N2, TC, 1, F_W), lambda i: (0, i, 0, 0)),
        out_shape=jax.ShapeDtypeStruct((N2, N1, 1, F_W), F32),
        scratch_shapes=[pltpu.VMEM((TC, N2P, 2 * F_W), F32)],
        compiler_params=_params(),
        name="dft_c",
    )(g, bf(jnp.asarray(ec)), bf(jnp.asarray(es)))

    out = pl.pallas_call(
        _post_kernel,
        grid=(N_TILES,),
        in_specs=[row_tile(D_MODEL), row_tile(ATTN_W), flat_rows(F_W), _resident((1, ATTN_W)),
                  _resident((1, F_W)), _resident((D_MODEL, D_MODEL)), _resident((1, D_MODEL)),
                  _resident((D_MODEL, D_FF)), _resident((D_MODEL, D_FF)), _resident((D_FF, D_MODEL)),
                  _resident((1, D_MODEL))],
        out_specs=row_tile(D_MODEL),
        out_shape=jax.ShapeDtypeStruct((SEQ, D_MODEL), F32),
        compiler_params=_params(),
        name="post",
    )(h1, a_out, y.reshape(L, 1, F_W), _row(attn_out_norm), _row(fourier_out_norm), bf(w_out), _row(ffn2_norm),
      bf(ffn2_w_gate), bf(ffn2_w_up), bf(ffn2_w_down), _row(final_norm))
    return out.reshape(1, SEQ, D_MODEL)
```

```python
import functools

import numpy as np
import jax
import jax.numpy as jnp
from jax import lax
from jax.experimental import pallas as pl
from jax.experimental.pallas import tpu as pltpu

F32 = jnp.float32
BF16 = jnp.bfloat16

D_MODEL = 1024
SEQ = 16384
N_META = 16
L = SEQ + N_META
HEAD_DIM = 64
N_HEADS = 8
N_KV = 2
Q_PER_KV = N_HEADS // N_KV
ATTN_W = N_HEADS * HEAD_DIM
KV_W = N_KV * HEAD_DIM
F_W = 512
F_GROUP = 64
IN_W = ATTN_W + 2 * KV_W + F_W
QK_W = ATTN_W + KV_W
WINDOW = 128
D_FF = 2816
FF_CHUNK = 256
N_FF_CHUNKS = D_FF // FF_CHUNK
EPS = 1e-6
NEG = -1e30

TM = 400
N_TILES = L // TM
META_ROW0 = SEQ - (N_TILES - 1) * TM

BQ = 128
N_QBLK = SEQ // BQ
QBLK_PER_STEP = 3
BAND = BQ + 2 * WINDOW
N_KEYS = BAND + N_META + 8
SINK_COL = BAND + N_META

N1 = 400
N2 = 41
N2P = 48
TC = 25

VMEM_LIMIT = 56 * 1024 * 1024


def _ones_blockdiag():
    m = np.kron(np.eye(4), np.ones((HEAD_DIM, HEAD_DIM)))
    return jnp.asarray(m, BF16)


def _channel_dft():
    c = np.arange(F_GROUP)
    ang = 2.0 * np.pi * np.outer(c, c) / F_GROUP
    cos = np.kron(np.eye(4), np.cos(ang) / 8.0)
    sin = np.kron(np.eye(4), np.sin(ang) / 8.0)
    return np.concatenate([cos, sin], axis=1).astype(np.float32)


def _stage_a_dft():
    a = np.arange(N1, dtype=np.int64)
    m = (N2 * np.outer(a + N_META, a)) % L
    ang = 2.0 * np.pi * m / L
    return np.concatenate([np.cos(ang), -np.sin(ang)], axis=0).astype(np.float32)


def _stage_c_dft():
    c = np.arange(N1, dtype=np.int64)[:, None, None]
    d = np.arange(N2, dtype=np.int64)[None, :, None]
    b = np.arange(N2, dtype=np.int64)[None, None, :]
    m = ((c + N1 * d + N_META) * (b + N_META)) % L
    ang = 2.0 * np.pi * m / L
    ec = np.zeros((N1, N2P, N2P), np.float32)
    es = np.zeros((N1, N2P, N2P), np.float32)
    ec[:, :N2, :N2] = np.cos(ang) / np.sqrt(L)
    es[:, :N2, :N2] = np.sin(ang) / np.sqrt(L)
    return ec, es


def _attn_bias_const():
    slopes = 2.0 ** -(np.arange(N_HEADS) + 1.0)
    i = np.arange(BQ)[:, None]
    j = np.arange(BAND)[None, :]
    m = np.arange(N_META)[None, :]
    out = np.full((4, N_KV, Q_PER_KV * BQ, N_KEYS), NEG, np.float64)
    for t in range(4):
        if t < 3:
            dist = np.abs(t * WINDOW + i - j)
            band_ok = dist <= WINDOW
            dist_m = np.minimum(t * WINDOW + N_META + i - m, WINDOW) if t == 0 else np.full((BQ, N_META), WINDOW)
        else:
            dist = N_META + j - i
            band_ok = (dist <= WINDOW) & (i < N_META)
            dist_m = np.abs(i - m)
        for h in range(N_HEADS):
            g, hl = divmod(h, Q_PER_KV)
            rows = slice(hl * BQ, (hl + 1) * BQ)
            band = np.where(band_ok, -slopes[h] * dist, NEG)
            if t == 3:
                band = np.where(i < N_META, band, 0.0)
            out[t, g, rows, :BAND] = band
            out[t, g, rows, BAND:BAND + N_META] = -slopes[h] * dist_m
    return np.ascontiguousarray(out.transpose(0, 1, 3, 2)).astype(np.float32)


def _rms(x, g):
    return x * lax.rsqrt(jnp.mean(x * x, axis=-1, keepdims=True) + EPS) * g


def _swiglu_half(x, g_ref, wgu_s, wd_s):
    xn = _rms(x, g_ref[...]).astype(BF16)
    acc = None
    for c in range(N_FF_CHUNKS):
        gu = jnp.dot(xn, wgu_s[c], preferred_element_type=F32)
        gate, up = gu[:, :FF_CHUNK], gu[:, FF_CHUNK:]
        act = (gate * jax.nn.sigmoid(gate) * up).astype(BF16)
        part = jnp.dot(act, wd_s[c], preferred_element_type=F32)
        acc = part if acc is None else acc + part
    return x + 0.5 * acc


def _cast_weights(s, wg_ref, wu_ref, wd_ref, wgu_s, wd_s, wx_ref, wx_s):
    wgu_s[s, :, :FF_CHUNK] = wg_ref[...].astype(BF16)
    wgu_s[s, :, FF_CHUNK:] = wu_ref[...].astype(BF16)
    wd_s[s] = wd_ref[...].astype(BF16)
    for c in range(wx_s.shape[1] // FF_CHUNK):
        @pl.when(s == c)
        def _():
            wx_s[:, c * FF_CHUNK:(c + 1) * FF_CHUNK] = wx_ref[...].astype(BF16)


def _pre_kernel(x_ref, meta_ref, g1_ref, wg_ref, wu_ref, wd_ref, gm_ref, win_ref, gqk_ref,
                ones_ref, cdft_ref, h1_ref, q_ref, kv_ref, z_ref, wgu_s, wd_s, win_s):
    s = pl.program_id(0)

    @pl.when(s < N_FF_CHUNKS)
    def _():
        _cast_weights(s, wg_ref, wu_ref, wd_ref, wgu_s, wd_s, win_ref, win_s)

    @pl.when(s >= N_FF_CHUNKS)
    def _():
        _pre_tile(s - N_FF_CHUNKS, x_ref, meta_ref, g1_ref, gm_ref, gqk_ref, ones_ref, cdft_ref,
                  h1_ref, q_ref, kv_ref, z_ref, wgu_s, wd_s, win_s)


def _pre_tile(i, x_ref, meta_ref, g1_ref, gm_ref, gqk_ref, ones_ref, cdft_ref,
              h1_ref, q_ref, kv_ref, z_ref, wgu_s, wd_s, win_s):
    row = lax.broadcasted_iota(jnp.int32, (TM, 1), 0)
    is_meta = jnp.logical_and(i == N_TILES - 1, row >= META_ROW0)
    x = jnp.where(is_meta, meta_ref[...], x_ref[...])

    h1 = _swiglu_half(x, g1_ref, wgu_s, wd_s)
    h1_ref[...] = h1

    u = jnp.dot(_rms(h1, gm_ref[...]).astype(BF16), win_s[...], preferred_element_type=F32)

    qk = u[:, :QK_W]
    sq = (qk * qk).astype(BF16)
    ones = ones_ref[...]
    ss = jnp.concatenate([
        jnp.dot(sq[:, 0:256], ones, preferred_element_type=F32),
        jnp.dot(sq[:, 256:512], ones, preferred_element_type=F32),
        jnp.dot(sq[:, 512:640], ones[:KV_W, :KV_W], preferred_element_type=F32)], axis=1)
    qkn = qk * lax.rsqrt(ss * (1.0 / HEAD_DIM) + EPS) * gqk_ref[...]
    q_ref[...] = qkn[:, :ATTN_W].astype(BF16)
    kv_ref[...] = jnp.concatenate([qkn[:, ATTN_W:], u[:, QK_W:QK_W + KV_W]], axis=1).astype(BF16)

    f = u[:, QK_W + KV_W:].astype(BF16)
    cdft = cdft_ref[...]
    lo = jnp.dot(f[:, :256], cdft, preferred_element_type=F32)
    hi = jnp.dot(f[:, 256:], cdft, preferred_element_type=F32)
    z_ref[...] = jnp.concatenate([lo[:, :256], hi[:, :256], lo[:, 256:], hi[:, 256:]], axis=1)


def _attn_kernel(q_ref, kv_ref, bias_ref, o_ref):
    step = pl.program_id(0)
    kv_meta = kv_ref[SEQ:L, :]
    zero_keys = jnp.zeros((8, 2 * KV_W), BF16)
    for j in range(QBLK_PER_STEP):
        n = step * QBLK_PER_STEP + j
        btype = jnp.where(n == 0, 0, jnp.where(n == N_QBLK - 1, 2, jnp.where(n == N_QBLK, 3, 1)))
        start = jnp.where(n == N_QBLK, 0, jnp.clip((n - 1) * BQ, 0, SEQ - BAND))
        start = pl.multiple_of(start, BQ)
        kv_cat = jnp.concatenate([kv_ref[pl.ds(start, BAND), :], kv_meta, zero_keys], axis=0)
        q = q_ref[j * BQ:(j + 1) * BQ, :]
        for g in range(N_KV):
            k = kv_cat[:, g * HEAD_DIM:(g + 1) * HEAD_DIM]
            v = kv_cat[:, KV_W + g * HEAD_DIM:KV_W + (g + 1) * HEAD_DIM]
            qg = jnp.concatenate(
                [q[:, (Q_PER_KV * g + hl) * HEAD_DIM:(Q_PER_KV * g + hl + 1) * HEAD_DIM]
                 for hl in range(Q_PER_KV)], axis=0)
            st = lax.dot_general(k, qg, (((1,), (1,)), ((), ())), preferred_element_type=F32)
            st = st + bias_ref[btype, g]
            m = jnp.max(st, axis=0, keepdims=True)
            p = jnp.exp(st - m)
            denom = jnp.sum(p, axis=0, keepdims=True)
            ot = lax.dot_general(v, p.astype(BF16), (((0,), (0,)), ((), ())), preferred_element_type=F32)
            ot = ot / denom
            for hp in range(Q_PER_KV // 2):
                h0 = Q_PER_KV * g + 2 * hp
                two = jnp.concatenate([ot[:, 2 * hp * BQ:(2 * hp + 1) * BQ],
                                       ot[:, (2 * hp + 1) * BQ:(2 * hp + 2) * BQ]], axis=0)
                o_ref[j * BQ:(j + 1) * BQ, h0 * HEAD_DIM:(h0 + 2) * HEAD_DIM] = two.T


def _dft_a_kernel(z_ref, wa_ref, g_ref, zs_ref, gs_ref):
    zs_ref[...] = z_ref[...]
    t = jnp.dot(wa_ref[...], zs_ref[...].astype(BF16), preferred_element_type=F32)
    gs_ref[:, :F_W] = t[:N1, :F_W] + t[N1:, F_W:]
    gs_ref[:, F_W:] = t[N1:, :F_W] - t[:N1, F_W:]
    g_ref[...] = gs_ref[...]


def _dft_c_kernel(g_ref, ec_ref, es_ref, y_ref, gs_ref):
    gs_ref[:, N2:, :] = jnp.zeros((TC, N2P - N2, 2 * F_W), F32)
    for c in range(TC):
        gs_ref[c, :N2, :] = g_ref[c]
        g = gs_ref[c].astype(BF16)
        y = (jnp.dot(ec_ref[c], g[:, :F_W], preferred_element_type=F32)
             + jnp.dot(es_ref[c], g[:, F_W:], preferred_element_type=F32))
        y_ref[:, c, 0, :] = y[:N2]


def _post_kernel(h1_ref, a_ref, y_ref, ga_ref, gf_ref, wout_ref, g2_ref, wg_ref, wu_ref, wd_ref,
                 gfin_ref, o_ref, wgu_s, wd_s, wout_s):
    s = pl.program_id(0)

    @pl.when(s < N_FF_CHUNKS)
    def _():
        _cast_weights(s, wg_ref, wu_ref, wd_ref, wgu_s, wd_s, wout_ref, wout_s)

    @pl.when(s >= N_FF_CHUNKS)
    def _():
        mixed = jnp.concatenate([_rms(a_ref[...], ga_ref[...]), _rms(y_ref[...], gf_ref[...])], axis=1)
        h2 = h1_ref[...] + jnp.dot(mixed.astype(BF16), wout_s[...], preferred_element_type=F32)
        h3 = _swiglu_half(h2, g2_ref, wgu_s, wd_s)
        o_ref[...] = _rms(h3, gfin_ref[...])


def _resident(shape):
    zeros = (0,) * len(shape)
    return pl.BlockSpec(shape, lambda *_: zeros, pipeline_mode=pl.Buffered(1))


def _params(flags=None):
    return pltpu.CompilerParams(dimension_semantics=("arbitrary",), vmem_limit_bytes=VMEM_LIMIT, flags=flags)


def _row(g):
    return g.astype(F32).reshape(1, -1)


def kernel(x, meta_tokens, ffn1_norm, ffn1_w_gate, ffn1_w_up, ffn1_w_down, mix_norm, w_in, q_norm, k_norm, sink, attn_out_norm, fourier_out_norm, w_out, ffn2_norm, ffn2_w_gate, ffn2_w_up, ffn2_w_down, final_norm):
    assert x.shape == (1, SEQ, D_MODEL) and x.dtype == F32
    x2 = x.reshape(SEQ, D_MODEL)
    meta_pad = jnp.pad(meta_tokens.astype(F32), ((META_ROW0, 0), (0, 0)))
    gqk = jnp.concatenate([jnp.tile(q_norm.astype(F32), N_HEADS) * (HEAD_DIM ** -0.5),
                           jnp.tile(k_norm.astype(F32), N_KV)]).reshape(1, QK_W)
    bf = lambda w: w.astype(BF16)

    tile_idx = lambda s: jnp.maximum(s - N_FF_CHUNKS, 0)
    row_tile = lambda w: pl.BlockSpec((TM, w), lambda s: (tile_idx(s), 0))
    flat_rows = lambda w: pl.BlockSpec((TM, None, w), lambda s: (tile_idx(s), 0, 0))
    ff_cols = pl.BlockSpec((D_MODEL, FF_CHUNK), lambda s: (0, jnp.minimum(s, N_FF_CHUNKS - 1)))
    ff_rows = pl.BlockSpec((FF_CHUNK, D_MODEL), lambda s: (jnp.minimum(s, N_FF_CHUNKS - 1), 0))
    proj_cols = lambda n: pl.BlockSpec((D_MODEL, FF_CHUNK), lambda s: (0, jnp.minimum(s, n // FF_CHUNK - 1)))
    ffn_scratch = [pltpu.VMEM((N_FF_CHUNKS, D_MODEL, 2 * FF_CHUNK), BF16),
                   pltpu.VMEM((N_FF_CHUNKS, FF_CHUNK, D_MODEL), BF16)]
    f32 = lambda w: w.astype(F32)

    h1, q, kv, z = pl.pallas_call(
        _pre_kernel,
        grid=(N_FF_CHUNKS + N_TILES,),
        in_specs=[row_tile(D_MODEL), _resident((TM, D_MODEL)), _resident((1, D_MODEL)),
                  ff_cols, ff_cols, ff_rows,
                  _resident((1, D_MODEL)), proj_cols(IN_W), _resident((1, QK_W)),
                  _resident((256, 256)), _resident((256, 512))],
        out_specs=[row_tile(D_MODEL), row_tile(ATTN_W), row_tile(2 * KV_W), flat_rows(2 * F_W)],
        out_shape=[jax.ShapeDtypeStruct((L, D_MODEL), F32), jax.ShapeDtypeStruct((L, ATTN_W), BF16),
                   jax.ShapeDtypeStruct((L, 2 * KV_W), BF16), jax.ShapeDtypeStruct((L, 1, 2 * F_W), F32)],
        scratch_shapes=ffn_scratch + [pltpu.VMEM((D_MODEL, IN_W), BF16)],
        compiler_params=_params(),
        name="pre",
    )(x2, meta_pad, _row(ffn1_norm), f32(ffn1_w_gate), f32(ffn1_w_up), f32(ffn1_w_down), _row(mix_norm),
      f32(w_in), gqk, _ones_blockdiag(), bf(jnp.asarray(_channel_dft())))

    sink_rows = jnp.repeat(sink.astype(F32).reshape(N_KV, Q_PER_KV), BQ, axis=1)
    bias_c = jnp.asarray(_attn_bias_const())
    bias = jnp.concatenate(
        [bias_c[:, :, :SINK_COL],
         jnp.broadcast_to(sink_rows[None, :, None, :], (4, N_KV, 1, Q_PER_KV * BQ)),
         bias_c[:, :, SINK_COL + 1:]], axis=2)

    q_rows = pl.BlockSpec((QBLK_PER_STEP * BQ, ATTN_W), lambda s: (s, 0))
    a_out = pl.pallas_call(
        _attn_kernel,
        grid=((N_QBLK + 1) // QBLK_PER_STEP,),
        in_specs=[q_rows, _resident((L, 2 * KV_W)), _resident((4, N_KV, N_KEYS, Q_PER_KV * BQ))],
        out_specs=q_rows,
        out_shape=jax.ShapeDtypeStruct((L, ATTN_W), F32),
        compiler_params=_params(),
        name="attn",
    )(q, kv, bias)

    row_set = pl.BlockSpec((N1, None, None, 2 * F_W), lambda b: (0, b, 0, 0))
    g = pl.pallas_call(
        _dft_a_kernel,
        grid=(N2,),
        in_specs=[row_set, _resident((2 * N1, N1))],
        out_specs=row_set,
        out_shape=jax.ShapeDtypeStruct((N1, N2, 1, 2 * F_W), F32),
        scratch_shapes=[pltpu.VMEM((N1, 2 * F_W), F32), pltpu.VMEM((N1, 2 * F_W), F32)],
        compiler_params=_params(),
        name="dft_a",
    )(z.reshape(N1, N2, 1, 2 * F_W), bf(jnp.asarray(_stage_a_dft())))

    ec, es = _stage_c_dft()
    y = pl.pallas_call(
        _dft_c_kernel,
        grid=(N1 // TC,),
        in_specs=[pl.BlockSpec((TC, N2, None, 2 * F_W), lambda i: (i, 0, 0, 0)),
                  pl.BlockSpec((TC, N2P, N2P), lambda i: (i, 0, 0)),
                  pl.BlockSpec((TC, N2P, N2P), lambda i: (i, 0, 0))],
        out_specs=pl.BlockSpec((N2, TC, 1, F_W), lambda i: (0, i, 0, 0)),
        out_shape=jax.ShapeDtypeStruct((N2, N1, 1, F_W), F32),
        scratch_shapes=[pltpu.VMEM((TC, N2P, 2 * F_W), F32)],
        compiler_params=_params(),
        name="dft_c",
    )(g, bf(jnp.asarray(ec)), bf(jnp.asarray(es)))

    out = pl.pallas_call(
        _post_kernel,
        grid=(N_FF_CHUNKS + N_TILES,),
        in_specs=[row_tile(D_MODEL), row_tile(ATTN_W), flat_rows(F_W), _resident((1, ATTN_W)),
                  _resident((1, F_W)), proj_cols(D_MODEL), _resident((1, D_MODEL)),
                  ff_cols, ff_cols, ff_rows, _resident((1, D_MODEL))],
        out_specs=row_tile(D_MODEL),
        out_shape=jax.ShapeDtypeStruct((SEQ, D_MODEL), F32),
        scratch_shapes=ffn_scratch + [pltpu.VMEM((D_MODEL, D_MODEL), BF16)],
        compiler_params=_params(),
        name="post",
    )(h1, a_out, y.reshape(L, 1, F_W), _row(attn_out_norm), _row(fourier_out_norm), f32(w_out), _row(ffn2_norm),
      f32(ffn2_w_gate), f32(ffn2_w_up), f32(ffn2_w_down), _row(final_norm))
    return out.reshape(1, SEQ, D_MODEL)
```

```python
import functools

import numpy as np
import jax
import jax.numpy as jnp
from jax import lax
from jax.experimental import pallas as pl
from jax.experimental.pallas import tpu as pltpu

F32 = jnp.float32
BF16 = jnp.bfloat16

D_MODEL = 1024
SEQ = 16384
N_META = 16
L = SEQ + N_META
HEAD_DIM = 64
N_HEADS = 8
N_KV = 2
Q_PER_KV = N_HEADS // N_KV
ATTN_W = N_HEADS * HEAD_DIM
KV_W = N_KV * HEAD_DIM
F_W = 512
F_GROUP = 64
IN_W = ATTN_W + 2 * KV_W + F_W
QK_W = ATTN_W + KV_W
WINDOW = 128
D_FF = 2816
N_CAST = 8
EPS = 1e-6
NEG = -1e30

TM = 400
N_TILES = L // TM
META_ROW0 = SEQ - (N_TILES - 1) * TM

BQ = 128
N_QBLK = SEQ // BQ
QBLK_PER_STEP = 3
BAND = BQ + 2 * WINDOW
N_KEYS = BAND + N_META + 8
SINK_COL = BAND + N_META

N1 = 400
N2 = 41
N2P = 48
TC = 25

VMEM_LIMIT = 56 * 1024 * 1024


def _ones_blockdiag():
    m = np.kron(np.eye(4), np.ones((HEAD_DIM, HEAD_DIM)))
    return jnp.asarray(m, BF16)


def _channel_dft():
    c = np.arange(F_GROUP)
    ang = 2.0 * np.pi * np.outer(c, c) / F_GROUP
    cos = np.kron(np.eye(4), np.cos(ang) / 8.0)
    sin = np.kron(np.eye(4), np.sin(ang) / 8.0)
    return np.concatenate([cos, sin], axis=1).astype(np.float32)


def _stage_a_dft():
    a = np.arange(N1, dtype=np.int64)
    m = (N2 * np.outer(a + N_META, a)) % L
    ang = 2.0 * np.pi * m / L
    return np.concatenate([np.cos(ang), -np.sin(ang)], axis=0).astype(np.float32)


def _stage_c_dft():
    c = np.arange(N1, dtype=np.int64)[:, None, None]
    d = np.arange(N2, dtype=np.int64)[None, :, None]
    b = np.arange(N2, dtype=np.int64)[None, None, :]
    m = ((c + N1 * d + N_META) * (b + N_META)) % L
    ang = 2.0 * np.pi * m / L
    ec = np.zeros((N1, N2P, N2P), np.float32)
    es = np.zeros((N1, N2P, N2P), np.float32)
    ec[:, :N2, :N2] = np.cos(ang) / np.sqrt(L)
    es[:, :N2, :N2] = np.sin(ang) / np.sqrt(L)
    return ec, es


def _attn_bias_const():
    slopes = 2.0 ** -(np.arange(N_HEADS) + 1.0)
    i = np.arange(BQ)[:, None]
    j = np.arange(BAND)[None, :]
    m = np.arange(N_META)[None, :]
    out = np.full((4, N_KV, Q_PER_KV * BQ, N_KEYS), NEG, np.float64)
    for t in range(4):
        if t < 3:
            dist = np.abs(t * WINDOW + i - j)
            band_ok = dist <= WINDOW
            dist_m = np.minimum(t * WINDOW + N_META + i - m, WINDOW) if t == 0 else np.full((BQ, N_META), WINDOW)
        else:
            dist = N_META + j - i
            band_ok = (dist <= WINDOW) & (i < N_META)
            dist_m = np.abs(i - m)
        for h in range(N_HEADS):
            g, hl = divmod(h, Q_PER_KV)
            rows = slice(hl * BQ, (hl + 1) * BQ)
            band = np.where(band_ok, -slopes[h] * dist, NEG)
            if t == 3:
                band = np.where(i < N_META, band, 0.0)
            out[t, g, rows, :BAND] = band
            out[t, g, rows, BAND:BAND + N_META] = -slopes[h] * dist_m
    return np.ascontiguousarray(out.transpose(0, 1, 3, 2)).astype(np.float32)


def _rms(x, g):
    return x * lax.rsqrt(jnp.mean(x * x, axis=-1, keepdims=True) + EPS) * g


def _swiglu_half(x, g_ref, wg_s, wu_s, wd_s):
    xn = _rms(x, g_ref[...]).astype(BF16)
    gate = jnp.dot(xn, wg_s[...], preferred_element_type=F32)
    up = jnp.dot(xn, wu_s[...], preferred_element_type=F32)
    act = (gate * jax.nn.sigmoid(gate) * up).astype(BF16)
    return x + 0.5 * jnp.dot(act, wd_s[...], preferred_element_type=F32)


def _cast_weights(s, f32_refs, bf16_scratch):
    for w_ref, w_s in zip(f32_refs, bf16_scratch):
        rows = w_ref.shape[0]
        w_s[pl.ds(pl.multiple_of(s * rows, rows), rows), :] = w_ref[...].astype(BF16)


def _pre_kernel(x_ref, meta_ref, g1_ref, wg_ref, wu_ref, wd_ref, gm_ref, win_ref, gqk_ref,
                ones_ref, cdft_ref, h1_ref, q_ref, kv_ref, z_ref, wg_s, wu_s, wd_s, win_s):
    s = pl.program_id(0)

    @pl.when(s < N_CAST)
    def _():
        _cast_weights(s, (wg_ref, wu_ref, wd_ref, win_ref), (wg_s, wu_s, wd_s, win_s))

    @pl.when(s >= N_CAST)
    def _():
        _pre_tile(s - N_CAST, x_ref, meta_ref, g1_ref, gm_ref, gqk_ref, ones_ref, cdft_ref,
                  h1_ref, q_ref, kv_ref, z_ref, wg_s, wu_s, wd_s, win_s)


def _pre_tile(i, x_ref, meta_ref, g1_ref, gm_ref, gqk_ref, ones_ref, cdft_ref,
              h1_ref, q_ref, kv_ref, z_ref, wg_s, wu_s, wd_s, win_s):
    row = lax.broadcasted_iota(jnp.int32, (TM, 1), 0)
    is_meta = jnp.logical_and(i == N_TILES - 1, row >= META_ROW0)
    x = jnp.where(is_meta, meta_ref[...], x_ref[...])

    h1 = _swiglu_half(x, g1_ref, wg_s, wu_s, wd_s)
    h1_ref[...] = h1

    u = jnp.dot(_rms(h1, gm_ref[...]).astype(BF16), win_s[...], preferred_element_type=F32)

    qk = u[:, :QK_W]
    sq = (qk * qk).astype(BF16)
    ones = ones_ref[...]
    ss = jnp.concatenate([
        jnp.dot(sq[:, 0:256], ones, preferred_element_type=F32),
        jnp.dot(sq[:, 256:512], ones, preferred_element_type=F32),
        jnp.dot(sq[:, 512:640], ones[:KV_W, :KV_W], preferred_element_type=F32)], axis=1)
    qkn = qk * lax.rsqrt(ss * (1.0 / HEAD_DIM) + EPS) * gqk_ref[...]
    q_ref[...] = qkn[:, :ATTN_W].astype(BF16)
    kv_ref[...] = jnp.concatenate([qkn[:, ATTN_W:], u[:, QK_W:QK_W + KV_W]], axis=1).astype(BF16)

    f = u[:, QK_W + KV_W:].astype(BF16)
    cdft = cdft_ref[...]
    lo = jnp.dot(f[:, :256], cdft, preferred_element_type=F32)
    hi = jnp.dot(f[:, 256:], cdft, preferred_element_type=F32)
    z_ref[...] = jnp.concatenate([lo[:, :256], hi[:, :256], lo[:, 256:], hi[:, 256:]], axis=1)


def _attn_kernel(q_ref, kv_ref, bias_ref, o_ref):
    step = pl.program_id(0)
    kv_meta = kv_ref[SEQ:L, :]
    zero_keys = jnp.zeros((8, 2 * KV_W), BF16)
    for j in range(QBLK_PER_STEP):
        n = step * QBLK_PER_STEP + j
        btype = jnp.where(n == 0, 0, jnp.where(n == N_QBLK - 1, 2, jnp.where(n == N_QBLK, 3, 1)))
        start = jnp.where(n == N_QBLK, 0, jnp.clip((n - 1) * BQ, 0, SEQ - BAND))
        start = pl.multiple_of(start, BQ)
        kv_cat = jnp.concatenate([kv_ref[pl.ds(start, BAND), :], kv_meta, zero_keys], axis=0)
        q = q_ref[j * BQ:(j + 1) * BQ, :]
        for g in range(N_KV):
            k = kv_cat[:, g * HEAD_DIM:(g + 1) * HEAD_DIM]
            v = kv_cat[:, KV_W + g * HEAD_DIM:KV_W + (g + 1) * HEAD_DIM]
            qg = jnp.concatenate(
                [q[:, (Q_PER_KV * g + hl) * HEAD_DIM:(Q_PER_KV * g + hl + 1) * HEAD_DIM]
                 for hl in range(Q_PER_KV)], axis=0)
            st = lax.dot_general(k, qg, (((1,), (1,)), ((), ())), preferred_element_type=F32)
            st = st + bias_ref[btype, g]
            m = jnp.max(st, axis=0, keepdims=True)
            p = jnp.exp(st - m)
            denom = jnp.sum(p, axis=0, keepdims=True)
            ot = lax.dot_general(v, p.astype(BF16), (((0,), (0,)), ((), ())), preferred_element_type=F32)
            ot = ot / denom
            for hp in range(Q_PER_KV // 2):
                h0 = Q_PER_KV * g + 2 * hp
                two = jnp.concatenate([ot[:, 2 * hp * BQ:(2 * hp + 1) * BQ],
                                       ot[:, (2 * hp + 1) * BQ:(2 * hp + 2) * BQ]], axis=0)
                o_ref[j * BQ:(j + 1) * BQ, h0 * HEAD_DIM:(h0 + 2) * HEAD_DIM] = two.T


def _dft_a_kernel(z_ref, wa_ref, g_ref, zs_ref, gs_ref):
    zs_ref[...] = z_ref[...]
    t = jnp.dot(wa_ref[...], zs_ref[...].astype(BF16), preferred_element_type=F32)
    gs_ref[:, :F_W] = t[:N1, :F_W] + t[N1:, F_W:]
    gs_ref[:, F_W:] = t[N1:, :F_W] - t[:N1, F_W:]
    g_ref[...] = gs_ref[...]


def _dft_c_kernel(g_ref, ec_ref, es_ref, y_ref, gs_ref):
    gs_ref[:, N2:, :] = jnp.zeros((TC, N2P - N2, 2 * F_W), F32)
    for c in range(TC):
        gs_ref[c, :N2, :] = g_ref[c]
        g = gs_ref[c].astype(BF16)
        y = (jnp.dot(ec_ref[c], g[:, :F_W], preferred_element_type=F32)
             + jnp.dot(es_ref[c], g[:, F_W:], preferred_element_type=F32))
        y_ref[:, c, 0, :] = y[:N2]


def _post_kernel(h1_ref, a_ref, y_ref, ga_ref, gf_ref, wout_ref, g2_ref, wg_ref, wu_ref, wd_ref,
                 gfin_ref, o_ref, wg_s, wu_s, wd_s, wout_s):
    s = pl.program_id(0)

    @pl.when(s < N_CAST)
    def _():
        _cast_weights(s, (wg_ref, wu_ref, wd_ref, wout_ref), (wg_s, wu_s, wd_s, wout_s))

    @pl.when(s >= N_CAST)
    def _():
        mixed = jnp.concatenate([_rms(a_ref[...], ga_ref[...]), _rms(y_ref[...], gf_ref[...])], axis=1)
        h2 = h1_ref[...] + jnp.dot(mixed.astype(BF16), wout_s[...], preferred_element_type=F32)
        h3 = _swiglu_half(h2, g2_ref, wg_s, wu_s, wd_s)
        o_ref[...] = _rms(h3, gfin_ref[...])


def _resident(shape):
    zeros = (0,) * len(shape)
    return pl.BlockSpec(shape, lambda *_: zeros, pipeline_mode=pl.Buffered(1))


def _params(flags=None):
    return pltpu.CompilerParams(dimension_semantics=("arbitrary",), vmem_limit_bytes=VMEM_LIMIT, flags=flags)


def _row(g):
    return g.astype(F32).reshape(1, -1)


def kernel(x, meta_tokens, ffn1_norm, ffn1_w_gate, ffn1_w_up, ffn1_w_down, mix_norm, w_in, q_norm, k_norm, sink, attn_out_norm, fourier_out_norm, w_out, ffn2_norm, ffn2_w_gate, ffn2_w_up, ffn2_w_down, final_norm):
    assert x.shape == (1, SEQ, D_MODEL) and x.dtype == F32
    x2 = x.reshape(SEQ, D_MODEL)
    meta_pad = jnp.pad(meta_tokens.astype(F32), ((META_ROW0, 0), (0, 0)))
    gqk = jnp.concatenate([jnp.tile(q_norm.astype(F32), N_HEADS) * (HEAD_DIM ** -0.5),
                           jnp.tile(k_norm.astype(F32), N_KV)]).reshape(1, QK_W)
    bf = lambda w: w.astype(BF16)

    tile_idx = lambda s: jnp.maximum(s - N_CAST, 0)
    row_tile = lambda w: pl.BlockSpec((TM, w), lambda s: (tile_idx(s), 0))
    flat_rows = lambda w: pl.BlockSpec((TM, None, w), lambda s: (tile_idx(s), 0, 0))
    cast_rows = lambda r, c: pl.BlockSpec((r // N_CAST, c), lambda s: (jnp.minimum(s, N_CAST - 1), 0))
    bf16_copy = lambda r, c: pltpu.VMEM((r, c), BF16)
    ffn_in = [cast_rows(D_MODEL, D_FF), cast_rows(D_MODEL, D_FF), cast_rows(D_FF, D_MODEL)]
    ffn_scratch = [bf16_copy(D_MODEL, D_FF), bf16_copy(D_MODEL, D_FF), bf16_copy(D_FF, D_MODEL)]
    f32 = lambda w: w.astype(F32)

    h1, q, kv, z = pl.pallas_call(
        _pre_kernel,
        grid=(N_CAST + N_TILES,),
        in_specs=[row_tile(D_MODEL), _resident((TM, D_MODEL)), _resident((1, D_MODEL)), *ffn_in,
                  _resident((1, D_MODEL)), cast_rows(D_MODEL, IN_W), _resident((1, QK_W)),
                  _resident((256, 256)), _resident((256, 512))],
        out_specs=[row_tile(D_MODEL), row_tile(ATTN_W), row_tile(2 * KV_W), flat_rows(2 * F_W)],
        out_shape=[jax.ShapeDtypeStruct((L, D_MODEL), F32), jax.ShapeDtypeStruct((L, ATTN_W), BF16),
                   jax.ShapeDtypeStruct((L, 2 * KV_W), BF16), jax.ShapeDtypeStruct((L, 1, 2 * F_W), F32)],
        scratch_shapes=ffn_scratch + [bf16_copy(D_MODEL, IN_W)],
        compiler_params=_params(),
        name="pre",
    )(x2, meta_pad, _row(ffn1_norm), f32(ffn1_w_gate), f32(ffn1_w_up), f32(ffn1_w_down), _row(mix_norm),
      f32(w_in), gqk, _ones_blockdiag(), bf(jnp.asarray(_channel_dft())))

    sink_rows = jnp.repeat(sink.astype(F32).reshape(N_KV, Q_PER_KV), BQ, axis=1)
    bias_c = jnp.asarray(_attn_bias_const())
    bias = jnp.concatenate(
        [bias_c[:, :, :SINK_COL],
         jnp.broadcast_to(sink_rows[None, :, None, :], (4, N_KV, 1, Q_PER_KV * BQ)),
         bias_c[:, :, SINK_COL + 1:]], axis=2)

    q_rows = pl.BlockSpec((QBLK_PER_STEP * BQ, ATTN_W), lambda s: (s, 0))
    a_out = pl.pallas_call(
        _attn_kernel,
        grid=((N_QBLK + 1) // QBLK_PER_STEP,),
        in_specs=[q_rows, _resident((L, 2 * KV_W)), _resident((4, N_KV, N_KEYS, Q_PER_KV * BQ))],
        out_specs=q_rows,
        out_shape=jax.ShapeDtypeStruct((L, ATTN_W), F32),
        compiler_params=_params(),
        name="attn",
    )(q, kv, bias)

    row_set = pl.BlockSpec((N1, None, None, 2 * F_W), lambda b: (0, b, 0, 0))
    g = pl.pallas_call(
        _dft_a_kernel,
        grid=(N2,),
        in_specs=[row_set, _resident((2 * N1, N1))],
        out_specs=row_set,
        out_shape=jax.ShapeDtypeStruct((N1, N2, 1, 2 * F_W), F32),
        scratch_shapes=[pltpu.VMEM((N1, 2 * F_W), F32), pltpu.VMEM((N1, 2 * F_W), F32)],
        compiler_params=_params(),
        name="dft_a",
    )(z.reshape(N1, N2, 1, 2 * F_W), bf(jnp.asarray(_stage_a_dft())))

    ec, es = _stage_c_dft()
    y = pl.pallas_call(
        _dft_c_kernel,
        grid=(N1 // TC,),
        in_specs=[pl.BlockSpec((TC, N2, None, 2 * F_W), lambda i: (i, 0, 0, 0)),
                  pl.BlockSpec((TC, N2P, N2P), lambda i: (i, 0, 0)),
                  pl.BlockSpec((TC, N2P, N2P), lambda i: (i, 0, 0))],
        out_specs=pl.BlockSpec((N2, TC, 1, F_W), lambda i: (0, i, 0, 0)),
        out_shape=jax.ShapeDtypeStruct((N2, N1, 1, F_W), F32),
        scratch_shapes=[pltpu.VMEM((TC, N2P, 2 * F_W), F32)],
        compiler_params=_params(),
        name="dft_c",
    )(g, bf(jnp.asarray(ec)), bf(jnp.asarray(es)))

    out = pl.pallas_call(
        _post_kernel,
        grid=(N_CAST + N_TILES,),
        in_specs=[row_tile(D_MODEL), row_tile(ATTN_W), flat_rows(F_W), _resident((1, ATTN_W)),
                  _resident((1, F_W)), cast_rows(D_MODEL, D_MODEL), _resident((1, D_MODEL)),
                  *ffn_in, _resident((1, D_MODEL))],
        out_specs=row_tile(D_MODEL),
        out_shape=jax.ShapeDtypeStruct((SEQ, D_MODEL), F32),
        scratch_shapes=ffn_scratch + [bf16_copy(D_MODEL, D_MODEL)],
        compiler_params=_params(),
        name="post",
    )(h1, a_out, y.reshape(L, 1, F_W), _row(attn_out_norm), _row(fourier_out_norm), f32(w_out), _row(ffn2_norm),
      f32(ffn2_w_gate), f32(ffn2_w_up), f32(ffn2_w_down), _row(final_norm))
    return out.reshape(1, SEQ, D_MODEL)
```

```python
import numpy as np
import jax
import jax.numpy as jnp
from jax import lax
from jax.experimental import pallas as pl
from jax.experimental.pallas import tpu as pltpu

F32 = jnp.float32
BF16 = jnp.bfloat16

D_MODEL = 1024
SEQ = 16384
N_META = 16
L = SEQ + N_META
HEAD_DIM = 64
N_HEADS = 8
N_KV = 2
Q_PER_KV = N_HEADS // N_KV
ATTN_W = N_HEADS * HEAD_DIM
KV_W = N_KV * HEAD_DIM
F_W = 512
F_GROUP = 64
IN_W = ATTN_W + 2 * KV_W + F_W
QK_W = ATTN_W + KV_W
WINDOW = 128
D_FF = 2816
N_CAST = 8
EPS = 1e-6
NEG = -1e30
LOG2E = 1.4426950408889634

TM = 400
N_TILES = L // TM
META_ROW0 = SEQ - (N_TILES - 1) * TM

BQ = 128
N_QBLK = SEQ // BQ
QBLK_PER_STEP = 3
BAND = BQ + 2 * WINDOW
N_KEYS = BAND + N_META + 8
SINK_COL = BAND + N_META

N1 = 400
N2 = 41
N2P = 48
TC = 25

VMEM_LIMIT = 56 * 1024 * 1024


def _ones_blockdiag():
    m = np.kron(np.eye(4), np.ones((HEAD_DIM, HEAD_DIM)))
    return jnp.asarray(m, BF16)


def _channel_dft():
    c = np.arange(F_GROUP)
    ang = 2.0 * np.pi * np.outer(c, c) / F_GROUP
    cos = np.kron(np.eye(4), np.cos(ang) / 8.0)
    sin = np.kron(np.eye(4), np.sin(ang) / 8.0)
    return np.concatenate([cos, sin], axis=1).astype(np.float32)


def _stage_a_dft():
    a = np.arange(N1, dtype=np.int64)
    m = (N2 * np.outer(a + N_META, a)) % L
    ang = 2.0 * np.pi * m / L
    return np.concatenate([np.cos(ang), -np.sin(ang)], axis=0).astype(np.float32)


def _stage_c_dft():
    c = np.arange(N1, dtype=np.int64)[:, None, None]
    d = np.arange(N2, dtype=np.int64)[None, :, None]
    b = np.arange(N2, dtype=np.int64)[None, None, :]
    m = ((c + N1 * d + N_META) * (b + N_META)) % L
    ang = 2.0 * np.pi * m / L
    ec = np.zeros((N1, N2P, N2P), np.float32)
    es = np.zeros((N1, N2P, N2P), np.float32)
    ec[:, :N2, :N2] = np.cos(ang) / np.sqrt(L)
    es[:, :N2, :N2] = np.sin(ang) / np.sqrt(L)
    return ec, es


def _attn_bias_const():
    slopes = 2.0 ** -(np.arange(N_HEADS) + 1.0)
    i = np.arange(BQ)[:, None]
    j = np.arange(BAND)[None, :]
    m = np.arange(N_META)[None, :]
    out = np.full((4, N_KV, Q_PER_KV * BQ, SINK_COL), NEG, np.float64)
    for t in range(4):
        if t < 3:
            dist = np.abs(t * WINDOW + i - j)
            band_ok = dist <= WINDOW
            dist_m = np.minimum(t * WINDOW + N_META + i - m, WINDOW) if t == 0 else np.full((BQ, N_META), WINDOW)
        else:
            dist = N_META + j - i
            band_ok = (dist <= WINDOW) & (i < N_META)
            dist_m = np.abs(i - m)
        for h in range(N_HEADS):
            g, hl = divmod(h, Q_PER_KV)
            rows = slice(hl * BQ, (hl + 1) * BQ)
            band = np.where(band_ok, -slopes[h] * dist, NEG)
            if t == 3:
                band = np.where(i < N_META, band, 0.0)
            out[t, g, rows, :BAND] = band
            out[t, g, rows, BAND:BAND + N_META] = -slopes[h] * dist_m
    return np.ascontiguousarray(LOG2E * out.transpose(0, 1, 3, 2)).astype(np.float32)


def _rms(x, g):
    return x * lax.rsqrt(jnp.mean(x * x, axis=-1, keepdims=True) + EPS) * g


def _swiglu_half(x, g_ref, wg_s, wu_s, wd_s):
    xn = _rms(x, g_ref[...]).astype(BF16)
    gate = jnp.dot(xn, wg_s[...], preferred_element_type=F32)
    up = jnp.dot(xn, wu_s[...], preferred_element_type=F32)
    act = (gate * jax.nn.sigmoid(gate) * up).astype(BF16)
    return x + 0.5 * jnp.dot(act, wd_s[...], preferred_element_type=F32)


def _cast_weights(s, f32_refs, bf16_scratch):
    for w_ref, w_s in zip(f32_refs, bf16_scratch):
        rows = w_ref.shape[0]
        w_s[pl.ds(pl.multiple_of(s * rows, rows), rows), :] = w_ref[...].astype(BF16)


def _pre_kernel(x_ref, meta_ref, g1_ref, wg_ref, wu_ref, wd_ref, gm_ref, win_ref, gqk_ref,
                ones_ref, h1_ref, q_ref, kv_ref, f_ref, wg_s, wu_s, wd_s, win_s):
    s = pl.program_id(0)

    @pl.when(s < N_CAST)
    def _():
        _cast_weights(s, (wg_ref, wu_ref, wd_ref, win_ref), (wg_s, wu_s, wd_s, win_s))

    @pl.when(s >= N_CAST)
    def _():
        _pre_tile(s - N_CAST, x_ref, meta_ref, g1_ref, gm_ref, gqk_ref, ones_ref,
                  h1_ref, q_ref, kv_ref, f_ref, wg_s, wu_s, wd_s, win_s)


def _pre_tile(i, x_ref, meta_ref, g1_ref, gm_ref, gqk_ref, ones_ref,
              h1_ref, q_ref, kv_ref, f_ref, wg_s, wu_s, wd_s, win_s):
    row = lax.broadcasted_iota(jnp.int32, (TM, 1), 0)
    is_meta = jnp.logical_and(i == N_TILES - 1, row >= META_ROW0)
    x = jnp.where(is_meta, meta_ref[...], x_ref[...])

    h1 = _swiglu_half(x, g1_ref, wg_s, wu_s, wd_s)
    h1_ref[...] = h1

    u = jnp.dot(_rms(h1, gm_ref[...]).astype(BF16), win_s[...], preferred_element_type=F32)

    qk = u[:, :QK_W]
    sq = (qk * qk).astype(BF16)
    ones = ones_ref[...]
    ss = jnp.concatenate([
        jnp.dot(sq[:, 0:256], ones, preferred_element_type=F32),
        jnp.dot(sq[:, 256:512], ones, preferred_element_type=F32),
        jnp.dot(sq[:, 512:640], ones[:KV_W, :KV_W], preferred_element_type=F32)], axis=1)
    qkn = qk * lax.rsqrt(ss * (1.0 / HEAD_DIM) + EPS) * gqk_ref[...]
    q_ref[...] = qkn[:, :ATTN_W].astype(BF16)
    kv_ref[...] = jnp.concatenate([qkn[:, ATTN_W:], u[:, QK_W:QK_W + KV_W]], axis=1).astype(BF16)

    f_ref[...] = u[:, QK_W + KV_W:]


def _attn_kernel(q_ref, kv_ref, bias_ref, tail_ref, o_ref):
    step = pl.program_id(0)
    kv_meta = kv_ref[SEQ:L, :]
    zero_keys = jnp.zeros((8, 2 * KV_W), BF16)
    for j in range(QBLK_PER_STEP):
        n = step * QBLK_PER_STEP + j
        btype = jnp.where(n == 0, 0, jnp.where(n == N_QBLK - 1, 2, jnp.where(n == N_QBLK, 3, 1)))
        start = jnp.where(n == N_QBLK, 0, jnp.clip((n - 1) * BQ, 0, SEQ - BAND))
        start = pl.multiple_of(start, BQ)
        kv_cat = jnp.concatenate([kv_ref[pl.ds(start, BAND), :], kv_meta, zero_keys], axis=0)
        q = q_ref[j * BQ:(j + 1) * BQ, :]
        for g in range(N_KV):
            k = kv_cat[:, g * HEAD_DIM:(g + 1) * HEAD_DIM]
            v = kv_cat[:, KV_W + g * HEAD_DIM:KV_W + (g + 1) * HEAD_DIM]
            qg = jnp.concatenate(
                [q[:, (Q_PER_KV * g + hl) * HEAD_DIM:(Q_PER_KV * g + hl + 1) * HEAD_DIM]
                 for hl in range(Q_PER_KV)], axis=0)
            st = lax.dot_general(k, qg, (((1,), (1,)), ((), ())), preferred_element_type=F32)
            st = st + jnp.concatenate([bias_ref[btype, g], tail_ref[g]], axis=0)
            m = jnp.max(st, axis=0, keepdims=True)
            p = jnp.exp2(st - m)
            denom = jnp.sum(p, axis=0, keepdims=True)
            ot = lax.dot_general(v, p.astype(BF16), (((0,), (0,)), ((), ())), preferred_element_type=F32)
            ot = ot / denom
            for hp in range(Q_PER_KV // 2):
                h0 = Q_PER_KV * g + 2 * hp
                two = jnp.concatenate([ot[:, 2 * hp * BQ:(2 * hp + 1) * BQ],
                                       ot[:, (2 * hp + 1) * BQ:(2 * hp + 2) * BQ]], axis=0)
                o_ref[j * BQ:(j + 1) * BQ, h0 * HEAD_DIM:(h0 + 2) * HEAD_DIM] = two.T.astype(BF16)


def _dft_a_kernel(f_ref, wa_ref, cdft_ref, g_ref, fs_ref, gs_ref):
    fs_ref[...] = f_ref[...]
    h = jnp.dot(wa_ref[...], fs_ref[...].astype(BF16), preferred_element_type=F32).astype(BF16)
    cdft = cdft_ref[...]
    for half in range(2):
        p = jnp.dot(h[:, half * 256:(half + 1) * 256], cdft, preferred_element_type=F32)
        gs_ref[:, half * 256:(half + 1) * 256] = p[:N1, :256] + p[N1:, 256:]
        gs_ref[:, F_W + half * 256:F_W + (half + 1) * 256] = p[N1:, :256] - p[:N1, 256:]
    g_ref[...] = gs_ref[...]


def _dft_c_kernel(g_ref, ec_ref, es_ref, y_ref, gs_ref):
    gs_ref[:, N2:, :] = jnp.zeros((TC, N2P - N2, 2 * F_W), F32)
    for c in range(TC):
        gs_ref[c, :N2, :] = g_ref[c]
        g = gs_ref[c].astype(BF16)
        y = (jnp.dot(ec_ref[c], g[:, :F_W], preferred_element_type=F32)
             + jnp.dot(es_ref[c], g[:, F_W:], preferred_element_type=F32))
        y_ref[:, c, 0, :] = y[:N2]


def _post_kernel(h1_ref, a_ref, y_ref, ga_ref, gf_ref, wout_ref, g2_ref, wg_ref, wu_ref, wd_ref,
                 gfin_ref, o_ref, wg_s, wu_s, wd_s, wout_s):
    s = pl.program_id(0)

    @pl.when(s < N_CAST)
    def _():
        _cast_weights(s, (wg_ref, wu_ref, wd_ref, wout_ref), (wg_s, wu_s, wd_s, wout_s))

    @pl.when(s >= N_CAST)
    def _():
        mixed = jnp.concatenate([_rms(a_ref[...].astype(F32), ga_ref[...]), _rms(y_ref[...], gf_ref[...])], axis=1)
        h2 = h1_ref[...] + jnp.dot(mixed.astype(BF16), wout_s[...], preferred_element_type=F32)
        h3 = _swiglu_half(h2, g2_ref, wg_s, wu_s, wd_s)
        o_ref[...] = _rms(h3, gfin_ref[...])


def _resident(shape):
    zeros = (0,) * len(shape)
    return pl.BlockSpec(shape, lambda *_: zeros, pipeline_mode=pl.Buffered(1))


def _params():
    return pltpu.CompilerParams(dimension_semantics=("arbitrary",), vmem_limit_bytes=VMEM_LIMIT)


def _row(g):
    return g.astype(F32).reshape(1, -1)


def kernel(x, meta_tokens, ffn1_norm, ffn1_w_gate, ffn1_w_up, ffn1_w_down, mix_norm, w_in, q_norm, k_norm, sink, attn_out_norm, fourier_out_norm, w_out, ffn2_norm, ffn2_w_gate, ffn2_w_up, ffn2_w_down, final_norm):
    assert x.shape == (1, SEQ, D_MODEL) and x.dtype == F32
    x2 = x.reshape(SEQ, D_MODEL)
    meta_pad = jnp.pad(meta_tokens.astype(F32), ((META_ROW0, 0), (0, 0)))
    gqk = jnp.concatenate([jnp.tile(q_norm.astype(F32), N_HEADS) * (HEAD_DIM ** -0.5 * LOG2E),
                           jnp.tile(k_norm.astype(F32), N_KV)]).reshape(1, QK_W)
    bf = lambda w: w.astype(BF16)

    tile_idx = lambda s: jnp.maximum(s - N_CAST, 0)
    row_tile = lambda w: pl.BlockSpec((TM, w), lambda s: (tile_idx(s), 0))
    flat_rows = lambda w: pl.BlockSpec((TM, None, w), lambda s: (tile_idx(s), 0, 0))
    cast_rows = lambda r, c: pl.BlockSpec((r // N_CAST, c), lambda s: (jnp.minimum(s, N_CAST - 1), 0))
    bf16_copy = lambda r, c: pltpu.VMEM((r, c), BF16)
    ffn_in = [cast_rows(D_MODEL, D_FF), cast_rows(D_MODEL, D_FF), cast_rows(D_FF, D_MODEL)]
    ffn_scratch = [bf16_copy(D_MODEL, D_FF), bf16_copy(D_MODEL, D_FF), bf16_copy(D_FF, D_MODEL)]
    f32 = lambda w: w.astype(F32)

    h1, q, kv, f = pl.pallas_call(
        _pre_kernel,
        grid=(N_CAST + N_TILES,),
        in_specs=[row_tile(D_MODEL), _resident((TM, D_MODEL)), _resident((1, D_MODEL)), *ffn_in,
                  _resident((1, D_MODEL)), cast_rows(D_MODEL, IN_W), _resident((1, QK_W)),
                  _resident((256, 256))],
        out_specs=[row_tile(D_MODEL), row_tile(ATTN_W), row_tile(2 * KV_W), flat_rows(F_W)],
        out_shape=[jax.ShapeDtypeStruct((L, D_MODEL), F32), jax.ShapeDtypeStruct((L, ATTN_W), BF16),
                   jax.ShapeDtypeStruct((L, 2 * KV_W), BF16), jax.ShapeDtypeStruct((L, 1, F_W), F32)],
        scratch_shapes=ffn_scratch + [bf16_copy(D_MODEL, IN_W)],
        compiler_params=_params(),
        name="pre",
    )(x2, meta_pad, _row(ffn1_norm), f32(ffn1_w_gate), f32(ffn1_w_up), f32(ffn1_w_down), _row(mix_norm),
      f32(w_in), gqk, _ones_blockdiag())

    sink_rows = jnp.repeat(sink.astype(F32).reshape(N_KV, Q_PER_KV), BQ, axis=1) * LOG2E
    bias_tail = jnp.concatenate([sink_rows[:, None, :],
                                 jnp.full((N_KV, N_KEYS - SINK_COL - 1, Q_PER_KV * BQ), NEG, F32)], axis=1)

    q_rows = pl.BlockSpec((QBLK_PER_STEP * BQ, ATTN_W), lambda s: (s, 0))
    a_out = pl.pallas_call(
        _attn_kernel,
        grid=((N_QBLK + 1) // QBLK_PER_STEP,),
        in_specs=[q_rows, _resident((L, 2 * KV_W)), _resident((4, N_KV, SINK_COL, Q_PER_KV * BQ)),
                  _resident((N_KV, N_KEYS - SINK_COL, Q_PER_KV * BQ))],
        out_specs=q_rows,
        out_shape=jax.ShapeDtypeStruct((L, ATTN_W), BF16),
        compiler_params=_params(),
        name="attn",
    )(q, kv, jnp.asarray(_attn_bias_const()), bias_tail)

    row_set = lambda w: pl.BlockSpec((N1, None, None, w), lambda b: (0, b, 0, 0))
    g = pl.pallas_call(
        _dft_a_kernel,
        grid=(N2,),
        in_specs=[row_set(F_W), _resident((2 * N1, N1)), _resident((256, 512))],
        out_specs=row_set(2 * F_W),
        out_shape=jax.ShapeDtypeStruct((N1, N2, 1, 2 * F_W), F32),
        scratch_shapes=[pltpu.VMEM((N1, F_W), F32), pltpu.VMEM((N1, 2 * F_W), F32)],
        compiler_params=_params(),
        name="dft_a",
    )(f.reshape(N1, N2, 1, F_W), bf(jnp.asarray(_stage_a_dft())), bf(jnp.asarray(_channel_dft())))

    ec, es = _stage_c_dft()
    y = pl.pallas_call(
        _dft_c_kernel,
        grid=(N1 // TC,),
        in_specs=[pl.BlockSpec((TC, N2, None, 2 * F_W), lambda i: (i, 0, 0, 0)),
                  pl.BlockSpec((TC, N2P, N2P), lambda i: (i, 0, 0)),
                  pl.BlockSpec((TC, N2P, N2P), lambda i: (i, 0, 0))],
        out_specs=pl.BlockSpec((N2, TC, 1, F_W), lambda i: (0, i, 0, 0)),
        out_shape=jax.ShapeDtypeStruct((N2, N1, 1, F_W), F32),
        scratch_shapes=[pltpu.VMEM((TC, N2P, 2 * F_W), F32)],
        compiler_params=_params(),
        name="dft_c",
    )(g, bf(jnp.asarray(ec)), bf(jnp.asarray(es)))

    out = pl.pallas_call(
        _post_kernel,
        grid=(N_CAST + N_TILES,),
        in_specs=[row_tile(D_MODEL), row_tile(ATTN_W), flat_rows(F_W), _resident((1, ATTN_W)),
                  _resident((1, F_W)), cast_rows(D_MODEL, D_MODEL), _resident((1, D_MODEL)),
                  *ffn_in, _resident((1, D_MODEL))],
        out_specs=row_tile(D_MODEL),
        out_shape=jax.ShapeDtypeStruct((SEQ, D_MODEL), F32),
        scratch_shapes=ffn_scratch + [bf16_copy(D_MODEL, D_MODEL)],
        compiler_params=_params(),
        name="post",
    )(h1, a_out, y.reshape(L, 1, F_W), _row(attn_out_norm), _row(fourier_out_norm), f32(w_out), _row(ffn2_norm),
      f32(ffn2_w_gate), f32(ffn2_w_up), f32(ffn2_w_down), _row(final_norm))
    return out.reshape(1, SEQ, D_MODEL)
```

```python
import numpy as np
import jax
import jax.numpy as jnp
from jax import lax
from jax.experimental import pallas as pl
from jax.experimental.pallas import tpu as pltpu

F32 = jnp.float32
BF16 = jnp.bfloat16

D_MODEL = 1024
SEQ = 16384
N_META = 16
L = SEQ + N_META
HEAD_DIM = 64
N_HEADS = 8
N_KV = 2
Q_PER_KV = N_HEADS // N_KV
ATTN_W = N_HEADS * HEAD_DIM
KV_W = N_KV * HEAD_DIM
F_W = 512
F_GROUP = 64
IN_W = ATTN_W + 2 * KV_W + F_W
QK_W = ATTN_W + KV_W
WINDOW = 128
D_FF = 2816
N_CAST = 8
EPS = 1e-6
NEG = -1e30
LOG2E = 1.4426950408889634

TM = 400
N_TILES = L // TM
META_ROW0 = SEQ - (N_TILES - 1) * TM

BQ = 128
N_QBLK = SEQ // BQ
QBLK_PER_STEP = 6
BAND = BQ + 2 * WINDOW
N_KEYS = BAND + N_META + 8
SINK_COL = BAND + N_META

N1 = 400
N2 = 41
N2P = 48
B_PER_STEP = 2
TC = 25

VMEM_LIMIT = 56 * 1024 * 1024


def _ones_blockdiag():
    m = np.kron(np.eye(4), np.ones((HEAD_DIM, HEAD_DIM)))
    return jnp.asarray(m, BF16)


def _channel_dft():
    c = np.arange(F_GROUP)
    ang = 2.0 * np.pi * np.outer(c, c) / F_GROUP
    cos = np.kron(np.eye(4), np.cos(ang) / 8.0)
    sin = np.kron(np.eye(4), np.sin(ang) / 8.0)
    return np.concatenate([cos, sin], axis=1).astype(np.float32)


def _stage_a_dft():
    a = np.arange(N1, dtype=np.int64)
    m = (N2 * np.outer(a + N_META, a)) % L
    ang = 2.0 * np.pi * m / L
    return np.concatenate([np.cos(ang), -np.sin(ang)], axis=0).astype(np.float32)


def _stage_c_dft():
    c = np.arange(N1, dtype=np.int64)[:, None, None]
    d = np.arange(N2, dtype=np.int64)[None, :, None]
    b = np.arange(N2, dtype=np.int64)[None, None, :]
    m = ((c + N1 * d + N_META) * (b + N_META)) % L
    ang = 2.0 * np.pi * m / L
    ec = np.zeros((N1, N2P, N2P), np.float32)
    es = np.zeros((N1, N2P, N2P), np.float32)
    ec[:, :N2, :N2] = np.cos(ang) / np.sqrt(L)
    es[:, :N2, :N2] = np.sin(ang) / np.sqrt(L)
    return ec, es


def _attn_bias_const():
    slopes = 2.0 ** -(np.arange(N_HEADS) + 1.0)
    i = np.arange(BQ)[:, None]
    j = np.arange(BAND)[None, :]
    m = np.arange(N_META)[None, :]
    out = np.full((4, N_KV, Q_PER_KV * BQ, SINK_COL), NEG, np.float64)
    for t in range(4):
        if t < 3:
            dist = np.abs(t * WINDOW + i - j)
            band_ok = dist <= WINDOW
            dist_m = np.minimum(t * WINDOW + N_META + i - m, WINDOW) if t == 0 else np.full((BQ, N_META), WINDOW)
        else:
            dist = N_META + j - i
            band_ok = (dist <= WINDOW) & (i < N_META)
            dist_m = np.abs(i - m)
        for h in range(N_HEADS):
            g, hl = divmod(h, Q_PER_KV)
            rows = slice(hl * BQ, (hl + 1) * BQ)
            band = np.where(band_ok, -slopes[h] * dist, NEG)
            if t == 3:
                band = np.where(i < N_META, band, 0.0)
            out[t, g, rows, :BAND] = band
            out[t, g, rows, BAND:BAND + N_META] = -slopes[h] * dist_m
    return np.ascontiguousarray(LOG2E * out.transpose(0, 1, 3, 2)).astype(np.float32)


def _rms(x, g):
    return x * lax.rsqrt(jnp.mean(x * x, axis=-1, keepdims=True) + EPS) * g


def _swiglu_half(x, g_ref, wg_s, wu_s, wd_s):
    xn = _rms(x, g_ref[...]).astype(BF16)
    gate = jnp.dot(xn, wg_s[...], preferred_element_type=F32)
    up = jnp.dot(xn, wu_s[...], preferred_element_type=F32)
    act = (gate * jax.nn.sigmoid(gate) * up).astype(BF16)
    return x + 0.5 * jnp.dot(act, wd_s[...], preferred_element_type=F32)


def _cast_weights(s, f32_refs, bf16_scratch):
    for w_ref, w_s in zip(f32_refs, bf16_scratch):
        rows = w_ref.shape[0]
        w_s[pl.ds(pl.multiple_of(s * rows, rows), rows), :] = w_ref[...].astype(BF16)


def _pre_kernel(x_ref, meta_ref, g1_ref, wg_ref, wu_ref, wd_ref, gm_ref, win_ref, gqk_ref,
                ones_ref, h1_ref, q_ref, kv_ref, f_ref, wg_s, wu_s, wd_s, win_s):
    s = pl.program_id(0)

    @pl.when(s < N_CAST)
    def _():
        _cast_weights(s, (wg_ref, wu_ref, wd_ref, win_ref), (wg_s, wu_s, wd_s, win_s))

    @pl.when(s >= N_CAST)
    def _():
        _pre_tile(s - N_CAST, x_ref, meta_ref, g1_ref, gm_ref, gqk_ref, ones_ref,
                  h1_ref, q_ref, kv_ref, f_ref, wg_s, wu_s, wd_s, win_s)


def _pre_tile(i, x_ref, meta_ref, g1_ref, gm_ref, gqk_ref, ones_ref,
              h1_ref, q_ref, kv_ref, f_ref, wg_s, wu_s, wd_s, win_s):
    row = lax.broadcasted_iota(jnp.int32, (TM, 1), 0)
    is_meta = jnp.logical_and(i == N_TILES - 1, row >= META_ROW0)
    x = jnp.where(is_meta, meta_ref[...], x_ref[...])

    h1 = _swiglu_half(x, g1_ref, wg_s, wu_s, wd_s)
    h1_ref[...] = h1

    u = jnp.dot(_rms(h1, gm_ref[...]).astype(BF16), win_s[...], preferred_element_type=F32)

    qk = u[:, :QK_W]
    sq = (qk * qk).astype(BF16)
    ones = ones_ref[...]
    ss = jnp.concatenate([
        jnp.dot(sq[:, 0:256], ones, preferred_element_type=F32),
        jnp.dot(sq[:, 256:512], ones, preferred_element_type=F32),
        jnp.dot(sq[:, 512:640], ones[:KV_W, :KV_W], preferred_element_type=F32)], axis=1)
    qkn = qk * lax.rsqrt(ss * (1.0 / HEAD_DIM) + EPS) * gqk_ref[...]
    q_ref[...] = qkn[:, :ATTN_W].astype(BF16)
    kv_ref[...] = jnp.concatenate([qkn[:, ATTN_W:], u[:, QK_W:QK_W + KV_W]], axis=1).astype(BF16)

    f_ref[...] = u[:, QK_W + KV_W:]


def _attn_kernel(q_ref, kv_ref, bias_ref, tail_ref, o_ref):
    step = pl.program_id(0)
    kv_meta = kv_ref[SEQ:L, :]
    zero_keys = jnp.zeros((8, 2 * KV_W), BF16)
    for j in range(QBLK_PER_STEP):
        n = step * QBLK_PER_STEP + j
        btype = jnp.where(n == 0, 0, jnp.where(n == N_QBLK - 1, 2, jnp.where(n == N_QBLK, 3, 1)))
        start = jnp.where(n == N_QBLK, 0, jnp.clip((n - 1) * BQ, 0, SEQ - BAND))
        start = pl.multiple_of(start, BQ)
        kv_cat = jnp.concatenate([kv_ref[pl.ds(start, BAND), :], kv_meta, zero_keys], axis=0)
        q = q_ref[j * BQ:(j + 1) * BQ, :]
        for g in range(N_KV):
            k = kv_cat[:, g * HEAD_DIM:(g + 1) * HEAD_DIM]
            v = kv_cat[:, KV_W + g * HEAD_DIM:KV_W + (g + 1) * HEAD_DIM]
            qg = jnp.concatenate(
                [q[:, (Q_PER_KV * g + hl) * HEAD_DIM:(Q_PER_KV * g + hl + 1) * HEAD_DIM]
                 for hl in range(Q_PER_KV)], axis=0)
            st = lax.dot_general(k, qg, (((1,), (1,)), ((), ())), preferred_element_type=F32)
            st = st + jnp.concatenate([bias_ref[btype, g], tail_ref[g]], axis=0)
            m = jnp.max(st, axis=0, keepdims=True)
            p = jnp.exp2(st - m)
            denom = jnp.sum(p, axis=0, keepdims=True)
            ot = lax.dot_general(v, p.astype(BF16), (((0,), (0,)), ((), ())), preferred_element_type=F32)
            ot = ot / denom
            for hp in range(Q_PER_KV // 2):
                h0 = Q_PER_KV * g + 2 * hp
                two = jnp.concatenate([ot[:, 2 * hp * BQ:(2 * hp + 1) * BQ],
                                       ot[:, (2 * hp + 1) * BQ:(2 * hp + 2) * BQ]], axis=0)
                o_ref[j * BQ:(j + 1) * BQ, h0 * HEAD_DIM:(h0 + 2) * HEAD_DIM] = two.T.astype(BF16)


def _dft_a_kernel(f_ref, wa_ref, cdft_ref, g_ref, fs_ref, gs_ref):
    cdft = cdft_ref[...]
    for k in range(B_PER_STEP):
        fs_ref[...] = f_ref[:, k, 0, :]
        h = jnp.dot(wa_ref[...], fs_ref[...].astype(BF16), preferred_element_type=F32).astype(BF16)
        for half in range(2):
            p = jnp.dot(h[:, half * 256:(half + 1) * 256], cdft, preferred_element_type=F32)
            gs_ref[:, half * 256:(half + 1) * 256] = p[:N1, :256] + p[N1:, 256:]
            gs_ref[:, F_W + half * 256:F_W + (half + 1) * 256] = p[N1:, :256] - p[:N1, 256:]
        g_ref[:, k, 0, :] = gs_ref[...]


def _dft_c_kernel(g_ref, ec_ref, es_ref, y_ref, gs_ref):
    gs_ref[:, N2:, :] = jnp.zeros((TC, N2P - N2, 2 * F_W), F32)
    for c in range(TC):
        gs_ref[c, :N2, :] = g_ref[c]
        g = gs_ref[c].astype(BF16)
        y = (jnp.dot(ec_ref[c], g[:, :F_W], preferred_element_type=F32)
             + jnp.dot(es_ref[c], g[:, F_W:], preferred_element_type=F32))
        y_ref[:, c, 0, :] = y[:N2]


def _post_kernel(h1_ref, a_ref, y_ref, ga_ref, gf_ref, wout_ref, g2_ref, wg_ref, wu_ref, wd_ref,
                 gfin_ref, o_ref, wg_s, wu_s, wd_s, wout_s):
    s = pl.program_id(0)

    @pl.when(s < N_CAST)
    def _():
        _cast_weights(s, (wg_ref, wu_ref, wd_ref, wout_ref), (wg_s, wu_s, wd_s, wout_s))

    @pl.when(s >= N_CAST)
    def _():
        mixed = jnp.concatenate([_rms(a_ref[...].astype(F32), ga_ref[...]), _rms(y_ref[...], gf_ref[...])], axis=1)
        h2 = h1_ref[...] + jnp.dot(mixed.astype(BF16), wout_s[...], preferred_element_type=F32)
        h3 = _swiglu_half(h2, g2_ref, wg_s, wu_s, wd_s)
        o_ref[...] = _rms(h3, gfin_ref[...])


def _resident(shape):
    zeros = (0,) * len(shape)
    return pl.BlockSpec(shape, lambda *_: zeros, pipeline_mode=pl.Buffered(1))


def _params():
    return pltpu.CompilerParams(dimension_semantics=("arbitrary",), vmem_limit_bytes=VMEM_LIMIT)


def _row(g):
    return g.astype(F32).reshape(1, -1)


def kernel(x, meta_tokens, ffn1_norm, ffn1_w_gate, ffn1_w_up, ffn1_w_down, mix_norm, w_in, q_norm, k_norm, sink, attn_out_norm, fourier_out_norm, w_out, ffn2_norm, ffn2_w_gate, ffn2_w_up, ffn2_w_down, final_norm):
    assert x.shape == (1, SEQ, D_MODEL) and x.dtype == F32
    x2 = x.reshape(SEQ, D_MODEL)
    meta_pad = jnp.pad(meta_tokens.astype(F32), ((META_ROW0, 0), (0, 0)))
    gqk = jnp.concatenate([jnp.tile(q_norm.astype(F32), N_HEADS) * (HEAD_DIM ** -0.5 * LOG2E),
                           jnp.tile(k_norm.astype(F32), N_KV)]).reshape(1, QK_W)
    bf = lambda w: w.astype(BF16)

    tile_idx = lambda s: jnp.maximum(s - N_CAST, 0)
    row_tile = lambda w: pl.BlockSpec((TM, w), lambda s: (tile_idx(s), 0))
    flat_rows = lambda w: pl.BlockSpec((TM, None, w), lambda s: (tile_idx(s), 0, 0))
    cast_rows = lambda r, c: pl.BlockSpec((r // N_CAST, c), lambda s: (jnp.minimum(s, N_CAST - 1), 0))
    bf16_copy = lambda r, c: pltpu.VMEM((r, c), BF16)
    ffn_in = [cast_rows(D_MODEL, D_FF), cast_rows(D_MODEL, D_FF), cast_rows(D_FF, D_MODEL)]
    ffn_scratch = [bf16_copy(D_MODEL, D_FF), bf16_copy(D_MODEL, D_FF), bf16_copy(D_FF, D_MODEL)]
    f32 = lambda w: w.astype(F32)

    h1, q, kv, f = pl.pallas_call(
        _pre_kernel,
        grid=(N_CAST + N_TILES,),
        in_specs=[row_tile(D_MODEL), _resident((TM, D_MODEL)), _resident((1, D_MODEL)), *ffn_in,
                  _resident((1, D_MODEL)), cast_rows(D_MODEL, IN_W), _resident((1, QK_W)),
                  _resident((256, 256))],
        out_specs=[row_tile(D_MODEL), row_tile(ATTN_W), row_tile(2 * KV_W), flat_rows(F_W)],
        out_shape=[jax.ShapeDtypeStruct((L, D_MODEL), F32), jax.ShapeDtypeStruct((L, ATTN_W), BF16),
                   jax.ShapeDtypeStruct((L, 2 * KV_W), BF16), jax.ShapeDtypeStruct((L, 1, F_W), F32)],
        scratch_shapes=ffn_scratch + [bf16_copy(D_MODEL, IN_W)],
        compiler_params=_params(),
        name="pre",
    )(x2, meta_pad, _row(ffn1_norm), f32(ffn1_w_gate), f32(ffn1_w_up), f32(ffn1_w_down), _row(mix_norm),
      f32(w_in), gqk, _ones_blockdiag())

    sink_rows = jnp.repeat(sink.astype(F32).reshape(N_KV, Q_PER_KV), BQ, axis=1) * LOG2E
    bias_tail = jnp.concatenate([sink_rows[:, None, :],
                                 jnp.full((N_KV, N_KEYS - SINK_COL - 1, Q_PER_KV * BQ), NEG, F32)], axis=1)

    q_rows = pl.BlockSpec((QBLK_PER_STEP * BQ, ATTN_W), lambda s: (s, 0))
    a_out = pl.pallas_call(
        _attn_kernel,
        grid=(pl.cdiv(N_QBLK + 1, QBLK_PER_STEP),),
        in_specs=[q_rows, _resident((L, 2 * KV_W)), _resident((4, N_KV, SINK_COL, Q_PER_KV * BQ)),
                  _resident((N_KV, N_KEYS - SINK_COL, Q_PER_KV * BQ))],
        out_specs=q_rows,
        out_shape=jax.ShapeDtypeStruct((L, ATTN_W), BF16),
        compiler_params=_params(),
        name="attn",
    )(q, kv, jnp.asarray(_attn_bias_const()), bias_tail)

    row_set = lambda w: pl.BlockSpec((N1, B_PER_STEP, 1, w), lambda s: (0, s, 0, 0))
    g = pl.pallas_call(
        _dft_a_kernel,
        grid=(pl.cdiv(N2, B_PER_STEP),),
        in_specs=[row_set(F_W), _resident((2 * N1, N1)), _resident((256, 512))],
        out_specs=row_set(2 * F_W),
        out_shape=jax.ShapeDtypeStruct((N1, N2, 1, 2 * F_W), F32),
        scratch_shapes=[pltpu.VMEM((N1, F_W), F32), pltpu.VMEM((N1, 2 * F_W), F32)],
        compiler_params=_params(),
        name="dft_a",
    )(f.reshape(N1, N2, 1, F_W), bf(jnp.asarray(_stage_a_dft())), bf(jnp.asarray(_channel_dft())))

    ec, es = _stage_c_dft()
    y = pl.pallas_call(
        _dft_c_kernel,
        grid=(N1 // TC,),
        in_specs=[pl.BlockSpec((TC, N2, None, 2 * F_W), lambda i: (i, 0, 0, 0)),
                  pl.BlockSpec((TC, N2P, N2P), lambda i: (i, 0, 0)),
                  pl.BlockSpec((TC, N2P, N2P), lambda i: (i, 0, 0))],
        out_specs=pl.BlockSpec((N2, TC, 1, F_W), lambda i: (0, i, 0, 0)),
        out_shape=jax.ShapeDtypeStruct((N2, N1, 1, F_W), F32),
        scratch_shapes=[pltpu.VMEM((TC, N2P, 2 * F_W), F32)],
        compiler_params=_params(),
        name="dft_c",
    )(g, bf(jnp.asarray(ec)), bf(jnp.asarray(es)))

    out = pl.pallas_call(
        _post_kernel,
        grid=(N_CAST + N_TILES,),
        in_specs=[row_tile(D_MODEL), row_tile(ATTN_W), flat_rows(F_W), _resident((1, ATTN_W)),
                  _resident((1, F_W)), cast_rows(D_MODEL, D_MODEL), _resident((1, D_MODEL)),
                  *ffn_in, _resident((1, D_MODEL))],
        out_specs=row_tile(D_MODEL),
        out_shape=jax.ShapeDtypeStruct((SEQ, D_MODEL), F32),
        scratch_shapes=ffn_scratch + [bf16_copy(D_MODEL, D_MODEL)],
        compiler_params=_params(),
        name="post",
    )(h1, a_out, y.reshape(L, 1, F_W), _row(attn_out_norm), _row(fourier_out_norm), f32(w_out), _row(ffn2_norm),
      f32(ffn2_w_gate), f32(ffn2_w_up), f32(ffn2_w_down), _row(final_norm))
    return out.reshape(1, SEQ, D_MODEL)
```

```python
import numpy as np
import jax
import jax.numpy as jnp
from jax import lax
from jax.experimental import pallas as pl
from jax.experimental.pallas import tpu as pltpu

F32 = jnp.float32
BF16 = jnp.bfloat16

D_MODEL = 1024
SEQ = 16384
N_META = 16
L = SEQ + N_META
HEAD_DIM = 64
N_HEADS = 8
N_KV = 2
Q_PER_KV = N_HEADS // N_KV
ATTN_W = N_HEADS * HEAD_DIM
KV_W = N_KV * HEAD_DIM
F_W = 512
F_GROUP = 64
IN_W = ATTN_W + 2 * KV_W + F_W
QK_W = ATTN_W + KV_W
WINDOW = 128
D_FF = 2816
N_CAST = 8
EPS = 1e-6
NEG = -1e30
LOG2E = 1.4426950408889634

TM = 400
N_TILES = L // TM
META_ROW0 = SEQ - (N_TILES - 1) * TM

BQ = 128
N_QBLK = SEQ // BQ
QBLK_PER_STEP = 6
BAND = BQ + 2 * WINDOW
N_KEYS = BAND + N_META + 8
SINK_COL = BAND + N_META

N1 = 400
N2 = 41
N2P = 48
B_PER_STEP = 2
TC = 25

VMEM_LIMIT = 56 * 1024 * 1024


def _ones_blockdiag():
    m = np.kron(np.eye(4), np.ones((HEAD_DIM, HEAD_DIM)))
    return jnp.asarray(m, BF16)


def _channel_dft():
    c = np.arange(F_GROUP)
    ang = 2.0 * np.pi * np.outer(c, c) / F_GROUP
    cos = np.kron(np.eye(4), np.cos(ang) / 8.0)
    sin = np.kron(np.eye(4), np.sin(ang) / 8.0)
    return np.concatenate([cos, sin], axis=1).astype(np.float32)


def _stage_a_dft():
    a = np.arange(N1, dtype=np.int64)
    m = (N2 * np.outer(a + N_META, a)) % L
    ang = 2.0 * np.pi * m / L
    return np.concatenate([np.cos(ang), -np.sin(ang)], axis=0).astype(np.float32)


def _stage_c_dft():
    c = np.arange(N1, dtype=np.int64)[:, None, None]
    d = np.arange(N2, dtype=np.int64)[None, :, None]
    b = np.arange(N2, dtype=np.int64)[None, None, :]
    m = ((c + N1 * d + N_META) * (b + N_META)) % L
    ang = 2.0 * np.pi * m / L
    ec = np.zeros((N1, N2P, N2P), np.float32)
    es = np.zeros((N1, N2P, N2P), np.float32)
    ec[:, :N2, :N2] = np.cos(ang) / np.sqrt(L)
    es[:, :N2, :N2] = np.sin(ang) / np.sqrt(L)
    return ec, es


def _attn_bias_const():
    slopes = 2.0 ** -(np.arange(N_HEADS) + 1.0)
    i = np.arange(BQ)[:, None]
    j = np.arange(BAND)[None, :]
    m = np.arange(N_META)[None, :]
    out = np.full((4, N_KV, Q_PER_KV * BQ, SINK_COL), NEG, np.float64)
    for t in range(4):
        if t < 3:
            dist = np.abs(t * WINDOW + i - j)
            band_ok = dist <= WINDOW
            dist_m = np.minimum(t * WINDOW + N_META + i - m, WINDOW) if t == 0 else np.full((BQ, N_META), WINDOW)
        else:
            dist = N_META + j - i
            band_ok = (dist <= WINDOW) & (i < N_META)
            dist_m = np.abs(i - m)
        for h in range(N_HEADS):
            g, hl = divmod(h, Q_PER_KV)
            rows = slice(hl * BQ, (hl + 1) * BQ)
            band = np.where(band_ok, -slopes[h] * dist, NEG)
            if t == 3:
                band = np.where(i < N_META, band, 0.0)
            out[t, g, rows, :BAND] = band
            out[t, g, rows, BAND:BAND + N_META] = -slopes[h] * dist_m
    return np.ascontiguousarray(LOG2E * out.transpose(0, 1, 3, 2)).astype(np.float32)


def _rms(x, g):
    return x * lax.rsqrt(jnp.mean(x * x, axis=-1, keepdims=True) + EPS) * g


def _swiglu_half(x, g_ref, wg_s, wu_s, wd_s):
    xn = _rms(x, g_ref[...]).astype(BF16)
    gate = jnp.dot(xn, wg_s[...], preferred_element_type=F32)
    up = jnp.dot(xn, wu_s[...], preferred_element_type=F32)
    act = (gate * jax.nn.sigmoid(gate) * up).astype(BF16)
    return x + 0.5 * jnp.dot(act, wd_s[...], preferred_element_type=F32)


def _cast_weights(s, f32_refs, bf16_scratch):
    for w_ref, w_s in zip(f32_refs, bf16_scratch):
        rows = w_ref.shape[0]
        w_s[pl.ds(pl.multiple_of(s * rows, rows), rows), :] = w_ref[...].astype(BF16)


def _pre_kernel(x_ref, meta_ref, g1_ref, wg_ref, wu_ref, wd_ref, gm_ref, win_ref, gqk_ref,
                ones_ref, h1_ref, q_ref, kv_ref, f_ref, wg_s, wu_s, wd_s, win_s):
    s = pl.program_id(0)

    @pl.when(s < N_CAST)
    def _():
        _cast_weights(s, (wg_ref, wu_ref, wd_ref, win_ref), (wg_s, wu_s, wd_s, win_s))

    @pl.when(s >= N_CAST)
    def _():
        _pre_tile(s - N_CAST, x_ref, meta_ref, g1_ref, gm_ref, gqk_ref, ones_ref,
                  h1_ref, q_ref, kv_ref, f_ref, wg_s, wu_s, wd_s, win_s)


def _pre_tile(i, x_ref, meta_ref, g1_ref, gm_ref, gqk_ref, ones_ref,
              h1_ref, q_ref, kv_ref, f_ref, wg_s, wu_s, wd_s, win_s):
    row = lax.broadcasted_iota(jnp.int32, (TM, 1), 0)
    is_meta = jnp.logical_and(i == N_TILES - 1, row >= META_ROW0)
    x = jnp.where(is_meta, meta_ref[...], x_ref[...])

    h1 = _swiglu_half(x, g1_ref, wg_s, wu_s, wd_s)
    h1_ref[...] = h1

    u = jnp.dot(_rms(h1, gm_ref[...]).astype(BF16), win_s[...], preferred_element_type=F32)

    qk = u[:, :QK_W]
    sq = (qk * qk).astype(BF16)
    ones = ones_ref[...]
    ss = jnp.concatenate([
        jnp.dot(sq[:, 0:256], ones, preferred_element_type=F32),
        jnp.dot(sq[:, 256:512], ones, preferred_element_type=F32),
        jnp.dot(sq[:, 512:640], ones[:KV_W, :KV_W], preferred_element_type=F32)], axis=1)
    qkn = qk * lax.rsqrt(ss * (1.0 / HEAD_DIM) + EPS) * gqk_ref[...]
    q_ref[...] = qkn[:, :ATTN_W].astype(BF16)
    kv_ref[...] = jnp.concatenate([qkn[:, ATTN_W:], u[:, QK_W:QK_W + KV_W]], axis=1).astype(BF16)

    f_ref[...] = u[:, QK_W + KV_W:]


def _attn_kernel(q_ref, kv_ref, bias_ref, tail_ref, o_ref):
    step = pl.program_id(0)
    kv_meta = kv_ref[SEQ:L, :]
    zero_keys = jnp.zeros((8, 2 * KV_W), BF16)

    units = []
    for j in range(QBLK_PER_STEP):
        n = step * QBLK_PER_STEP + j
        btype = jnp.where(n == 0, 0, jnp.where(n == N_QBLK - 1, 2, jnp.where(n == N_QBLK, 3, 1)))
        start = jnp.where(n == N_QBLK, 0, jnp.clip((n - 1) * BQ, 0, SEQ - BAND))
        start = pl.multiple_of(start, BQ)
        kv_cat = jnp.concatenate([kv_ref[pl.ds(start, BAND), :], kv_meta, zero_keys], axis=0)
        q_ok = lax.broadcasted_iota(jnp.int32, (BQ, 1), 0) < L - n * BQ
        units.extend((j, g, btype, kv_cat, q_ok) for g in range(N_KV))

    def scores(unit):
        j, g, _, kv_cat, q_ok = unit
        q = jnp.where(q_ok, q_ref[j * BQ:(j + 1) * BQ, :], 0)
        qg = jnp.concatenate(
            [q[:, (Q_PER_KV * g + hl) * HEAD_DIM:(Q_PER_KV * g + hl + 1) * HEAD_DIM]
             for hl in range(Q_PER_KV)], axis=0)
        k = kv_cat[:, g * HEAD_DIM:(g + 1) * HEAD_DIM]
        return lax.dot_general(k, qg, (((1,), (1,)), ((), ())), preferred_element_type=F32)

    st_next = scores(units[0])
    for u, (j, g, btype, kv_cat, _) in enumerate(units):
        st = st_next + jnp.concatenate([bias_ref[btype, g], tail_ref[g]], axis=0)
        m = jnp.max(st, axis=0, keepdims=True)
        p = jnp.exp2(st - m)
        denom = jnp.sum(p, axis=0, keepdims=True)
        if u + 1 < len(units):
            st_next = scores(units[u + 1])
            p = jnp.concatenate([jnp.maximum(p[:16], st_next[:16] * 0.0 - 1.0), p[16:]], axis=0)
        v = kv_cat[:, KV_W + g * HEAD_DIM:KV_W + (g + 1) * HEAD_DIM]
        ot = lax.dot_general(v, p.astype(BF16), (((0,), (0,)), ((), ())), preferred_element_type=F32)
        ot = ot / denom
        for hp in range(Q_PER_KV // 2):
            h0 = Q_PER_KV * g + 2 * hp
            two = jnp.concatenate([ot[:, 2 * hp * BQ:(2 * hp + 1) * BQ],
                                   ot[:, (2 * hp + 1) * BQ:(2 * hp + 2) * BQ]], axis=0)
            o_ref[j * BQ:(j + 1) * BQ, h0 * HEAD_DIM:(h0 + 2) * HEAD_DIM] = two.T.astype(BF16)


def _dft_a_kernel(f_ref, wa_ref, cdft_ref, g_ref, fs_ref, gs_ref):
    cdft = cdft_ref[...]
    for k in range(B_PER_STEP):
        fs_ref[...] = f_ref[:, k, 0, :]
        h = jnp.dot(wa_ref[...], fs_ref[...].astype(BF16), preferred_element_type=F32).astype(BF16)
        for half in range(2):
            p = jnp.dot(h[:, half * 256:(half + 1) * 256], cdft, preferred_element_type=F32)
            gs_ref[:, half * 256:(half + 1) * 256] = p[:N1, :256] + p[N1:, 256:]
            gs_ref[:, F_W + half * 256:F_W + (half + 1) * 256] = p[N1:, :256] - p[:N1, 256:]
        g_ref[:, k, 0, :] = gs_ref[...]


def _dft_c_kernel(g_ref, ec_ref, es_ref, y_ref, gs_ref):
    gs_ref[:, N2:, :] = jnp.zeros((TC, N2P - N2, 2 * F_W), F32)
    for c in range(TC):
        gs_ref[c, :N2, :] = g_ref[c]
        g = gs_ref[c].astype(BF16)
        y = (jnp.dot(ec_ref[c], g[:, :F_W], preferred_element_type=F32)
             + jnp.dot(es_ref[c], g[:, F_W:], preferred_element_type=F32))
        y_ref[:, c, 0, :] = y[:N2]


def _post_kernel(h1_ref, a_ref, y_ref, ga_ref, gf_ref, wout_ref, g2_ref, wg_ref, wu_ref, wd_ref,
                 gfin_ref, o_ref, wg_s, wu_s, wd_s, wout_s):
    s = pl.program_id(0)

    @pl.when(s < N_CAST)
    def _():
        _cast_weights(s, (wg_ref, wu_ref, wd_ref, wout_ref), (wg_s, wu_s, wd_s, wout_s))

    @pl.when(s >= N_CAST)
    def _():
        mixed = jnp.concatenate([_rms(a_ref[...].astype(F32), ga_ref[...]), _rms(y_ref[...], gf_ref[...])], axis=1)
        h2 = h1_ref[...] + jnp.dot(mixed.astype(BF16), wout_s[...], preferred_element_type=F32)
        h3 = _swiglu_half(h2, g2_ref, wg_s, wu_s, wd_s)
        o_ref[...] = _rms(h3, gfin_ref[...])


def _resident(shape):
    zeros = (0,) * len(shape)
    return pl.BlockSpec(shape, lambda *_: zeros, pipeline_mode=pl.Buffered(1))


def _params():
    return pltpu.CompilerParams(dimension_semantics=("arbitrary",), vmem_limit_bytes=VMEM_LIMIT)


def _row(g):
    return g.astype(F32).reshape(1, -1)


def kernel(x, meta_tokens, ffn1_norm, ffn1_w_gate, ffn1_w_up, ffn1_w_down, mix_norm, w_in, q_norm, k_norm, sink, attn_out_norm, fourier_out_norm, w_out, ffn2_norm, ffn2_w_gate, ffn2_w_up, ffn2_w_down, final_norm):
    assert x.shape == (1, SEQ, D_MODEL) and x.dtype == F32
    x2 = x.reshape(SEQ, D_MODEL)
    meta_pad = jnp.pad(meta_tokens.astype(F32), ((META_ROW0, 0), (0, 0)))
    gqk = jnp.concatenate([jnp.tile(q_norm.astype(F32), N_HEADS) * (HEAD_DIM ** -0.5 * LOG2E),
                           jnp.tile(k_norm.astype(F32), N_KV)]).reshape(1, QK_W)
    bf = lambda w: w.astype(BF16)

    tile_idx = lambda s: jnp.maximum(s - N_CAST, 0)
    row_tile = lambda w: pl.BlockSpec((TM, w), lambda s: (tile_idx(s), 0))
    flat_rows = lambda w: pl.BlockSpec((TM, None, w), lambda s: (tile_idx(s), 0, 0))
    cast_rows = lambda r, c: pl.BlockSpec((r // N_CAST, c), lambda s: (jnp.minimum(s, N_CAST - 1), 0))
    bf16_copy = lambda r, c: pltpu.VMEM((r, c), BF16)
    ffn_in = [cast_rows(D_MODEL, D_FF), cast_rows(D_MODEL, D_FF), cast_rows(D_FF, D_MODEL)]
    ffn_scratch = [bf16_copy(D_MODEL, D_FF), bf16_copy(D_MODEL, D_FF), bf16_copy(D_FF, D_MODEL)]
    f32 = lambda w: w.astype(F32)

    h1, q, kv, f = pl.pallas_call(
        _pre_kernel,
        grid=(N_CAST + N_TILES,),
        in_specs=[row_tile(D_MODEL), _resident((TM, D_MODEL)), _resident((1, D_MODEL)), *ffn_in,
                  _resident((1, D_MODEL)), cast_rows(D_MODEL, IN_W), _resident((1, QK_W)),
                  _resident((256, 256))],
        out_specs=[row_tile(D_MODEL), row_tile(ATTN_W), row_tile(2 * KV_W), flat_rows(F_W)],
        out_shape=[jax.ShapeDtypeStruct((L, D_MODEL), F32), jax.ShapeDtypeStruct((L, ATTN_W), BF16),
                   jax.ShapeDtypeStruct((L, 2 * KV_W), BF16), jax.ShapeDtypeStruct((L, 1, F_W), F32)],
        scratch_shapes=ffn_scratch + [bf16_copy(D_MODEL, IN_W)],
        compiler_params=_params(),
        name="pre",
    )(x2, meta_pad, _row(ffn1_norm), f32(ffn1_w_gate), f32(ffn1_w_up), f32(ffn1_w_down), _row(mix_norm),
      f32(w_in), gqk, _ones_blockdiag())

    sink_rows = jnp.repeat(sink.astype(F32).reshape(N_KV, Q_PER_KV), BQ, axis=1) * LOG2E
    bias_tail = jnp.concatenate([sink_rows[:, None, :],
                                 jnp.full((N_KV, N_KEYS - SINK_COL - 1, Q_PER_KV * BQ), NEG, F32)], axis=1)

    q_rows = pl.BlockSpec((QBLK_PER_STEP * BQ, ATTN_W), lambda s: (s, 0))
    a_out = pl.pallas_call(
        _attn_kernel,
        grid=(pl.cdiv(N_QBLK + 1, QBLK_PER_STEP),),
        in_specs=[q_rows, _resident((L, 2 * KV_W)), _resident((4, N_KV, SINK_COL, Q_PER_KV * BQ)),
                  _resident((N_KV, N_KEYS - SINK_COL, Q_PER_KV * BQ))],
        out_specs=q_rows,
        out_shape=jax.ShapeDtypeStruct((L, ATTN_W), BF16),
        compiler_params=_params(),
        name="attn",
    )(q, kv, jnp.asarray(_attn_bias_const()), bias_tail)

    row_set = lambda w: pl.BlockSpec((N1, B_PER_STEP, 1, w), lambda s: (0, s, 0, 0))
    g = pl.pallas_call(
        _dft_a_kernel,
        grid=(pl.cdiv(N2, B_PER_STEP),),
        in_specs=[row_set(F_W), _resident((2 * N1, N1)), _resident((256, 512))],
        out_specs=row_set(2 * F_W),
        out_shape=jax.ShapeDtypeStruct((N1, N2, 1, 2 * F_W), F32),
        scratch_shapes=[pltpu.VMEM((N1, F_W), F32), pltpu.VMEM((N1, 2 * F_W), F32)],
        compiler_params=_params(),
        name="dft_a",
    )(f.reshape(N1, N2, 1, F_W), bf(jnp.asarray(_stage_a_dft())), bf(jnp.asarray(_channel_dft())))

    ec, es = _stage_c_dft()
    y = pl.pallas_call(
        _dft_c_kernel,
        grid=(N1 // TC,),
        in_specs=[pl.BlockSpec((TC, N2, None, 2 * F_W), lambda i: (i, 0, 0, 0)),
                  pl.BlockSpec((TC, N2P, N2P), lambda i: (i, 0, 0)),
                  pl.BlockSpec((TC, N2P, N2P), lambda i: (i, 0, 0))],
        out_specs=pl.BlockSpec((N2, TC, 1, F_W), lambda i: (0, i, 0, 0)),
        out_shape=jax.ShapeDtypeStruct((N2, N1, 1, F_W), F32),
        scratch_shapes=[pltpu.VMEM((TC, N2P, 2 * F_W), F32)],
        compiler_params=_params(),
        name="dft_c",
    )(g, bf(jnp.asarray(ec)), bf(jnp.asarray(es)))

    out = pl.pallas_call(
        _post_kernel,
        grid=(N_CAST + N_TILES,),
        in_specs=[row_tile(D_MODEL), row_tile(ATTN_W), flat_rows(F_W), _resident((1, ATTN_W)),
                  _resident((1, F_W)), cast_rows(D_MODEL, D_MODEL), _resident((1, D_MODEL)),
                  *ffn_in, _resident((1, D_MODEL))],
        out_specs=row_tile(D_MODEL),
        out_shape=jax.ShapeDtypeStruct((SEQ, D_MODEL), F32),
        scratch_shapes=ffn_scratch + [bf16_copy(D_MODEL, D_MODEL)],
        compiler_params=_params(),
        name="post",
    )(h1, a_out, y.reshape(L, 1, F_W), _row(attn_out_norm), _row(fourier_out_norm), f32(w_out), _row(ffn2_norm),
      f32(ffn2_w_gate), f32(ffn2_w_up), f32(ffn2_w_down), _row(final_norm))
    return out.reshape(1, SEQ, D_MODEL)
```

```python
import numpy as np
import jax
import jax.numpy as jnp
from jax import lax
from jax.experimental import pallas as pl
from jax.experimental.pallas import tpu as pltpu

F32 = jnp.float32
BF16 = jnp.bfloat16

D_MODEL = 1024
SEQ = 16384
N_META = 16
L = SEQ + N_META
HEAD_DIM = 64
N_HEADS = 8
N_KV = 2
Q_PER_KV = N_HEADS // N_KV
ATTN_W = N_HEADS * HEAD_DIM
KV_W = N_KV * HEAD_DIM
F_W = 512
F_GROUP = 64
IN_W = ATTN_W + 2 * KV_W + F_W
QK_W = ATTN_W + KV_W
WINDOW = 128
D_FF = 2816
N_CAST = 8
EPS = 1e-6
NEG = -1e30
LOG2E = 1.4426950408889634

TM = 400
N_TILES = L // TM
META_ROW0 = SEQ - (N_TILES - 1) * TM

BQ = 128
N_QBLK = SEQ // BQ
QBLK_PER_STEP = 12
BAND = BQ + 2 * WINDOW
N_KEYS = BAND + N_META + 8
SINK_COL = BAND + N_META

N1 = 400
N2 = 41
N2P = 48
B_PER_STEP = 2
TC = 25

VMEM_LIMIT = 56 * 1024 * 1024


def _ones_blockdiag():
    m = np.kron(np.eye(4), np.ones((HEAD_DIM, HEAD_DIM)))
    return jnp.asarray(m, BF16)


def _channel_dft():
    c = np.arange(F_GROUP)
    ang = 2.0 * np.pi * np.outer(c, c) / F_GROUP
    cos = np.kron(np.eye(4), np.cos(ang) / 8.0)
    sin = np.kron(np.eye(4), np.sin(ang) / 8.0)
    return np.concatenate([cos, sin], axis=1).astype(np.float32)


def _stage_a_dft():
    a = np.arange(N1, dtype=np.int64)
    m = (N2 * np.outer(a + N_META, a)) % L
    ang = 2.0 * np.pi * m / L
    return np.concatenate([np.cos(ang), -np.sin(ang)], axis=0).astype(np.float32)


def _stage_c_dft():
    c = np.arange(N1, dtype=np.int64)[:, None, None]
    d = np.arange(N2, dtype=np.int64)[None, :, None]
    b = np.arange(N2, dtype=np.int64)[None, None, :]
    m = ((c + N1 * d + N_META) * (b + N_META)) % L
    ang = 2.0 * np.pi * m / L
    ec = np.zeros((N1, N2P, N2P), np.float32)
    es = np.zeros((N1, N2P, N2P), np.float32)
    ec[:, :N2, :N2] = np.cos(ang) / np.sqrt(L)
    es[:, :N2, :N2] = np.sin(ang) / np.sqrt(L)
    return ec, es


def _attn_bias_const():
    slopes = 2.0 ** -(np.arange(N_HEADS) + 1.0)
    i = np.arange(BQ)[:, None]
    j = np.arange(BAND)[None, :]
    m = np.arange(N_META)[None, :]
    out = np.full((4, N_KV, Q_PER_KV * BQ, SINK_COL), NEG, np.float64)
    for t in range(4):
        if t < 3:
            dist = np.abs(t * WINDOW + i - j)
            band_ok = dist <= WINDOW
            dist_m = np.minimum(t * WINDOW + N_META + i - m, WINDOW) if t == 0 else np.full((BQ, N_META), WINDOW)
        else:
            dist = N_META + j - i
            band_ok = (dist <= WINDOW) & (i < N_META)
            dist_m = np.abs(i - m)
        for h in range(N_HEADS):
            g, hl = divmod(h, Q_PER_KV)
            rows = slice(hl * BQ, (hl + 1) * BQ)
            band = np.where(band_ok, -slopes[h] * dist, NEG)
            if t == 3:
                band = np.where(i < N_META, band, 0.0)
            out[t, g, rows, :BAND] = band
            out[t, g, rows, BAND:BAND + N_META] = -slopes[h] * dist_m
    return np.ascontiguousarray(LOG2E * out.transpose(0, 1, 3, 2)).astype(np.float32)


def _rms(x, g):
    return x * lax.rsqrt(jnp.mean(x * x, axis=-1, keepdims=True) + EPS) * g


def _swiglu_half(x, g_ref, wg_s, wu_s, wd_s):
    xn = _rms(x, g_ref[...]).astype(BF16)
    gate = jnp.dot(xn, wg_s[...], preferred_element_type=F32)
    up = jnp.dot(xn, wu_s[...], preferred_element_type=F32)
    act = (gate * jax.nn.sigmoid(gate) * up).astype(BF16)
    return x + 0.5 * jnp.dot(act, wd_s[...], preferred_element_type=F32)


def _cast_weights(s, f32_refs, bf16_scratch):
    for w_ref, w_s in zip(f32_refs, bf16_scratch):
        rows = w_ref.shape[0]
        w_s[pl.ds(pl.multiple_of(s * rows, rows), rows), :] = w_ref[...].astype(BF16)


def _pre_kernel(x_ref, meta_ref, g1_ref, wg_ref, wu_ref, wd_ref, gm_ref, win_ref, gqk_ref,
                ones_ref, h1_ref, q_ref, kv_ref, f_ref, wg_s, wu_s, wd_s, win_s):
    s = pl.program_id(0)

    @pl.when(s < N_CAST)
    def _():
        _cast_weights(s, (wg_ref, wu_ref, wd_ref, win_ref), (wg_s, wu_s, wd_s, win_s))

    @pl.when(s >= N_CAST)
    def _():
        _pre_tile(s - N_CAST, x_ref, meta_ref, g1_ref, gm_ref, gqk_ref, ones_ref,
                  h1_ref, q_ref, kv_ref, f_ref, wg_s, wu_s, wd_s, win_s)


def _pre_tile(i, x_ref, meta_ref, g1_ref, gm_ref, gqk_ref, ones_ref,
              h1_ref, q_ref, kv_ref, f_ref, wg_s, wu_s, wd_s, win_s):
    row = lax.broadcasted_iota(jnp.int32, (TM, 1), 0)
    is_meta = jnp.logical_and(i == N_TILES - 1, row >= META_ROW0)
    x = jnp.where(is_meta, meta_ref[...], x_ref[...])

    h1 = _swiglu_half(x, g1_ref, wg_s, wu_s, wd_s)
    h1_ref[...] = h1

    u = jnp.dot(_rms(h1, gm_ref[...]).astype(BF16), win_s[...], preferred_element_type=F32)

    qk = u[:, :QK_W]
    sq = (qk * qk).astype(BF16)
    ones = ones_ref[...]
    ss = jnp.concatenate([
        jnp.dot(sq[:, 0:256], ones, preferred_element_type=F32),
        jnp.dot(sq[:, 256:512], ones, preferred_element_type=F32),
        jnp.dot(sq[:, 512:640], ones[:KV_W, :KV_W], preferred_element_type=F32)], axis=1)
    qkn = qk * lax.rsqrt(ss * (1.0 / HEAD_DIM) + EPS) * gqk_ref[...]
    q_ref[...] = qkn[:, :ATTN_W].astype(BF16)
    kv_ref[...] = jnp.concatenate([qkn[:, ATTN_W:], u[:, QK_W:QK_W + KV_W]], axis=1).astype(BF16)

    f_ref[...] = u[:, QK_W + KV_W:]


def _attn_kernel(q_ref, kv_ref, bias_ref, tail_ref, o_ref):
    step = pl.program_id(0)
    kv_meta = kv_ref[SEQ:L, :]
    zero_keys = jnp.zeros((8, 2 * KV_W), BF16)

    units = []
    for j in range(QBLK_PER_STEP):
        n = step * QBLK_PER_STEP + j
        btype = jnp.where(n == 0, 0, jnp.where(n == N_QBLK - 1, 2, jnp.where(n == N_QBLK, 3, 1)))
        start = jnp.where(n == N_QBLK, 0, jnp.clip((n - 1) * BQ, 0, SEQ - BAND))
        start = pl.multiple_of(start, BQ)
        kv_cat = jnp.concatenate([kv_ref[pl.ds(start, BAND), :], kv_meta, zero_keys], axis=0)
        q_ok = lax.broadcasted_iota(jnp.int32, (BQ, 1), 0) < L - n * BQ
        units.extend((j, g, btype, kv_cat, q_ok) for g in range(N_KV))

    def scores(unit):
        j, g, _, kv_cat, q_ok = unit
        q = jnp.where(q_ok, q_ref[j * BQ:(j + 1) * BQ, :], 0)
        qg = jnp.concatenate(
            [q[:, (Q_PER_KV * g + hl) * HEAD_DIM:(Q_PER_KV * g + hl + 1) * HEAD_DIM]
             for hl in range(Q_PER_KV)], axis=0)
        k = kv_cat[:, g * HEAD_DIM:(g + 1) * HEAD_DIM]
        return lax.dot_general(k, qg, (((1,), (1,)), ((), ())), preferred_element_type=F32)

    st_next = scores(units[0])
    for u, (j, g, btype, kv_cat, _) in enumerate(units):
        st = st_next + jnp.concatenate([bias_ref[btype, g], tail_ref[g]], axis=0)
        m = jnp.max(st, axis=0, keepdims=True)
        p = jnp.exp2(st - m)
        denom = jnp.sum(p, axis=0, keepdims=True)
        if u + 1 < len(units):
            st_next = scores(units[u + 1])
            p = jnp.concatenate([jnp.maximum(p[:16], st_next[:16] * 0.0 - 1.0), p[16:]], axis=0)
        v = kv_cat[:, KV_W + g * HEAD_DIM:KV_W + (g + 1) * HEAD_DIM]
        ot = lax.dot_general(v, p.astype(BF16), (((0,), (0,)), ((), ())), preferred_element_type=F32)
        ot = ot / denom
        for hp in range(Q_PER_KV // 2):
            h0 = Q_PER_KV * g + 2 * hp
            two = jnp.concatenate([ot[:, 2 * hp * BQ:(2 * hp + 1) * BQ],
                                   ot[:, (2 * hp + 1) * BQ:(2 * hp + 2) * BQ]], axis=0)
            o_ref[j * BQ:(j + 1) * BQ, h0 * HEAD_DIM:(h0 + 2) * HEAD_DIM] = two.T.astype(BF16)


def _dft_a_kernel(f_ref, wa_ref, cdft_ref, g_ref, fs_ref, gs_ref):
    cdft = cdft_ref[...]
    for k in range(B_PER_STEP):
        fs_ref[...] = f_ref[:, k, 0, :]
        h = jnp.dot(wa_ref[...], fs_ref[...].astype(BF16), preferred_element_type=F32).astype(BF16)
        for half in range(2):
            p = jnp.dot(h[:, half * 256:(half + 1) * 256], cdft, preferred_element_type=F32)
            gs_ref[:, half * 256:(half + 1) * 256] = p[:N1, :256] + p[N1:, 256:]
            gs_ref[:, F_W + half * 256:F_W + (half + 1) * 256] = p[N1:, :256] - p[:N1, 256:]
        g_ref[:, k, 0, :] = gs_ref[...]


def _dft_c_kernel(g_ref, ec_ref, es_ref, y_ref, gs_ref):
    gs_ref[:, N2:, :] = jnp.zeros((TC, N2P - N2, 2 * F_W), F32)
    for c in range(TC):
        gs_ref[c, :N2, :] = g_ref[c]
        g = gs_ref[c].astype(BF16)
        y = (jnp.dot(ec_ref[c], g[:, :F_W], preferred_element_type=F32)
             + jnp.dot(es_ref[c], g[:, F_W:], preferred_element_type=F32))
        y_ref[:, c, 0, :] = y[:N2]


def _post_kernel(h1_ref, a_ref, y_ref, ga_ref, gf_ref, wout_ref, g2_ref, wg_ref, wu_ref, wd_ref,
                 gfin_ref, o_ref, wg_s, wu_s, wd_s, wout_s):
    s = pl.program_id(0)

    @pl.when(s < N_CAST)
    def _():
        _cast_weights(s, (wg_ref, wu_ref, wd_ref, wout_ref), (wg_s, wu_s, wd_s, wout_s))

    @pl.when(s >= N_CAST)
    def _():
        mixed = jnp.concatenate([_rms(a_ref[...].astype(F32), ga_ref[...]), _rms(y_ref[...], gf_ref[...])], axis=1)
        h2 = h1_ref[...] + jnp.dot(mixed.astype(BF16), wout_s[...], preferred_element_type=F32)
        h3 = _swiglu_half(h2, g2_ref, wg_s, wu_s, wd_s)
        o_ref[...] = _rms(h3, gfin_ref[...])


def _resident(shape):
    zeros = (0,) * len(shape)
    return pl.BlockSpec(shape, lambda *_: zeros, pipeline_mode=pl.Buffered(1))


def _params():
    return pltpu.CompilerParams(dimension_semantics=("arbitrary",), vmem_limit_bytes=VMEM_LIMIT)


def _row(g):
    return g.astype(F32).reshape(1, -1)


def kernel(x, meta_tokens, ffn1_norm, ffn1_w_gate, ffn1_w_up, ffn1_w_down, mix_norm, w_in, q_norm, k_norm, sink, attn_out_norm, fourier_out_norm, w_out, ffn2_norm, ffn2_w_gate, ffn2_w_up, ffn2_w_down, final_norm):
    assert x.shape == (1, SEQ, D_MODEL) and x.dtype == F32
    x2 = x.reshape(SEQ, D_MODEL)
    meta_pad = jnp.pad(meta_tokens.astype(F32), ((META_ROW0, 0), (0, 0)))
    gqk = jnp.concatenate([jnp.tile(q_norm.astype(F32), N_HEADS) * (HEAD_DIM ** -0.5 * LOG2E),
                           jnp.tile(k_norm.astype(F32), N_KV)]).reshape(1, QK_W)
    bf = lambda w: w.astype(BF16)

    tile_idx = lambda s: jnp.maximum(s - N_CAST, 0)
    row_tile = lambda w: pl.BlockSpec((TM, w), lambda s: (tile_idx(s), 0))
    flat_rows = lambda w: pl.BlockSpec((TM, None, w), lambda s: (tile_idx(s), 0, 0))
    cast_rows = lambda r, c: pl.BlockSpec((r // N_CAST, c), lambda s: (jnp.minimum(s, N_CAST - 1), 0))
    bf16_copy = lambda r, c: pltpu.VMEM((r, c), BF16)
    ffn_in = [cast_rows(D_MODEL, D_FF), cast_rows(D_MODEL, D_FF), cast_rows(D_FF, D_MODEL)]
    ffn_scratch = [bf16_copy(D_MODEL, D_FF), bf16_copy(D_MODEL, D_FF), bf16_copy(D_FF, D_MODEL)]
    f32 = lambda w: w.astype(F32)

    h1, q, kv, f = pl.pallas_call(
        _pre_kernel,
        grid=(N_CAST + N_TILES,),
        in_specs=[row_tile(D_MODEL), _resident((TM, D_MODEL)), _resident((1, D_MODEL)), *ffn_in,
                  _resident((1, D_MODEL)), cast_rows(D_MODEL, IN_W), _resident((1, QK_W)),
                  _resident((256, 256))],
        out_specs=[row_tile(D_MODEL), row_tile(ATTN_W), row_tile(2 * KV_W), flat_rows(F_W)],
        out_shape=[jax.ShapeDtypeStruct((L, D_MODEL), F32), jax.ShapeDtypeStruct((L, ATTN_W), BF16),
                   jax.ShapeDtypeStruct((L, 2 * KV_W), BF16), jax.ShapeDtypeStruct((L, 1, F_W), F32)],
        scratch_shapes=ffn_scratch + [bf16_copy(D_MODEL, IN_W)],
        compiler_params=_params(),
        name="pre",
    )(x2, meta_pad, _row(ffn1_norm), f32(ffn1_w_gate), f32(ffn1_w_up), f32(ffn1_w_down), _row(mix_norm),
      f32(w_in), gqk, _ones_blockdiag())

    sink_rows = jnp.repeat(sink.astype(F32).reshape(N_KV, Q_PER_KV), BQ, axis=1) * LOG2E
    bias_tail = jnp.concatenate([sink_rows[:, None, :],
                                 jnp.full((N_KV, N_KEYS - SINK_COL - 1, Q_PER_KV * BQ), NEG, F32)], axis=1)

    q_rows = pl.BlockSpec((QBLK_PER_STEP * BQ, ATTN_W), lambda s: (s, 0))
    a_out = pl.pallas_call(
        _attn_kernel,
        grid=(pl.cdiv(N_QBLK + 1, QBLK_PER_STEP),),
        in_specs=[q_rows, _resident((L, 2 * KV_W)), _resident((4, N_KV, SINK_COL, Q_PER_KV * BQ)),
                  _resident((N_KV, N_KEYS - SINK_COL, Q_PER_KV * BQ))],
        out_specs=q_rows,
        out_shape=jax.ShapeDtypeStruct((L, ATTN_W), BF16),
        compiler_params=_params(),
        name="attn",
    )(q, kv, jnp.asarray(_attn_bias_const()), bias_tail)

    row_set = lambda w: pl.BlockSpec((N1, B_PER_STEP, 1, w), lambda s: (0, s, 0, 0))
    g = pl.pallas_call(
        _dft_a_kernel,
        grid=(pl.cdiv(N2, B_PER_STEP),),
        in_specs=[row_set(F_W), _resident((2 * N1, N1)), _resident((256, 512))],
        out_specs=row_set(2 * F_W),
        out_shape=jax.ShapeDtypeStruct((N1, N2, 1, 2 * F_W), F32),
        scratch_shapes=[pltpu.VMEM((N1, F_W), F32), pltpu.VMEM((N1, 2 * F_W), F32)],
        compiler_params=_params(),
        name="dft_a",
    )(f.reshape(N1, N2, 1, F_W), bf(jnp.asarray(_stage_a_dft())), bf(jnp.asarray(_channel_dft())))

    ec, es = _stage_c_dft()
    y = pl.pallas_call(
        _dft_c_kernel,
        grid=(N1 // TC,),
        in_specs=[pl.BlockSpec((TC, N2, None, 2 * F_W), lambda i: (i, 0, 0, 0)),
                  pl.BlockSpec((TC, N2P, N2P), lambda i: (i, 0, 0)),
                  pl.BlockSpec((TC, N2P, N2P), lambda i: (i, 0, 0))],
        out_specs=pl.BlockSpec((N2, TC, 1, F_W), lambda i: (0, i, 0, 0)),
        out_shape=jax.ShapeDtypeStruct((N2, N1, 1, F_W), F32),
        scratch_shapes=[pltpu.VMEM((TC, N2P, 2 * F_W), F32)],
        compiler_params=_params(),
        name="dft_c",
    )(g, bf(jnp.asarray(ec)), bf(jnp.asarray(es)))

    out = pl.pallas_call(
        _post_kernel,
        grid=(N_CAST + N_TILES,),
        in_specs=[row_tile(D_MODEL), row_tile(ATTN_W), flat_rows(F_W), _resident((1, ATTN_W)),
                  _resident((1, F_W)), cast_rows(D_MODEL, D_MODEL), _resident((1, D_MODEL)),
                  *ffn_in, _resident((1, D_MODEL))],
        out_specs=row_tile(D_MODEL),
        out_shape=jax.ShapeDtypeStruct((SEQ, D_MODEL), F32),
        scratch_shapes=ffn_scratch + [bf16_copy(D_MODEL, D_MODEL)],
        compiler_params=_params(),
        name="post",
    )(h1, a_out, y.reshape(L, 1, F_W), _row(attn_out_norm), _row(fourier_out_norm), f32(w_out), _row(ffn2_norm),
      f32(ffn2_w_gate), f32(ffn2_w_up), f32(ffn2_w_down), _row(final_norm))
    return out.reshape(1, SEQ, D_MODEL)
```

```python
import numpy as np
import jax
import jax.numpy as jnp
from jax import lax
from jax.experimental import pallas as pl
from jax.experimental.pallas import tpu as pltpu

F32 = jnp.float32
BF16 = jnp.bfloat16

D_MODEL = 1024
SEQ = 16384
N_META = 16
L = SEQ + N_META
HEAD_DIM = 64
N_HEADS = 8
N_KV = 2
Q_PER_KV = N_HEADS // N_KV
ATTN_W = N_HEADS * HEAD_DIM
KV_W = N_KV * HEAD_DIM
F_W = 512
F_GROUP = 64
IN_W = ATTN_W + 2 * KV_W + F_W
QK_W = ATTN_W + KV_W
WINDOW = 128
D_FF = 2816
N_CAST = 8
EPS = 1e-6
NEG = -1e30
LOG2E = 1.4426950408889634

TM = 400
N_TILES = L // TM
META_ROW0 = SEQ - (N_TILES - 1) * TM

BQ = 128
N_QBLK = SEQ // BQ
QBLK_PER_STEP = 12
BAND = BQ + 2 * WINDOW
N_KEYS = BAND + N_META

N1 = 400
N2 = 41
N2P = 48
B_PER_STEP = 2
TC = 25

VMEM_LIMIT = 56 * 1024 * 1024


def _ones_blockdiag():
    m = np.kron(np.eye(4), np.ones((HEAD_DIM, HEAD_DIM)))
    return jnp.asarray(m, BF16)


def _channel_dft():
    c = np.arange(F_GROUP)
    ang = 2.0 * np.pi * np.outer(c, c) / F_GROUP
    cos = np.kron(np.eye(4), np.cos(ang) / 8.0)
    sin = np.kron(np.eye(4), np.sin(ang) / 8.0)
    return np.concatenate([cos, sin], axis=1).astype(np.float32)


def _stage_a_dft():
    a = np.arange(N1, dtype=np.int64)
    m = (N2 * np.outer(a + N_META, a)) % L
    ang = 2.0 * np.pi * m / L
    return np.concatenate([np.cos(ang), -np.sin(ang)], axis=0).astype(np.float32)


def _stage_c_dft():
    c = np.arange(N1, dtype=np.int64)[:, None, None]
    d = np.arange(N2, dtype=np.int64)[None, :, None]
    b = np.arange(N2, dtype=np.int64)[None, None, :]
    m = ((c + N1 * d + N_META) * (b + N_META)) % L
    ang = 2.0 * np.pi * m / L
    ec = np.zeros((N1, N2P, N2P), np.float32)
    es = np.zeros((N1, N2P, N2P), np.float32)
    ec[:, :N2, :N2] = np.cos(ang) / np.sqrt(L)
    es[:, :N2, :N2] = np.sin(ang) / np.sqrt(L)
    return ec, es


def _attn_bias_const():
    slopes = 2.0 ** -(np.arange(N_HEADS) + 1.0)
    i = np.arange(BQ)[:, None]
    j = np.arange(BAND)[None, :]
    m = np.arange(N_META)[None, :]
    out = np.full((4, N_KV, Q_PER_KV * BQ, N_KEYS), NEG, np.float64)
    for t in range(4):
        if t < 3:
            dist = np.abs(t * WINDOW + i - j)
            band_ok = dist <= WINDOW
            dist_m = np.minimum(t * WINDOW + N_META + i - m, WINDOW) if t == 0 else np.full((BQ, N_META), WINDOW)
        else:
            dist = N_META + j - i
            band_ok = (dist <= WINDOW) & (i < N_META)
            dist_m = np.abs(i - m)
        for h in range(N_HEADS):
            g, hl = divmod(h, Q_PER_KV)
            rows = slice(hl * BQ, (hl + 1) * BQ)
            band = np.where(band_ok, -slopes[h] * dist, NEG)
            if t == 3:
                band = np.where(i < N_META, band, 0.0)
            out[t, g, rows, :BAND] = band
            out[t, g, rows, BAND:BAND + N_META] = -slopes[h] * dist_m
    return np.ascontiguousarray(LOG2E * out.transpose(0, 1, 3, 2)).astype(np.float32)


def _unit_rms(x):
    return x * lax.rsqrt(jnp.mean(x * x, axis=-1, keepdims=True) + EPS)


def _rms(x, g):
    return _unit_rms(x) * g


def _swiglu_half(x, wg_s, wu_s, wd_s):
    xn = _unit_rms(x).astype(BF16)
    gate = jnp.dot(xn, wg_s[...], preferred_element_type=F32)
    up = jnp.dot(xn, wu_s[...], preferred_element_type=F32)
    act = (gate * jax.nn.sigmoid(gate) * up).astype(BF16)
    return x + 0.5 * jnp.dot(act, wd_s[...], preferred_element_type=F32)


def _cast_weights(s, f32_refs, gain_refs, bf16_scratch):
    for w_ref, g_ref, w_s in zip(f32_refs, gain_refs, bf16_scratch):
        rows = w_ref.shape[0]
        w = w_ref[...] if g_ref is None else w_ref[...] * g_ref[...]
        w_s[pl.ds(pl.multiple_of(s * rows, rows), rows), :] = w.astype(BF16)


def _pre_kernel(x_ref, meta_ref, g1_ref, wg_ref, wu_ref, wd_ref, gm_ref, win_ref, gqk_ref,
                ones_ref, h1_ref, q_ref, kv_ref, f_ref, wg_s, wu_s, wd_s, win_s):
    s = pl.program_id(0)

    @pl.when(s < N_CAST)
    def _():
        _cast_weights(s, (wg_ref, wu_ref, wd_ref, win_ref), (g1_ref, g1_ref, None, gm_ref),
                      (wg_s, wu_s, wd_s, win_s))

    @pl.when(s >= N_CAST)
    def _():
        _pre_tile(s - N_CAST, x_ref, meta_ref, gqk_ref, ones_ref,
                  h1_ref, q_ref, kv_ref, f_ref, wg_s, wu_s, wd_s, win_s)


def _pre_tile(i, x_ref, meta_ref, gqk_ref, ones_ref,
              h1_ref, q_ref, kv_ref, f_ref, wg_s, wu_s, wd_s, win_s):
    row = lax.broadcasted_iota(jnp.int32, (TM, 1), 0)
    is_meta = jnp.logical_and(i == N_TILES - 1, row >= META_ROW0)
    x = jnp.where(is_meta, meta_ref[...], x_ref[...])

    h1 = _swiglu_half(x, wg_s, wu_s, wd_s)
    h1_ref[...] = h1

    u = jnp.dot(_unit_rms(h1).astype(BF16), win_s[...], preferred_element_type=F32)

    qk = u[:, :QK_W]
    sq = (qk * qk).astype(BF16)
    ones = ones_ref[...]
    ss = jnp.concatenate([
        jnp.dot(sq[:, 0:256], ones, preferred_element_type=F32),
        jnp.dot(sq[:, 256:512], ones, preferred_element_type=F32),
        jnp.dot(sq[:, 512:640], ones[:KV_W, :KV_W], preferred_element_type=F32)], axis=1)
    qkn = qk * lax.rsqrt(ss * (1.0 / HEAD_DIM) + EPS) * gqk_ref[...]
    q_ref[...] = qkn[:, :ATTN_W].astype(BF16)
    kv_ref[...] = jnp.concatenate([qkn[:, ATTN_W:], u[:, QK_W:QK_W + KV_W]], axis=1).astype(BF16)

    f_ref[...] = u[:, QK_W + KV_W:]


def _attn_kernel(q_ref, kv_ref, bias_ref, sink_ref, o_ref):
    step = pl.program_id(0)
    kv_meta = kv_ref[SEQ:L, :]

    units = []
    for j in range(QBLK_PER_STEP):
        n = step * QBLK_PER_STEP + j
        btype = jnp.where(n == 0, 0, jnp.where(n == N_QBLK - 1, 2, jnp.where(n == N_QBLK, 3, 1)))
        start = jnp.where(n == N_QBLK, 0, jnp.clip((n - 1) * BQ, 0, SEQ - BAND))
        start = pl.multiple_of(start, BQ)
        kv_cat = jnp.concatenate([kv_ref[pl.ds(start, BAND), :], kv_meta], axis=0)
        q_ok = lax.broadcasted_iota(jnp.int32, (BQ, 1), 0) < L - n * BQ
        units.extend((j, g, btype, kv_cat, q_ok) for g in range(N_KV))

    def scores(unit):
        j, g, _, kv_cat, q_ok = unit
        q = jnp.where(q_ok, q_ref[j * BQ:(j + 1) * BQ, :], 0)
        qg = jnp.concatenate(
            [q[:, (Q_PER_KV * g + hl) * HEAD_DIM:(Q_PER_KV * g + hl + 1) * HEAD_DIM]
             for hl in range(Q_PER_KV)], axis=0)
        k = kv_cat[:, g * HEAD_DIM:(g + 1) * HEAD_DIM]
        return lax.dot_general(k, qg, (((1,), (1,)), ((), ())), preferred_element_type=F32)

    st_next = scores(units[0])
    for u, (j, g, btype, kv_cat, _) in enumerate(units):
        st = st_next + bias_ref[btype, g]
        sink = sink_ref[g]
        m = jnp.maximum(jnp.max(st, axis=0, keepdims=True), sink)
        p = jnp.exp2(st - m)
        denom = jnp.sum(p, axis=0, keepdims=True) + jnp.exp2(sink - m)
        if u + 1 < len(units):
            st_next = scores(units[u + 1])
            p = jnp.concatenate([jnp.maximum(p[:16], st_next[:16] * 0.0 - 1.0), p[16:]], axis=0)
        v = kv_cat[:, KV_W + g * HEAD_DIM:KV_W + (g + 1) * HEAD_DIM]
        ot = lax.dot_general(v, p.astype(BF16), (((0,), (0,)), ((), ())), preferred_element_type=F32)
        ot = ot / denom
        for hp in range(Q_PER_KV // 2):
            h0 = Q_PER_KV * g + 2 * hp
            two = jnp.concatenate([ot[:, 2 * hp * BQ:(2 * hp + 1) * BQ],
                                   ot[:, (2 * hp + 1) * BQ:(2 * hp + 2) * BQ]], axis=0)
            o_ref[j * BQ:(j + 1) * BQ, h0 * HEAD_DIM:(h0 + 2) * HEAD_DIM] = two.T.astype(BF16)


def _dft_a_kernel(f_ref, wa_ref, cdft_ref, g_ref, fs_ref, gs_ref):
    cdft = cdft_ref[...]
    for k in range(B_PER_STEP):
        fs_ref[...] = f_ref[:, k, 0, :]
        h = jnp.dot(wa_ref[...], fs_ref[...].astype(BF16), preferred_element_type=F32).astype(BF16)
        for half in range(2):
            p = jnp.dot(h[:, half * 256:(half + 1) * 256], cdft, preferred_element_type=F32)
            gs_ref[:, half * 256:(half + 1) * 256] = p[:N1, :256] + p[N1:, 256:]
            gs_ref[:, F_W + half * 256:F_W + (half + 1) * 256] = p[N1:, :256] - p[:N1, 256:]
        g_ref[:, k, 0, :] = gs_ref[...]


def _dft_c_kernel(g_ref, ec_ref, es_ref, y_ref, gs_ref):
    gs_ref[:, N2:, :] = jnp.zeros((TC, N2P - N2, 2 * F_W), F32)
    for c in range(TC):
        gs_ref[c, :N2, :] = g_ref[c]
        g = gs_ref[c].astype(BF16)
        y = (jnp.dot(ec_ref[c], g[:, :F_W], preferred_element_type=F32)
             + jnp.dot(es_ref[c], g[:, F_W:], preferred_element_type=F32))
        y_ref[:, c, 0, :] = y[:N2]


def _post_kernel(h1_ref, a_ref, y_ref, gmix_ref, wout_ref, g2_ref, wg_ref, wu_ref, wd_ref,
                 gfin_ref, o_ref, wg_s, wu_s, wd_s, wout_s):
    s = pl.program_id(0)

    @pl.when(s < N_CAST)
    def _():
        _cast_weights(s, (wg_ref, wu_ref, wd_ref, wout_ref), (g2_ref, g2_ref, None, gmix_ref),
                      (wg_s, wu_s, wd_s, wout_s))

    @pl.when(s >= N_CAST)
    def _():
        mixed = jnp.concatenate([_unit_rms(a_ref[...].astype(F32)), _unit_rms(y_ref[...])], axis=1)
        h2 = h1_ref[...] + jnp.dot(mixed.astype(BF16), wout_s[...], preferred_element_type=F32)
        h3 = _swiglu_half(h2, wg_s, wu_s, wd_s)
        o_ref[...] = _rms(h3, gfin_ref[...])


def _resident(shape):
    zeros = (0,) * len(shape)
    return pl.BlockSpec(shape, lambda *_: zeros, pipeline_mode=pl.Buffered(1))


def _params():
    return pltpu.CompilerParams(dimension_semantics=("arbitrary",), vmem_limit_bytes=VMEM_LIMIT)


def _row(g):
    return g.astype(F32).reshape(1, -1)


def kernel(x, meta_tokens, ffn1_norm, ffn1_w_gate, ffn1_w_up, ffn1_w_down, mix_norm, w_in, q_norm, k_norm, sink, attn_out_norm, fourier_out_norm, w_out, ffn2_norm, ffn2_w_gate, ffn2_w_up, ffn2_w_down, final_norm):
    assert x.shape == (1, SEQ, D_MODEL) and x.dtype == F32
    x2 = x.reshape(SEQ, D_MODEL)
    meta_pad = jnp.pad(meta_tokens.astype(F32), ((META_ROW0, 0), (0, 0)))
    gqk = jnp.concatenate([jnp.tile(q_norm.astype(F32), N_HEADS) * (HEAD_DIM ** -0.5 * LOG2E),
                           jnp.tile(k_norm.astype(F32), N_KV)]).reshape(1, QK_W)
    bf = lambda w: w.astype(BF16)

    tile_idx = lambda s: jnp.maximum(s - N_CAST, 0)
    row_tile = lambda w: pl.BlockSpec((TM, w), lambda s: (tile_idx(s), 0))
    flat_rows = lambda w: pl.BlockSpec((TM, None, w), lambda s: (tile_idx(s), 0, 0))
    cast_rows = lambda r, c: pl.BlockSpec((r // N_CAST, c), lambda s: (jnp.minimum(s, N_CAST - 1), 0))
    gain_col = lambda g: g.astype(F32).reshape(-1, 1)
    bf16_copy = lambda r, c: pltpu.VMEM((r, c), BF16)
    ffn_in = [cast_rows(D_MODEL, D_FF), cast_rows(D_MODEL, D_FF), cast_rows(D_FF, D_MODEL)]
    ffn_scratch = [bf16_copy(D_MODEL, D_FF), bf16_copy(D_MODEL, D_FF), bf16_copy(D_FF, D_MODEL)]
    f32 = lambda w: w.astype(F32)

    h1, q, kv, f = pl.pallas_call(
        _pre_kernel,
        grid=(N_CAST + N_TILES,),
        in_specs=[row_tile(D_MODEL), _resident((TM, D_MODEL)), cast_rows(D_MODEL, 1), *ffn_in,
                  cast_rows(D_MODEL, 1), cast_rows(D_MODEL, IN_W), _resident((1, QK_W)),
                  _resident((256, 256))],
        out_specs=[row_tile(D_MODEL), row_tile(ATTN_W), row_tile(2 * KV_W), flat_rows(F_W)],
        out_shape=[jax.ShapeDtypeStruct((L, D_MODEL), F32), jax.ShapeDtypeStruct((L, ATTN_W), BF16),
                   jax.ShapeDtypeStruct((L, 2 * KV_W), BF16), jax.ShapeDtypeStruct((L, 1, F_W), F32)],
        scratch_shapes=ffn_scratch + [bf16_copy(D_MODEL, IN_W)],
        compiler_params=_params(),
        name="pre",
    )(x2, meta_pad, gain_col(ffn1_norm), f32(ffn1_w_gate), f32(ffn1_w_up), f32(ffn1_w_down), gain_col(mix_norm),
      f32(w_in), gqk, _ones_blockdiag())

    sink_rows = (jnp.repeat(sink.astype(F32).reshape(N_KV, Q_PER_KV), BQ, axis=1) * LOG2E)[:, None, :]

    q_rows = pl.BlockSpec((QBLK_PER_STEP * BQ, ATTN_W), lambda s: (s, 0))
    a_out = pl.pallas_call(
        _attn_kernel,
        grid=(pl.cdiv(N_QBLK + 1, QBLK_PER_STEP),),
        in_specs=[q_rows, _resident((L, 2 * KV_W)), _resident((4, N_KV, N_KEYS, Q_PER_KV * BQ)),
                  _resident((N_KV, 1, Q_PER_KV * BQ))],
        out_specs=q_rows,
        out_shape=jax.ShapeDtypeStruct((L, ATTN_W), BF16),
        compiler_params=_params(),
        name="attn",
    )(q, kv, jnp.asarray(_attn_bias_const()), sink_rows)

    row_set = lambda w: pl.BlockSpec((N1, B_PER_STEP, 1, w), lambda s: (0, s, 0, 0))
    g = pl.pallas_call(
        _dft_a_kernel,
        grid=(pl.cdiv(N2, B_PER_STEP),),
        in_specs=[row_set(F_W), _resident((2 * N1, N1)), _resident((256, 512))],
        out_specs=row_set(2 * F_W),
        out_shape=jax.ShapeDtypeStruct((N1, N2, 1, 2 * F_W), F32),
        scratch_shapes=[pltpu.VMEM((N1, F_W), F32), pltpu.VMEM((N1, 2 * F_W), F32)],
        compiler_params=_params(),
        name="dft_a",
    )(f.reshape(N1, N2, 1, F_W), bf(jnp.asarray(_stage_a_dft())), bf(jnp.asarray(_channel_dft())))

    ec, es = _stage_c_dft()
    y = pl.pallas_call(
        _dft_c_kernel,
        grid=(N1 // TC,),
        in_specs=[pl.BlockSpec((TC, N2, None, 2 * F_W), lambda i: (i, 0, 0, 0)),
                  pl.BlockSpec((TC, N2P, N2P), lambda i: (i, 0, 0)),
                  pl.BlockSpec((TC, N2P, N2P), lambda i: (i, 0, 0))],
        out_specs=pl.BlockSpec((N2, TC, 1, F_W), lambda i: (0, i, 0, 0)),
        out_shape=jax.ShapeDtypeStruct((N2, N1, 1, F_W), F32),
        scratch_shapes=[pltpu.VMEM((TC, N2P, 2 * F_W), F32)],
        compiler_params=_params(),
        name="dft_c",
    )(g, bf(jnp.asarray(ec)), bf(jnp.asarray(es)))

    out = pl.pallas_call(
        _post_kernel,
        grid=(N_CAST + N_TILES,),
        in_specs=[row_tile(D_MODEL), row_tile(ATTN_W), flat_rows(F_W), cast_rows(D_MODEL, 1),
                  cast_rows(D_MODEL, D_MODEL), cast_rows(D_MODEL, 1), *ffn_in, _resident((1, D_MODEL))],
        out_specs=row_tile(D_MODEL),
        out_shape=jax.ShapeDtypeStruct((SEQ, D_MODEL), F32),
        scratch_shapes=ffn_scratch + [bf16_copy(D_MODEL, D_MODEL)],
        compiler_params=_params(),
        name="post",
    )(h1, a_out, y.reshape(L, 1, F_W), gain_col(jnp.concatenate([attn_out_norm, fourier_out_norm])), f32(w_out),
      gain_col(ffn2_norm),
      f32(ffn2_w_gate), f32(ffn2_w_up), f32(ffn2_w_down), _row(final_norm))
    return out.reshape(1, SEQ, D_MODEL)
```

```python
import numpy as np
import jax
import jax.numpy as jnp
from jax import lax
from jax.experimental import pallas as pl
from jax.experimental.pallas import tpu as pltpu

F32 = jnp.float32
BF16 = jnp.bfloat16

D_MODEL = 1024
SEQ = 16384
N_META = 16
L = SEQ + N_META
HEAD_DIM = 64
N_HEADS = 8
N_KV = 2
Q_PER_KV = N_HEADS // N_KV
ATTN_W = N_HEADS * HEAD_DIM
KV_W = N_KV * HEAD_DIM
F_W = 512
F_GROUP = 64
IN_W = ATTN_W + 2 * KV_W + F_W
QK_W = ATTN_W + KV_W
WINDOW = 128
D_FF = 2816
N_CAST = 8
EPS = 1e-6
NEG = -1e30
LOG2E = 1.4426950408889634

TM = 656
N_TILES = L // TM
META_ROW0 = SEQ - (N_TILES - 1) * TM

BQ = 128
N_QBLK = SEQ // BQ
QBLK_PER_STEP = 12
BAND = BQ + 2 * WINDOW
N_KEYS = BAND + N_META

N1 = 400
N2 = 41
N2P = 48
B_PER_STEP = 2
TC = 25

VMEM_LIMIT = 56 * 1024 * 1024


def _ones_blockdiag():
    m = np.kron(np.eye(4), np.ones((HEAD_DIM, HEAD_DIM)))
    return jnp.asarray(m, BF16)


def _channel_dft():
    c = np.arange(F_GROUP)
    ang = 2.0 * np.pi * np.outer(c, c) / F_GROUP
    cos = np.kron(np.eye(4), np.cos(ang) / 8.0)
    sin = np.kron(np.eye(4), np.sin(ang) / 8.0)
    return np.concatenate([cos, sin], axis=1).astype(np.float32)


def _stage_a_dft():
    a = np.arange(N1, dtype=np.int64)
    m = (N2 * np.outer(a + N_META, a)) % L
    ang = 2.0 * np.pi * m / L
    return np.concatenate([np.cos(ang), -np.sin(ang)], axis=0).astype(np.float32)


def _stage_c_dft():
    c = np.arange(N1, dtype=np.int64)[:, None, None]
    d = np.arange(N2, dtype=np.int64)[None, :, None]
    b = np.arange(N2, dtype=np.int64)[None, None, :]
    m = ((c + N1 * d + N_META) * (b + N_META)) % L
    ang = 2.0 * np.pi * m / L
    ec = np.zeros((N1, N2P, N2P), np.float32)
    es = np.zeros((N1, N2P, N2P), np.float32)
    ec[:, :N2, :N2] = np.cos(ang) / np.sqrt(L)
    es[:, :N2, :N2] = np.sin(ang) / np.sqrt(L)
    return ec, es


def _attn_bias_const():
    slopes = 2.0 ** -(np.arange(N_HEADS) + 1.0)
    i = np.arange(BQ)[:, None]
    j = np.arange(BAND)[None, :]
    m = np.arange(N_META)[None, :]
    out = np.full((4, N_KV, Q_PER_KV * BQ, N_KEYS), NEG, np.float64)
    for t in range(4):
        if t < 3:
            dist = np.abs(t * WINDOW + i - j)
            band_ok = dist <= WINDOW
            dist_m = np.minimum(t * WINDOW + N_META + i - m, WINDOW) if t == 0 else np.full((BQ, N_META), WINDOW)
        else:
            dist = N_META + j - i
            band_ok = (dist <= WINDOW) & (i < N_META)
            dist_m = np.abs(i - m)
        for h in range(N_HEADS):
            g, hl = divmod(h, Q_PER_KV)
            rows = slice(hl * BQ, (hl + 1) * BQ)
            band = np.where(band_ok, -slopes[h] * dist, NEG)
            if t == 3:
                band = np.where(i < N_META, band, 0.0)
            out[t, g, rows, :BAND] = band
            out[t, g, rows, BAND:BAND + N_META] = -slopes[h] * dist_m
    return np.ascontiguousarray(LOG2E * out.transpose(0, 1, 3, 2)).astype(np.float32)


def _unit_rms(x):
    return x * lax.rsqrt(jnp.mean(x * x, axis=-1, keepdims=True) + EPS)


def _rms(x, g):
    return _unit_rms(x) * g


def _swiglu_half(x, wg_s, wu_s, wd_s):
    xn = _unit_rms(x).astype(BF16)
    gate = jnp.dot(xn, wg_s[...], preferred_element_type=F32)
    up = jnp.dot(xn, wu_s[...], preferred_element_type=F32)
    act = (gate * jax.nn.sigmoid(gate) * up).astype(BF16)
    return x + 0.5 * jnp.dot(act, wd_s[...], preferred_element_type=F32)


def _cast_weights(s, f32_refs, gain_refs, bf16_scratch):
    for w_ref, g_ref, w_s in zip(f32_refs, gain_refs, bf16_scratch):
        rows = w_ref.shape[0]
        w = w_ref[...] if g_ref is None else w_ref[...] * g_ref[...]
        w_s[pl.ds(pl.multiple_of(s * rows, rows), rows), :] = w.astype(BF16)


def _pre_kernel(x_ref, meta_ref, g1_ref, wg_ref, wu_ref, wd_ref, gm_ref, win_ref, gqk_ref,
                ones_ref, h1_ref, q_ref, kv_ref, f_ref, wg_s, wu_s, wd_s, win_s):
    s = pl.program_id(0)

    @pl.when(s < N_CAST)
    def _():
        _cast_weights(s, (wg_ref, wu_ref, wd_ref, win_ref), (g1_ref, g1_ref, None, gm_ref),
                      (wg_s, wu_s, wd_s, win_s))

    @pl.when(s >= N_CAST)
    def _():
        _pre_tile(s - N_CAST, x_ref, meta_ref, gqk_ref, ones_ref,
                  h1_ref, q_ref, kv_ref, f_ref, wg_s, wu_s, wd_s, win_s)


def _pre_tile(i, x_ref, meta_ref, gqk_ref, ones_ref,
              h1_ref, q_ref, kv_ref, f_ref, wg_s, wu_s, wd_s, win_s):
    row = lax.broadcasted_iota(jnp.int32, (TM, 1), 0)
    is_meta = jnp.logical_and(i == N_TILES - 1, row >= META_ROW0)
    x = jnp.where(is_meta, meta_ref[...], x_ref[...])

    h1 = _swiglu_half(x, wg_s, wu_s, wd_s)
    h1_ref[...] = h1

    u = jnp.dot(_unit_rms(h1).astype(BF16), win_s[...], preferred_element_type=F32)

    qk = u[:, :QK_W]
    sq = (qk * qk).astype(BF16)
    ones = ones_ref[...]
    ss = jnp.concatenate([
        jnp.dot(sq[:, 0:256], ones, preferred_element_type=F32),
        jnp.dot(sq[:, 256:512], ones, preferred_element_type=F32),
        jnp.dot(sq[:, 512:640], ones[:KV_W, :KV_W], preferred_element_type=F32)], axis=1)
    qkn = qk * lax.rsqrt(ss * (1.0 / HEAD_DIM) + EPS) * gqk_ref[...]
    q_ref[...] = qkn[:, :ATTN_W].astype(BF16)
    kv_ref[...] = jnp.concatenate([qkn[:, ATTN_W:], u[:, QK_W:QK_W + KV_W]], axis=1).astype(BF16)

    f_ref[...] = u[:, QK_W + KV_W:]


def _attn_kernel(q_ref, kv_ref, bias_ref, sink_ref, o_ref):
    step = pl.program_id(0)
    kv_meta = kv_ref[SEQ:L, :]

    units = []
    for j in range(QBLK_PER_STEP):
        n = step * QBLK_PER_STEP + j
        btype = jnp.where(n == 0, 0, jnp.where(n == N_QBLK - 1, 2, jnp.where(n == N_QBLK, 3, 1)))
        start = jnp.where(n == N_QBLK, 0, jnp.clip((n - 1) * BQ, 0, SEQ - BAND))
        start = pl.multiple_of(start, BQ)
        kv_cat = jnp.concatenate([kv_ref[pl.ds(start, BAND), :], kv_meta], axis=0)
        q_ok = lax.broadcasted_iota(jnp.int32, (BQ, 1), 0) < L - n * BQ
        units.extend((j, g, btype, kv_cat, q_ok) for g in range(N_KV))

    def scores(unit):
        j, g, _, kv_cat, q_ok = unit
        q = jnp.where(q_ok, q_ref[j * BQ:(j + 1) * BQ, :], 0)
        qg = jnp.concatenate(
            [q[:, (Q_PER_KV * g + hl) * HEAD_DIM:(Q_PER_KV * g + hl + 1) * HEAD_DIM]
             for hl in range(Q_PER_KV)], axis=0)
        k = kv_cat[:, g * HEAD_DIM:(g + 1) * HEAD_DIM]
        return lax.dot_general(k, qg, (((1,), (1,)), ((), ())), preferred_element_type=F32)

    st_next = scores(units[0])
    for u, (j, g, btype, kv_cat, _) in enumerate(units):
        st = st_next + bias_ref[btype, g]
        sink = sink_ref[g]
        m = jnp.maximum(jnp.max(st, axis=0, keepdims=True), sink)
        p = jnp.exp2(st - m)
        denom = jnp.sum(p, axis=0, keepdims=True) + jnp.exp2(sink - m)
        if u + 1 < len(units):
            st_next = scores(units[u + 1])
            p = jnp.concatenate([jnp.maximum(p[:16], st_next[:16] * 0.0 - 1.0), p[16:]], axis=0)
        v = kv_cat[:, KV_W + g * HEAD_DIM:KV_W + (g + 1) * HEAD_DIM]
        ot = lax.dot_general(v, p.astype(BF16), (((0,), (0,)), ((), ())), preferred_element_type=F32)
        ot = ot / denom
        for hp in range(Q_PER_KV // 2):
            h0 = Q_PER_KV * g + 2 * hp
            two = jnp.concatenate([ot[:, 2 * hp * BQ:(2 * hp + 1) * BQ],
                                   ot[:, (2 * hp + 1) * BQ:(2 * hp + 2) * BQ]], axis=0)
            o_ref[j * BQ:(j + 1) * BQ, h0 * HEAD_DIM:(h0 + 2) * HEAD_DIM] = two.T.astype(BF16)


def _dft_a_kernel(f_ref, wa_ref, cdft_ref, g_ref, fs_ref, gs_ref):
    cdft = cdft_ref[...]
    for k in range(B_PER_STEP):
        fs_ref[...] = f_ref[:, k, 0, :]
        h = jnp.dot(wa_ref[...], fs_ref[...].astype(BF16), preferred_element_type=F32).astype(BF16)
        for half in range(2):
            p = jnp.dot(h[:, half * 256:(half + 1) * 256], cdft, preferred_element_type=F32)
            gs_ref[:, half * 256:(half + 1) * 256] = p[:N1, :256] + p[N1:, 256:]
            gs_ref[:, F_W + half * 256:F_W + (half + 1) * 256] = p[N1:, :256] - p[:N1, 256:]
        g_ref[:, k, 0, :] = gs_ref[...]


def _dft_c_kernel(g_ref, ec_ref, es_ref, y_ref, gs_ref):
    gs_ref[:, N2:, :] = jnp.zeros((TC, N2P - N2, 2 * F_W), F32)
    for c in range(TC):
        gs_ref[c, :N2, :] = g_ref[c]
        g = gs_ref[c].astype(BF16)
        y = (jnp.dot(ec_ref[c], g[:, :F_W], preferred_element_type=F32)
             + jnp.dot(es_ref[c], g[:, F_W:], preferred_element_type=F32))
        y_ref[:, c, 0, :] = y[:N2]


def _post_kernel(h1_ref, a_ref, y_ref, gmix_ref, wout_ref, g2_ref, wg_ref, wu_ref, wd_ref,
                 gfin_ref, o_ref, wg_s, wu_s, wd_s, wout_s):
    s = pl.program_id(0)

    @pl.when(s < N_CAST)
    def _():
        _cast_weights(s, (wg_ref, wu_ref, wd_ref, wout_ref), (g2_ref, g2_ref, None, gmix_ref),
                      (wg_s, wu_s, wd_s, wout_s))

    @pl.when(s >= N_CAST)
    def _():
        mixed = jnp.concatenate([_unit_rms(a_ref[...].astype(F32)), _unit_rms(y_ref[...])], axis=1)
        h2 = h1_ref[...] + jnp.dot(mixed.astype(BF16), wout_s[...], preferred_element_type=F32)
        h3 = _swiglu_half(h2, wg_s, wu_s, wd_s)
        o_ref[...] = _rms(h3, gfin_ref[...])


def _resident(shape):
    zeros = (0,) * len(shape)
    return pl.BlockSpec(shape, lambda *_: zeros, pipeline_mode=pl.Buffered(1))


def _params():
    return pltpu.CompilerParams(dimension_semantics=("arbitrary",), vmem_limit_bytes=VMEM_LIMIT)


def _row(g):
    return g.astype(F32).reshape(1, -1)


def kernel(x, meta_tokens, ffn1_norm, ffn1_w_gate, ffn1_w_up, ffn1_w_down, mix_norm, w_in, q_norm, k_norm, sink, attn_out_norm, fourier_out_norm, w_out, ffn2_norm, ffn2_w_gate, ffn2_w_up, ffn2_w_down, final_norm):
    assert x.shape == (1, SEQ, D_MODEL) and x.dtype == F32
    x2 = x.reshape(SEQ, D_MODEL)
    meta_pad = jnp.pad(meta_tokens.astype(F32), ((META_ROW0, 0), (0, 0)))
    gqk = jnp.concatenate([jnp.tile(q_norm.astype(F32), N_HEADS) * (HEAD_DIM ** -0.5 * LOG2E),
                           jnp.tile(k_norm.astype(F32), N_KV)]).reshape(1, QK_W)
    bf = lambda w: w.astype(BF16)

    tile_idx = lambda s: jnp.maximum(s - N_CAST, 0)
    row_tile = lambda w: pl.BlockSpec((TM, w), lambda s: (tile_idx(s), 0))
    flat_rows = lambda w: pl.BlockSpec((TM, None, w), lambda s: (tile_idx(s), 0, 0))
    cast_rows = lambda r, c: pl.BlockSpec((r // N_CAST, c), lambda s: (jnp.minimum(s, N_CAST - 1), 0))
    gain_col = lambda g: g.astype(F32).reshape(-1, 1)
    bf16_copy = lambda r, c: pltpu.VMEM((r, c), BF16)
    ffn_in = [cast_rows(D_MODEL, D_FF), cast_rows(D_MODEL, D_FF), cast_rows(D_FF, D_MODEL)]
    ffn_scratch = [bf16_copy(D_MODEL, D_FF), bf16_copy(D_MODEL, D_FF), bf16_copy(D_FF, D_MODEL)]
    f32 = lambda w: w.astype(F32)

    h1, q, kv, f = pl.pallas_call(
        _pre_kernel,
        grid=(N_CAST + N_TILES,),
        in_specs=[row_tile(D_MODEL), _resident((TM, D_MODEL)), cast_rows(D_MODEL, 1), *ffn_in,
                  cast_rows(D_MODEL, 1), cast_rows(D_MODEL, IN_W), _resident((1, QK_W)),
                  _resident((256, 256))],
        out_specs=[row_tile(D_MODEL), row_tile(ATTN_W), row_tile(2 * KV_W), flat_rows(F_W)],
        out_shape=[jax.ShapeDtypeStruct((L, D_MODEL), F32), jax.ShapeDtypeStruct((L, ATTN_W), BF16),
                   jax.ShapeDtypeStruct((L, 2 * KV_W), BF16), jax.ShapeDtypeStruct((L, 1, F_W), F32)],
        scratch_shapes=ffn_scratch + [bf16_copy(D_MODEL, IN_W)],
        compiler_params=_params(),
        name="pre",
    )(x2, meta_pad, gain_col(ffn1_norm), f32(ffn1_w_gate), f32(ffn1_w_up), f32(ffn1_w_down), gain_col(mix_norm),
      f32(w_in), gqk, _ones_blockdiag())

    sink_rows = (jnp.repeat(sink.astype(F32).reshape(N_KV, Q_PER_KV), BQ, axis=1) * LOG2E)[:, None, :]

    q_rows = pl.BlockSpec((QBLK_PER_STEP * BQ, ATTN_W), lambda s: (s, 0))
    a_out = pl.pallas_call(
        _attn_kernel,
        grid=(pl.cdiv(N_QBLK + 1, QBLK_PER_STEP),),
        in_specs=[q_rows, _resident((L, 2 * KV_W)), _resident((4, N_KV, N_KEYS, Q_PER_KV * BQ)),
                  _resident((N_KV, 1, Q_PER_KV * BQ))],
        out_specs=q_rows,
        out_shape=jax.ShapeDtypeStruct((L, ATTN_W), BF16),
        compiler_params=_params(),
        name="attn",
    )(q, kv, jnp.asarray(_attn_bias_const()), sink_rows)

    row_set = lambda w: pl.BlockSpec((N1, B_PER_STEP, 1, w), lambda s: (0, s, 0, 0))
    g = pl.pallas_call(
        _dft_a_kernel,
        grid=(pl.cdiv(N2, B_PER_STEP),),
        in_specs=[row_set(F_W), _resident((2 * N1, N1)), _resident((256, 512))],
        out_specs=row_set(2 * F_W),
        out_shape=jax.ShapeDtypeStruct((N1, N2, 1, 2 * F_W), F32),
        scratch_shapes=[pltpu.VMEM((N1, F_W), F32), pltpu.VMEM((N1, 2 * F_W), F32)],
        compiler_params=_params(),
        name="dft_a",
    )(f.reshape(N1, N2, 1, F_W), bf(jnp.asarray(_stage_a_dft())), bf(jnp.asarray(_channel_dft())))

    ec, es = _stage_c_dft()
    y = pl.pallas_call(
        _dft_c_kernel,
        grid=(N1 // TC,),
        in_specs=[pl.BlockSpec((TC, N2, None, 2 * F_W), lambda i: (i, 0, 0, 0)),
                  pl.BlockSpec((TC, N2P, N2P), lambda i: (i, 0, 0)),
                  pl.BlockSpec((TC, N2P, N2P), lambda i: (i, 0, 0))],
        out_specs=pl.BlockSpec((N2, TC, 1, F_W), lambda i: (0, i, 0, 0)),
        out_shape=jax.ShapeDtypeStruct((N2, N1, 1, F_W), F32),
        scratch_shapes=[pltpu.VMEM((TC, N2P, 2 * F_W), F32)],
        compiler_params=_params(),
        name="dft_c",
    )(g, bf(jnp.asarray(ec)), bf(jnp.asarray(es)))

    out = pl.pallas_call(
        _post_kernel,
        grid=(N_CAST + N_TILES,),
        in_specs=[row_tile(D_MODEL), row_tile(ATTN_W), flat_rows(F_W), cast_rows(D_MODEL, 1),
                  cast_rows(D_MODEL, D_MODEL), cast_rows(D_MODEL, 1), *ffn_in, _resident((1, D_MODEL))],
        out_specs=row_tile(D_MODEL),
        out_shape=jax.ShapeDtypeStruct((SEQ, D_MODEL), F32),
        scratch_shapes=ffn_scratch + [bf16_copy(D_MODEL, D_MODEL)],
        compiler_params=_params(),
        name="post",
    )(h1, a_out, y.reshape(L, 1, F_W), gain_col(jnp.concatenate([attn_out_norm, fourier_out_norm])), f32(w_out),
      gain_col(ffn2_norm),
      f32(ffn2_w_gate), f32(ffn2_w_up), f32(ffn2_w_down), _row(final_norm))
    return out.reshape(1, SEQ, D_MODEL)
```

```python
import numpy as np
import jax
import jax.numpy as jnp
from jax import lax
from jax.experimental import pallas as pl
from jax.experimental.pallas import tpu as pltpu

F32 = jnp.float32
BF16 = jnp.bfloat16

D_MODEL = 1024
SEQ = 16384
N_META = 16
L = SEQ + N_META
HEAD_DIM = 64
N_HEADS = 8
N_KV = 2
Q_PER_KV = N_HEADS // N_KV
ATTN_W = N_HEADS * HEAD_DIM
KV_W = N_KV * HEAD_DIM
F_W = 512
F_GROUP = 64
IN_W = ATTN_W + 2 * KV_W + F_W
QK_W = ATTN_W + KV_W
WINDOW = 128
D_FF = 2816
N_CAST = 8
EPS = 1e-6
NEG = -1e30
LOG2E = 1.4426950408889634

TM = 656
N_TILES = L // TM
META_ROW0 = SEQ - (N_TILES - 1) * TM

BQ = 128
N_QBLK = SEQ // BQ
QBLK_PER_STEP = 12
BAND = BQ + 2 * WINDOW
N_KEYS = BAND + N_META

N1 = 400
N2 = 41
N2P = 48
K2P = 96
B_PER_STEP = 2
TCP = 25

VMEM_LIMIT = 56 * 1024 * 1024


def _ones_blockdiag():
    m = np.kron(np.eye(4), np.ones((HEAD_DIM, HEAD_DIM)))
    return jnp.asarray(m, BF16)


def _channel_dft():
    c = np.arange(F_GROUP)
    ang = 2.0 * np.pi * np.outer(c, c) / F_GROUP
    cos = np.kron(np.eye(4), np.cos(ang) / 8.0)
    sin = np.kron(np.eye(4), np.sin(ang) / 8.0)
    return np.concatenate([cos, sin], axis=1).astype(np.float32)


def _stage_a_dft():
    a = np.arange(N1, dtype=np.int64)
    m = (N2 * np.outer(a + N_META, a)) % L
    ang = 2.0 * np.pi * m / L
    return np.concatenate([np.cos(ang), -np.sin(ang)], axis=0).astype(np.float32)


def _stage_c_dft():
    c = np.arange(N1, dtype=np.int64)[:, None, None]
    d = np.arange(N2, dtype=np.int64)[None, :, None]
    b = np.arange(N2, dtype=np.int64)[None, None, :]
    m = ((c + N1 * d + N_META) * (b + N_META)) % L
    ang = 2.0 * np.pi * m / L
    ec = np.zeros((N1, N2P, K2P), np.float32)
    es = np.zeros((N1, N2P, K2P), np.float32)
    for e in range(2):
        ec[e::2, :N2, e:2 * N2:2] = (np.cos(ang) / np.sqrt(L))[e::2]
        es[e::2, :N2, e:2 * N2:2] = (np.sin(ang) / np.sqrt(L))[e::2]
    return ec, es


def _attn_bias_const():
    slopes = 2.0 ** -(np.arange(N_HEADS) + 1.0)
    i = np.arange(BQ)[:, None]
    j = np.arange(BAND)[None, :]
    m = np.arange(N_META)[None, :]
    out = np.full((4, N_KV, Q_PER_KV * BQ, N_KEYS), NEG, np.float64)
    for t in range(4):
        if t < 3:
            dist = np.abs(t * WINDOW + i - j)
            band_ok = dist <= WINDOW
            dist_m = np.minimum(t * WINDOW + N_META + i - m, WINDOW) if t == 0 else np.full((BQ, N_META), WINDOW)
        else:
            dist = N_META + j - i
            band_ok = (dist <= WINDOW) & (i < N_META)
            dist_m = np.abs(i - m)
        for h in range(N_HEADS):
            g, hl = divmod(h, Q_PER_KV)
            rows = slice(hl * BQ, (hl + 1) * BQ)
            band = np.where(band_ok, -slopes[h] * dist, NEG)
            if t == 3:
                band = np.where(i < N_META, band, 0.0)
            out[t, g, rows, :BAND] = band
            out[t, g, rows, BAND:BAND + N_META] = -slopes[h] * dist_m
    return np.ascontiguousarray(LOG2E * out.transpose(0, 1, 3, 2)).astype(np.float32)


def _unit_rms(x):
    return x * lax.rsqrt(jnp.mean(x * x, axis=-1, keepdims=True) + EPS)


def _rms(x, g):
    return _unit_rms(x) * g


def _swiglu_half(x, wg_s, wu_s, wd_s):
    xn = _unit_rms(x).astype(BF16)
    gate = jnp.dot(xn, wg_s[...], preferred_element_type=F32)
    up = jnp.dot(xn, wu_s[...], preferred_element_type=F32)
    act = (gate * jax.nn.sigmoid(gate) * up).astype(BF16)
    return x + 0.5 * jnp.dot(act, wd_s[...], preferred_element_type=F32)


def _cast_weights(s, f32_refs, gain_refs, bf16_scratch):
    for w_ref, g_ref, w_s in zip(f32_refs, gain_refs, bf16_scratch):
        rows = w_ref.shape[0]
        w = w_ref[...] if g_ref is None else w_ref[...] * g_ref[...]
        w_s[pl.ds(pl.multiple_of(s * rows, rows), rows), :] = w.astype(BF16)


def _pre_kernel(x_ref, meta_ref, g1_ref, wg_ref, wu_ref, wd_ref, gm_ref, win_ref, gqk_ref,
                ones_ref, h1_ref, q_ref, kv_ref, f_ref, wg_s, wu_s, wd_s, win_s):
    s = pl.program_id(0)

    @pl.when(s < N_CAST)
    def _():
        _cast_weights(s, (wg_ref, wu_ref, wd_ref, win_ref), (g1_ref, g1_ref, None, gm_ref),
                      (wg_s, wu_s, wd_s, win_s))

    @pl.when(s >= N_CAST)
    def _():
        _pre_tile(s - N_CAST, x_ref, meta_ref, gqk_ref, ones_ref,
                  h1_ref, q_ref, kv_ref, f_ref, wg_s, wu_s, wd_s, win_s)


def _pre_tile(i, x_ref, meta_ref, gqk_ref, ones_ref,
              h1_ref, q_ref, kv_ref, f_ref, wg_s, wu_s, wd_s, win_s):
    row = lax.broadcasted_iota(jnp.int32, (TM, 1), 0)
    is_meta = jnp.logical_and(i == N_TILES - 1, row >= META_ROW0)
    x = jnp.where(is_meta, meta_ref[...], x_ref[...])

    h1 = _swiglu_half(x, wg_s, wu_s, wd_s)
    h1_ref[...] = h1

    u = jnp.dot(_unit_rms(h1).astype(BF16), win_s[...], preferred_element_type=F32)

    qk = u[:, :QK_W]
    sq = (qk * qk).astype(BF16)
    ones = ones_ref[...]
    ss = jnp.concatenate([
        jnp.dot(sq[:, 0:256], ones, preferred_element_type=F32),
        jnp.dot(sq[:, 256:512], ones, preferred_element_type=F32),
        jnp.dot(sq[:, 512:640], ones[:KV_W, :KV_W], preferred_element_type=F32)], axis=1)
    qkn = qk * lax.rsqrt(ss * (1.0 / HEAD_DIM) + EPS) * gqk_ref[...]
    q_ref[...] = qkn[:, :ATTN_W].astype(BF16)
    kv_ref[...] = jnp.concatenate([qkn[:, ATTN_W:], u[:, QK_W:QK_W + KV_W]], axis=1).astype(BF16)

    f_ref[...] = u[:, QK_W + KV_W:]


def _attn_kernel(q_ref, kv_ref, bias_ref, sink_ref, o_ref):
    step = pl.program_id(0)
    kv_meta = kv_ref[SEQ:L, :]

    units = []
    for j in range(QBLK_PER_STEP):
        n = step * QBLK_PER_STEP + j
        btype = jnp.where(n == 0, 0, jnp.where(n == N_QBLK - 1, 2, jnp.where(n == N_QBLK, 3, 1)))
        start = jnp.where(n == N_QBLK, 0, jnp.clip((n - 1) * BQ, 0, SEQ - BAND))
        start = pl.multiple_of(start, BQ)
        kv_cat = jnp.concatenate([kv_ref[pl.ds(start, BAND), :], kv_meta], axis=0)
        q_ok = lax.broadcasted_iota(jnp.int32, (BQ, 1), 0) < L - n * BQ
        units.extend((j, g, btype, kv_cat, q_ok) for g in range(N_KV))

    def scores(unit):
        j, g, _, kv_cat, q_ok = unit
        q = jnp.where(q_ok, q_ref[j * BQ:(j + 1) * BQ, :], 0)
        qg = jnp.concatenate(
            [q[:, (Q_PER_KV * g + hl) * HEAD_DIM:(Q_PER_KV * g + hl + 1) * HEAD_DIM]
             for hl in range(Q_PER_KV)], axis=0)
        k = kv_cat[:, g * HEAD_DIM:(g + 1) * HEAD_DIM]
        return lax.dot_general(k, qg, (((1,), (1,)), ((), ())), preferred_element_type=F32)

    st_next = scores(units[0])
    for u, (j, g, btype, kv_cat, _) in enumerate(units):
        st = st_next + bias_ref[btype, g]
        sink = sink_ref[g]
        m = jnp.maximum(jnp.max(st, axis=0, keepdims=True), sink)
        p = jnp.exp2(st - m)
        denom = jnp.sum(p, axis=0, keepdims=True) + jnp.exp2(sink - m)
        if u + 1 < len(units):
            st_next = scores(units[u + 1])
            p = jnp.concatenate([jnp.maximum(p[:16], st_next[:16] * 0.0 - 1.0), p[16:]], axis=0)
        v = kv_cat[:, KV_W + g * HEAD_DIM:KV_W + (g + 1) * HEAD_DIM]
        ot = lax.dot_general(v, p.astype(BF16), (((0,), (0,)), ((), ())), preferred_element_type=F32)
        ot = ot / denom
        for hp in range(Q_PER_KV // 2):
            h0 = Q_PER_KV * g + 2 * hp
            two = jnp.concatenate([ot[:, 2 * hp * BQ:(2 * hp + 1) * BQ],
                                   ot[:, (2 * hp + 1) * BQ:(2 * hp + 2) * BQ]], axis=0)
            o_ref[j * BQ:(j + 1) * BQ, h0 * HEAD_DIM:(h0 + 2) * HEAD_DIM] = two.T.astype(BF16)


def _dft_a_kernel(f_ref, wa_ref, cdft_ref, g_ref, fs_ref, gs_ref):
    cdft = cdft_ref[...]
    for k in range(B_PER_STEP):
        fs_ref[...] = f_ref[:, k, 0, :]
        h = jnp.dot(wa_ref[...], fs_ref[...].astype(BF16), preferred_element_type=F32).astype(BF16)
        for half in range(2):
            p = jnp.dot(h[:, half * 256:(half + 1) * 256], cdft, preferred_element_type=F32)
            gs_ref[:, half * 256:(half + 1) * 256] = p[:N1, :256] + p[N1:, 256:]
            gs_ref[:, F_W + half * 256:F_W + (half + 1) * 256] = p[N1:, :256] - p[:N1, 256:]
        g_ref[:, k, :, :] = gs_ref[...].astype(BF16).reshape(N1 // 2, 2, 2 * F_W)


def _dft_c_kernel(g_ref, ec_ref, es_ref, y_ref, gs_ref):
    gs_ref[:, 2 * N2:, :] = jnp.zeros((TCP, K2P - 2 * N2, 2 * F_W), BF16)
    for i in range(TCP):
        gs_ref[i, :2 * N2, :] = g_ref[i].reshape(2 * N2, 2 * F_W)
        g = gs_ref[i]
        for e in range(2):
            c = 2 * i + e
            y = (jnp.dot(ec_ref[c], g[:, :F_W], preferred_element_type=F32)
                 + jnp.dot(es_ref[c], g[:, F_W:], preferred_element_type=F32))
            y_ref[:, c, 0, :] = y[:N2]


def _post_kernel(h1_ref, a_ref, y_ref, gmix_ref, wout_ref, g2_ref, wg_ref, wu_ref, wd_ref,
                 gfin_ref, o_ref, wg_s, wu_s, wd_s, wout_s):
    s = pl.program_id(0)

    @pl.when(s < N_CAST)
    def _():
        _cast_weights(s, (wg_ref, wu_ref, wd_ref, wout_ref), (g2_ref, g2_ref, None, gmix_ref),
                      (wg_s, wu_s, wd_s, wout_s))

    @pl.when(s >= N_CAST)
    def _():
        mixed = jnp.concatenate([_unit_rms(a_ref[...].astype(F32)), _unit_rms(y_ref[...])], axis=1)
        h2 = h1_ref[...] + jnp.dot(mixed.astype(BF16), wout_s[...], preferred_element_type=F32)
        h3 = _swiglu_half(h2, wg_s, wu_s, wd_s)
        o_ref[...] = _rms(h3, gfin_ref[...])


def _resident(shape):
    zeros = (0,) * len(shape)
    return pl.BlockSpec(shape, lambda *_: zeros, pipeline_mode=pl.Buffered(1))


def _params():
    return pltpu.CompilerParams(dimension_semantics=("arbitrary",), vmem_limit_bytes=VMEM_LIMIT)


def _row(g):
    return g.astype(F32).reshape(1, -1)


def kernel(x, meta_tokens, ffn1_norm, ffn1_w_gate, ffn1_w_up, ffn1_w_down, mix_norm, w_in, q_norm, k_norm, sink, attn_out_norm, fourier_out_norm, w_out, ffn2_norm, ffn2_w_gate, ffn2_w_up, ffn2_w_down, final_norm):
    assert x.shape == (1, SEQ, D_MODEL) and x.dtype == F32
    x2 = x.reshape(SEQ, D_MODEL)
    meta_pad = jnp.pad(meta_tokens.astype(F32), ((META_ROW0, 0), (0, 0)))
    gqk = jnp.concatenate([jnp.tile(q_norm.astype(F32), N_HEADS) * (HEAD_DIM ** -0.5 * LOG2E),
                           jnp.tile(k_norm.astype(F32), N_KV)]).reshape(1, QK_W)
    bf = lambda w: w.astype(BF16)

    tile_idx = lambda s: jnp.maximum(s - N_CAST, 0)
    row_tile = lambda w: pl.BlockSpec((TM, w), lambda s: (tile_idx(s), 0))
    flat_rows = lambda w: pl.BlockSpec((TM, None, w), lambda s: (tile_idx(s), 0, 0))
    cast_rows = lambda r, c: pl.BlockSpec((r // N_CAST, c), lambda s: (jnp.minimum(s, N_CAST - 1), 0))
    gain_col = lambda g: g.astype(F32).reshape(-1, 1)
    bf16_copy = lambda r, c: pltpu.VMEM((r, c), BF16)
    ffn_in = [cast_rows(D_MODEL, D_FF), cast_rows(D_MODEL, D_FF), cast_rows(D_FF, D_MODEL)]
    ffn_scratch = [bf16_copy(D_MODEL, D_FF), bf16_copy(D_MODEL, D_FF), bf16_copy(D_FF, D_MODEL)]
    f32 = lambda w: w.astype(F32)

    h1, q, kv, f = pl.pallas_call(
        _pre_kernel,
        grid=(N_CAST + N_TILES,),
        in_specs=[row_tile(D_MODEL), _resident((TM, D_MODEL)), cast_rows(D_MODEL, 1), *ffn_in,
                  cast_rows(D_MODEL, 1), cast_rows(D_MODEL, IN_W), _resident((1, QK_W)),
                  _resident((256, 256))],
        out_specs=[row_tile(D_MODEL), row_tile(ATTN_W), row_tile(2 * KV_W), flat_rows(F_W)],
        out_shape=[jax.ShapeDtypeStruct((L, D_MODEL), F32), jax.ShapeDtypeStruct((L, ATTN_W), BF16),
                   jax.ShapeDtypeStruct((L, 2 * KV_W), BF16), jax.ShapeDtypeStruct((L, 1, F_W), F32)],
        scratch_shapes=ffn_scratch + [bf16_copy(D_MODEL, IN_W)],
        compiler_params=_params(),
        name="pre",
    )(x2, meta_pad, gain_col(ffn1_norm), f32(ffn1_w_gate), f32(ffn1_w_up), f32(ffn1_w_down), gain_col(mix_norm),
      f32(w_in), gqk, _ones_blockdiag())

    sink_rows = (jnp.repeat(sink.astype(F32).reshape(N_KV, Q_PER_KV), BQ, axis=1) * LOG2E)[:, None, :]

    q_rows = pl.BlockSpec((QBLK_PER_STEP * BQ, ATTN_W), lambda s: (s, 0))
    a_out = pl.pallas_call(
        _attn_kernel,
        grid=(pl.cdiv(N_QBLK + 1, QBLK_PER_STEP),),
        in_specs=[q_rows, _resident((L, 2 * KV_W)), _resident((4, N_KV, N_KEYS, Q_PER_KV * BQ)),
                  _resident((N_KV, 1, Q_PER_KV * BQ))],
        out_specs=q_rows,
        out_shape=jax.ShapeDtypeStruct((L, ATTN_W), BF16),
        compiler_params=_params(),
        name="attn",
    )(q, kv, jnp.asarray(_attn_bias_const()), sink_rows)

    row_set = lambda w: pl.BlockSpec((N1, B_PER_STEP, 1, w), lambda s: (0, s, 0, 0))
    g = pl.pallas_call(
        _dft_a_kernel,
        grid=(pl.cdiv(N2, B_PER_STEP),),
        in_specs=[row_set(F_W), _resident((2 * N1, N1)), _resident((256, 512))],
        out_specs=pl.BlockSpec((N1 // 2, B_PER_STEP, 2, 2 * F_W), lambda s: (0, s, 0, 0)),
        out_shape=jax.ShapeDtypeStruct((N1 // 2, N2, 2, 2 * F_W), BF16),
        scratch_shapes=[pltpu.VMEM((N1, F_W), F32), pltpu.VMEM((N1, 2 * F_W), F32)],
        compiler_params=_params(),
        name="dft_a",
    )(f.reshape(N1, N2, 1, F_W), bf(jnp.asarray(_stage_a_dft())), bf(jnp.asarray(_channel_dft())))

    ec, es = _stage_c_dft()
    y = pl.pallas_call(
        _dft_c_kernel,
        grid=(N1 // (2 * TCP),),
        in_specs=[pl.BlockSpec((TCP, N2, 2, 2 * F_W), lambda i: (i, 0, 0, 0)),
                  pl.BlockSpec((2 * TCP, N2P, K2P), lambda i: (i, 0, 0)),
                  pl.BlockSpec((2 * TCP, N2P, K2P), lambda i: (i, 0, 0))],
        out_specs=pl.BlockSpec((N2, 2 * TCP, 1, F_W), lambda i: (0, i, 0, 0)),
        out_shape=jax.ShapeDtypeStruct((N2, N1, 1, F_W), F32),
        scratch_shapes=[pltpu.VMEM((TCP, K2P, 2 * F_W), BF16)],
        compiler_params=_params(),
        name="dft_c",
    )(g, bf(jnp.asarray(ec)), bf(jnp.asarray(es)))

    out = pl.pallas_call(
        _post_kernel,
        grid=(N_CAST + N_TILES,),
        in_specs=[row_tile(D_MODEL), row_tile(ATTN_W), flat_rows(F_W), cast_rows(D_MODEL, 1),
                  cast_rows(D_MODEL, D_MODEL), cast_rows(D_MODEL, 1), *ffn_in, _resident((1, D_MODEL))],
        out_specs=row_tile(D_MODEL),
        out_shape=jax.ShapeDtypeStruct((SEQ, D_MODEL), F32),
        scratch_shapes=ffn_scratch + [bf16_copy(D_MODEL, D_MODEL)],
        compiler_params=_params(),
        name="post",
    )(h1, a_out, y.reshape(L, 1, F_W), gain_col(jnp.concatenate([attn_out_norm, fourier_out_norm])), f32(w_out),
      gain_col(ffn2_norm),
      f32(ffn2_w_gate), f32(ffn2_w_up), f32(ffn2_w_down), _row(final_norm))
    return out.reshape(1, SEQ, D_MODEL)
```

```python
import numpy as np
import jax
import jax.numpy as jnp
from jax import lax
from jax.experimental import pallas as pl
from jax.experimental.pallas import tpu as pltpu

F32 = jnp.float32
BF16 = jnp.bfloat16

D_MODEL = 1024
SEQ = 16384
N_META = 16
L = SEQ + N_META
HEAD_DIM = 64
N_HEADS = 8
N_KV = 2
Q_PER_KV = N_HEADS // N_KV
ATTN_W = N_HEADS * HEAD_DIM
KV_W = N_KV * HEAD_DIM
F_W = 512
F_GROUP = 64
IN_W = ATTN_W + 2 * KV_W + F_W
QK_W = ATTN_W + KV_W
WINDOW = 128
D_FF = 2816
N_CAST = 8
EPS = 1e-6
NEG = -1e30
LOG2E = 1.4426950408889634

TM = 656
N_TILES = L // TM
META_ROW0 = SEQ - (N_TILES - 1) * TM

BQ = 128
N_QBLK = SEQ // BQ
QBLK_PER_STEP = 12
BAND = BQ + 2 * WINDOW
N_KEYS = BAND + N_META

N1 = 400
N2 = 41
K2P = 96
B_PER_STEP = 2
TCP = 25

VMEM_LIMIT = 56 * 1024 * 1024


def _ones_blockdiag():
    m = np.kron(np.eye(4), np.ones((HEAD_DIM, HEAD_DIM)))
    return jnp.asarray(m, BF16)


def _channel_dft():
    c = np.arange(F_GROUP)
    ang = 2.0 * np.pi * np.outer(c, c) / F_GROUP
    cos = np.kron(np.eye(4), np.cos(ang) / 8.0)
    sin = np.kron(np.eye(4), np.sin(ang) / 8.0)
    return np.concatenate([cos, sin], axis=1).astype(np.float32)


def _stage_a_dft():
    a = np.arange(N1, dtype=np.int64)
    m = (N2 * np.outer(a + N_META, a)) % L
    ang = 2.0 * np.pi * m / L
    return np.concatenate([np.cos(ang), -np.sin(ang)], axis=0).astype(np.float32)


def _stage_c_dft():
    c = np.arange(N1, dtype=np.int64)[:, None, None]
    d = np.arange(N2, dtype=np.int64)[None, :, None]
    b = np.arange(N2, dtype=np.int64)[None, None, :]
    m = ((c + N1 * d + N_META) * (b + N_META)) % L
    ang = 2.0 * np.pi * m / L
    ec = np.zeros((N1 // 2, K2P, K2P), np.float32)
    es = np.zeros((N1 // 2, K2P, K2P), np.float32)
    for e in range(2):
        ec[:, e:2 * N2:2, e:2 * N2:2] = (np.cos(ang) / np.sqrt(L))[e::2]
        es[:, e:2 * N2:2, e:2 * N2:2] = (np.sin(ang) / np.sqrt(L))[e::2]
    return ec, es


def _attn_bias_const():
    slopes = 2.0 ** -(np.arange(N_HEADS) + 1.0)
    i = np.arange(BQ)[:, None]
    j = np.arange(BAND)[None, :]
    m = np.arange(N_META)[None, :]
    out = np.full((4, N_KV, Q_PER_KV * BQ, N_KEYS), NEG, np.float64)
    for t in range(4):
        if t < 3:
            dist = np.abs(t * WINDOW + i - j)
            band_ok = dist <= WINDOW
            dist_m = np.minimum(t * WINDOW + N_META + i - m, WINDOW) if t == 0 else np.full((BQ, N_META), WINDOW)
        else:
            dist = N_META + j - i
            band_ok = (dist <= WINDOW) & (i < N_META)
            dist_m = np.abs(i - m)
        for h in range(N_HEADS):
            g, hl = divmod(h, Q_PER_KV)
            rows = slice(hl * BQ, (hl + 1) * BQ)
            band = np.where(band_ok, -slopes[h] * dist, NEG)
            if t == 3:
                band = np.where(i < N_META, band, 0.0)
            out[t, g, rows, :BAND] = band
            out[t, g, rows, BAND:BAND + N_META] = -slopes[h] * dist_m
    return np.ascontiguousarray(LOG2E * out.transpose(0, 1, 3, 2)).astype(np.float32)


def _unit_rms(x):
    return x * lax.rsqrt(jnp.mean(x * x, axis=-1, keepdims=True) + EPS)


def _rms(x, g):
    return _unit_rms(x) * g


def _swiglu_half(x, wg_s, wu_s, wd_s):
    xn = _unit_rms(x).astype(BF16)
    gate = jnp.dot(xn, wg_s[...], preferred_element_type=F32)
    up = jnp.dot(xn, wu_s[...], preferred_element_type=F32)
    act = (gate * jax.nn.sigmoid(gate) * up).astype(BF16)
    return x + 0.5 * jnp.dot(act, wd_s[...], preferred_element_type=F32)


def _cast_weights(s, f32_refs, gain_refs, bf16_scratch):
    for w_ref, g_ref, w_s in zip(f32_refs, gain_refs, bf16_scratch):
        rows = w_ref.shape[0]
        w = w_ref[...] if g_ref is None else w_ref[...] * g_ref[...]
        w_s[pl.ds(pl.multiple_of(s * rows, rows), rows), :] = w.astype(BF16)


def _pre_kernel(x_ref, meta_ref, g1_ref, wg_ref, wu_ref, wd_ref, gm_ref, win_ref, gqk_ref,
                ones_ref, h1_ref, q_ref, kv_ref, f_ref, wg_s, wu_s, wd_s, win_s):
    s = pl.program_id(0)

    @pl.when(s < N_CAST)
    def _():
        _cast_weights(s, (wg_ref, wu_ref, wd_ref, win_ref), (g1_ref, g1_ref, None, gm_ref),
                      (wg_s, wu_s, wd_s, win_s))

    @pl.when(s >= N_CAST)
    def _():
        _pre_tile(s - N_CAST, x_ref, meta_ref, gqk_ref, ones_ref,
                  h1_ref, q_ref, kv_ref, f_ref, wg_s, wu_s, wd_s, win_s)


def _pre_tile(i, x_ref, meta_ref, gqk_ref, ones_ref,
              h1_ref, q_ref, kv_ref, f_ref, wg_s, wu_s, wd_s, win_s):
    row = lax.broadcasted_iota(jnp.int32, (TM, 1), 0)
    is_meta = jnp.logical_and(i == N_TILES - 1, row >= META_ROW0)
    x = jnp.where(is_meta, meta_ref[...], x_ref[...])

    h1 = _swiglu_half(x, wg_s, wu_s, wd_s)
    h1_ref[...] = h1

    u = jnp.dot(_unit_rms(h1).astype(BF16), win_s[...], preferred_element_type=F32)

    qk = u[:, :QK_W]
    sq = (qk * qk).astype(BF16)
    ones = ones_ref[...]
    ss = jnp.concatenate([
        jnp.dot(sq[:, 0:256], ones, preferred_element_type=F32),
        jnp.dot(sq[:, 256:512], ones, preferred_element_type=F32),
        jnp.dot(sq[:, 512:640], ones[:KV_W, :KV_W], preferred_element_type=F32)], axis=1)
    qkn = qk * lax.rsqrt(ss * (1.0 / HEAD_DIM) + EPS) * gqk_ref[...]
    q_ref[...] = qkn[:, :ATTN_W].astype(BF16)
    kv_ref[...] = jnp.concatenate([qkn[:, ATTN_W:], u[:, QK_W:QK_W + KV_W]], axis=1).astype(BF16)

    f_ref[...] = u[:, QK_W + KV_W:]


def _attn_kernel(q_ref, kv_ref, bias_ref, sink_ref, o_ref):
    step = pl.program_id(0)
    kv_meta = kv_ref[SEQ:L, :]

    units = []
    for j in range(QBLK_PER_STEP):
        n = step * QBLK_PER_STEP + j
        btype = jnp.where(n == 0, 0, jnp.where(n == N_QBLK - 1, 2, jnp.where(n == N_QBLK, 3, 1)))
        start = jnp.where(n == N_QBLK, 0, jnp.clip((n - 1) * BQ, 0, SEQ - BAND))
        start = pl.multiple_of(start, BQ)
        kv_cat = jnp.concatenate([kv_ref[pl.ds(start, BAND), :], kv_meta], axis=0)
        q_ok = lax.broadcasted_iota(jnp.int32, (BQ, 1), 0) < L - n * BQ
        units.extend((j, g, btype, kv_cat, q_ok) for g in range(N_KV))

    def scores(unit):
        j, g, _, kv_cat, q_ok = unit
        q = jnp.where(q_ok, q_ref[j * BQ:(j + 1) * BQ, :], 0)
        qg = jnp.concatenate(
            [q[:, (Q_PER_KV * g + hl) * HEAD_DIM:(Q_PER_KV * g + hl + 1) * HEAD_DIM]
             for hl in range(Q_PER_KV)], axis=0)
        k = kv_cat[:, g * HEAD_DIM:(g + 1) * HEAD_DIM]
        return lax.dot_general(k, qg, (((1,), (1,)), ((), ())), preferred_element_type=F32)

    st_next = scores(units[0])
    for u, (j, g, btype, kv_cat, _) in enumerate(units):
        st = st_next + bias_ref[btype, g]
        sink = sink_ref[g]
        m = jnp.maximum(jnp.max(st, axis=0, keepdims=True), sink)
        p = jnp.exp2(st - m)
        denom = jnp.sum(p, axis=0, keepdims=True) + jnp.exp2(sink - m)
        if u + 1 < len(units):
            st_next = scores(units[u + 1])
            p = jnp.concatenate([jnp.maximum(p[:16], st_next[:16] * 0.0 - 1.0), p[16:]], axis=0)
        v = kv_cat[:, KV_W + g * HEAD_DIM:KV_W + (g + 1) * HEAD_DIM]
        ot = lax.dot_general(v, p.astype(BF16), (((0,), (0,)), ((), ())), preferred_element_type=F32)
        ot = ot / denom
        for hp in range(Q_PER_KV // 2):
            h0 = Q_PER_KV * g + 2 * hp
            two = jnp.concatenate([ot[:, 2 * hp * BQ:(2 * hp + 1) * BQ],
                                   ot[:, (2 * hp + 1) * BQ:(2 * hp + 2) * BQ]], axis=0)
            o_ref[j * BQ:(j + 1) * BQ, h0 * HEAD_DIM:(h0 + 2) * HEAD_DIM] = two.T.astype(BF16)


def _dft_a_kernel(f_ref, wa_ref, cdft_ref, g_ref, fs_ref, gs_ref):
    cdft = cdft_ref[...]
    for k in range(B_PER_STEP):
        fs_ref[...] = f_ref[:, k, 0, :]
        h = jnp.dot(wa_ref[...], fs_ref[...].astype(BF16), preferred_element_type=F32).astype(BF16)
        for half in range(2):
            p = jnp.dot(h[:, half * 256:(half + 1) * 256], cdft, preferred_element_type=F32)
            gs_ref[:, half * 256:(half + 1) * 256] = p[:N1, :256] + p[N1:, 256:]
            gs_ref[:, F_W + half * 256:F_W + (half + 1) * 256] = p[N1:, :256] - p[:N1, 256:]
        g_ref[:, k, :, :] = gs_ref[...].astype(BF16).reshape(N1 // 2, 2, 2 * F_W)


def _dft_c_kernel(g_ref, ec_ref, es_ref, y_ref, gs_ref):
    gs_ref[:, 2 * N2:, :] = jnp.zeros((TCP, K2P - 2 * N2, 2 * F_W), BF16)
    for i in range(TCP):
        gs_ref[i, :2 * N2, :] = g_ref[i].reshape(2 * N2, 2 * F_W)
        g = gs_ref[i]
        y = (jnp.dot(ec_ref[i], g[:, :F_W], preferred_element_type=F32)
             + jnp.dot(es_ref[i], g[:, F_W:], preferred_element_type=F32))
        y_ref[:, i, :, :] = y[:2 * N2].astype(BF16).reshape(N2, 2, F_W)


def _post_kernel(h1_ref, a_ref, y_ref, gmix_ref, wout_ref, g2_ref, wg_ref, wu_ref, wd_ref,
                 gfin_ref, o_ref, wg_s, wu_s, wd_s, wout_s):
    s = pl.program_id(0)

    @pl.when(s < N_CAST)
    def _():
        _cast_weights(s, (wg_ref, wu_ref, wd_ref, wout_ref), (g2_ref, g2_ref, None, gmix_ref),
                      (wg_s, wu_s, wd_s, wout_s))

    @pl.when(s >= N_CAST)
    def _():
        y = y_ref[...].reshape(TM, F_W).astype(F32)
        mixed = jnp.concatenate([_unit_rms(a_ref[...].astype(F32)), _unit_rms(y)], axis=1)
        h2 = h1_ref[...] + jnp.dot(mixed.astype(BF16), wout_s[...], preferred_element_type=F32)
        h3 = _swiglu_half(h2, wg_s, wu_s, wd_s)
        o_ref[...] = _rms(h3, gfin_ref[...])


def _resident(shape):
    zeros = (0,) * len(shape)
    return pl.BlockSpec(shape, lambda *_: zeros, pipeline_mode=pl.Buffered(1))


def _params():
    return pltpu.CompilerParams(dimension_semantics=("arbitrary",), vmem_limit_bytes=VMEM_LIMIT)


def _row(g):
    return g.astype(F32).reshape(1, -1)


def kernel(x, meta_tokens, ffn1_norm, ffn1_w_gate, ffn1_w_up, ffn1_w_down, mix_norm, w_in, q_norm, k_norm, sink, attn_out_norm, fourier_out_norm, w_out, ffn2_norm, ffn2_w_gate, ffn2_w_up, ffn2_w_down, final_norm):
    assert x.shape == (1, SEQ, D_MODEL) and x.dtype == F32
    x2 = x.reshape(SEQ, D_MODEL)
    meta_pad = jnp.pad(meta_tokens.astype(F32), ((META_ROW0, 0), (0, 0)))
    gqk = jnp.concatenate([jnp.tile(q_norm.astype(F32), N_HEADS) * (HEAD_DIM ** -0.5 * LOG2E),
                           jnp.tile(k_norm.astype(F32), N_KV)]).reshape(1, QK_W)
    bf = lambda w: w.astype(BF16)

    tile_idx = lambda s: jnp.maximum(s - N_CAST, 0)
    row_tile = lambda w: pl.BlockSpec((TM, w), lambda s: (tile_idx(s), 0))
    flat_rows = lambda w: pl.BlockSpec((TM, None, w), lambda s: (tile_idx(s), 0, 0))
    cast_rows = lambda r, c: pl.BlockSpec((r // N_CAST, c), lambda s: (jnp.minimum(s, N_CAST - 1), 0))
    gain_col = lambda g: g.astype(F32).reshape(-1, 1)
    bf16_copy = lambda r, c: pltpu.VMEM((r, c), BF16)
    ffn_in = [cast_rows(D_MODEL, D_FF), cast_rows(D_MODEL, D_FF), cast_rows(D_FF, D_MODEL)]
    ffn_scratch = [bf16_copy(D_MODEL, D_FF), bf16_copy(D_MODEL, D_FF), bf16_copy(D_FF, D_MODEL)]
    f32 = lambda w: w.astype(F32)

    h1, q, kv, f = pl.pallas_call(
        _pre_kernel,
        grid=(N_CAST + N_TILES,),
        in_specs=[row_tile(D_MODEL), _resident((TM, D_MODEL)), cast_rows(D_MODEL, 1), *ffn_in,
                  cast_rows(D_MODEL, 1), cast_rows(D_MODEL, IN_W), _resident((1, QK_W)),
                  _resident((256, 256))],
        out_specs=[row_tile(D_MODEL), row_tile(ATTN_W), row_tile(2 * KV_W), flat_rows(F_W)],
        out_shape=[jax.ShapeDtypeStruct((L, D_MODEL), F32), jax.ShapeDtypeStruct((L, ATTN_W), BF16),
                   jax.ShapeDtypeStruct((L, 2 * KV_W), BF16), jax.ShapeDtypeStruct((L, 1, F_W), F32)],
        scratch_shapes=ffn_scratch + [bf16_copy(D_MODEL, IN_W)],
        compiler_params=_params(),
        name="pre",
    )(x2, meta_pad, gain_col(ffn1_norm), f32(ffn1_w_gate), f32(ffn1_w_up), f32(ffn1_w_down), gain_col(mix_norm),
      f32(w_in), gqk, _ones_blockdiag())

    sink_rows = (jnp.repeat(sink.astype(F32).reshape(N_KV, Q_PER_KV), BQ, axis=1) * LOG2E)[:, None, :]

    q_rows = pl.BlockSpec((QBLK_PER_STEP * BQ, ATTN_W), lambda s: (s, 0))
    a_out = pl.pallas_call(
        _attn_kernel,
        grid=(pl.cdiv(N_QBLK + 1, QBLK_PER_STEP),),
        in_specs=[q_rows, _resident((L, 2 * KV_W)), _resident((4, N_KV, N_KEYS, Q_PER_KV * BQ)),
                  _resident((N_KV, 1, Q_PER_KV * BQ))],
        out_specs=q_rows,
        out_shape=jax.ShapeDtypeStruct((L, ATTN_W), BF16),
        compiler_params=_params(),
        name="attn",
    )(q, kv, jnp.asarray(_attn_bias_const()), sink_rows)

    row_set = lambda w: pl.BlockSpec((N1, B_PER_STEP, 1, w), lambda s: (0, s, 0, 0))
    g = pl.pallas_call(
        _dft_a_kernel,
        grid=(pl.cdiv(N2, B_PER_STEP),),
        in_specs=[row_set(F_W), _resident((2 * N1, N1)), _resident((256, 512))],
        out_specs=pl.BlockSpec((N1 // 2, B_PER_STEP, 2, 2 * F_W), lambda s: (0, s, 0, 0)),
        out_shape=jax.ShapeDtypeStruct((N1 // 2, N2, 2, 2 * F_W), BF16),
        scratch_shapes=[pltpu.VMEM((N1, F_W), F32), pltpu.VMEM((N1, 2 * F_W), F32)],
        compiler_params=_params(),
        name="dft_a",
    )(f.reshape(N1, N2, 1, F_W), bf(jnp.asarray(_stage_a_dft())), bf(jnp.asarray(_channel_dft())))

    ec, es = _stage_c_dft()
    y = pl.pallas_call(
        _dft_c_kernel,
        grid=(N1 // (2 * TCP),),
        in_specs=[pl.BlockSpec((TCP, N2, 2, 2 * F_W), lambda i: (i, 0, 0, 0)),
                  pl.BlockSpec((TCP, K2P, K2P), lambda i: (i, 0, 0)),
                  pl.BlockSpec((TCP, K2P, K2P), lambda i: (i, 0, 0))],
        out_specs=pl.BlockSpec((N2, TCP, 2, F_W), lambda i: (0, i, 0, 0)),
        out_shape=jax.ShapeDtypeStruct((N2, N1 // 2, 2, F_W), BF16),
        scratch_shapes=[pltpu.VMEM((TCP, K2P, 2 * F_W), BF16)],
        compiler_params=_params(),
        name="dft_c",
    )(g, bf(jnp.asarray(ec)), bf(jnp.asarray(es)))

    out = pl.pallas_call(
        _post_kernel,
        grid=(N_CAST + N_TILES,),
        in_specs=[row_tile(D_MODEL), row_tile(ATTN_W),
                  pl.BlockSpec((TM // 2, 2, F_W), lambda s: (tile_idx(s), 0, 0)), cast_rows(D_MODEL, 1),
                  cast_rows(D_MODEL, D_MODEL), cast_rows(D_MODEL, 1), *ffn_in, _resident((1, D_MODEL))],
        out_specs=row_tile(D_MODEL),
        out_shape=jax.ShapeDtypeStruct((SEQ, D_MODEL), F32),
        scratch_shapes=ffn_scratch + [bf16_copy(D_MODEL, D_MODEL)],
        compiler_params=_params(),
        name="post",
    )(h1, a_out, y.reshape(L // 2, 2, F_W), gain_col(jnp.concatenate([attn_out_norm, fourier_out_norm])), f32(w_out),
      gain_col(ffn2_norm),
      f32(ffn2_w_gate), f32(ffn2_w_up), f32(ffn2_w_down), _row(final_norm))
    return out.reshape(1, SEQ, D_MODEL)
```

```python
import numpy as np
import jax
import jax.numpy as jnp
from jax import lax
from jax.experimental import pallas as pl
from jax.experimental.pallas import tpu as pltpu

F32 = jnp.float32
BF16 = jnp.bfloat16

D_MODEL = 1024
SEQ = 16384
N_META = 16
L = SEQ + N_META
HEAD_DIM = 64
N_HEADS = 8
N_KV = 2
Q_PER_KV = N_HEADS // N_KV
ATTN_W = N_HEADS * HEAD_DIM
KV_W = N_KV * HEAD_DIM
F_W = 512
F_GROUP = 64
IN_W = ATTN_W + 2 * KV_W + F_W
QK_W = ATTN_W + KV_W
WINDOW = 128
D_FF = 2816
N_CAST = 8
EPS = 1e-6
NEG = -1e30
LOG2E = 1.4426950408889634

TM = 656
N_TILES = L // TM
META_ROW0 = SEQ - (N_TILES - 1) * TM

BQ = 128
N_QBLK = SEQ // BQ
QBLK_PER_STEP = 12
BAND = BQ + 2 * WINDOW
N_KEYS = BAND + N_META

N1 = 200
N2 = 82
K2P = 176
B_PER_STEP = 4
TCP = 25

VMEM_LIMIT = 56 * 1024 * 1024


def _ones_blockdiag():
    m = np.kron(np.eye(4), np.ones((HEAD_DIM, HEAD_DIM)))
    return jnp.asarray(m, BF16)


def _channel_dft():
    c = np.arange(F_GROUP)
    ang = 2.0 * np.pi * np.outer(c, c) / F_GROUP
    cos = np.kron(np.eye(4), np.cos(ang) / 8.0)
    sin = np.kron(np.eye(4), np.sin(ang) / 8.0)
    return np.concatenate([cos, sin], axis=1).astype(np.float32)


def _stage_a_dft():
    a = np.arange(N1, dtype=np.int64)
    m = (N2 * np.outer(a + N_META, a)) % L
    ang = 2.0 * np.pi * m / L
    return np.concatenate([np.cos(ang), -np.sin(ang)], axis=0).astype(np.float32)


def _stage_c_dft():
    c = np.arange(N1, dtype=np.int64)[:, None, None]
    d = np.arange(N2, dtype=np.int64)[None, :, None]
    b = np.arange(N2, dtype=np.int64)[None, None, :]
    m = ((c + N1 * d + N_META) * (b + N_META)) % L
    ang = 2.0 * np.pi * m / L
    ec = np.zeros((N1 // 2, K2P, K2P), np.float32)
    es = np.zeros((N1 // 2, K2P, K2P), np.float32)
    for e in range(2):
        ec[:, e:2 * N2:2, e:2 * N2:2] = (np.cos(ang) / np.sqrt(L))[e::2]
        es[:, e:2 * N2:2, e:2 * N2:2] = (np.sin(ang) / np.sqrt(L))[e::2]
    return ec, es


def _attn_bias_const():
    slopes = 2.0 ** -(np.arange(N_HEADS) + 1.0)
    i = np.arange(BQ)[:, None]
    j = np.arange(BAND)[None, :]
    m = np.arange(N_META)[None, :]
    out = np.full((4, N_KV, Q_PER_KV * BQ, N_KEYS), NEG, np.float64)
    for t in range(4):
        if t < 3:
            dist = np.abs(t * WINDOW + i - j)
            band_ok = dist <= WINDOW
            dist_m = np.minimum(t * WINDOW + N_META + i - m, WINDOW) if t == 0 else np.full((BQ, N_META), WINDOW)
        else:
            dist = N_META + j - i
            band_ok = (dist <= WINDOW) & (i < N_META)
            dist_m = np.abs(i - m)
        for h in range(N_HEADS):
            g, hl = divmod(h, Q_PER_KV)
            rows = slice(hl * BQ, (hl + 1) * BQ)
            band = np.where(band_ok, -slopes[h] * dist, NEG)
            if t == 3:
                band = np.where(i < N_META, band, 0.0)
            out[t, g, rows, :BAND] = band
            out[t, g, rows, BAND:BAND + N_META] = -slopes[h] * dist_m
    return np.ascontiguousarray(LOG2E * out.transpose(0, 1, 3, 2)).astype(np.float32)


def _unit_rms(x):
    return x * lax.rsqrt(jnp.mean(x * x, axis=-1, keepdims=True) + EPS)


def _rms(x, g):
    return _unit_rms(x) * g


def _swiglu_half(x, wg_s, wu_s, wd_s):
    xn = _unit_rms(x).astype(BF16)
    gate = jnp.dot(xn, wg_s[...], preferred_element_type=F32)
    up = jnp.dot(xn, wu_s[...], preferred_element_type=F32)
    act = (gate * jax.nn.sigmoid(gate) * up).astype(BF16)
    return x + 0.5 * jnp.dot(act, wd_s[...], preferred_element_type=F32)


def _cast_weights(s, f32_refs, gain_refs, bf16_scratch):
    for w_ref, g_ref, w_s in zip(f32_refs, gain_refs, bf16_scratch):
        rows = w_ref.shape[0]
        w = w_ref[...] if g_ref is None else w_ref[...] * g_ref[...]
        w_s[pl.ds(pl.multiple_of(s * rows, rows), rows), :] = w.astype(BF16)


def _pre_kernel(x_ref, meta_ref, g1_ref, wg_ref, wu_ref, wd_ref, gm_ref, win_ref, gqk_ref,
                ones_ref, h1_ref, q_ref, kv_ref, f_ref, wg_s, wu_s, wd_s, win_s):
    s = pl.program_id(0)

    @pl.when(s < N_CAST)
    def _():
        _cast_weights(s, (wg_ref, wu_ref, wd_ref, win_ref), (g1_ref, g1_ref, None, gm_ref),
                      (wg_s, wu_s, wd_s, win_s))

    @pl.when(s >= N_CAST)
    def _():
        _pre_tile(s - N_CAST, x_ref, meta_ref, gqk_ref, ones_ref,
                  h1_ref, q_ref, kv_ref, f_ref, wg_s, wu_s, wd_s, win_s)


def _pre_tile(i, x_ref, meta_ref, gqk_ref, ones_ref,
              h1_ref, q_ref, kv_ref, f_ref, wg_s, wu_s, wd_s, win_s):
    row = lax.broadcasted_iota(jnp.int32, (TM, 1), 0)
    is_meta = jnp.logical_and(i == N_TILES - 1, row >= META_ROW0)
    x = jnp.where(is_meta, meta_ref[...], x_ref[...])

    h1 = _swiglu_half(x, wg_s, wu_s, wd_s)
    h1_ref[...] = h1

    u = jnp.dot(_unit_rms(h1).astype(BF16), win_s[...], preferred_element_type=F32)

    qk = u[:, :QK_W]
    sq = (qk * qk).astype(BF16)
    ones = ones_ref[...]
    ss = jnp.concatenate([
        jnp.dot(sq[:, 0:256], ones, preferred_element_type=F32),
        jnp.dot(sq[:, 256:512], ones, preferred_element_type=F32),
        jnp.dot(sq[:, 512:640], ones[:KV_W, :KV_W], preferred_element_type=F32)], axis=1)
    qkn = qk * lax.rsqrt(ss * (1.0 / HEAD_DIM) + EPS) * gqk_ref[...]
    q_ref[...] = qkn[:, :ATTN_W].astype(BF16)
    kv_ref[...] = jnp.concatenate([qkn[:, ATTN_W:], u[:, QK_W:QK_W + KV_W]], axis=1).astype(BF16)

    f_ref[...] = u[:, QK_W + KV_W:]


def _attn_kernel(q_ref, kv_ref, bias_ref, sink_ref, o_ref):
    step = pl.program_id(0)
    kv_meta = kv_ref[SEQ:L, :]

    units = []
    for j in range(QBLK_PER_STEP):
        n = step * QBLK_PER_STEP + j
        btype = jnp.where(n == 0, 0, jnp.where(n == N_QBLK - 1, 2, jnp.where(n == N_QBLK, 3, 1)))
        start = jnp.where(n == N_QBLK, 0, jnp.clip((n - 1) * BQ, 0, SEQ - BAND))
        start = pl.multiple_of(start, BQ)
        kv_cat = jnp.concatenate([kv_ref[pl.ds(start, BAND), :], kv_meta], axis=0)
        q_ok = lax.broadcasted_iota(jnp.int32, (BQ, 1), 0) < L - n * BQ
        units.extend((j, g, btype, kv_cat, q_ok) for g in range(N_KV))

    def scores(unit):
        j, g, _, kv_cat, q_ok = unit
        q = jnp.where(q_ok, q_ref[j * BQ:(j + 1) * BQ, :], 0)
        qg = jnp.concatenate(
            [q[:, (Q_PER_KV * g + hl) * HEAD_DIM:(Q_PER_KV * g + hl + 1) * HEAD_DIM]
             for hl in range(Q_PER_KV)], axis=0)
        k = kv_cat[:, g * HEAD_DIM:(g + 1) * HEAD_DIM]
        return lax.dot_general(k, qg, (((1,), (1,)), ((), ())), preferred_element_type=F32)

    st_next = scores(units[0])
    for u, (j, g, btype, kv_cat, _) in enumerate(units):
        st = st_next + bias_ref[btype, g]
        sink = sink_ref[g]
        m = jnp.maximum(jnp.max(st, axis=0, keepdims=True), sink)
        p = jnp.exp2(st - m)
        denom = jnp.sum(p, axis=0, keepdims=True) + jnp.exp2(sink - m)
        if u + 1 < len(units):
            st_next = scores(units[u + 1])
            p = jnp.concatenate([jnp.maximum(p[:16], st_next[:16] * 0.0 - 1.0), p[16:]], axis=0)
        v = kv_cat[:, KV_W + g * HEAD_DIM:KV_W + (g + 1) * HEAD_DIM]
        ot = lax.dot_general(v, p.astype(BF16), (((0,), (0,)), ((), ())), preferred_element_type=F32)
        ot = ot / denom
        for hp in range(Q_PER_KV // 2):
            h0 = Q_PER_KV * g + 2 * hp
            two = jnp.concatenate([ot[:, 2 * hp * BQ:(2 * hp + 1) * BQ],
                                   ot[:, (2 * hp + 1) * BQ:(2 * hp + 2) * BQ]], axis=0)
            o_ref[j * BQ:(j + 1) * BQ, h0 * HEAD_DIM:(h0 + 2) * HEAD_DIM] = two.T.astype(BF16)


def _dft_a_kernel(f_ref, wa_ref, cdft_ref, g_ref, fs_ref, gs_ref):
    cdft = cdft_ref[...]
    for k in range(B_PER_STEP):
        fs_ref[...] = f_ref[:, k, 0, :]
        h = jnp.dot(wa_ref[...], fs_ref[...].astype(BF16), preferred_element_type=F32).astype(BF16)
        for half in range(2):
            p = jnp.dot(h[:, half * 256:(half + 1) * 256], cdft, preferred_element_type=F32)
            gs_ref[:, half * 256:(half + 1) * 256] = p[:N1, :256] + p[N1:, 256:]
            gs_ref[:, F_W + half * 256:F_W + (half + 1) * 256] = p[N1:, :256] - p[:N1, 256:]
        g_ref[:, k, :, :] = gs_ref[...].astype(BF16).reshape(N1 // 2, 2, 2 * F_W)


def _dft_c_kernel(g_ref, ec_ref, es_ref, y_ref, gs_ref):
    gs_ref[:, 2 * N2:, :] = jnp.zeros((TCP, K2P - 2 * N2, 2 * F_W), BF16)
    for i in range(TCP):
        gs_ref[i, :2 * N2, :] = g_ref[i].reshape(2 * N2, 2 * F_W)
        g = gs_ref[i]
        y = (jnp.dot(ec_ref[i], g[:, :F_W], preferred_element_type=F32)
             + jnp.dot(es_ref[i], g[:, F_W:], preferred_element_type=F32))
        y_ref[:, i, :, :] = y[:2 * N2].astype(BF16).reshape(N2, 2, F_W)


def _post_kernel(h1_ref, a_ref, y_ref, gmix_ref, wout_ref, g2_ref, wg_ref, wu_ref, wd_ref,
                 gfin_ref, o_ref, wg_s, wu_s, wd_s, wout_s):
    s = pl.program_id(0)

    @pl.when(s < N_CAST)
    def _():
        _cast_weights(s, (wg_ref, wu_ref, wd_ref, wout_ref), (g2_ref, g2_ref, None, gmix_ref),
                      (wg_s, wu_s, wd_s, wout_s))

    @pl.when(s >= N_CAST)
    def _():
        y = y_ref[...].reshape(TM, F_W).astype(F32)
        mixed = jnp.concatenate([_unit_rms(a_ref[...].astype(F32)), _unit_rms(y)], axis=1)
        h2 = h1_ref[...] + jnp.dot(mixed.astype(BF16), wout_s[...], preferred_element_type=F32)
        h3 = _swiglu_half(h2, wg_s, wu_s, wd_s)
        o_ref[...] = _rms(h3, gfin_ref[...])


def _resident(shape):
    zeros = (0,) * len(shape)
    return pl.BlockSpec(shape, lambda *_: zeros, pipeline_mode=pl.Buffered(1))


def _params():
    return pltpu.CompilerParams(dimension_semantics=("arbitrary",), vmem_limit_bytes=VMEM_LIMIT)


def _row(g):
    return g.astype(F32).reshape(1, -1)


def kernel(x, meta_tokens, ffn1_norm, ffn1_w_gate, ffn1_w_up, ffn1_w_down, mix_norm, w_in, q_norm, k_norm, sink, attn_out_norm, fourier_out_norm, w_out, ffn2_norm, ffn2_w_gate, ffn2_w_up, ffn2_w_down, final_norm):
    assert x.shape == (1, SEQ, D_MODEL) and x.dtype == F32
    x2 = x.reshape(SEQ, D_MODEL)
    meta_pad = jnp.pad(meta_tokens.astype(F32), ((META_ROW0, 0), (0, 0)))
    gqk = jnp.concatenate([jnp.tile(q_norm.astype(F32), N_HEADS) * (HEAD_DIM ** -0.5 * LOG2E),
                           jnp.tile(k_norm.astype(F32), N_KV)]).reshape(1, QK_W)
    bf = lambda w: w.astype(BF16)

    tile_idx = lambda s: jnp.maximum(s - N_CAST, 0)
    row_tile = lambda w: pl.BlockSpec((TM, w), lambda s: (tile_idx(s), 0))
    flat_rows = lambda w: pl.BlockSpec((TM, None, w), lambda s: (tile_idx(s), 0, 0))
    cast_rows = lambda r, c: pl.BlockSpec((r // N_CAST, c), lambda s: (jnp.minimum(s, N_CAST - 1), 0))
    gain_col = lambda g: g.astype(F32).reshape(-1, 1)
    bf16_copy = lambda r, c: pltpu.VMEM((r, c), BF16)
    ffn_in = [cast_rows(D_MODEL, D_FF), cast_rows(D_MODEL, D_FF), cast_rows(D_FF, D_MODEL)]
    ffn_scratch = [bf16_copy(D_MODEL, D_FF), bf16_copy(D_MODEL, D_FF), bf16_copy(D_FF, D_MODEL)]
    f32 = lambda w: w.astype(F32)

    h1, q, kv, f = pl.pallas_call(
        _pre_kernel,
        grid=(N_CAST + N_TILES,),
        in_specs=[row_tile(D_MODEL), _resident((TM, D_MODEL)), cast_rows(D_MODEL, 1), *ffn_in,
                  cast_rows(D_MODEL, 1), cast_rows(D_MODEL, IN_W), _resident((1, QK_W)),
                  _resident((256, 256))],
        out_specs=[row_tile(D_MODEL), row_tile(ATTN_W), row_tile(2 * KV_W), flat_rows(F_W)],
        out_shape=[jax.ShapeDtypeStruct((L, D_MODEL), F32), jax.ShapeDtypeStruct((L, ATTN_W), BF16),
                   jax.ShapeDtypeStruct((L, 2 * KV_W), BF16), jax.ShapeDtypeStruct((L, 1, F_W), F32)],
        scratch_shapes=ffn_scratch + [bf16_copy(D_MODEL, IN_W)],
        compiler_params=_params(),
        name="pre",
    )(x2, meta_pad, gain_col(ffn1_norm), f32(ffn1_w_gate), f32(ffn1_w_up), f32(ffn1_w_down), gain_col(mix_norm),
      f32(w_in), gqk, _ones_blockdiag())

    sink_rows = (jnp.repeat(sink.astype(F32).reshape(N_KV, Q_PER_KV), BQ, axis=1) * LOG2E)[:, None, :]

    q_rows = pl.BlockSpec((QBLK_PER_STEP * BQ, ATTN_W), lambda s: (s, 0))
    a_out = pl.pallas_call(
        _attn_kernel,
        grid=(pl.cdiv(N_QBLK + 1, QBLK_PER_STEP),),
        in_specs=[q_rows, _resident((L, 2 * KV_W)), _resident((4, N_KV, N_KEYS, Q_PER_KV * BQ)),
                  _resident((N_KV, 1, Q_PER_KV * BQ))],
        out_specs=q_rows,
        out_shape=jax.ShapeDtypeStruct((L, ATTN_W), BF16),
        compiler_params=_params(),
        name="attn",
    )(q, kv, jnp.asarray(_attn_bias_const()), sink_rows)

    row_set = lambda w: pl.BlockSpec((N1, B_PER_STEP, 1, w), lambda s: (0, s, 0, 0))
    g = pl.pallas_call(
        _dft_a_kernel,
        grid=(pl.cdiv(N2, B_PER_STEP),),
        in_specs=[row_set(F_W), _resident((2 * N1, N1)), _resident((256, 512))],
        out_specs=pl.BlockSpec((N1 // 2, B_PER_STEP, 2, 2 * F_W), lambda s: (0, s, 0, 0)),
        out_shape=jax.ShapeDtypeStruct((N1 // 2, N2, 2, 2 * F_W), BF16),
        scratch_shapes=[pltpu.VMEM((N1, F_W), F32), pltpu.VMEM((N1, 2 * F_W), F32)],
        compiler_params=_params(),
        name="dft_a",
    )(f.reshape(N1, N2, 1, F_W), bf(jnp.asarray(_stage_a_dft())), bf(jnp.asarray(_channel_dft())))

    ec, es = _stage_c_dft()
    y = pl.pallas_call(
        _dft_c_kernel,
        grid=(N1 // (2 * TCP),),
        in_specs=[pl.BlockSpec((TCP, N2, 2, 2 * F_W), lambda i: (i, 0, 0, 0)),
                  pl.BlockSpec((TCP, K2P, K2P), lambda i: (i, 0, 0)),
                  pl.BlockSpec((TCP, K2P, K2P), lambda i: (i, 0, 0))],
        out_specs=pl.BlockSpec((N2, TCP, 2, F_W), lambda i: (0, i, 0, 0)),
        out_shape=jax.ShapeDtypeStruct((N2, N1 // 2, 2, F_W), BF16),
        scratch_shapes=[pltpu.VMEM((TCP, K2P, 2 * F_W), BF16)],
        compiler_params=_params(),
        name="dft_c",
    )(g, bf(jnp.asarray(ec)), bf(jnp.asarray(es)))

    out = pl.pallas_call(
        _post_kernel,
        grid=(N_CAST + N_TILES,),
        in_specs=[row_tile(D_MODEL), row_tile(ATTN_W),
                  pl.BlockSpec((TM // 2, 2, F_W), lambda s: (tile_idx(s), 0, 0)), cast_rows(D_MODEL, 1),
                  cast_rows(D_MODEL, D_MODEL), cast_rows(D_MODEL, 1), *ffn_in, _resident((1, D_MODEL))],
        out_specs=row_tile(D_MODEL),
        out_shape=jax.ShapeDtypeStruct((SEQ, D_MODEL), F32),
        scratch_shapes=ffn_scratch + [bf16_copy(D_MODEL, D_MODEL)],
        compiler_params=_params(),
        name="post",
    )(h1, a_out, y.reshape(L // 2, 2, F_W), gain_col(jnp.concatenate([attn_out_norm, fourier_out_norm])), f32(w_out),
      gain_col(ffn2_norm),
      f32(ffn2_w_gate), f32(ffn2_w_up), f32(ffn2_w_down), _row(final_norm))
    return out.reshape(1, SEQ, D_MODEL)
```

```python
import numpy as np
import jax
import jax.numpy as jnp
from jax import lax
from jax.experimental import pallas as pl
from jax.experimental.pallas import tpu as pltpu

F32 = jnp.float32
BF16 = jnp.bfloat16

D_MODEL = 1024
SEQ = 16384
N_META = 16
L = SEQ + N_META
HEAD_DIM = 64
N_HEADS = 8
N_KV = 2
Q_PER_KV = N_HEADS // N_KV
ATTN_W = N_HEADS * HEAD_DIM
KV_W = N_KV * HEAD_DIM
F_W = 512
F_GROUP = 64
IN_W = ATTN_W + 2 * KV_W + F_W
QK_W = ATTN_W + KV_W
WINDOW = 128
D_FF = 2816
N_CAST = 8
EPS = 1e-6
NEG = -1e30
LOG2E = 1.4426950408889634

TM = 656
N_TILES = L // TM
META_ROW0 = SEQ - (N_TILES - 1) * TM

BQ = 128
N_QBLK = SEQ // BQ
QBLK_PER_STEP = 12
BAND = BQ + 2 * WINDOW
N_KEYS = BAND + N_META

N1 = 200
N2 = 82
K2P = 176
B_PER_STEP = 4
TCP = 25

VMEM_LIMIT = 56 * 1024 * 1024


def _ones_blockdiag():
    m = np.kron(np.eye(4), np.ones((HEAD_DIM, HEAD_DIM)))
    return jnp.asarray(m, BF16)


def _channel_dft():
    c = np.arange(F_GROUP)
    ang = 2.0 * np.pi * np.outer(c, c) / F_GROUP
    cos = np.kron(np.eye(4), np.cos(ang) / 8.0)
    sin = np.kron(np.eye(4), np.sin(ang) / 8.0)
    return np.concatenate([cos, sin], axis=1).astype(np.float32)


def _stage_a_dft():
    a = np.arange(N1, dtype=np.int64)
    m = (N2 * np.outer(a + N_META, a)) % L
    ang = 2.0 * np.pi * m / L
    return np.concatenate([np.cos(ang), -np.sin(ang)], axis=0).astype(np.float32)


def _stage_c_dft():
    c = np.arange(N1, dtype=np.int64)[:, None, None]
    d = np.arange(N2, dtype=np.int64)[None, :, None]
    b = np.arange(N2, dtype=np.int64)[None, None, :]
    m = ((c + N1 * d + N_META) * (b + N_META)) % L
    ang = 2.0 * np.pi * m / L
    ec = np.zeros((N1 // 2, K2P, K2P), np.float32)
    es = np.zeros((N1 // 2, K2P, K2P), np.float32)
    for e in range(2):
        ec[:, e:2 * N2:2, e:2 * N2:2] = (np.cos(ang) / np.sqrt(L))[e::2]
        es[:, e:2 * N2:2, e:2 * N2:2] = (np.sin(ang) / np.sqrt(L))[e::2]
    return ec, es


def _attn_bias_const():
    slopes = 2.0 ** -(np.arange(N_HEADS) + 1.0)
    i = np.arange(BQ)[:, None]
    j = np.arange(BAND)[None, :]
    m = np.arange(N_META)[None, :]
    out = np.full((4, N_KV, Q_PER_KV * BQ, N_KEYS), NEG, np.float64)
    for t in range(4):
        if t < 3:
            dist = np.abs(t * WINDOW + i - j)
            band_ok = dist <= WINDOW
            dist_m = np.minimum(t * WINDOW + N_META + i - m, WINDOW) if t == 0 else np.full((BQ, N_META), WINDOW)
        else:
            dist = N_META + j - i
            band_ok = (dist <= WINDOW) & (i < N_META)
            dist_m = np.abs(i - m)
        for h in range(N_HEADS):
            g, hl = divmod(h, Q_PER_KV)
            rows = slice(hl * BQ, (hl + 1) * BQ)
            band = np.where(band_ok, -slopes[h] * dist, NEG)
            if t == 3:
                band = np.where(i < N_META, band, 0.0)
            out[t, g, rows, :BAND] = band
            out[t, g, rows, BAND:BAND + N_META] = -slopes[h] * dist_m
    return np.ascontiguousarray(LOG2E * out.transpose(0, 1, 3, 2)).astype(np.float32)


def _unit_rms(x):
    return x * lax.rsqrt(jnp.mean(x * x, axis=-1, keepdims=True) + EPS)


def _rms(x, g):
    return _unit_rms(x) * g


def _swiglu_half(x, wg_s, wu_s, wd_s):
    xn = _unit_rms(x).astype(BF16)
    gate = jnp.dot(xn, wg_s[...], preferred_element_type=F32)
    up = jnp.dot(xn, wu_s[...], preferred_element_type=F32)
    act = (gate * jax.nn.sigmoid(gate) * up).astype(BF16)
    return x + 0.5 * jnp.dot(act, wd_s[...], preferred_element_type=F32)


def _cast_weights(s, f32_refs, gain_refs, bf16_scratch):
    for w_ref, g_ref, w_s in zip(f32_refs, gain_refs, bf16_scratch):
        rows = w_ref.shape[0]
        w = w_ref[...] if g_ref is None else w_ref[...] * g_ref[...]
        w_s[pl.ds(pl.multiple_of(s * rows, rows), rows), :] = w.astype(BF16)


def _pre_kernel(x_ref, meta_ref, g1_ref, wg_ref, wu_ref, wd_ref, gm_ref, win_ref, gqk_ref,
                ones_ref, h1_ref, q_ref, kv_ref, f_ref, wg_s, wu_s, wd_s, win_s):
    s = pl.program_id(0)

    @pl.when(s < N_CAST)
    def _():
        _cast_weights(s, (wg_ref, wu_ref, wd_ref, win_ref), (g1_ref, g1_ref, None, gm_ref),
                      (wg_s, wu_s, wd_s, win_s))

    @pl.when(s >= N_CAST)
    def _():
        _pre_tile(s - N_CAST, x_ref, meta_ref, gqk_ref, ones_ref,
                  h1_ref, q_ref, kv_ref, f_ref, wg_s, wu_s, wd_s, win_s)


def _pre_tile(i, x_ref, meta_ref, gqk_ref, ones_ref,
              h1_ref, q_ref, kv_ref, f_ref, wg_s, wu_s, wd_s, win_s):
    row = lax.broadcasted_iota(jnp.int32, (TM, 1), 0)
    is_meta = jnp.logical_and(i == N_TILES - 1, row >= META_ROW0)
    x = jnp.where(is_meta, meta_ref[...], x_ref[...])

    h1 = _swiglu_half(x, wg_s, wu_s, wd_s)
    h1_ref[...] = h1

    u = jnp.dot(_unit_rms(h1).astype(BF16), win_s[...], preferred_element_type=F32)

    qk = u[:, :QK_W]
    sq = (qk * qk).astype(BF16)
    ones = ones_ref[...]
    ss = jnp.concatenate([
        jnp.dot(sq[:, 0:256], ones, preferred_element_type=F32),
        jnp.dot(sq[:, 256:512], ones, preferred_element_type=F32),
        jnp.dot(sq[:, 512:640], ones[:KV_W, :KV_W], preferred_element_type=F32)], axis=1)
    qkn = qk * lax.rsqrt(ss * (1.0 / HEAD_DIM) + EPS) * gqk_ref[...]
    q_ref[...] = qkn[:, :ATTN_W].astype(BF16)
    kv_ref[...] = jnp.concatenate([qkn[:, ATTN_W:], u[:, QK_W:QK_W + KV_W]], axis=1).astype(BF16)

    f_ref[...] = u[:, QK_W + KV_W:]


def _attn_kernel(q_ref, kv_ref, bias_ref, sink_ref, o_ref):
    step = pl.program_id(0)
    kv_meta = kv_ref[SEQ:L, :]
    ones_col = (lax.broadcasted_iota(jnp.int32, (N_KEYS, 16), 1) == 0).astype(BF16)

    units = []
    for j in range(QBLK_PER_STEP):
        n = step * QBLK_PER_STEP + j
        btype = jnp.where(n == 0, 0, jnp.where(n == N_QBLK - 1, 2, jnp.where(n == N_QBLK, 3, 1)))
        start = jnp.where(n == N_QBLK, 0, jnp.clip((n - 1) * BQ, 0, SEQ - BAND))
        start = pl.multiple_of(start, BQ)
        kv_cat = jnp.concatenate([kv_ref[pl.ds(start, BAND), :], kv_meta], axis=0)
        q_ok = lax.broadcasted_iota(jnp.int32, (BQ, 1), 0) < L - n * BQ
        units.extend((j, g, btype, kv_cat, q_ok) for g in range(N_KV))

    def scores(unit):
        j, g, _, kv_cat, q_ok = unit
        q = jnp.where(q_ok, q_ref[j * BQ:(j + 1) * BQ, :], 0)
        qg = jnp.concatenate(
            [q[:, (Q_PER_KV * g + hl) * HEAD_DIM:(Q_PER_KV * g + hl + 1) * HEAD_DIM]
             for hl in range(Q_PER_KV)], axis=0)
        k = kv_cat[:, g * HEAD_DIM:(g + 1) * HEAD_DIM]
        return lax.dot_general(k, qg, (((1,), (1,)), ((), ())), preferred_element_type=F32)

    st_next = scores(units[0])
    for u, (j, g, btype, kv_cat, _) in enumerate(units):
        st = st_next + bias_ref[btype, g]
        sink = sink_ref[g]
        m = jnp.maximum(jnp.max(st, axis=0, keepdims=True), sink)
        p = jnp.exp2(st - m)
        if u + 1 < len(units):
            st_next = scores(units[u + 1])
            p = jnp.concatenate([jnp.maximum(p[:16], st_next[:16] * 0.0 - 1.0), p[16:]], axis=0)
        v1 = jnp.concatenate([kv_cat[:, KV_W + g * HEAD_DIM:KV_W + (g + 1) * HEAD_DIM], ones_col], axis=1)
        ot = lax.dot_general(v1, p.astype(BF16), (((0,), (0,)), ((), ())), preferred_element_type=F32)
        denom = ot[HEAD_DIM:HEAD_DIM + 1] + jnp.exp2(sink - m)
        ot = ot[:HEAD_DIM] / denom
        for hp in range(Q_PER_KV // 2):
            h0 = Q_PER_KV * g + 2 * hp
            two = jnp.concatenate([ot[:, 2 * hp * BQ:(2 * hp + 1) * BQ],
                                   ot[:, (2 * hp + 1) * BQ:(2 * hp + 2) * BQ]], axis=0)
            o_ref[j * BQ:(j + 1) * BQ, h0 * HEAD_DIM:(h0 + 2) * HEAD_DIM] = two.T.astype(BF16)


def _dft_a_kernel(f_ref, wa_ref, cdft_ref, g_ref, fs_ref, gs_ref):
    cdft = cdft_ref[...]
    for k in range(B_PER_STEP):
        fs_ref[...] = f_ref[:, k, 0, :]
        h = jnp.dot(wa_ref[...], fs_ref[...].astype(BF16), preferred_element_type=F32).astype(BF16)
        for half in range(2):
            p = jnp.dot(h[:, half * 256:(half + 1) * 256], cdft, preferred_element_type=F32)
            gs_ref[:, half * 256:(half + 1) * 256] = p[:N1, :256] + p[N1:, 256:]
            gs_ref[:, F_W + half * 256:F_W + (half + 1) * 256] = p[N1:, :256] - p[:N1, 256:]
        g_ref[:, k, :, :] = gs_ref[...].astype(BF16).reshape(N1 // 2, 2, 2 * F_W)


def _dft_c_kernel(g_ref, ec_ref, es_ref, y_ref, gs_ref):
    gs_ref[:, 2 * N2:, :] = jnp.zeros((TCP, K2P - 2 * N2, 2 * F_W), BF16)
    for i in range(TCP):
        gs_ref[i, :2 * N2, :] = g_ref[i].reshape(2 * N2, 2 * F_W)
        g = gs_ref[i]
        y = (jnp.dot(ec_ref[i], g[:, :F_W], preferred_element_type=F32)
             + jnp.dot(es_ref[i], g[:, F_W:], preferred_element_type=F32))
        y_ref[:, i, :, :] = y[:2 * N2].astype(BF16).reshape(N2, 2, F_W)


def _post_kernel(h1_ref, a_ref, y_ref, gmix_ref, wout_ref, g2_ref, wg_ref, wu_ref, wd_ref,
                 gfin_ref, o_ref, wg_s, wu_s, wd_s, wout_s):
    s = pl.program_id(0)

    @pl.when(s < N_CAST)
    def _():
        _cast_weights(s, (wg_ref, wu_ref, wd_ref, wout_ref), (g2_ref, g2_ref, None, gmix_ref),
                      (wg_s, wu_s, wd_s, wout_s))

    @pl.when(s >= N_CAST)
    def _():
        y = y_ref[...].reshape(TM, F_W).astype(F32)
        mixed = jnp.concatenate([_unit_rms(a_ref[...].astype(F32)), _unit_rms(y)], axis=1)
        h2 = h1_ref[...] + jnp.dot(mixed.astype(BF16), wout_s[...], preferred_element_type=F32)
        h3 = _swiglu_half(h2, wg_s, wu_s, wd_s)
        o_ref[...] = _rms(h3, gfin_ref[...])


def _resident(shape):
    zeros = (0,) * len(shape)
    return pl.BlockSpec(shape, lambda *_: zeros, pipeline_mode=pl.Buffered(1))


def _params():
    return pltpu.CompilerParams(dimension_semantics=("arbitrary",), vmem_limit_bytes=VMEM_LIMIT)


def _row(g):
    return g.astype(F32).reshape(1, -1)


def kernel(x, meta_tokens, ffn1_norm, ffn1_w_gate, ffn1_w_up, ffn1_w_down, mix_norm, w_in, q_norm, k_norm, sink, attn_out_norm, fourier_out_norm, w_out, ffn2_norm, ffn2_w_gate, ffn2_w_up, ffn2_w_down, final_norm):
    assert x.shape == (1, SEQ, D_MODEL) and x.dtype == F32
    x2 = x.reshape(SEQ, D_MODEL)
    meta_pad = jnp.pad(meta_tokens.astype(F32), ((META_ROW0, 0), (0, 0)))
    gqk = jnp.concatenate([jnp.tile(q_norm.astype(F32), N_HEADS) * (HEAD_DIM ** -0.5 * LOG2E),
                           jnp.tile(k_norm.astype(F32), N_KV)]).reshape(1, QK_W)
    bf = lambda w: w.astype(BF16)

    tile_idx = lambda s: jnp.maximum(s - N_CAST, 0)
    row_tile = lambda w: pl.BlockSpec((TM, w), lambda s: (tile_idx(s), 0))
    flat_rows = lambda w: pl.BlockSpec((TM, None, w), lambda s: (tile_idx(s), 0, 0))
    cast_rows = lambda r, c: pl.BlockSpec((r // N_CAST, c), lambda s: (jnp.minimum(s, N_CAST - 1), 0))
    gain_col = lambda g: g.astype(F32).reshape(-1, 1)
    bf16_copy = lambda r, c: pltpu.VMEM((r, c), BF16)
    ffn_in = [cast_rows(D_MODEL, D_FF), cast_rows(D_MODEL, D_FF), cast_rows(D_FF, D_MODEL)]
    ffn_scratch = [bf16_copy(D_MODEL, D_FF), bf16_copy(D_MODEL, D_FF), bf16_copy(D_FF, D_MODEL)]
    f32 = lambda w: w.astype(F32)

    h1, q, kv, f = pl.pallas_call(
        _pre_kernel,
        grid=(N_CAST + N_TILES,),
        in_specs=[row_tile(D_MODEL), _resident((TM, D_MODEL)), cast_rows(D_MODEL, 1), *ffn_in,
                  cast_rows(D_MODEL, 1), cast_rows(D_MODEL, IN_W), _resident((1, QK_W)),
                  _resident((256, 256))],
        out_specs=[row_tile(D_MODEL), row_tile(ATTN_W), row_tile(2 * KV_W), flat_rows(F_W)],
        out_shape=[jax.ShapeDtypeStruct((L, D_MODEL), F32), jax.ShapeDtypeStruct((L, ATTN_W), BF16),
                   jax.ShapeDtypeStruct((L, 2 * KV_W), BF16), jax.ShapeDtypeStruct((L, 1, F_W), F32)],
        scratch_shapes=ffn_scratch + [bf16_copy(D_MODEL, IN_W)],
        compiler_params=_params(),
        name="pre",
    )(x2, meta_pad, gain_col(ffn1_norm), f32(ffn1_w_gate), f32(ffn1_w_up), f32(ffn1_w_down), gain_col(mix_norm),
      f32(w_in), gqk, _ones_blockdiag())

    sink_rows = (jnp.repeat(sink.astype(F32).reshape(N_KV, Q_PER_KV), BQ, axis=1) * LOG2E)[:, None, :]

    q_rows = pl.BlockSpec((QBLK_PER_STEP * BQ, ATTN_W), lambda s: (s, 0))
    a_out = pl.pallas_call(
        _attn_kernel,
        grid=(pl.cdiv(N_QBLK + 1, QBLK_PER_STEP),),
        in_specs=[q_rows, _resident((L, 2 * KV_W)), _resident((4, N_KV, N_KEYS, Q_PER_KV * BQ)),
                  _resident((N_KV, 1, Q_PER_KV * BQ))],
        out_specs=q_rows,
        out_shape=jax.ShapeDtypeStruct((L, ATTN_W), BF16),
        compiler_params=_params(),
        name="attn",
    )(q, kv, jnp.asarray(_attn_bias_const()), sink_rows)

    row_set = lambda w: pl.BlockSpec((N1, B_PER_STEP, 1, w), lambda s: (0, s, 0, 0))
    g = pl.pallas_call(
        _dft_a_kernel,
        grid=(pl.cdiv(N2, B_PER_STEP),),
        in_specs=[row_set(F_W), _resident((2 * N1, N1)), _resident((256, 512))],
        out_specs=pl.BlockSpec((N1 // 2, B_PER_STEP, 2, 2 * F_W), lambda s: (0, s, 0, 0)),
        out_shape=jax.ShapeDtypeStruct((N1 // 2, N2, 2, 2 * F_W), BF16),
        scratch_shapes=[pltpu.VMEM((N1, F_W), F32), pltpu.VMEM((N1, 2 * F_W), F32)],
        compiler_params=_params(),
        name="dft_a",
    )(f.reshape(N1, N2, 1, F_W), bf(jnp.asarray(_stage_a_dft())), bf(jnp.asarray(_channel_dft())))

    ec, es = _stage_c_dft()
    y = pl.pallas_call(
        _dft_c_kernel,
        grid=(N1 // (2 * TCP),),
        in_specs=[pl.BlockSpec((TCP, N2, 2, 2 * F_W), lambda i: (i, 0, 0, 0)),
                  pl.BlockSpec((TCP, K2P, K2P), lambda i: (i, 0, 0)),
                  pl.BlockSpec((TCP, K2P, K2P), lambda i: (i, 0, 0))],
        out_specs=pl.BlockSpec((N2, TCP, 2, F_W), lambda i: (0, i, 0, 0)),
        out_shape=jax.ShapeDtypeStruct((N2, N1 // 2, 2, F_W), BF16),
        scratch_shapes=[pltpu.VMEM((TCP, K2P, 2 * F_W), BF16)],
        compiler_params=_params(),
        name="dft_c",
    )(g, bf(jnp.asarray(ec)), bf(jnp.asarray(es)))

    out = pl.pallas_call(
        _post_kernel,
        grid=(N_CAST + N_TILES,),
        in_specs=[row_tile(D_MODEL), row_tile(ATTN_W),
                  pl.BlockSpec((TM // 2, 2, F_W), lambda s: (tile_idx(s), 0, 0)), cast_rows(D_MODEL, 1),
                  cast_rows(D_MODEL, D_MODEL), cast_rows(D_MODEL, 1), *ffn_in, _resident((1, D_MODEL))],
        out_specs=row_tile(D_MODEL),
        out_shape=jax.ShapeDtypeStruct((SEQ, D_MODEL), F32),
        scratch_shapes=ffn_scratch + [bf16_copy(D_MODEL, D_MODEL)],
        compiler_params=_params(),
        name="post",
    )(h1, a_out, y.reshape(L // 2, 2, F_W), gain_col(jnp.concatenate([attn_out_norm, fourier_out_norm])), f32(w_out),
      gain_col(ffn2_norm),
      f32(ffn2_w_gate), f32(ffn2_w_up), f32(ffn2_w_down), _row(final_norm))
    return out.reshape(1, SEQ, D_MODEL)
```

```python
import numpy as np
import jax
import jax.numpy as jnp
from jax import lax
from jax.experimental import pallas as pl
from jax.experimental.pallas import tpu as pltpu

F32 = jnp.float32
BF16 = jnp.bfloat16

D_MODEL = 1024
SEQ = 16384
N_META = 16
L = SEQ + N_META
HEAD_DIM = 64
N_HEADS = 8
N_KV = 2
Q_PER_KV = N_HEADS // N_KV
ATTN_W = N_HEADS * HEAD_DIM
KV_W = N_KV * HEAD_DIM
F_W = 512
F_GROUP = 64
IN_W = ATTN_W + 2 * KV_W + F_W
QK_W = ATTN_W + KV_W
WINDOW = 128
D_FF = 2816
N_CAST = 8
EPS = 1e-6
NEG = -1e30
LOG2E = 1.4426950408889634

TM = 656
N_TILES = L // TM
META_ROW0 = SEQ - (N_TILES - 1) * TM

BQ = 128
N_QBLK = SEQ // BQ
QBLK_PER_STEP = 12
BAND = BQ + 2 * WINDOW
N_KEYS = BAND + N_META

N1 = 200
N2 = 82
K2P = 176
B_PER_STEP = 4
TCP = 25

VMEM_LIMIT = 56 * 1024 * 1024


def _ones_blockdiag():
    m = np.kron(np.eye(4), np.ones((HEAD_DIM, HEAD_DIM)))
    return jnp.asarray(m, BF16)


def _channel_dft():
    c = np.arange(F_GROUP)
    ang = 2.0 * np.pi * np.outer(c, c) / F_GROUP
    cos = np.kron(np.eye(4), np.cos(ang) / 8.0)
    sin = np.kron(np.eye(4), np.sin(ang) / 8.0)
    return np.concatenate([cos, sin], axis=1).astype(np.float32)


def _stage_a_dft():
    a = np.arange(N1, dtype=np.int64)
    m = (N2 * np.outer(a + N_META, a)) % L
    ang = 2.0 * np.pi * m / L
    return np.concatenate([np.cos(ang), -np.sin(ang)], axis=0).astype(np.float32)


def _stage_c_dft():
    c = np.arange(N1, dtype=np.int64)[:, None, None]
    d = np.arange(N2, dtype=np.int64)[None, :, None]
    b = np.arange(N2, dtype=np.int64)[None, None, :]
    m = ((c + N1 * d + N_META) * (b + N_META)) % L
    ang = 2.0 * np.pi * m / L
    ec = np.zeros((N1 // 2, K2P, K2P), np.float32)
    es = np.zeros((N1 // 2, K2P, K2P), np.float32)
    for e in range(2):
        ec[:, e:2 * N2:2, e:2 * N2:2] = (np.cos(ang) / np.sqrt(L))[e::2]
        es[:, e:2 * N2:2, e:2 * N2:2] = (np.sin(ang) / np.sqrt(L))[e::2]
    return ec, es


def _attn_bias_const():
    slopes = 2.0 ** -(np.arange(N_HEADS) + 1.0)
    i = np.arange(BQ)[:, None]
    j = np.arange(BAND)[None, :]
    m = np.arange(N_META)[None, :]
    out = np.full((4, N_KV, Q_PER_KV * BQ, N_KEYS), NEG, np.float64)
    for t in range(4):
        if t < 3:
            dist = np.abs(t * WINDOW + i - j)
            band_ok = dist <= WINDOW
            dist_m = np.minimum(t * WINDOW + N_META + i - m, WINDOW) if t == 0 else np.full((BQ, N_META), WINDOW)
        else:
            dist = N_META + j - i
            band_ok = (dist <= WINDOW) & (i < N_META)
            dist_m = np.abs(i - m)
        for h in range(N_HEADS):
            g, hl = divmod(h, Q_PER_KV)
            rows = slice(hl * BQ, (hl + 1) * BQ)
            band = np.where(band_ok, -slopes[h] * dist, NEG)
            if t == 3:
                band = np.where(i < N_META, band, 0.0)
            out[t, g, rows, :BAND] = band
            out[t, g, rows, BAND:BAND + N_META] = -slopes[h] * dist_m
    return np.ascontiguousarray(LOG2E * out.transpose(0, 1, 3, 2)).astype(np.float32)


def _unit_rms(x):
    return x * lax.rsqrt(jnp.mean(x * x, axis=-1, keepdims=True) + EPS)


def _rms(x, g):
    return _unit_rms(x) * g


def _swiglu_half(x, wg_s, wu_s, wd_s):
    xn = _unit_rms(x).astype(BF16)
    gate = jnp.dot(xn, wg_s[...], preferred_element_type=F32)
    up = jnp.dot(xn, wu_s[...], preferred_element_type=F32)
    act = (gate * jax.nn.sigmoid(gate) * up).astype(BF16)
    return x + 0.5 * jnp.dot(act, wd_s[...], preferred_element_type=F32)


def _cast_weights(s, f32_refs, gain_refs, bf16_scratch):
    for w_ref, g_ref, w_s in zip(f32_refs, gain_refs, bf16_scratch):
        rows = w_ref.shape[0]
        w = w_ref[...] if g_ref is None else w_ref[...] * g_ref[...]
        w_s[pl.ds(pl.multiple_of(s * rows, rows), rows), :] = w.astype(BF16)


def _pre_kernel(x_ref, meta_ref, g1_ref, wg_ref, wu_ref, wd_ref, gm_ref, win_ref, gqk_ref,
                ones_ref, h1_ref, q_ref, kv_ref, f_ref, wg_s, wu_s, wd_s, win_s):
    s = pl.program_id(0)

    @pl.when(s < N_CAST)
    def _():
        _cast_weights(s, (wg_ref, wu_ref, wd_ref, win_ref), (g1_ref, g1_ref, None, gm_ref),
                      (wg_s, wu_s, wd_s, win_s))

    @pl.when(s >= N_CAST)
    def _():
        _pre_tile(s - N_CAST, x_ref, meta_ref, gqk_ref, ones_ref,
                  h1_ref, q_ref, kv_ref, f_ref, wg_s, wu_s, wd_s, win_s)


def _pre_tile(i, x_ref, meta_ref, gqk_ref, ones_ref,
              h1_ref, q_ref, kv_ref, f_ref, wg_s, wu_s, wd_s, win_s):
    tail = jnp.where(i == N_TILES - 1, meta_ref[...], x_ref[META_ROW0:, :])
    x = jnp.concatenate([x_ref[:META_ROW0, :], tail], axis=0)

    h1 = _swiglu_half(x, wg_s, wu_s, wd_s)
    h1_ref[...] = h1

    u = jnp.dot(_unit_rms(h1).astype(BF16), win_s[...], preferred_element_type=F32)

    qk = u[:, :QK_W]
    sq = (qk * qk).astype(BF16)
    ones = ones_ref[...]
    ss = jnp.concatenate([
        jnp.dot(sq[:, 0:256], ones, preferred_element_type=F32),
        jnp.dot(sq[:, 256:512], ones, preferred_element_type=F32),
        jnp.dot(sq[:, 512:640], ones[:KV_W, :KV_W], preferred_element_type=F32)], axis=1)
    qkn = qk * lax.rsqrt(ss * (1.0 / HEAD_DIM) + EPS) * gqk_ref[...]
    q_ref[...] = qkn[:, :ATTN_W].astype(BF16)
    kv_ref[...] = jnp.concatenate([qkn[:, ATTN_W:], u[:, QK_W:QK_W + KV_W]], axis=1).astype(BF16)

    f_ref[...] = u[:, QK_W + KV_W:]


def _attn_kernel(q_ref, kv_ref, bias_ref, sink_ref, o_ref):
    step = pl.program_id(0)
    kv_meta = kv_ref[SEQ:L, :]
    ones_col = (lax.broadcasted_iota(jnp.int32, (N_KEYS, 16), 1) == 0).astype(BF16)

    units = []
    for j in range(QBLK_PER_STEP):
        n = step * QBLK_PER_STEP + j
        btype = jnp.where(n == 0, 0, jnp.where(n == N_QBLK - 1, 2, jnp.where(n == N_QBLK, 3, 1)))
        start = jnp.where(n == N_QBLK, 0, jnp.clip((n - 1) * BQ, 0, SEQ - BAND))
        start = pl.multiple_of(start, BQ)
        kv_cat = jnp.concatenate([kv_ref[pl.ds(start, BAND), :], kv_meta], axis=0)
        q_ok = lax.broadcasted_iota(jnp.int32, (BQ, 1), 0) < L - n * BQ
        units.extend((j, g, btype, kv_cat, q_ok) for g in range(N_KV))

    def scores(unit):
        j, g, _, kv_cat, q_ok = unit
        q = jnp.where(q_ok, q_ref[j * BQ:(j + 1) * BQ, :], 0)
        qg = jnp.concatenate(
            [q[:, (Q_PER_KV * g + hl) * HEAD_DIM:(Q_PER_KV * g + hl + 1) * HEAD_DIM]
             for hl in range(Q_PER_KV)], axis=0)
        k = kv_cat[:, g * HEAD_DIM:(g + 1) * HEAD_DIM]
        return lax.dot_general(k, qg, (((1,), (1,)), ((), ())), preferred_element_type=F32)

    st_next = scores(units[0])
    for u, (j, g, btype, kv_cat, _) in enumerate(units):
        st = st_next + bias_ref[btype, g]
        sink = sink_ref[g]
        m = jnp.maximum(jnp.max(st, axis=0, keepdims=True), sink)
        p = jnp.exp2(st - m)
        if u + 1 < len(units):
            st_next = scores(units[u + 1])
            p = jnp.concatenate([jnp.maximum(p[:16], st_next[:16] * 0.0 - 1.0), p[16:]], axis=0)
        v1 = jnp.concatenate([kv_cat[:, KV_W + g * HEAD_DIM:KV_W + (g + 1) * HEAD_DIM], ones_col], axis=1)
        ot = lax.dot_general(v1, p.astype(BF16), (((0,), (0,)), ((), ())), preferred_element_type=F32)
        denom = ot[HEAD_DIM:HEAD_DIM + 1] + jnp.exp2(sink - m)
        ot = ot[:HEAD_DIM] / denom
        for hp in range(Q_PER_KV // 2):
            h0 = Q_PER_KV * g + 2 * hp
            two = jnp.concatenate([ot[:, 2 * hp * BQ:(2 * hp + 1) * BQ],
                                   ot[:, (2 * hp + 1) * BQ:(2 * hp + 2) * BQ]], axis=0)
            o_ref[j * BQ:(j + 1) * BQ, h0 * HEAD_DIM:(h0 + 2) * HEAD_DIM] = two.T.astype(BF16)


def _dft_a_kernel(f_ref, wa_ref, cdft_ref, g_ref, fs_ref, gs_ref):
    cdft = cdft_ref[...]
    for k in range(B_PER_STEP):
        fs_ref[...] = f_ref[:, k, 0, :]
        h = jnp.dot(wa_ref[...], fs_ref[...].astype(BF16), preferred_element_type=F32).astype(BF16)
        for half in range(2):
            p = jnp.dot(h[:, half * 256:(half + 1) * 256], cdft, preferred_element_type=F32)
            gs_ref[:, half * 256:(half + 1) * 256] = p[:N1, :256] + p[N1:, 256:]
            gs_ref[:, F_W + half * 256:F_W + (half + 1) * 256] = p[N1:, :256] - p[:N1, 256:]
        g_ref[:, k, :, :] = gs_ref[...].astype(BF16).reshape(N1 // 2, 2, 2 * F_W)


def _dft_c_kernel(g_ref, ec_ref, es_ref, y_ref, gs_ref):
    gs_ref[:, 2 * N2:, :] = jnp.zeros((TCP, K2P - 2 * N2, 2 * F_W), BF16)
    for i in range(TCP):
        gs_ref[i, :2 * N2, :] = g_ref[i].reshape(2 * N2, 2 * F_W)
        g = gs_ref[i]
        y = (jnp.dot(ec_ref[i], g[:, :F_W], preferred_element_type=F32)
             + jnp.dot(es_ref[i], g[:, F_W:], preferred_element_type=F32))
        y_ref[:, i, :, :] = _unit_rms(y[:2 * N2]).astype(BF16).reshape(N2, 2, F_W)


def _post_kernel(h1_ref, a_ref, y_ref, gmix_ref, wout_ref, g2_ref, wg_ref, wu_ref, wd_ref,
                 gfin_ref, o_ref, wg_s, wu_s, wd_s, wout_s):
    s = pl.program_id(0)

    @pl.when(s < N_CAST)
    def _():
        _cast_weights(s, (wg_ref, wu_ref, wd_ref, wout_ref), (g2_ref, g2_ref, None, gmix_ref),
                      (wg_s, wu_s, wd_s, wout_s))

    @pl.when(s >= N_CAST)
    def _():
        y = y_ref[...].reshape(TM, F_W)
        mixed = jnp.concatenate([_unit_rms(a_ref[...].astype(F32)).astype(BF16), y], axis=1)
        h2 = h1_ref[...] + jnp.dot(mixed, wout_s[...], preferred_element_type=F32)
        h3 = _swiglu_half(h2, wg_s, wu_s, wd_s)
        o_ref[...] = _rms(h3, gfin_ref[...])


def _resident(shape):
    zeros = (0,) * len(shape)
    return pl.BlockSpec(shape, lambda *_: zeros, pipeline_mode=pl.Buffered(1))


def _params():
    return pltpu.CompilerParams(dimension_semantics=("arbitrary",), vmem_limit_bytes=VMEM_LIMIT)


def _row(g):
    return g.astype(F32).reshape(1, -1)


def kernel(x, meta_tokens, ffn1_norm, ffn1_w_gate, ffn1_w_up, ffn1_w_down, mix_norm, w_in, q_norm, k_norm, sink, attn_out_norm, fourier_out_norm, w_out, ffn2_norm, ffn2_w_gate, ffn2_w_up, ffn2_w_down, final_norm):
    assert x.shape == (1, SEQ, D_MODEL) and x.dtype == F32
    x2 = x.reshape(SEQ, D_MODEL)
    gqk = jnp.concatenate([jnp.tile(q_norm.astype(F32), N_HEADS) * (HEAD_DIM ** -0.5 * LOG2E),
                           jnp.tile(k_norm.astype(F32), N_KV)]).reshape(1, QK_W)
    bf = lambda w: w.astype(BF16)

    tile_idx = lambda s: jnp.maximum(s - N_CAST, 0)
    row_tile = lambda w: pl.BlockSpec((TM, w), lambda s: (tile_idx(s), 0))
    flat_rows = lambda w: pl.BlockSpec((TM, None, w), lambda s: (tile_idx(s), 0, 0))
    cast_rows = lambda r, c: pl.BlockSpec((r // N_CAST, c), lambda s: (jnp.minimum(s, N_CAST - 1), 0))
    gain_col = lambda g: g.astype(F32).reshape(-1, 1)
    bf16_copy = lambda r, c: pltpu.VMEM((r, c), BF16)
    ffn_in = [cast_rows(D_MODEL, D_FF), cast_rows(D_MODEL, D_FF), cast_rows(D_FF, D_MODEL)]
    ffn_scratch = [bf16_copy(D_MODEL, D_FF), bf16_copy(D_MODEL, D_FF), bf16_copy(D_FF, D_MODEL)]
    f32 = lambda w: w.astype(F32)

    h1, q, kv, f = pl.pallas_call(
        _pre_kernel,
        grid=(N_CAST + N_TILES,),
        in_specs=[row_tile(D_MODEL), _resident((N_META, D_MODEL)), cast_rows(D_MODEL, 1), *ffn_in,
                  cast_rows(D_MODEL, 1), cast_rows(D_MODEL, IN_W), _resident((1, QK_W)),
                  _resident((256, 256))],
        out_specs=[row_tile(D_MODEL), row_tile(ATTN_W), row_tile(2 * KV_W), flat_rows(F_W)],
        out_shape=[jax.ShapeDtypeStruct((L, D_MODEL), F32), jax.ShapeDtypeStruct((L, ATTN_W), BF16),
                   jax.ShapeDtypeStruct((L, 2 * KV_W), BF16), jax.ShapeDtypeStruct((L, 1, F_W), F32)],
        scratch_shapes=ffn_scratch + [bf16_copy(D_MODEL, IN_W)],
        compiler_params=_params(),
        name="pre",
    )(x2, meta_tokens.astype(F32), gain_col(ffn1_norm), f32(ffn1_w_gate), f32(ffn1_w_up), f32(ffn1_w_down), gain_col(mix_norm),
      f32(w_in), gqk, _ones_blockdiag())

    sink_rows = (jnp.repeat(sink.astype(F32).reshape(N_KV, Q_PER_KV), BQ, axis=1) * LOG2E)[:, None, :]

    q_rows = pl.BlockSpec((QBLK_PER_STEP * BQ, ATTN_W), lambda s: (s, 0))
    a_out = pl.pallas_call(
        _attn_kernel,
        grid=(pl.cdiv(N_QBLK + 1, QBLK_PER_STEP),),
        in_specs=[q_rows, _resident((L, 2 * KV_W)), _resident((4, N_KV, N_KEYS, Q_PER_KV * BQ)),
                  _resident((N_KV, 1, Q_PER_KV * BQ))],
        out_specs=q_rows,
        out_shape=jax.ShapeDtypeStruct((L, ATTN_W), BF16),
        compiler_params=_params(),
        name="attn",
    )(q, kv, jnp.asarray(_attn_bias_const()), sink_rows)

    row_set = lambda w: pl.BlockSpec((N1, B_PER_STEP, 1, w), lambda s: (0, s, 0, 0))
    g = pl.pallas_call(
        _dft_a_kernel,
        grid=(pl.cdiv(N2, B_PER_STEP),),
        in_specs=[row_set(F_W), _resident((2 * N1, N1)), _resident((256, 512))],
        out_specs=pl.BlockSpec((N1 // 2, B_PER_STEP, 2, 2 * F_W), lambda s: (0, s, 0, 0)),
        out_shape=jax.ShapeDtypeStruct((N1 // 2, N2, 2, 2 * F_W), BF16),
        scratch_shapes=[pltpu.VMEM((N1, F_W), F32), pltpu.VMEM((N1, 2 * F_W), F32)],
        compiler_params=_params(),
        name="dft_a",
    )(f.reshape(N1, N2, 1, F_W), bf(jnp.asarray(_stage_a_dft())), bf(jnp.asarray(_channel_dft())))

    ec, es = _stage_c_dft()
    y = pl.pallas_call(
        _dft_c_kernel,
        grid=(N1 // (2 * TCP),),
        in_specs=[pl.BlockSpec((TCP, N2, 2, 2 * F_W), lambda i: (i, 0, 0, 0)),
                  pl.BlockSpec((TCP, K2P, K2P), lambda i: (i, 0, 0)),
                  pl.BlockSpec((TCP, K2P, K2P), lambda i: (i, 0, 0))],
        out_specs=pl.BlockSpec((N2, TCP, 2, F_W), lambda i: (0, i, 0, 0)),
        out_shape=jax.ShapeDtypeStruct((N2, N1 // 2, 2, F_W), BF16),
        scratch_shapes=[pltpu.VMEM((TCP, K2P, 2 * F_W), BF16)],
        compiler_params=_params(),
        name="dft_c",
    )(g, bf(jnp.asarray(ec)), bf(jnp.asarray(es)))

    out = pl.pallas_call(
        _post_kernel,
        grid=(N_CAST + N_TILES,),
        in_specs=[row_tile(D_MODEL), row_tile(ATTN_W),
                  pl.BlockSpec((TM // 2, 2, F_W), lambda s: (tile_idx(s), 0, 0)), cast_rows(D_MODEL, 1),
                  cast_rows(D_MODEL, D_MODEL), cast_rows(D_MODEL, 1), *ffn_in, _resident((1, D_MODEL))],
        out_specs=row_tile(D_MODEL),
        out_shape=jax.ShapeDtypeStruct((SEQ, D_MODEL), F32),
        scratch_shapes=ffn_scratch + [bf16_copy(D_MODEL, D_MODEL)],
        compiler_params=_params(),
        name="post",
    )(h1, a_out, y.reshape(L // 2, 2, F_W), gain_col(jnp.concatenate([attn_out_norm, fourier_out_norm])), f32(w_out),
      gain_col(ffn2_norm),
      f32(ffn2_w_gate), f32(ffn2_w_up), f32(ffn2_w_down), _row(final_norm))
    return out.reshape(1, SEQ, D_MODEL)
```

```python
import numpy as np
import jax
import jax.numpy as jnp
from jax import lax
from jax.experimental import pallas as pl
from jax.experimental.pallas import tpu as pltpu

F32 = jnp.float32
BF16 = jnp.bfloat16

D_MODEL = 1024
SEQ = 16384
N_META = 16
L = SEQ + N_META
HEAD_DIM = 64
N_HEADS = 8
N_KV = 2
Q_PER_KV = N_HEADS // N_KV
ATTN_W = N_HEADS * HEAD_DIM
KV_W = N_KV * HEAD_DIM
F_W = 512
F_GROUP = 64
IN_W = ATTN_W + 2 * KV_W + F_W
QK_W = ATTN_W + KV_W
WINDOW = 128
D_FF = 2816
N_CAST = 8
EPS = 1e-6
NEG = -1e30
LOG2E = 1.4426950408889634

TM = 656
N_TILES = L // TM
META_ROW0 = SEQ - (N_TILES - 1) * TM

BQ = 128
N_QBLK = SEQ // BQ
QBLK_PER_STEP = 12
BAND = BQ + 2 * WINDOW
N_KEYS = BAND + N_META

N1 = 200
N2 = 82
K2P = 176
B_PER_STEP = 4
TCP = 25

VMEM_LIMIT = 56 * 1024 * 1024


def _ones_blockdiag():
    m = np.kron(np.eye(4), np.ones((HEAD_DIM, HEAD_DIM)))
    return jnp.asarray(m, BF16)


def _channel_dft():
    c = np.arange(F_GROUP)
    ang = 2.0 * np.pi * np.outer(c, c) / F_GROUP
    cos = np.kron(np.eye(4), np.cos(ang) / 8.0)
    sin = np.kron(np.eye(4), np.sin(ang) / 8.0)
    return np.concatenate([cos, sin], axis=1).astype(np.float32)


def _stage_a_dft():
    a = np.arange(N1, dtype=np.int64)
    m = (N2 * np.outer(a + N_META, a)) % L
    ang = 2.0 * np.pi * m / L
    return np.concatenate([np.cos(ang), -np.sin(ang)], axis=0).astype(np.float32)


def _stage_c_dft():
    c = np.arange(N1, dtype=np.int64)[:, None, None]
    d = np.arange(N2, dtype=np.int64)[None, :, None]
    b = np.arange(N2, dtype=np.int64)[None, None, :]
    m = ((c + N1 * d + N_META) * (b + N_META)) % L
    ang = 2.0 * np.pi * m / L
    ec = np.zeros((N1 // 2, K2P, K2P), np.float32)
    es = np.zeros((N1 // 2, K2P, K2P), np.float32)
    for e in range(2):
        ec[:, e:2 * N2:2, e:2 * N2:2] = (np.cos(ang) / np.sqrt(L))[e::2]
        es[:, e:2 * N2:2, e:2 * N2:2] = (np.sin(ang) / np.sqrt(L))[e::2]
    return ec, es


def _attn_bias_const():
    slopes = 2.0 ** -(np.arange(N_HEADS) + 1.0)
    i = np.arange(BQ)[:, None]
    j = np.arange(BAND)[None, :]
    m = np.arange(N_META)[None, :]
    out = np.full((4, N_KV, Q_PER_KV * BQ, N_KEYS), NEG, np.float64)
    for t in range(4):
        if t < 3:
            dist = np.abs(t * WINDOW + i - j)
            band_ok = dist <= WINDOW
            dist_m = np.minimum(t * WINDOW + N_META + i - m, WINDOW) if t == 0 else np.full((BQ, N_META), WINDOW)
        else:
            dist = N_META + j - i
            band_ok = (dist <= WINDOW) & (i < N_META)
            dist_m = np.abs(i - m)
        for h in range(N_HEADS):
            g, hl = divmod(h, Q_PER_KV)
            rows = slice(hl * BQ, (hl + 1) * BQ)
            band = np.where(band_ok, -slopes[h] * dist, NEG)
            if t == 3:
                band = np.where(i < N_META, band, 0.0)
            out[t, g, rows, :BAND] = band
            out[t, g, rows, BAND:BAND + N_META] = -slopes[h] * dist_m
    return np.ascontiguousarray(LOG2E * out.transpose(0, 1, 3, 2)).astype(np.float32)


def _rms_scale(x):
    return lax.rsqrt(jnp.mean(x * x, axis=-1, keepdims=True) + EPS)


def _unit_rms(x):
    return x * _rms_scale(x)


def _rms(x, g):
    return _unit_rms(x) * g


def _swiglu_half(x, wg_s, wu_s, wd_s):
    r = _rms_scale(x)
    xb = x.astype(BF16)
    gate = jnp.dot(xb, wg_s[...], preferred_element_type=F32) * r
    up = jnp.dot(xb, wu_s[...], preferred_element_type=F32) * r
    act = (gate * jax.nn.sigmoid(gate) * up).astype(BF16)
    return x + 0.5 * jnp.dot(act, wd_s[...], preferred_element_type=F32)


def _cast_weights(s, f32_refs, gain_refs, bf16_scratch):
    for w_ref, g_ref, w_s in zip(f32_refs, gain_refs, bf16_scratch):
        rows = w_ref.shape[0]
        w = w_ref[...] if g_ref is None else w_ref[...] * g_ref[...]
        w_s[pl.ds(pl.multiple_of(s * rows, rows), rows), :] = w.astype(BF16)


def _pre_kernel(x_ref, meta_ref, g1_ref, wg_ref, wu_ref, wd_ref, gm_ref, win_ref, gqk_ref,
                ones_ref, h1_ref, q_ref, kv_ref, f_ref, wg_s, wu_s, wd_s, win_s):
    s = pl.program_id(0)

    @pl.when(s < N_CAST)
    def _():
        _cast_weights(s, (wg_ref, wu_ref, wd_ref, win_ref), (g1_ref, g1_ref, None, gm_ref),
                      (wg_s, wu_s, wd_s, win_s))

    @pl.when(s >= N_CAST)
    def _():
        _pre_tile(s - N_CAST, x_ref, meta_ref, gqk_ref, ones_ref,
                  h1_ref, q_ref, kv_ref, f_ref, wg_s, wu_s, wd_s, win_s)


def _pre_tile(i, x_ref, meta_ref, gqk_ref, ones_ref,
              h1_ref, q_ref, kv_ref, f_ref, wg_s, wu_s, wd_s, win_s):
    tail = jnp.where(i == N_TILES - 1, meta_ref[...], x_ref[META_ROW0:, :])
    x = jnp.concatenate([x_ref[:META_ROW0, :], tail], axis=0)

    h1 = _swiglu_half(x, wg_s, wu_s, wd_s)
    h1_ref[...] = h1

    u = jnp.dot(h1.astype(BF16), win_s[...], preferred_element_type=F32) * _rms_scale(h1)

    qk = u[:, :QK_W]
    sq = (qk * qk).astype(BF16)
    ones = ones_ref[...]
    ss = jnp.concatenate([
        jnp.dot(sq[:, 0:256], ones, preferred_element_type=F32),
        jnp.dot(sq[:, 256:512], ones, preferred_element_type=F32),
        jnp.dot(sq[:, 512:640], ones[:KV_W, :KV_W], preferred_element_type=F32)], axis=1)
    qkn = qk * lax.rsqrt(ss * (1.0 / HEAD_DIM) + EPS) * gqk_ref[...]
    q_ref[...] = qkn[:, :ATTN_W].astype(BF16)
    kv_ref[...] = jnp.concatenate([qkn[:, ATTN_W:], u[:, QK_W:QK_W + KV_W]], axis=1).astype(BF16)

    f_ref[...] = u[:, QK_W + KV_W:]


def _attn_kernel(q_ref, kv_ref, bias_ref, sink_ref, o_ref):
    step = pl.program_id(0)
    kv_meta = kv_ref[SEQ:L, :]
    ones_col = (lax.broadcasted_iota(jnp.int32, (N_KEYS, 16), 1) == 0).astype(BF16)

    units = []
    for j in range(QBLK_PER_STEP):
        n = step * QBLK_PER_STEP + j
        btype = jnp.where(n == 0, 0, jnp.where(n == N_QBLK - 1, 2, jnp.where(n == N_QBLK, 3, 1)))
        start = jnp.where(n == N_QBLK, 0, jnp.clip((n - 1) * BQ, 0, SEQ - BAND))
        start = pl.multiple_of(start, BQ)
        kv_cat = jnp.concatenate([kv_ref[pl.ds(start, BAND), :], kv_meta], axis=0)
        q_ok = lax.broadcasted_iota(jnp.int32, (BQ, 1), 0) < L - n * BQ
        units.extend((j, g, btype, kv_cat, q_ok) for g in range(N_KV))

    def scores(unit):
        j, g, _, kv_cat, q_ok = unit
        q = jnp.where(q_ok, q_ref[j * BQ:(j + 1) * BQ, :], 0)
        qg = jnp.concatenate(
            [q[:, (Q_PER_KV * g + hl) * HEAD_DIM:(Q_PER_KV * g + hl + 1) * HEAD_DIM]
             for hl in range(Q_PER_KV)], axis=0)
        k = kv_cat[:, g * HEAD_DIM:(g + 1) * HEAD_DIM]
        return lax.dot_general(k, qg, (((1,), (1,)), ((), ())), preferred_element_type=F32)

    st_next = scores(units[0])
    for u, (j, g, btype, kv_cat, _) in enumerate(units):
        st = st_next + bias_ref[btype, g]
        sink = sink_ref[g]
        m = jnp.maximum(jnp.max(st, axis=0, keepdims=True), sink)
        p = jnp.exp2(st - m)
        if u + 1 < len(units):
            st_next = scores(units[u + 1])
            p = jnp.concatenate([jnp.maximum(p[:16], st_next[:16] * 0.0 - 1.0), p[16:]], axis=0)
        v1 = jnp.concatenate([kv_cat[:, KV_W + g * HEAD_DIM:KV_W + (g + 1) * HEAD_DIM], ones_col], axis=1)
        ot = lax.dot_general(v1, p.astype(BF16), (((0,), (0,)), ((), ())), preferred_element_type=F32)
        denom = ot[HEAD_DIM:HEAD_DIM + 1] + jnp.exp2(sink - m)
        ot = ot[:HEAD_DIM] / denom
        for hp in range(Q_PER_KV // 2):
            h0 = Q_PER_KV * g + 2 * hp
            two = jnp.concatenate([ot[:, 2 * hp * BQ:(2 * hp + 1) * BQ],
                                   ot[:, (2 * hp + 1) * BQ:(2 * hp + 2) * BQ]], axis=0)
            o_ref[j * BQ:(j + 1) * BQ, h0 * HEAD_DIM:(h0 + 2) * HEAD_DIM] = two.T.astype(BF16)


def _dft_a_kernel(f_ref, wa_ref, cdft_ref, g_ref, fs_ref, gs_ref):
    cdft = cdft_ref[...]
    for k in range(B_PER_STEP):
        fs_ref[...] = f_ref[:, k, 0, :]
        h = jnp.dot(wa_ref[...], fs_ref[...].astype(BF16), preferred_element_type=F32).astype(BF16)
        for half in range(2):
            p = jnp.dot(h[:, half * 256:(half + 1) * 256], cdft, preferred_element_type=F32)
            gs_ref[:, half * 256:(half + 1) * 256] = p[:N1, :256] + p[N1:, 256:]
            gs_ref[:, F_W + half * 256:F_W + (half + 1) * 256] = p[N1:, :256] - p[:N1, 256:]
        g_ref[:, k, :, :] = gs_ref[...].astype(BF16).reshape(N1 // 2, 2, 2 * F_W)


def _dft_c_kernel(g_ref, ec_ref, es_ref, y_ref, gs_ref):
    gs_ref[:, 2 * N2:, :] = jnp.zeros((TCP, K2P - 2 * N2, 2 * F_W), BF16)
    for i in range(TCP):
        gs_ref[i, :2 * N2, :] = g_ref[i].reshape(2 * N2, 2 * F_W)
        g = gs_ref[i]
        y = (jnp.dot(ec_ref[i], g[:, :F_W], preferred_element_type=F32)
             + jnp.dot(es_ref[i], g[:, F_W:], preferred_element_type=F32))
        y_ref[:, i, :, :] = _unit_rms(y[:2 * N2]).astype(BF16).reshape(N2, 2, F_W)


def _post_kernel(h1_ref, a_ref, y_ref, gmix_ref, wout_ref, g2_ref, wg_ref, wu_ref, wd_ref,
                 gfin_ref, o_ref, wg_s, wu_s, wd_s, wout_s):
    s = pl.program_id(0)

    @pl.when(s < N_CAST)
    def _():
        _cast_weights(s, (wg_ref, wu_ref, wd_ref, wout_ref), (g2_ref, g2_ref, None, gmix_ref),
                      (wg_s, wu_s, wd_s, wout_s))

    @pl.when(s >= N_CAST)
    def _():
        y = y_ref[...].reshape(TM, F_W)
        a = a_ref[...]
        h2 = (h1_ref[...]
              + jnp.dot(a, wout_s[:ATTN_W, :], preferred_element_type=F32) * _rms_scale(a.astype(F32))
              + jnp.dot(y, wout_s[ATTN_W:, :], preferred_element_type=F32))
        h3 = _swiglu_half(h2, wg_s, wu_s, wd_s)
        o_ref[...] = _rms(h3, gfin_ref[...])


def _resident(shape):
    zeros = (0,) * len(shape)
    return pl.BlockSpec(shape, lambda *_: zeros, pipeline_mode=pl.Buffered(1))


def _params():
    return pltpu.CompilerParams(dimension_semantics=("arbitrary",), vmem_limit_bytes=VMEM_LIMIT)


def _row(g):
    return g.astype(F32).reshape(1, -1)


def kernel(x, meta_tokens, ffn1_norm, ffn1_w_gate, ffn1_w_up, ffn1_w_down, mix_norm, w_in, q_norm, k_norm, sink, attn_out_norm, fourier_out_norm, w_out, ffn2_norm, ffn2_w_gate, ffn2_w_up, ffn2_w_down, final_norm):
    assert x.shape == (1, SEQ, D_MODEL) and x.dtype == F32
    x2 = x.reshape(SEQ, D_MODEL)
    gqk = jnp.concatenate([jnp.tile(q_norm.astype(F32), N_HEADS) * (HEAD_DIM ** -0.5 * LOG2E),
                           jnp.tile(k_norm.astype(F32), N_KV)]).reshape(1, QK_W)
    bf = lambda w: w.astype(BF16)

    tile_idx = lambda s: jnp.maximum(s - N_CAST, 0)
    row_tile = lambda w: pl.BlockSpec((TM, w), lambda s: (tile_idx(s), 0))
    flat_rows = lambda w: pl.BlockSpec((TM, None, w), lambda s: (tile_idx(s), 0, 0))
    cast_rows = lambda r, c: pl.BlockSpec((r // N_CAST, c), lambda s: (jnp.minimum(s, N_CAST - 1), 0))
    gain_col = lambda g: g.astype(F32).reshape(-1, 1)
    bf16_copy = lambda r, c: pltpu.VMEM((r, c), BF16)
    ffn_in = [cast_rows(D_MODEL, D_FF), cast_rows(D_MODEL, D_FF), cast_rows(D_FF, D_MODEL)]
    ffn_scratch = [bf16_copy(D_MODEL, D_FF), bf16_copy(D_MODEL, D_FF), bf16_copy(D_FF, D_MODEL)]
    f32 = lambda w: w.astype(F32)

    h1, q, kv, f = pl.pallas_call(
        _pre_kernel,
        grid=(N_CAST + N_TILES,),
        in_specs=[row_tile(D_MODEL), _resident((N_META, D_MODEL)), cast_rows(D_MODEL, 1), *ffn_in,
                  cast_rows(D_MODEL, 1), cast_rows(D_MODEL, IN_W), _resident((1, QK_W)),
                  _resident((256, 256))],
        out_specs=[row_tile(D_MODEL), row_tile(ATTN_W), row_tile(2 * KV_W), flat_rows(F_W)],
        out_shape=[jax.ShapeDtypeStruct((L, D_MODEL), F32), jax.ShapeDtypeStruct((L, ATTN_W), BF16),
                   jax.ShapeDtypeStruct((L, 2 * KV_W), BF16), jax.ShapeDtypeStruct((L, 1, F_W), F32)],
        scratch_shapes=ffn_scratch + [bf16_copy(D_MODEL, IN_W)],
        compiler_params=_params(),
        name="pre",
    )(x2, meta_tokens.astype(F32), gain_col(ffn1_norm), f32(ffn1_w_gate), f32(ffn1_w_up), f32(ffn1_w_down), gain_col(mix_norm),
      f32(w_in), gqk, _ones_blockdiag())

    sink_rows = (jnp.repeat(sink.astype(F32).reshape(N_KV, Q_PER_KV), BQ, axis=1) * LOG2E)[:, None, :]

    q_rows = pl.BlockSpec((QBLK_PER_STEP * BQ, ATTN_W), lambda s: (s, 0))
    a_out = pl.pallas_call(
        _attn_kernel,
        grid=(pl.cdiv(N_QBLK + 1, QBLK_PER_STEP),),
        in_specs=[q_rows, _resident((L, 2 * KV_W)), _resident((4, N_KV, N_KEYS, Q_PER_KV * BQ)),
                  _resident((N_KV, 1, Q_PER_KV * BQ))],
        out_specs=q_rows,
        out_shape=jax.ShapeDtypeStruct((L, ATTN_W), BF16),
        compiler_params=_params(),
        name="attn",
    )(q, kv, jnp.asarray(_attn_bias_const()), sink_rows)

    row_set = lambda w: pl.BlockSpec((N1, B_PER_STEP, 1, w), lambda s: (0, s, 0, 0))
    g = pl.pallas_call(
        _dft_a_kernel,
        grid=(pl.cdiv(N2, B_PER_STEP),),
        in_specs=[row_set(F_W), _resident((2 * N1, N1)), _resident((256, 512))],
        out_specs=pl.BlockSpec((N1 // 2, B_PER_STEP, 2, 2 * F_W), lambda s: (0, s, 0, 0)),
        out_shape=jax.ShapeDtypeStruct((N1 // 2, N2, 2, 2 * F_W), BF16),
        scratch_shapes=[pltpu.VMEM((N1, F_W), F32), pltpu.VMEM((N1, 2 * F_W), F32)],
        compiler_params=_params(),
        name="dft_a",
    )(f.reshape(N1, N2, 1, F_W), bf(jnp.asarray(_stage_a_dft())), bf(jnp.asarray(_channel_dft())))

    ec, es = _stage_c_dft()
    y = pl.pallas_call(
        _dft_c_kernel,
        grid=(N1 // (2 * TCP),),
        in_specs=[pl.BlockSpec((TCP, N2, 2, 2 * F_W), lambda i: (i, 0, 0, 0)),
                  pl.BlockSpec((TCP, K2P, K2P), lambda i: (i, 0, 0)),
                  pl.BlockSpec((TCP, K2P, K2P), lambda i: (i, 0, 0))],
        out_specs=pl.BlockSpec((N2, TCP, 2, F_W), lambda i: (0, i, 0, 0)),
        out_shape=jax.ShapeDtypeStruct((N2, N1 // 2, 2, F_W), BF16),
        scratch_shapes=[pltpu.VMEM((TCP, K2P, 2 * F_W), BF16)],
        compiler_params=_params(),
        name="dft_c",
    )(g, bf(jnp.asarray(ec)), bf(jnp.asarray(es)))

    out = pl.pallas_call(
        _post_kernel,
        grid=(N_CAST + N_TILES,),
        in_specs=[row_tile(D_MODEL), row_tile(ATTN_W),
                  pl.BlockSpec((TM // 2, 2, F_W), lambda s: (tile_idx(s), 0, 0)), cast_rows(D_MODEL, 1),
                  cast_rows(D_MODEL, D_MODEL), cast_rows(D_MODEL, 1), *ffn_in, _resident((1, D_MODEL))],
        out_specs=row_tile(D_MODEL),
        out_shape=jax.ShapeDtypeStruct((SEQ, D_MODEL), F32),
        scratch_shapes=ffn_scratch + [bf16_copy(D_MODEL, D_MODEL)],
        compiler_params=_params(),
        name="post",
    )(h1, a_out, y.reshape(L // 2, 2, F_W), gain_col(jnp.concatenate([attn_out_norm, fourier_out_norm])), f32(w_out),
      gain_col(ffn2_norm),
      f32(ffn2_w_gate), f32(ffn2_w_up), f32(ffn2_w_down), _row(final_norm))
    return out.reshape(1, SEQ, D_MODEL)
```

```python
import numpy as np
import jax
import jax.numpy as jnp
from jax import lax
from jax.experimental import pallas as pl
from jax.experimental.pallas import tpu as pltpu

F32 = jnp.float32
BF16 = jnp.bfloat16

D_MODEL = 1024
SEQ = 16384
N_META = 16
L = SEQ + N_META
HEAD_DIM = 64
N_HEADS = 8
N_KV = 2
Q_PER_KV = N_HEADS // N_KV
ATTN_W = N_HEADS * HEAD_DIM
KV_W = N_KV * HEAD_DIM
F_W = 512
F_GROUP = 64
IN_W = ATTN_W + 2 * KV_W + F_W
QK_W = ATTN_W + KV_W
WINDOW = 128
D_FF = 2816
N_CAST = 8
N_POST_CAST = 32
EPS = 1e-6
NEG = -1e30
LOG2E = 1.4426950408889634

TM = 656
N_TILES = L // TM
META_ROW0 = SEQ - (N_TILES - 1) * TM

BQ = 128
N_QBLK = SEQ // BQ
QBLK_PER_STEP = 12
BAND = BQ + 2 * WINDOW
N_KEYS = BAND + N_META

N1 = 200
N2 = 82
K2P = 176
B_PER_STEP = 4
TCP = 25

VMEM_LIMIT = 56 * 1024 * 1024


def _ones_blockdiag():
    m = np.kron(np.eye(4), np.ones((HEAD_DIM, HEAD_DIM)))
    return jnp.asarray(m, BF16)


def _channel_dft():
    c = np.arange(F_GROUP)
    ang = 2.0 * np.pi * np.outer(c, c) / F_GROUP
    cos = np.kron(np.eye(4), np.cos(ang) / 8.0)
    sin = np.kron(np.eye(4), np.sin(ang) / 8.0)
    return np.concatenate([cos, sin], axis=1).astype(np.float32)


def _stage_a_dft():
    a = np.arange(N1, dtype=np.int64)
    m = (N2 * np.outer(a + N_META, a)) % L
    ang = 2.0 * np.pi * m / L
    return np.concatenate([np.cos(ang), -np.sin(ang)], axis=0).astype(np.float32)


def _stage_c_dft():
    c = np.arange(N1, dtype=np.int64)[:, None, None]
    d = np.arange(N2, dtype=np.int64)[None, :, None]
    b = np.arange(N2, dtype=np.int64)[None, None, :]
    m = ((c + N1 * d + N_META) * (b + N_META)) % L
    ang = 2.0 * np.pi * m / L
    ec = np.zeros((N1 // 2, K2P, K2P), np.float32)
    es = np.zeros((N1 // 2, K2P, K2P), np.float32)
    for e in range(2):
        ec[:, e:2 * N2:2, e:2 * N2:2] = (np.cos(ang) / np.sqrt(L))[e::2]
        es[:, e:2 * N2:2, e:2 * N2:2] = (np.sin(ang) / np.sqrt(L))[e::2]
    return ec, es


def _attn_bias_const():
    slopes = 2.0 ** -(np.arange(N_HEADS) + 1.0)
    i = np.arange(BQ)[:, None]
    j = np.arange(BAND)[None, :]
    m = np.arange(N_META)[None, :]
    out = np.full((4, N_KV, Q_PER_KV * BQ, N_KEYS), NEG, np.float64)
    for t in range(4):
        if t < 3:
            dist = np.abs(t * WINDOW + i - j)
            band_ok = dist <= WINDOW
            dist_m = np.minimum(t * WINDOW + N_META + i - m, WINDOW) if t == 0 else np.full((BQ, N_META), WINDOW)
        else:
            dist = N_META + j - i
            band_ok = (dist <= WINDOW) & (i < N_META)
            dist_m = np.abs(i - m)
        for h in range(N_HEADS):
            g, hl = divmod(h, Q_PER_KV)
            rows = slice(hl * BQ, (hl + 1) * BQ)
            band = np.where(band_ok, -slopes[h] * dist, NEG)
            if t == 3:
                band = np.where(i < N_META, band, 0.0)
            out[t, g, rows, :BAND] = band
            out[t, g, rows, BAND:BAND + N_META] = -slopes[h] * dist_m
    return np.ascontiguousarray(LOG2E * out.transpose(0, 1, 3, 2)).astype(np.float32)


def _rms_scale(x):
    return lax.rsqrt(jnp.mean(x * x, axis=-1, keepdims=True) + EPS)


def _unit_rms(x):
    return x * _rms_scale(x)


def _rms(x, g):
    return _unit_rms(x) * g


def _swiglu_half(x, wg_s, wu_s, wd_s):
    r = _rms_scale(x)
    xb = x.astype(BF16)
    gate = jnp.dot(xb, wg_s[...], preferred_element_type=F32) * r
    up = jnp.dot(xb, wu_s[...], preferred_element_type=F32) * r
    act = (gate * jax.nn.sigmoid(gate) * up).astype(BF16)
    return x + 0.5 * jnp.dot(act, wd_s[...], preferred_element_type=F32)


def _cast_weights(s, f32_refs, gain_refs, bf16_scratch):
    for w_ref, g_ref, w_s in zip(f32_refs, gain_refs, bf16_scratch):
        rows = w_ref.shape[0]
        w = w_ref[...] if g_ref is None else w_ref[...] * g_ref[...]
        w_s[pl.ds(pl.multiple_of(s * rows, rows), rows), :] = w.astype(BF16)


def _pre_kernel(x_ref, meta_ref, g1_ref, wg_ref, wu_ref, wd_ref, gm_ref, win_ref, gqk_ref, ones_ref,
                pg2_ref, pwg_ref, pwu_ref, pwd_ref, pgmix_ref, pwout_ref,
                h1_ref, q_ref, kv_ref, f_ref, pwg_o, pwu_o, pwd_o, pwout_o, wg_s, wu_s, wd_s, win_s):
    s = pl.program_id(0)

    def cast_post_block():
        pwg_o[...] = (pwg_ref[...] * pg2_ref[...]).astype(BF16)
        pwu_o[...] = (pwu_ref[...] * pg2_ref[...]).astype(BF16)
        pwd_o[...] = pwd_ref[...].astype(BF16)
        pwout_o[...] = (pwout_ref[...] * pgmix_ref[...]).astype(BF16)

    @pl.when(s < N_CAST)
    def _():
        _cast_weights(s, (wg_ref, wu_ref, wd_ref, win_ref), (g1_ref, g1_ref, None, gm_ref),
                      (wg_s, wu_s, wd_s, win_s))
        cast_post_block()

    @pl.when(s >= N_CAST)
    def _():
        cast_post_block()
        _pre_tile(s - N_CAST, x_ref, meta_ref, gqk_ref, ones_ref,
                  h1_ref, q_ref, kv_ref, f_ref, wg_s, wu_s, wd_s, win_s)


def _pre_tile(i, x_ref, meta_ref, gqk_ref, ones_ref,
              h1_ref, q_ref, kv_ref, f_ref, wg_s, wu_s, wd_s, win_s):
    tail = jnp.where(i == N_TILES - 1, meta_ref[...], x_ref[META_ROW0:, :])
    x = jnp.concatenate([x_ref[:META_ROW0, :], tail], axis=0)

    h1 = _swiglu_half(x, wg_s, wu_s, wd_s)
    h1_ref[...] = h1

    u = jnp.dot(h1.astype(BF16), win_s[...], preferred_element_type=F32) * _rms_scale(h1)

    qk = u[:, :QK_W]
    sq = (qk * qk).astype(BF16)
    ones = ones_ref[...]
    ss = jnp.concatenate([
        jnp.dot(sq[:, 0:256], ones, preferred_element_type=F32),
        jnp.dot(sq[:, 256:512], ones, preferred_element_type=F32),
        jnp.dot(sq[:, 512:640], ones[:KV_W, :KV_W], preferred_element_type=F32)], axis=1)
    qkn = qk * lax.rsqrt(ss * (1.0 / HEAD_DIM) + EPS) * gqk_ref[...]
    q_ref[...] = qkn[:, :ATTN_W].astype(BF16)
    kv_ref[...] = jnp.concatenate([qkn[:, ATTN_W:], u[:, QK_W:QK_W + KV_W]], axis=1).astype(BF16)

    f_ref[...] = u[:, QK_W + KV_W:]


def _attn_kernel(q_ref, kv_ref, bias_ref, sink_ref, o_ref):
    step = pl.program_id(0)
    kv_meta = kv_ref[SEQ:L, :]
    ones_col = (lax.broadcasted_iota(jnp.int32, (N_KEYS, 16), 1) == 0).astype(BF16)

    units = []
    for j in range(QBLK_PER_STEP):
        n = step * QBLK_PER_STEP + j
        btype = jnp.where(n == 0, 0, jnp.where(n == N_QBLK - 1, 2, jnp.where(n == N_QBLK, 3, 1)))
        start = jnp.where(n == N_QBLK, 0, jnp.clip((n - 1) * BQ, 0, SEQ - BAND))
        start = pl.multiple_of(start, BQ)
        kv_cat = jnp.concatenate([kv_ref[pl.ds(start, BAND), :], kv_meta], axis=0)
        q_ok = lax.broadcasted_iota(jnp.int32, (BQ, 1), 0) < L - n * BQ
        units.extend((j, g, btype, kv_cat, q_ok) for g in range(N_KV))

    def scores(unit):
        j, g, _, kv_cat, q_ok = unit
        q = jnp.where(q_ok, q_ref[j * BQ:(j + 1) * BQ, :], 0)
        qg = jnp.concatenate(
            [q[:, (Q_PER_KV * g + hl) * HEAD_DIM:(Q_PER_KV * g + hl + 1) * HEAD_DIM]
             for hl in range(Q_PER_KV)], axis=0)
        k = kv_cat[:, g * HEAD_DIM:(g + 1) * HEAD_DIM]
        return lax.dot_general(k, qg, (((1,), (1,)), ((), ())), preferred_element_type=F32)

    st_next = scores(units[0])
    for u, (j, g, btype, kv_cat, _) in enumerate(units):
        st = st_next + bias_ref[btype, g]
        sink = sink_ref[g]
        m = jnp.maximum(jnp.max(st, axis=0, keepdims=True), sink)
        p = jnp.exp2(st - m)
        if u + 1 < len(units):
            st_next = scores(units[u + 1])
            p = jnp.concatenate([jnp.maximum(p[:16], st_next[:16] * 0.0 - 1.0), p[16:]], axis=0)
        v1 = jnp.concatenate([kv_cat[:, KV_W + g * HEAD_DIM:KV_W + (g + 1) * HEAD_DIM], ones_col], axis=1)
        ot = lax.dot_general(v1, p.astype(BF16), (((0,), (0,)), ((), ())), preferred_element_type=F32)
        denom = ot[HEAD_DIM:HEAD_DIM + 1] + jnp.exp2(sink - m)
        ot = ot[:HEAD_DIM] / denom
        for hp in range(Q_PER_KV // 2):
            h0 = Q_PER_KV * g + 2 * hp
            two = jnp.concatenate([ot[:, 2 * hp * BQ:(2 * hp + 1) * BQ],
                                   ot[:, (2 * hp + 1) * BQ:(2 * hp + 2) * BQ]], axis=0)
            o_ref[j * BQ:(j + 1) * BQ, h0 * HEAD_DIM:(h0 + 2) * HEAD_DIM] = two.T.astype(BF16)


def _dft_a_kernel(f_ref, wa_ref, cdft_ref, g_ref, fs_ref, gs_ref):
    cdft = cdft_ref[...]
    for k in range(B_PER_STEP):
        fs_ref[...] = f_ref[:, k, 0, :]
        h = jnp.dot(wa_ref[...], fs_ref[...].astype(BF16), preferred_element_type=F32).astype(BF16)
        for half in range(2):
            p = jnp.dot(h[:, half * 256:(half + 1) * 256], cdft, preferred_element_type=F32)
            gs_ref[:, half * 256:(half + 1) * 256] = p[:N1, :256] + p[N1:, 256:]
            gs_ref[:, F_W + half * 256:F_W + (half + 1) * 256] = p[N1:, :256] - p[:N1, 256:]
        g_ref[:, k, :, :] = gs_ref[...].astype(BF16).reshape(N1 // 2, 2, 2 * F_W)


def _dft_c_kernel(g_ref, ec_ref, es_ref, y_ref, gs_ref):
    gs_ref[:, 2 * N2:, :] = jnp.zeros((TCP, K2P - 2 * N2, 2 * F_W), BF16)
    for i in range(TCP):
        gs_ref[i, :2 * N2, :] = g_ref[i].reshape(2 * N2, 2 * F_W)
        g = gs_ref[i]
        y = (jnp.dot(ec_ref[i], g[:, :F_W], preferred_element_type=F32)
             + jnp.dot(es_ref[i], g[:, F_W:], preferred_element_type=F32))
        y_ref[:, i, :, :] = _unit_rms(y[:2 * N2]).astype(BF16).reshape(N2, 2, F_W)


def _post_kernel(h1_ref, a_ref, y_ref, wout_ref, wg_ref, wu_ref, wd_ref, gfin_ref, o_ref):
    y = y_ref[...].reshape(TM, F_W)
    a = a_ref[...]
    h2 = (h1_ref[...]
          + jnp.dot(a, wout_ref[:ATTN_W, :], preferred_element_type=F32) * _rms_scale(a.astype(F32))
          + jnp.dot(y, wout_ref[ATTN_W:, :], preferred_element_type=F32))
    h3 = _swiglu_half(h2, wg_ref, wu_ref, wd_ref)
    o_ref[...] = _rms(h3, gfin_ref[...])


def _resident(shape):
    zeros = (0,) * len(shape)
    return pl.BlockSpec(shape, lambda *_: zeros, pipeline_mode=pl.Buffered(1))


def _params():
    return pltpu.CompilerParams(dimension_semantics=("arbitrary",), vmem_limit_bytes=VMEM_LIMIT)


def _row(g):
    return g.astype(F32).reshape(1, -1)


def kernel(x, meta_tokens, ffn1_norm, ffn1_w_gate, ffn1_w_up, ffn1_w_down, mix_norm, w_in, q_norm, k_norm, sink, attn_out_norm, fourier_out_norm, w_out, ffn2_norm, ffn2_w_gate, ffn2_w_up, ffn2_w_down, final_norm):
    assert x.shape == (1, SEQ, D_MODEL) and x.dtype == F32
    x2 = x.reshape(SEQ, D_MODEL)
    gqk = jnp.concatenate([jnp.tile(q_norm.astype(F32), N_HEADS) * (HEAD_DIM ** -0.5 * LOG2E),
                           jnp.tile(k_norm.astype(F32), N_KV)]).reshape(1, QK_W)
    bf = lambda w: w.astype(BF16)

    tile_idx = lambda s: jnp.maximum(s - N_CAST, 0)
    row_tile = lambda w: pl.BlockSpec((TM, w), lambda s: (tile_idx(s), 0))
    flat_rows = lambda w: pl.BlockSpec((TM, None, w), lambda s: (tile_idx(s), 0, 0))
    cast_rows = lambda r, c: pl.BlockSpec((r // N_CAST, c), lambda s: (jnp.minimum(s, N_CAST - 1), 0))
    gain_col = lambda g: g.astype(F32).reshape(-1, 1)
    bf16_copy = lambda r, c: pltpu.VMEM((r, c), BF16)
    ffn_in = [cast_rows(D_MODEL, D_FF), cast_rows(D_MODEL, D_FF), cast_rows(D_FF, D_MODEL)]
    ffn_scratch = [bf16_copy(D_MODEL, D_FF), bf16_copy(D_MODEL, D_FF), bf16_copy(D_FF, D_MODEL)]
    f32 = lambda w: w.astype(F32)

    assert N_POST_CAST <= N_CAST + N_TILES
    post_rows = lambda r, c: pl.BlockSpec((r // N_POST_CAST, c), lambda s: (jnp.minimum(s, N_POST_CAST - 1), 0))
    post_w = [post_rows(D_MODEL, D_FF), post_rows(D_MODEL, D_FF), post_rows(D_FF, D_MODEL)]
    post_w_shapes = [(D_MODEL, D_FF), (D_MODEL, D_FF), (D_FF, D_MODEL), (D_MODEL, D_MODEL)]

    h1, q, kv, f, wg2, wu2, wd2, wout2 = pl.pallas_call(
        _pre_kernel,
        grid=(N_CAST + N_TILES,),
        in_specs=[row_tile(D_MODEL), _resident((N_META, D_MODEL)), cast_rows(D_MODEL, 1), *ffn_in,
                  cast_rows(D_MODEL, 1), cast_rows(D_MODEL, IN_W), _resident((1, QK_W)),
                  _resident((256, 256)),
                  post_rows(D_MODEL, 1), *post_w, post_rows(D_MODEL, 1), post_rows(D_MODEL, D_MODEL)],
        out_specs=[row_tile(D_MODEL), row_tile(ATTN_W), row_tile(2 * KV_W), flat_rows(F_W),
                   *post_w, post_rows(D_MODEL, D_MODEL)],
        out_shape=[jax.ShapeDtypeStruct((L, D_MODEL), F32), jax.ShapeDtypeStruct((L, ATTN_W), BF16),
                   jax.ShapeDtypeStruct((L, 2 * KV_W), BF16), jax.ShapeDtypeStruct((L, 1, F_W), F32),
                   *[jax.ShapeDtypeStruct(shape, BF16) for shape in post_w_shapes]],
        scratch_shapes=ffn_scratch + [bf16_copy(D_MODEL, IN_W)],
        compiler_params=_params(),
        name="pre",
    )(x2, meta_tokens.astype(F32), gain_col(ffn1_norm), f32(ffn1_w_gate), f32(ffn1_w_up), f32(ffn1_w_down), gain_col(mix_norm),
      f32(w_in), gqk, _ones_blockdiag(),
      gain_col(ffn2_norm), f32(ffn2_w_gate), f32(ffn2_w_up), f32(ffn2_w_down),
      gain_col(jnp.concatenate([attn_out_norm, fourier_out_norm])), f32(w_out))

    sink_rows = (jnp.repeat(sink.astype(F32).reshape(N_KV, Q_PER_KV), BQ, axis=1) * LOG2E)[:, None, :]

    q_rows = pl.BlockSpec((QBLK_PER_STEP * BQ, ATTN_W), lambda s: (s, 0))
    a_out = pl.pallas_call(
        _attn_kernel,
        grid=(pl.cdiv(N_QBLK + 1, QBLK_PER_STEP),),
        in_specs=[q_rows, _resident((L, 2 * KV_W)), _resident((4, N_KV, N_KEYS, Q_PER_KV * BQ)),
                  _resident((N_KV, 1, Q_PER_KV * BQ))],
        out_specs=q_rows,
        out_shape=jax.ShapeDtypeStruct((L, ATTN_W), BF16),
        compiler_params=_params(),
        name="attn",
    )(q, kv, jnp.asarray(_attn_bias_const()), sink_rows)

    row_set = lambda w: pl.BlockSpec((N1, B_PER_STEP, 1, w), lambda s: (0, s, 0, 0))
    g = pl.pallas_call(
        _dft_a_kernel,
        grid=(pl.cdiv(N2, B_PER_STEP),),
        in_specs=[row_set(F_W), _resident((2 * N1, N1)), _resident((256, 512))],
        out_specs=pl.BlockSpec((N1 // 2, B_PER_STEP, 2, 2 * F_W), lambda s: (0, s, 0, 0)),
        out_shape=jax.ShapeDtypeStruct((N1 // 2, N2, 2, 2 * F_W), BF16),
        scratch_shapes=[pltpu.VMEM((N1, F_W), F32), pltpu.VMEM((N1, 2 * F_W), F32)],
        compiler_params=_params(),
        name="dft_a",
    )(f.reshape(N1, N2, 1, F_W), bf(jnp.asarray(_stage_a_dft())), bf(jnp.asarray(_channel_dft())))

    ec, es = _stage_c_dft()
    y = pl.pallas_call(
        _dft_c_kernel,
        grid=(N1 // (2 * TCP),),
        in_specs=[pl.BlockSpec((TCP, N2, 2, 2 * F_W), lambda i: (i, 0, 0, 0)),
                  pl.BlockSpec((TCP, K2P, K2P), lambda i: (i, 0, 0)),
                  pl.BlockSpec((TCP, K2P, K2P), lambda i: (i, 0, 0))],
        out_specs=pl.BlockSpec((N2, TCP, 2, F_W), lambda i: (0, i, 0, 0)),
        out_shape=jax.ShapeDtypeStruct((N2, N1 // 2, 2, F_W), BF16),
        scratch_shapes=[pltpu.VMEM((TCP, K2P, 2 * F_W), BF16)],
        compiler_params=_params(),
        name="dft_c",
    )(g, bf(jnp.asarray(ec)), bf(jnp.asarray(es)))

    post_tile = lambda w: pl.BlockSpec((TM, w), lambda i: (i, 0))
    out = pl.pallas_call(
        _post_kernel,
        grid=(N_TILES,),
        in_specs=[post_tile(D_MODEL), post_tile(ATTN_W), pl.BlockSpec((TM // 2, 2, F_W), lambda i: (i, 0, 0)),
                  _resident((D_MODEL, D_MODEL)), _resident((D_MODEL, D_FF)), _resident((D_MODEL, D_FF)),
                  _resident((D_FF, D_MODEL)), _resident((1, D_MODEL))],
        out_specs=post_tile(D_MODEL),
        out_shape=jax.ShapeDtypeStruct((SEQ, D_MODEL), F32),
        compiler_params=_params(),
        name="post",
    )(h1, a_out, y.reshape(L // 2, 2, F_W), wout2, wg2, wu2, wd2, _row(final_norm))
    return out.reshape(1, SEQ, D_MODEL)
```

```python
import numpy as np
import jax
import jax.numpy as jnp
from jax import lax
from jax.experimental import pallas as pl
from jax.experimental.pallas import tpu as pltpu

F32 = jnp.float32
BF16 = jnp.bfloat16

D_MODEL = 1024
SEQ = 16384
N_META = 16
L = SEQ + N_META
HEAD_DIM = 64
N_HEADS = 8
N_KV = 2
Q_PER_KV = N_HEADS // N_KV
ATTN_W = N_HEADS * HEAD_DIM
KV_W = N_KV * HEAD_DIM
F_W = 512
F_GROUP = 64
IN_W = ATTN_W + 2 * KV_W + F_W
QK_W = ATTN_W + KV_W
WINDOW = 128
D_FF = 2816
N_CAST = 8
EPS = 1e-6
NEG = -1e30
LOG2E = 1.4426950408889634

TM = 656
N_TILES = L // TM
META_ROW0 = SEQ - (N_TILES - 1) * TM

BQ = 128
N_QBLK = SEQ // BQ
QBLK_PER_STEP = 12
BAND = BQ + 2 * WINDOW
N_KEYS = BAND + N_META

N1 = 200
N2 = 82
K2P = 176
B_PER_STEP = 4
TCP = 25

VMEM_LIMIT = 56 * 1024 * 1024


def _ones_blockdiag():
    m = np.kron(np.eye(4), np.ones((HEAD_DIM, HEAD_DIM)))
    return jnp.asarray(m, BF16)


def _channel_dft():
    c = np.arange(F_GROUP)
    ang = 2.0 * np.pi * np.outer(c, c) / F_GROUP
    cos = np.kron(np.eye(4), np.cos(ang) / 8.0)
    sin = np.kron(np.eye(4), np.sin(ang) / 8.0)
    return np.concatenate([cos, sin], axis=1).astype(np.float32)


def _stage_a_dft():
    a = np.arange(N1, dtype=np.int64)
    m = (N2 * np.outer(a + N_META, a)) % L
    ang = 2.0 * np.pi * m / L
    return np.concatenate([np.cos(ang), -np.sin(ang)], axis=0).astype(np.float32)


def _stage_c_dft():
    c = np.arange(N1, dtype=np.int64)[:, None, None]
    d = np.arange(N2, dtype=np.int64)[None, :, None]
    b = np.arange(N2, dtype=np.int64)[None, None, :]
    m = ((c + N1 * d + N_META) * (b + N_META)) % L
    ang = 2.0 * np.pi * m / L
    ec = np.zeros((N1 // 2, K2P, K2P), np.float32)
    es = np.zeros((N1 // 2, K2P, K2P), np.float32)
    for e in range(2):
        ec[:, e:2 * N2:2, e:2 * N2:2] = (np.cos(ang) / np.sqrt(L))[e::2]
        es[:, e:2 * N2:2, e:2 * N2:2] = (np.sin(ang) / np.sqrt(L))[e::2]
    return ec, es


def _attn_bias_const():
    slopes = 2.0 ** -(np.arange(N_HEADS) + 1.0)
    i = np.arange(BQ)[:, None]
    j = np.arange(BAND)[None, :]
    m = np.arange(N_META)[None, :]
    out = np.full((4, N_KV, Q_PER_KV * BQ, N_KEYS), NEG, np.float64)
    for t in range(4):
        if t < 3:
            dist = np.abs(t * WINDOW + i - j)
            band_ok = dist <= WINDOW
            dist_m = np.minimum(t * WINDOW + N_META + i - m, WINDOW) if t == 0 else np.full((BQ, N_META), WINDOW)
        else:
            dist = N_META + j - i
            band_ok = (dist <= WINDOW) & (i < N_META)
            dist_m = np.abs(i - m)
        for h in range(N_HEADS):
            g, hl = divmod(h, Q_PER_KV)
            rows = slice(hl * BQ, (hl + 1) * BQ)
            band = np.where(band_ok, -slopes[h] * dist, NEG)
            if t == 3:
                band = np.where(i < N_META, band, 0.0)
            out[t, g, rows, :BAND] = band
            out[t, g, rows, BAND:BAND + N_META] = -slopes[h] * dist_m
    return np.ascontiguousarray(LOG2E * out.transpose(0, 1, 3, 2)).astype(np.float32)


def _rms_scale(x):
    return lax.rsqrt(jnp.mean(x * x, axis=-1, keepdims=True) + EPS)


def _unit_rms(x):
    return x * _rms_scale(x)


def _rms(x, g):
    return _unit_rms(x) * g


def _swiglu_half(x, wg_s, wu_s, wd_s):
    r = _rms_scale(x)
    xb = x.astype(BF16)
    gate = jnp.dot(xb, wg_s[...], preferred_element_type=F32) * r
    up = jnp.dot(xb, wu_s[...], preferred_element_type=F32) * r
    act = (gate * jax.nn.sigmoid(gate) * up).astype(BF16)
    return x + 0.5 * jnp.dot(act, wd_s[...], preferred_element_type=F32)


def _cast_weights(s, f32_refs, gain_refs, bf16_scratch):
    for w_ref, g_ref, w_s in zip(f32_refs, gain_refs, bf16_scratch):
        rows = w_ref.shape[0]
        w = w_ref[...]
        if g_ref is not None:
            w = w * jnp.tile(g_ref[...], (1, w.shape[1] // g_ref.shape[1]))
        w_s[pl.ds(pl.multiple_of(s * rows, rows), rows), :] = w.astype(BF16)


def _pre_kernel(x_ref, meta_ref, g1_ref, wg_ref, wu_ref, wd_ref, gm_ref, win_ref, gqk_ref,
                ones_ref, h1_ref, q_ref, kv_ref, f_ref, wg_s, wu_s, wd_s, win_s):
    s = pl.program_id(0)

    @pl.when(s < N_CAST)
    def _():
        _cast_weights(s, (wg_ref, wu_ref, wd_ref, win_ref), (g1_ref, g1_ref, None, gm_ref),
                      (wg_s, wu_s, wd_s, win_s))

    @pl.when(s >= N_CAST)
    def _():
        _pre_tile(s - N_CAST, x_ref, meta_ref, gqk_ref, ones_ref,
                  h1_ref, q_ref, kv_ref, f_ref, wg_s, wu_s, wd_s, win_s)


def _pre_tile(i, x_ref, meta_ref, gqk_ref, ones_ref,
              h1_ref, q_ref, kv_ref, f_ref, wg_s, wu_s, wd_s, win_s):
    tail = jnp.where(i == N_TILES - 1, meta_ref[...], x_ref[META_ROW0:, :])
    x = jnp.concatenate([x_ref[:META_ROW0, :], tail], axis=0)

    h1 = _swiglu_half(x, wg_s, wu_s, wd_s)
    h1_ref[...] = h1

    u = jnp.dot(h1.astype(BF16), win_s[...], preferred_element_type=F32) * _rms_scale(h1)

    qk = u[:, :QK_W]
    sq = (qk * qk).astype(BF16)
    ones = ones_ref[...]
    ss = jnp.concatenate([
        jnp.dot(sq[:, 0:256], ones, preferred_element_type=F32),
        jnp.dot(sq[:, 256:512], ones, preferred_element_type=F32),
        jnp.dot(sq[:, 512:640], ones[:KV_W, :KV_W], preferred_element_type=F32)], axis=1)
    qkn = qk * lax.rsqrt(ss * (1.0 / HEAD_DIM) + EPS) * gqk_ref[...]
    q_ref[...] = qkn[:, :ATTN_W].astype(BF16)
    kv_ref[...] = jnp.concatenate([qkn[:, ATTN_W:], u[:, QK_W:QK_W + KV_W]], axis=1).astype(BF16)

    f_ref[...] = u[:, QK_W + KV_W:]


def _attn_kernel(q_ref, kv_ref, bias_ref, sink_ref, o_ref):
    step = pl.program_id(0)
    kv_meta = kv_ref[SEQ:L, :]
    ones_col = (lax.broadcasted_iota(jnp.int32, (N_KEYS, 16), 1) == 0).astype(BF16)

    units = []
    for j in range(QBLK_PER_STEP):
        n = step * QBLK_PER_STEP + j
        btype = jnp.where(n == 0, 0, jnp.where(n == N_QBLK - 1, 2, jnp.where(n == N_QBLK, 3, 1)))
        start = jnp.where(n == N_QBLK, 0, jnp.clip((n - 1) * BQ, 0, SEQ - BAND))
        start = pl.multiple_of(start, BQ)
        kv_cat = jnp.concatenate([kv_ref[pl.ds(start, BAND), :], kv_meta], axis=0)
        q_ok = lax.broadcasted_iota(jnp.int32, (BQ, 1), 0) < L - n * BQ
        units.extend((j, g, btype, kv_cat, q_ok) for g in range(N_KV))

    def scores(unit):
        j, g, _, kv_cat, q_ok = unit
        q = jnp.where(q_ok, q_ref[j * BQ:(j + 1) * BQ, :], 0)
        qg = jnp.concatenate(
            [q[:, (Q_PER_KV * g + hl) * HEAD_DIM:(Q_PER_KV * g + hl + 1) * HEAD_DIM]
             for hl in range(Q_PER_KV)], axis=0)
        k = kv_cat[:, g * HEAD_DIM:(g + 1) * HEAD_DIM]
        return lax.dot_general(k, qg, (((1,), (1,)), ((), ())), preferred_element_type=F32)

    st_next = scores(units[0])
    for u, (j, g, btype, kv_cat, _) in enumerate(units):
        st = st_next + bias_ref[btype, g]
        sink = sink_ref[g]
        m = jnp.maximum(jnp.max(st, axis=0, keepdims=True), sink)
        p = jnp.exp2(st - m)
        if u + 1 < len(units):
            st_next = scores(units[u + 1])
            p = jnp.concatenate([jnp.maximum(p[:16], st_next[:16] * 0.0 - 1.0), p[16:]], axis=0)
        v1 = jnp.concatenate([kv_cat[:, KV_W + g * HEAD_DIM:KV_W + (g + 1) * HEAD_DIM], ones_col], axis=1)
        ot = lax.dot_general(v1, p.astype(BF16), (((0,), (0,)), ((), ())), preferred_element_type=F32)
        denom = ot[HEAD_DIM:HEAD_DIM + 1] + jnp.exp2(sink - m)
        ot = ot[:HEAD_DIM] / denom
        for hp in range(Q_PER_KV // 2):
            h0 = Q_PER_KV * g + 2 * hp
            two = jnp.concatenate([ot[:, 2 * hp * BQ:(2 * hp + 1) * BQ],
                                   ot[:, (2 * hp + 1) * BQ:(2 * hp + 2) * BQ]], axis=0)
            o_ref[j * BQ:(j + 1) * BQ, h0 * HEAD_DIM:(h0 + 2) * HEAD_DIM] = two.T.astype(BF16)


def _dft_a_kernel(f_ref, wa_ref, cdft_ref, g_ref, fs_ref, gs_ref):
    cdft = cdft_ref[...]
    for k in range(B_PER_STEP):
        fs_ref[...] = f_ref[:, k, 0, :]
        h = jnp.dot(wa_ref[...], fs_ref[...].astype(BF16), preferred_element_type=F32).astype(BF16)
        for half in range(2):
            p = jnp.dot(h[:, half * 256:(half + 1) * 256], cdft, preferred_element_type=F32)
            gs_ref[:, half * 256:(half + 1) * 256] = p[:N1, :256] + p[N1:, 256:]
            gs_ref[:, F_W + half * 256:F_W + (half + 1) * 256] = p[N1:, :256] - p[:N1, 256:]
        g_ref[:, k, :, :] = gs_ref[...].astype(BF16).reshape(N1 // 2, 2, 2 * F_W)


def _dft_c_kernel(g_ref, ec_ref, es_ref, y_ref, gs_ref):
    gs_ref[:, 2 * N2:, :] = jnp.zeros((TCP, K2P - 2 * N2, 2 * F_W), BF16)
    for i in range(TCP):
        gs_ref[i, :2 * N2, :] = g_ref[i].reshape(2 * N2, 2 * F_W)
        g = gs_ref[i]
        y = (jnp.dot(ec_ref[i], g[:, :F_W], preferred_element_type=F32)
             + jnp.dot(es_ref[i], g[:, F_W:], preferred_element_type=F32))
        y_ref[:, i, :, :] = _unit_rms(y[:2 * N2]).astype(BF16).reshape(N2, 2, F_W)


def _post_kernel(h1_ref, a_ref, y_ref, gmix_ref, wout_ref, g2_ref, wg_ref, wu_ref, wd_ref,
                 gfin_ref, o_ref, wg_s, wu_s, wd_s, wout_s):
    s = pl.program_id(0)

    @pl.when(s < N_CAST)
    def _():
        _cast_weights(s, (wg_ref, wu_ref, wd_ref, wout_ref), (g2_ref, g2_ref, None, gmix_ref),
                      (wg_s, wu_s, wd_s, wout_s))

    @pl.when(s >= N_CAST)
    def _():
        y = y_ref[...].reshape(TM, F_W)
        a = a_ref[...]
        h2 = (h1_ref[...]
              + jnp.dot(a, wout_s[:ATTN_W, :], preferred_element_type=F32) * _rms_scale(a.astype(F32))
              + jnp.dot(y, wout_s[ATTN_W:, :], preferred_element_type=F32))
        h3 = _swiglu_half(h2, wg_s, wu_s, wd_s)
        o_ref[...] = _rms(h3, gfin_ref[...])


def _resident(shape):
    zeros = (0,) * len(shape)
    return pl.BlockSpec(shape, lambda *_: zeros, pipeline_mode=pl.Buffered(1))


def _params():
    return pltpu.CompilerParams(dimension_semantics=("arbitrary",), vmem_limit_bytes=VMEM_LIMIT)


def _row(g):
    return g.astype(F32).reshape(1, -1)


def kernel(x, meta_tokens, ffn1_norm, ffn1_w_gate, ffn1_w_up, ffn1_w_down, mix_norm, w_in, q_norm, k_norm, sink, attn_out_norm, fourier_out_norm, w_out, ffn2_norm, ffn2_w_gate, ffn2_w_up, ffn2_w_down, final_norm):
    assert x.shape == (1, SEQ, D_MODEL) and x.dtype == F32
    x2 = x.reshape(SEQ, D_MODEL)
    gqk = jnp.concatenate([jnp.tile(q_norm.astype(F32), N_HEADS) * (HEAD_DIM ** -0.5 * LOG2E),
                           jnp.tile(k_norm.astype(F32), N_KV)]).reshape(1, QK_W)
    bf = lambda w: w.astype(BF16)

    tile_idx = lambda s: jnp.maximum(s - N_CAST, 0)
    row_tile = lambda w: pl.BlockSpec((TM, w), lambda s: (tile_idx(s), 0))
    flat_rows = lambda w: pl.BlockSpec((TM, None, w), lambda s: (tile_idx(s), 0, 0))
    cast_rows = lambda r, c: pl.BlockSpec((r // N_CAST, c), lambda s: (jnp.minimum(s, N_CAST - 1), 0))
    gains = jnp.broadcast_to(
        jnp.stack([ffn1_norm, mix_norm, jnp.concatenate([attn_out_norm, fourier_out_norm]), ffn2_norm])
        .astype(F32)[:, :, None], (4, D_MODEL, 128))
    gain_rows = lambda k: pl.BlockSpec((None, D_MODEL // N_CAST, 128), lambda s: (k, jnp.minimum(s, N_CAST - 1), 0))
    bf16_copy = lambda r, c: pltpu.VMEM((r, c), BF16)
    ffn_in = [cast_rows(D_MODEL, D_FF), cast_rows(D_MODEL, D_FF), cast_rows(D_FF, D_MODEL)]
    ffn_scratch = [bf16_copy(D_MODEL, D_FF), bf16_copy(D_MODEL, D_FF), bf16_copy(D_FF, D_MODEL)]
    f32 = lambda w: w.astype(F32)

    h1, q, kv, f = pl.pallas_call(
        _pre_kernel,
        grid=(N_CAST + N_TILES,),
        in_specs=[row_tile(D_MODEL), _resident((N_META, D_MODEL)), gain_rows(0), *ffn_in,
                  gain_rows(1), cast_rows(D_MODEL, IN_W), _resident((1, QK_W)),
                  _resident((256, 256))],
        out_specs=[row_tile(D_MODEL), row_tile(ATTN_W), row_tile(2 * KV_W), flat_rows(F_W)],
        out_shape=[jax.ShapeDtypeStruct((L, D_MODEL), F32), jax.ShapeDtypeStruct((L, ATTN_W), BF16),
                   jax.ShapeDtypeStruct((L, 2 * KV_W), BF16), jax.ShapeDtypeStruct((L, 1, F_W), F32)],
        scratch_shapes=ffn_scratch + [bf16_copy(D_MODEL, IN_W)],
        compiler_params=_params(),
        name="pre",
    )(x2, meta_tokens.astype(F32), gains, f32(ffn1_w_gate), f32(ffn1_w_up), f32(ffn1_w_down), gains,
      f32(w_in), gqk, _ones_blockdiag())

    sink_rows = (jnp.repeat(sink.astype(F32).reshape(N_KV, Q_PER_KV), BQ, axis=1) * LOG2E)[:, None, :]

    q_rows = pl.BlockSpec((QBLK_PER_STEP * BQ, ATTN_W), lambda s: (s, 0))
    a_out = pl.pallas_call(
        _attn_kernel,
        grid=(pl.cdiv(N_QBLK + 1, QBLK_PER_STEP),),
        in_specs=[q_rows, _resident((L, 2 * KV_W)), _resident((4, N_KV, N_KEYS, Q_PER_KV * BQ)),
                  _resident((N_KV, 1, Q_PER_KV * BQ))],
        out_specs=q_rows,
        out_shape=jax.ShapeDtypeStruct((L, ATTN_W), BF16),
        compiler_params=_params(),
        name="attn",
    )(q, kv, jnp.asarray(_attn_bias_const()), sink_rows)

    row_set = lambda w: pl.BlockSpec((N1, B_PER_STEP, 1, w), lambda s: (0, s, 0, 0))
    g = pl.pallas_call(
        _dft_a_kernel,
        grid=(pl.cdiv(N2, B_PER_STEP),),
        in_specs=[row_set(F_W), _resident((2 * N1, N1)), _resident((256, 512))],
        out_specs=pl.BlockSpec((N1 // 2, B_PER_STEP, 2, 2 * F_W), lambda s: (0, s, 0, 0)),
        out_shape=jax.ShapeDtypeStruct((N1 // 2, N2, 2, 2 * F_W), BF16),
        scratch_shapes=[pltpu.VMEM((N1, F_W), F32), pltpu.VMEM((N1, 2 * F_W), F32)],
        compiler_params=_params(),
        name="dft_a",
    )(f.reshape(N1, N2, 1, F_W), bf(jnp.asarray(_stage_a_dft())), bf(jnp.asarray(_channel_dft())))

    ec, es = _stage_c_dft()
    y = pl.pallas_call(
        _dft_c_kernel,
        grid=(N1 // (2 * TCP),),
        in_specs=[pl.BlockSpec((TCP, N2, 2, 2 * F_W), lambda i: (i, 0, 0, 0)),
                  pl.BlockSpec((TCP, K2P, K2P), lambda i: (i, 0, 0)),
                  pl.BlockSpec((TCP, K2P, K2P), lambda i: (i, 0, 0))],
        out_specs=pl.BlockSpec((N2, TCP, 2, F_W), lambda i: (0, i, 0, 0)),
        out_shape=jax.ShapeDtypeStruct((N2, N1 // 2, 2, F_W), BF16),
        scratch_shapes=[pltpu.VMEM((TCP, K2P, 2 * F_W), BF16)],
        compiler_params=_params(),
        name="dft_c",
    )(g, bf(jnp.asarray(ec)), bf(jnp.asarray(es)))

    out = pl.pallas_call(
        _post_kernel,
        grid=(N_CAST + N_TILES,),
        in_specs=[row_tile(D_MODEL), row_tile(ATTN_W),
                  pl.BlockSpec((TM // 2, 2, F_W), lambda s: (tile_idx(s), 0, 0)), gain_rows(2),
                  cast_rows(D_MODEL, D_MODEL), gain_rows(3), *ffn_in, _resident((1, D_MODEL))],
        out_specs=row_tile(D_MODEL),
        out_shape=jax.ShapeDtypeStruct((SEQ, D_MODEL), F32),
        scratch_shapes=ffn_scratch + [bf16_copy(D_MODEL, D_MODEL)],
        compiler_params=_params(),
        name="post",
    )(h1, a_out, y.reshape(L // 2, 2, F_W), gains, f32(w_out), gains,
      f32(ffn2_w_gate), f32(ffn2_w_up), f32(ffn2_w_down), _row(final_norm))
    return out.reshape(1, SEQ, D_MODEL)
```

```python
import numpy as np
import jax
import jax.numpy as jnp
from jax import lax
from jax.experimental import pallas as pl
from jax.experimental.pallas import tpu as pltpu

F32 = jnp.float32
BF16 = jnp.bfloat16

D_MODEL = 1024
SEQ = 16384
N_META = 16
L = SEQ + N_META
HEAD_DIM = 64
N_HEADS = 8
N_KV = 2
Q_PER_KV = N_HEADS // N_KV
ATTN_W = N_HEADS * HEAD_DIM
KV_W = N_KV * HEAD_DIM
F_W = 512
F_GROUP = 64
IN_W = ATTN_W + 2 * KV_W + F_W
QK_W = ATTN_W + KV_W
WINDOW = 128
D_FF = 2816
N_CAST = 8
EPS = 1e-6
NEG = -1e30
LOG2E = 1.4426950408889634

TM = 656
N_TILES = L // TM
META_ROW0 = SEQ - (N_TILES - 1) * TM

BQ = 128
N_QBLK = SEQ // BQ
QBLK_PER_STEP = 12
BAND = BQ + 2 * WINDOW
N_KEYS = BAND + N_META

N1 = 200
N2 = 82
K2P = 176
B_PER_STEP = 7
TCP = 10

VMEM_LIMIT = 56 * 1024 * 1024


def _ones_blockdiag():
    m = np.kron(np.eye(4), np.ones((HEAD_DIM, HEAD_DIM)))
    return jnp.asarray(m, BF16)


def _channel_dft():
    c = np.arange(F_GROUP)
    ang = 2.0 * np.pi * np.outer(c, c) / F_GROUP
    cos = np.kron(np.eye(4), np.cos(ang) / 8.0)
    sin = np.kron(np.eye(4), np.sin(ang) / 8.0)
    return np.concatenate([cos, sin], axis=1).astype(np.float32)


def _stage_a_dft():
    a = np.arange(N1, dtype=np.int64)
    m = (N2 * np.outer(a + N_META, a)) % L
    ang = 2.0 * np.pi * m / L
    return np.concatenate([np.cos(ang), -np.sin(ang)], axis=0).astype(np.float32)


def _stage_c_dft():
    c = np.arange(N1, dtype=np.int64)[:, None, None]
    d = np.arange(N2, dtype=np.int64)[None, :, None]
    b = np.arange(N2, dtype=np.int64)[None, None, :]
    m = ((c + N1 * d + N_META) * (b + N_META)) % L
    ang = 2.0 * np.pi * m / L
    ec = np.zeros((N1 // 2, K2P, K2P), np.float32)
    es = np.zeros((N1 // 2, K2P, K2P), np.float32)
    for e in range(2):
        ec[:, e:2 * N2:2, e:2 * N2:2] = (np.cos(ang) / np.sqrt(L))[e::2]
        es[:, e:2 * N2:2, e:2 * N2:2] = (np.sin(ang) / np.sqrt(L))[e::2]
    return ec, es


def _attn_bias_const():
    slopes = 2.0 ** -(np.arange(N_HEADS) + 1.0)
    i = np.arange(BQ)[:, None]
    j = np.arange(BAND)[None, :]
    m = np.arange(N_META)[None, :]
    out = np.full((4, N_KV, Q_PER_KV * BQ, N_KEYS), NEG, np.float64)
    for t in range(4):
        if t < 3:
            dist = np.abs(t * WINDOW + i - j)
            band_ok = dist <= WINDOW
            dist_m = np.minimum(t * WINDOW + N_META + i - m, WINDOW) if t == 0 else np.full((BQ, N_META), WINDOW)
        else:
            dist = N_META + j - i
            band_ok = (dist <= WINDOW) & (i < N_META)
            dist_m = np.abs(i - m)
        for h in range(N_HEADS):
            g, hl = divmod(h, Q_PER_KV)
            rows = slice(hl * BQ, (hl + 1) * BQ)
            band = np.where(band_ok, -slopes[h] * dist, NEG)
            if t == 3:
                band = np.where(i < N_META, band, 0.0)
            out[t, g, rows, :BAND] = band
            out[t, g, rows, BAND:BAND + N_META] = -slopes[h] * dist_m
    return np.ascontiguousarray(LOG2E * out.transpose(0, 1, 3, 2)).astype(np.float32)


def _rms_scale(x):
    return lax.rsqrt(jnp.mean(x * x, axis=-1, keepdims=True) + EPS)


def _unit_rms(x):
    return x * _rms_scale(x)


def _rms(x, g):
    return _unit_rms(x) * g


def _swiglu_half(x, wg_s, wu_s, wd_s):
    r = _rms_scale(x)
    xb = x.astype(BF16)
    gate = jnp.dot(xb, wg_s[...], preferred_element_type=F32) * r
    up = jnp.dot(xb, wu_s[...], preferred_element_type=F32) * r
    act = (gate * jax.nn.sigmoid(gate) * up).astype(BF16)
    return x + 0.5 * jnp.dot(act, wd_s[...], preferred_element_type=F32)


def _cast_weights(s, f32_refs, gain_refs, bf16_scratch):
    for w_ref, g_ref, w_s in zip(f32_refs, gain_refs, bf16_scratch):
        rows = w_ref.shape[0]
        w = w_ref[...]
        if g_ref is not None:
            w = w * jnp.tile(g_ref[...], (1, w.shape[1] // g_ref.shape[1]))
        w_s[pl.ds(pl.multiple_of(s * rows, rows), rows), :] = w.astype(BF16)


def _pre_kernel(x_ref, meta_ref, g1_ref, wg_ref, wu_ref, wd_ref, gm_ref, win_ref, gqk_ref,
                ones_ref, h1_ref, q_ref, kv_ref, f_ref, wg_s, wu_s, wd_s, win_s):
    s = pl.program_id(0)

    @pl.when(s < N_CAST)
    def _():
        _cast_weights(s, (wg_ref, wu_ref, wd_ref, win_ref), (g1_ref, g1_ref, None, gm_ref),
                      (wg_s, wu_s, wd_s, win_s))

    @pl.when(s >= N_CAST)
    def _():
        _pre_tile(s - N_CAST, x_ref, meta_ref, gqk_ref, ones_ref,
                  h1_ref, q_ref, kv_ref, f_ref, wg_s, wu_s, wd_s, win_s)


def _pre_tile(i, x_ref, meta_ref, gqk_ref, ones_ref,
              h1_ref, q_ref, kv_ref, f_ref, wg_s, wu_s, wd_s, win_s):
    tail = jnp.where(i == N_TILES - 1, meta_ref[...], x_ref[META_ROW0:, :])
    x = jnp.concatenate([x_ref[:META_ROW0, :], tail], axis=0)

    h1 = _swiglu_half(x, wg_s, wu_s, wd_s)
    h1_ref[...] = h1

    u = jnp.dot(h1.astype(BF16), win_s[...], preferred_element_type=F32) * _rms_scale(h1)

    qk = u[:, :QK_W]
    sq = (qk * qk).astype(BF16)
    ones = ones_ref[...]
    ss = jnp.concatenate([
        jnp.dot(sq[:, 0:256], ones, preferred_element_type=F32),
        jnp.dot(sq[:, 256:512], ones, preferred_element_type=F32),
        jnp.dot(sq[:, 512:640], ones[:KV_W, :KV_W], preferred_element_type=F32)], axis=1)
    qkn = qk * lax.rsqrt(ss * (1.0 / HEAD_DIM) + EPS) * gqk_ref[...]
    q_ref[...] = qkn[:, :ATTN_W].astype(BF16)
    kv_ref[...] = jnp.concatenate([qkn[:, ATTN_W:], u[:, QK_W:QK_W + KV_W]], axis=1).astype(BF16)

    f_ref[...] = u[:, QK_W + KV_W:]


def _attn_kernel(q_ref, kv_ref, bias_ref, sink_ref, o_ref):
    step = pl.program_id(0)
    kv_meta = kv_ref[SEQ:L, :]
    ones_col = (lax.broadcasted_iota(jnp.int32, (N_KEYS, 16), 1) == 0).astype(BF16)

    units = []
    for j in range(QBLK_PER_STEP):
        n = step * QBLK_PER_STEP + j
        btype = jnp.where(n == 0, 0, jnp.where(n == N_QBLK - 1, 2, jnp.where(n == N_QBLK, 3, 1)))
        start = jnp.where(n == N_QBLK, 0, jnp.clip((n - 1) * BQ, 0, SEQ - BAND))
        start = pl.multiple_of(start, BQ)
        kv_cat = jnp.concatenate([kv_ref[pl.ds(start, BAND), :], kv_meta], axis=0)
        q_ok = lax.broadcasted_iota(jnp.int32, (BQ, 1), 0) < L - n * BQ
        units.extend((j, g, btype, kv_cat, q_ok) for g in range(N_KV))

    def scores(unit):
        j, g, _, kv_cat, q_ok = unit
        q = jnp.where(q_ok, q_ref[j * BQ:(j + 1) * BQ, :], 0)
        qg = jnp.concatenate(
            [q[:, (Q_PER_KV * g + hl) * HEAD_DIM:(Q_PER_KV * g + hl + 1) * HEAD_DIM]
             for hl in range(Q_PER_KV)], axis=0)
        k = kv_cat[:, g * HEAD_DIM:(g + 1) * HEAD_DIM]
        return lax.dot_general(k, qg, (((1,), (1,)), ((), ())), preferred_element_type=F32)

    st_next = scores(units[0])
    for u, (j, g, btype, kv_cat, _) in enumerate(units):
        st = st_next + bias_ref[btype, g]
        sink = sink_ref[g]
        m = jnp.maximum(jnp.max(st, axis=0, keepdims=True), sink)
        p = jnp.exp2(st - m)
        if u + 1 < len(units):
            st_next = scores(units[u + 1])
            p = jnp.concatenate([jnp.maximum(p[:16], st_next[:16] * 0.0 - 1.0), p[16:]], axis=0)
        v1 = jnp.concatenate([kv_cat[:, KV_W + g * HEAD_DIM:KV_W + (g + 1) * HEAD_DIM], ones_col], axis=1)
        ot = lax.dot_general(v1, p.astype(BF16), (((0,), (0,)), ((), ())), preferred_element_type=F32)
        denom = ot[HEAD_DIM:HEAD_DIM + 1] + jnp.exp2(sink - m)
        ot = ot[:HEAD_DIM] / denom
        for hp in range(Q_PER_KV // 2):
            h0 = Q_PER_KV * g + 2 * hp
            two = jnp.concatenate([ot[:, 2 * hp * BQ:(2 * hp + 1) * BQ],
                                   ot[:, (2 * hp + 1) * BQ:(2 * hp + 2) * BQ]], axis=0)
            o_ref[j * BQ:(j + 1) * BQ, h0 * HEAD_DIM:(h0 + 2) * HEAD_DIM] = two.T.astype(BF16)


def _dft_a_kernel(f_ref, wa_ref, cdft_ref, g_ref, fs_ref, gs_ref):
    cdft = cdft_ref[...]
    for k in range(B_PER_STEP):
        fs_ref[...] = f_ref[:, k, 0, :]
        h = jnp.dot(wa_ref[...], fs_ref[...].astype(BF16), preferred_element_type=F32).astype(BF16)
        for half in range(2):
            p = jnp.dot(h[:, half * 256:(half + 1) * 256], cdft, preferred_element_type=F32)
            gs_ref[:, half * 256:(half + 1) * 256] = p[:N1, :256] + p[N1:, 256:]
            gs_ref[:, F_W + half * 256:F_W + (half + 1) * 256] = p[N1:, :256] - p[:N1, 256:]
        g_ref[:, k, :, :] = gs_ref[...].astype(BF16).reshape(N1 // 2, 2, 2 * F_W)


def _dft_c_kernel(g_ref, ec_ref, es_ref, y_ref, gs_ref):
    gs_ref[:, 2 * N2:, :] = jnp.zeros((TCP, K2P - 2 * N2, 2 * F_W), BF16)
    for i in range(TCP):
        gs_ref[i, :2 * N2, :] = g_ref[i].reshape(2 * N2, 2 * F_W)
        g = gs_ref[i]
        y = (jnp.dot(ec_ref[i], g[:, :F_W], preferred_element_type=F32)
             + jnp.dot(es_ref[i], g[:, F_W:], preferred_element_type=F32))
        y_ref[:, i, :, :] = _unit_rms(y[:2 * N2]).astype(BF16).reshape(N2, 2, F_W)


def _post_kernel(h1_ref, a_ref, y_ref, gmix_ref, wout_ref, g2_ref, wg_ref, wu_ref, wd_ref,
                 gfin_ref, o_ref, wg_s, wu_s, wd_s, wout_s):
    s = pl.program_id(0)

    @pl.when(s < N_CAST)
    def _():
        _cast_weights(s, (wg_ref, wu_ref, wd_ref, wout_ref), (g2_ref, g2_ref, None, gmix_ref),
                      (wg_s, wu_s, wd_s, wout_s))

    @pl.when(s >= N_CAST)
    def _():
        y = y_ref[...].reshape(TM, F_W)
        a = a_ref[...]
        h2 = (h1_ref[...]
              + jnp.dot(a, wout_s[:ATTN_W, :], preferred_element_type=F32) * _rms_scale(a.astype(F32))
              + jnp.dot(y, wout_s[ATTN_W:, :], preferred_element_type=F32))
        h3 = _swiglu_half(h2, wg_s, wu_s, wd_s)
        o_ref[...] = _rms(h3, gfin_ref[...])


def _resident(shape):
    zeros = (0,) * len(shape)
    return pl.BlockSpec(shape, lambda *_: zeros, pipeline_mode=pl.Buffered(1))


def _params():
    return pltpu.CompilerParams(dimension_semantics=("arbitrary",), vmem_limit_bytes=VMEM_LIMIT)


def _row(g):
    return g.astype(F32).reshape(1, -1)


def kernel(x, meta_tokens, ffn1_norm, ffn1_w_gate, ffn1_w_up, ffn1_w_down, mix_norm, w_in, q_norm, k_norm, sink, attn_out_norm, fourier_out_norm, w_out, ffn2_norm, ffn2_w_gate, ffn2_w_up, ffn2_w_down, final_norm):
    assert x.shape == (1, SEQ, D_MODEL) and x.dtype == F32
    x2 = x.reshape(SEQ, D_MODEL)
    gqk = jnp.concatenate([jnp.tile(q_norm.astype(F32), N_HEADS) * (HEAD_DIM ** -0.5 * LOG2E),
                           jnp.tile(k_norm.astype(F32), N_KV)]).reshape(1, QK_W)
    bf = lambda w: w.astype(BF16)

    tile_idx = lambda s: jnp.maximum(s - N_CAST, 0)
    row_tile = lambda w: pl.BlockSpec((TM, w), lambda s: (tile_idx(s), 0))
    flat_rows = lambda w: pl.BlockSpec((TM, None, w), lambda s: (tile_idx(s), 0, 0))
    cast_rows = lambda r, c: pl.BlockSpec((r // N_CAST, c), lambda s: (jnp.minimum(s, N_CAST - 1), 0))
    gains = jnp.broadcast_to(
        jnp.stack([ffn1_norm, mix_norm, jnp.concatenate([attn_out_norm, fourier_out_norm]), ffn2_norm])
        .astype(F32)[:, :, None], (4, D_MODEL, 128))
    gain_rows = lambda k: pl.BlockSpec((None, D_MODEL // N_CAST, 128), lambda s: (k, jnp.minimum(s, N_CAST - 1), 0))
    bf16_copy = lambda r, c: pltpu.VMEM((r, c), BF16)
    ffn_in = [cast_rows(D_MODEL, D_FF), cast_rows(D_MODEL, D_FF), cast_rows(D_FF, D_MODEL)]
    ffn_scratch = [bf16_copy(D_MODEL, D_FF), bf16_copy(D_MODEL, D_FF), bf16_copy(D_FF, D_MODEL)]
    f32 = lambda w: w.astype(F32)

    h1, q, kv, f = pl.pallas_call(
        _pre_kernel,
        grid=(N_CAST + N_TILES,),
        in_specs=[row_tile(D_MODEL), _resident((N_META, D_MODEL)), gain_rows(0), *ffn_in,
                  gain_rows(1), cast_rows(D_MODEL, IN_W), _resident((1, QK_W)),
                  _resident((256, 256))],
        out_specs=[row_tile(D_MODEL), row_tile(ATTN_W), row_tile(2 * KV_W), flat_rows(F_W)],
        out_shape=[jax.ShapeDtypeStruct((L, D_MODEL), F32), jax.ShapeDtypeStruct((L, ATTN_W), BF16),
                   jax.ShapeDtypeStruct((L, 2 * KV_W), BF16), jax.ShapeDtypeStruct((L, 1, F_W), F32)],
        scratch_shapes=ffn_scratch + [bf16_copy(D_MODEL, IN_W)],
        compiler_params=_params(),
        name="pre",
    )(x2, meta_tokens.astype(F32), gains, f32(ffn1_w_gate), f32(ffn1_w_up), f32(ffn1_w_down), gains,
      f32(w_in), gqk, _ones_blockdiag())

    sink_rows = (jnp.repeat(sink.astype(F32).reshape(N_KV, Q_PER_KV), BQ, axis=1) * LOG2E)[:, None, :]

    q_rows = pl.BlockSpec((QBLK_PER_STEP * BQ, ATTN_W), lambda s: (s, 0))
    a_out = pl.pallas_call(
        _attn_kernel,
        grid=(pl.cdiv(N_QBLK + 1, QBLK_PER_STEP),),
        in_specs=[q_rows, _resident((L, 2 * KV_W)), _resident((4, N_KV, N_KEYS, Q_PER_KV * BQ)),
                  _resident((N_KV, 1, Q_PER_KV * BQ))],
        out_specs=q_rows,
        out_shape=jax.ShapeDtypeStruct((L, ATTN_W), BF16),
        compiler_params=_params(),
        name="attn",
    )(q, kv, jnp.asarray(_attn_bias_const()), sink_rows)

    row_set = lambda w: pl.BlockSpec((N1, B_PER_STEP, 1, w), lambda s: (0, s, 0, 0))
    g = pl.pallas_call(
        _dft_a_kernel,
        grid=(pl.cdiv(N2, B_PER_STEP),),
        in_specs=[row_set(F_W), _resident((2 * N1, N1)), _resident((256, 512))],
        out_specs=pl.BlockSpec((N1 // 2, B_PER_STEP, 2, 2 * F_W), lambda s: (0, s, 0, 0)),
        out_shape=jax.ShapeDtypeStruct((N1 // 2, N2, 2, 2 * F_W), BF16),
        scratch_shapes=[pltpu.VMEM((N1, F_W), F32), pltpu.VMEM((N1, 2 * F_W), F32)],
        compiler_params=_params(),
        name="dft_a",
    )(f.reshape(N1, N2, 1, F_W), bf(jnp.asarray(_stage_a_dft())), bf(jnp.asarray(_channel_dft())))

    ec, es = _stage_c_dft()
    y = pl.pallas_call(
        _dft_c_kernel,
        grid=(N1 // (2 * TCP),),
        in_specs=[pl.BlockSpec((TCP, N2, 2, 2 * F_W), lambda i: (i, 0, 0, 0)),
                  pl.BlockSpec((TCP, K2P, K2P), lambda i: (i, 0, 0)),
                  pl.BlockSpec((TCP, K2P, K2P), lambda i: (i, 0, 0))],
        out_specs=pl.BlockSpec((N2, TCP, 2, F_W), lambda i: (0, i, 0, 0)),
        out_shape=jax.ShapeDtypeStruct((N2, N1 // 2, 2, F_W), BF16),
        scratch_shapes=[pltpu.VMEM((TCP, K2P, 2 * F_W), BF16)],
        compiler_params=_params(),
        name="dft_c",
    )(g, bf(jnp.asarray(ec)), bf(jnp.asarray(es)))

    out = pl.pallas_call(
        _post_kernel,
        grid=(N_CAST + N_TILES,),
        in_specs=[row_tile(D_MODEL), row_tile(ATTN_W),
                  pl.BlockSpec((TM // 2, 2, F_W), lambda s: (tile_idx(s), 0, 0)), gain_rows(2),
                  cast_rows(D_MODEL, D_MODEL), gain_rows(3), *ffn_in, _resident((1, D_MODEL))],
        out_specs=row_tile(D_MODEL),
        out_shape=jax.ShapeDtypeStruct((SEQ, D_MODEL), F32),
        scratch_shapes=ffn_scratch + [bf16_copy(D_MODEL, D_MODEL)],
        compiler_params=_params(),
        name="post",
    )(h1, a_out, y.reshape(L // 2, 2, F_W), gains, f32(w_out), gains,
      f32(ffn2_w_gate), f32(ffn2_w_up), f32(ffn2_w_down), _row(final_norm))
    return out.reshape(1, SEQ, D_MODEL)
```

```python
import numpy as np
import jax
import jax.numpy as jnp
from jax import lax
from jax.experimental import pallas as pl
from jax.experimental.pallas import tpu as pltpu

F32 = jnp.float32
BF16 = jnp.bfloat16

D_MODEL = 1024
SEQ = 16384
N_META = 16
L = SEQ + N_META
HEAD_DIM = 64
N_HEADS = 8
N_KV = 2
Q_PER_KV = N_HEADS // N_KV
ATTN_W = N_HEADS * HEAD_DIM
KV_W = N_KV * HEAD_DIM
F_W = 512
F_GROUP = 64
IN_W = ATTN_W + 2 * KV_W + F_W
QK_W = ATTN_W + KV_W
WINDOW = 128
D_FF = 2816
N_CAST = 8
EPS = 1e-6
NEG = -1e30
LOG2E = 1.4426950408889634

TM = 656
N_TILES = L // TM
META_ROW0 = SEQ - (N_TILES - 1) * TM

BQ = 128
N_QBLK = SEQ // BQ
QBLK_PER_STEP = 12
BAND = BQ + 2 * WINDOW
N_KEYS = BAND + N_META

N1 = 200
N2 = 82
K2P = 176
B_PER_STEP = 14
TCP = 25

VMEM_LIMIT = 56 * 1024 * 1024


def _ones_blockdiag():
    m = np.kron(np.eye(4), np.ones((HEAD_DIM, HEAD_DIM)))
    return jnp.asarray(m, BF16)


def _channel_dft():
    c = np.arange(F_GROUP)
    ang = 2.0 * np.pi * np.outer(c, c) / F_GROUP
    cos = np.kron(np.eye(4), np.cos(ang) / 8.0)
    sin = np.kron(np.eye(4), np.sin(ang) / 8.0)
    return np.concatenate([cos, sin], axis=1).astype(np.float32)


def _stage_a_dft():
    a = np.arange(N1, dtype=np.int64)
    m = (N2 * np.outer(a + N_META, a)) % L
    ang = 2.0 * np.pi * m / L
    return np.concatenate([np.cos(ang), -np.sin(ang)], axis=0).astype(np.float32)


def _stage_c_dft():
    c = np.arange(N1, dtype=np.int64)[:, None, None]
    d = np.arange(N2, dtype=np.int64)[None, :, None]
    b = np.arange(N2, dtype=np.int64)[None, None, :]
    m = ((c + N1 * d + N_META) * (b + N_META)) % L
    ang = 2.0 * np.pi * m / L
    ec = np.zeros((N1 // 2, K2P, K2P), np.float32)
    es = np.zeros((N1 // 2, K2P, K2P), np.float32)
    for e in range(2):
        ec[:, e:2 * N2:2, e:2 * N2:2] = (np.cos(ang) / np.sqrt(L))[e::2]
        es[:, e:2 * N2:2, e:2 * N2:2] = (np.sin(ang) / np.sqrt(L))[e::2]
    return ec, es


def _attn_bias_const():
    slopes = 2.0 ** -(np.arange(N_HEADS) + 1.0)
    i = np.arange(BQ)[:, None]
    j = np.arange(BAND)[None, :]
    m = np.arange(N_META)[None, :]
    out = np.full((4, N_KV, Q_PER_KV * BQ, N_KEYS), NEG, np.float64)
    for t in range(4):
        if t < 3:
            dist = np.abs(t * WINDOW + i - j)
            band_ok = dist <= WINDOW
            dist_m = np.minimum(t * WINDOW + N_META + i - m, WINDOW) if t == 0 else np.full((BQ, N_META), WINDOW)
        else:
            dist = N_META + j - i
            band_ok = (dist <= WINDOW) & (i < N_META)
            dist_m = np.abs(i - m)
        for h in range(N_HEADS):
            g, hl = divmod(h, Q_PER_KV)
            rows = slice(hl * BQ, (hl + 1) * BQ)
            band = np.where(band_ok, -slopes[h] * dist, NEG)
            if t == 3:
                band = np.where(i < N_META, band, 0.0)
            out[t, g, rows, :BAND] = band
            out[t, g, rows, BAND:BAND + N_META] = -slopes[h] * dist_m
    return np.ascontiguousarray(LOG2E * out.transpose(0, 1, 3, 2)).astype(np.float32)


def _rms_scale(x):
    return lax.rsqrt(jnp.mean(x * x, axis=-1, keepdims=True) + EPS)


def _unit_rms(x):
    return x * _rms_scale(x)


def _rms(x, g):
    return _unit_rms(x) * g


def _swiglu_half(x, wg_s, wu_s, wd_s):
    r = _rms_scale(x)
    xb = x.astype(BF16)
    gate = jnp.dot(xb, wg_s[...], preferred_element_type=F32) * r
    up = jnp.dot(xb, wu_s[...], preferred_element_type=F32) * r
    act = (gate * jax.nn.sigmoid(gate) * up).astype(BF16)
    return x + 0.5 * jnp.dot(act, wd_s[...], preferred_element_type=F32)


def _cast_weights(s, f32_refs, gain_refs, bf16_scratch):
    for w_ref, g_ref, w_s in zip(f32_refs, gain_refs, bf16_scratch):
        rows = w_ref.shape[0]
        w = w_ref[...]
        if g_ref is not None:
            w = w * jnp.tile(g_ref[...], (1, w.shape[1] // g_ref.shape[1]))
        w_s[pl.ds(pl.multiple_of(s * rows, rows), rows), :] = w.astype(BF16)


def _pre_kernel(x_ref, meta_ref, g1_ref, wg_ref, wu_ref, wd_ref, gm_ref, win_ref, gqk_ref,
                ones_ref, h1_ref, q_ref, kv_ref, f_ref, wg_s, wu_s, wd_s, win_s):
    s = pl.program_id(0)

    @pl.when(s < N_CAST)
    def _():
        _cast_weights(s, (wg_ref, wu_ref, wd_ref, win_ref), (g1_ref, g1_ref, None, gm_ref),
                      (wg_s, wu_s, wd_s, win_s))

    @pl.when(s >= N_CAST)
    def _():
        _pre_tile(s - N_CAST, x_ref, meta_ref, gqk_ref, ones_ref,
                  h1_ref, q_ref, kv_ref, f_ref, wg_s, wu_s, wd_s, win_s)


def _pre_tile(i, x_ref, meta_ref, gqk_ref, ones_ref,
              h1_ref, q_ref, kv_ref, f_ref, wg_s, wu_s, wd_s, win_s):
    tail = jnp.where(i == N_TILES - 1, meta_ref[...], x_ref[META_ROW0:, :])
    x = jnp.concatenate([x_ref[:META_ROW0, :], tail], axis=0)

    h1 = _swiglu_half(x, wg_s, wu_s, wd_s)
    h1_ref[...] = h1

    u = jnp.dot(h1.astype(BF16), win_s[...], preferred_element_type=F32) * _rms_scale(h1)

    qk = u[:, :QK_W]
    sq = (qk * qk).astype(BF16)
    ones = ones_ref[...]
    ss = jnp.concatenate([
        jnp.dot(sq[:, 0:256], ones, preferred_element_type=F32),
        jnp.dot(sq[:, 256:512], ones, preferred_element_type=F32),
        jnp.dot(sq[:, 512:640], ones[:KV_W, :KV_W], preferred_element_type=F32)], axis=1)
    qkn = qk * lax.rsqrt(ss * (1.0 / HEAD_DIM) + EPS) * gqk_ref[...]
    q_ref[...] = qkn[:, :ATTN_W].astype(BF16)
    kv_ref[...] = jnp.concatenate([qkn[:, ATTN_W:], u[:, QK_W:QK_W + KV_W]], axis=1).astype(BF16)

    f_ref[...] = u[:, QK_W + KV_W:]


def _attn_kernel(q_ref, kv_ref, bias_ref, sink_ref, o_ref):
    step = pl.program_id(0)
    kv_meta = kv_ref[SEQ:L, :]
    ones_col = (lax.broadcasted_iota(jnp.int32, (N_KEYS, 16), 1) == 0).astype(BF16)

    units = []
    for j in range(QBLK_PER_STEP):
        n = step * QBLK_PER_STEP + j
        btype = jnp.where(n == 0, 0, jnp.where(n == N_QBLK - 1, 2, jnp.where(n == N_QBLK, 3, 1)))
        start = jnp.where(n == N_QBLK, 0, jnp.clip((n - 1) * BQ, 0, SEQ - BAND))
        start = pl.multiple_of(start, BQ)
        kv_cat = jnp.concatenate([kv_ref[pl.ds(start, BAND), :], kv_meta], axis=0)
        q_ok = lax.broadcasted_iota(jnp.int32, (BQ, 1), 0) < L - n * BQ
        units.extend((j, g, btype, kv_cat, q_ok) for g in range(N_KV))

    def scores(unit):
        j, g, _, kv_cat, q_ok = unit
        q = jnp.where(q_ok, q_ref[j * BQ:(j + 1) * BQ, :], 0)
        qg = jnp.concatenate(
            [q[:, (Q_PER_KV * g + hl) * HEAD_DIM:(Q_PER_KV * g + hl + 1) * HEAD_DIM]
             for hl in range(Q_PER_KV)], axis=0)
        k = kv_cat[:, g * HEAD_DIM:(g + 1) * HEAD_DIM]
        return lax.dot_general(k, qg, (((1,), (1,)), ((), ())), preferred_element_type=F32)

    st_next = scores(units[0])
    for u, (j, g, btype, kv_cat, _) in enumerate(units):
        st = st_next + bias_ref[btype, g]
        sink = sink_ref[g]
        m = jnp.maximum(jnp.max(st, axis=0, keepdims=True), sink)
        p = jnp.exp2(st - m)
        if u + 1 < len(units):
            st_next = scores(units[u + 1])
            p = jnp.concatenate([jnp.maximum(p[:16], st_next[:16] * 0.0 - 1.0), p[16:]], axis=0)
        v1 = jnp.concatenate([kv_cat[:, KV_W + g * HEAD_DIM:KV_W + (g + 1) * HEAD_DIM], ones_col], axis=1)
        ot = lax.dot_general(v1, p.astype(BF16), (((0,), (0,)), ((), ())), preferred_element_type=F32)
        denom = ot[HEAD_DIM:HEAD_DIM + 1] + jnp.exp2(sink - m)
        ot = ot[:HEAD_DIM] / denom
        for hp in range(Q_PER_KV // 2):
            h0 = Q_PER_KV * g + 2 * hp
            two = jnp.concatenate([ot[:, 2 * hp * BQ:(2 * hp + 1) * BQ],
                                   ot[:, (2 * hp + 1) * BQ:(2 * hp + 2) * BQ]], axis=0)
            o_ref[j * BQ:(j + 1) * BQ, h0 * HEAD_DIM:(h0 + 2) * HEAD_DIM] = two.T.astype(BF16)


def _dft_a_kernel(f_ref, wa_ref, cdft_ref, g_ref, fs_ref, gs_ref):
    cdft = cdft_ref[...]
    for k in range(B_PER_STEP):
        fs_ref[...] = f_ref[:, k, 0, :]
        h = jnp.dot(wa_ref[...], fs_ref[...].astype(BF16), preferred_element_type=F32).astype(BF16)
        for half in range(2):
            p = jnp.dot(h[:, half * 256:(half + 1) * 256], cdft, preferred_element_type=F32)
            gs_ref[:, half * 256:(half + 1) * 256] = p[:N1, :256] + p[N1:, 256:]
            gs_ref[:, F_W + half * 256:F_W + (half + 1) * 256] = p[N1:, :256] - p[:N1, 256:]
        g_ref[:, k, :, :] = gs_ref[...].astype(BF16).reshape(N1 // 2, 2, 2 * F_W)


def _dft_c_kernel(g_ref, ec_ref, es_ref, y_ref, gs_ref):
    gs_ref[:, 2 * N2:, :] = jnp.zeros((TCP, K2P - 2 * N2, 2 * F_W), BF16)
    for i in range(TCP):
        gs_ref[i, :2 * N2, :] = g_ref[i].reshape(2 * N2, 2 * F_W)
        g = gs_ref[i]
        y = (jnp.dot(ec_ref[i], g[:, :F_W], preferred_element_type=F32)
             + jnp.dot(es_ref[i], g[:, F_W:], preferred_element_type=F32))
        y_ref[:, i, :, :] = _unit_rms(y[:2 * N2]).astype(BF16).reshape(N2, 2, F_W)


def _post_kernel(h1_ref, a_ref, y_ref, gmix_ref, wout_ref, g2_ref, wg_ref, wu_ref, wd_ref,
                 gfin_ref, o_ref, wg_s, wu_s, wd_s, wout_s):
    s = pl.program_id(0)

    @pl.when(s < N_CAST)
    def _():
        _cast_weights(s, (wg_ref, wu_ref, wd_ref, wout_ref), (g2_ref, g2_ref, None, gmix_ref),
                      (wg_s, wu_s, wd_s, wout_s))

    @pl.when(s >= N_CAST)
    def _():
        y = y_ref[...].reshape(TM, F_W)
        a = a_ref[...]
        h2 = (h1_ref[...]
              + jnp.dot(a, wout_s[:ATTN_W, :], preferred_element_type=F32) * _rms_scale(a.astype(F32))
              + jnp.dot(y, wout_s[ATTN_W:, :], preferred_element_type=F32))
        h3 = _swiglu_half(h2, wg_s, wu_s, wd_s)
        o_ref[...] = _rms(h3, gfin_ref[...])


def _resident(shape):
    zeros = (0,) * len(shape)
    return pl.BlockSpec(shape, lambda *_: zeros, pipeline_mode=pl.Buffered(1))


def _params():
    return pltpu.CompilerParams(dimension_semantics=("arbitrary",), vmem_limit_bytes=VMEM_LIMIT)


def _row(g):
    return g.astype(F32).reshape(1, -1)


def kernel(x, meta_tokens, ffn1_norm, ffn1_w_gate, ffn1_w_up, ffn1_w_down, mix_norm, w_in, q_norm, k_norm, sink, attn_out_norm, fourier_out_norm, w_out, ffn2_norm, ffn2_w_gate, ffn2_w_up, ffn2_w_down, final_norm):
    assert x.shape == (1, SEQ, D_MODEL) and x.dtype == F32
    x2 = x.reshape(SEQ, D_MODEL)
    gqk = jnp.concatenate([jnp.tile(q_norm.astype(F32), N_HEADS) * (HEAD_DIM ** -0.5 * LOG2E),
                           jnp.tile(k_norm.astype(F32), N_KV)]).reshape(1, QK_W)
    bf = lambda w: w.astype(BF16)

    tile_idx = lambda s: jnp.maximum(s - N_CAST, 0)
    row_tile = lambda w: pl.BlockSpec((TM, w), lambda s: (tile_idx(s), 0))
    flat_rows = lambda w: pl.BlockSpec((TM, None, w), lambda s: (tile_idx(s), 0, 0))
    cast_rows = lambda r, c: pl.BlockSpec((r // N_CAST, c), lambda s: (jnp.minimum(s, N_CAST - 1), 0))
    gains = jnp.broadcast_to(
        jnp.stack([ffn1_norm, mix_norm, jnp.concatenate([attn_out_norm, fourier_out_norm]), ffn2_norm])
        .astype(F32)[:, :, None], (4, D_MODEL, 128))
    gain_rows = lambda k: pl.BlockSpec((None, D_MODEL // N_CAST, 128), lambda s: (k, jnp.minimum(s, N_CAST - 1), 0))
    bf16_copy = lambda r, c: pltpu.VMEM((r, c), BF16)
    ffn_in = [cast_rows(D_MODEL, D_FF), cast_rows(D_MODEL, D_FF), cast_rows(D_FF, D_MODEL)]
    ffn_scratch = [bf16_copy(D_MODEL, D_FF), bf16_copy(D_MODEL, D_FF), bf16_copy(D_FF, D_MODEL)]
    f32 = lambda w: w.astype(F32)

    h1, q, kv, f = pl.pallas_call(
        _pre_kernel,
        grid=(N_CAST + N_TILES,),
        in_specs=[row_tile(D_MODEL), _resident((N_META, D_MODEL)), gain_rows(0), *ffn_in,
                  gain_rows(1), cast_rows(D_MODEL, IN_W), _resident((1, QK_W)),
                  _resident((256, 256))],
        out_specs=[row_tile(D_MODEL), row_tile(ATTN_W), row_tile(2 * KV_W), flat_rows(F_W)],
        out_shape=[jax.ShapeDtypeStruct((L, D_MODEL), F32), jax.ShapeDtypeStruct((L, ATTN_W), BF16),
                   jax.ShapeDtypeStruct((L, 2 * KV_W), BF16), jax.ShapeDtypeStruct((L, 1, F_W), F32)],
        scratch_shapes=ffn_scratch + [bf16_copy(D_MODEL, IN_W)],
        compiler_params=_params(),
        name="pre",
    )(x2, meta_tokens.astype(F32), gains, f32(ffn1_w_gate), f32(ffn1_w_up), f32(ffn1_w_down), gains,
      f32(w_in), gqk, _ones_blockdiag())

    sink_rows = (jnp.repeat(sink.astype(F32).reshape(N_KV, Q_PER_KV), BQ, axis=1) * LOG2E)[:, None, :]

    q_rows = pl.BlockSpec((QBLK_PER_STEP * BQ, ATTN_W), lambda s: (s, 0))
    a_out = pl.pallas_call(
        _attn_kernel,
        grid=(pl.cdiv(N_QBLK + 1, QBLK_PER_STEP),),
        in_specs=[q_rows, _resident((L, 2 * KV_W)), _resident((4, N_KV, N_KEYS, Q_PER_KV * BQ)),
                  _resident((N_KV, 1, Q_PER_KV * BQ))],
        out_specs=q_rows,
        out_shape=jax.ShapeDtypeStruct((L, ATTN_W), BF16),
        compiler_params=_params(),
        name="attn",
    )(q, kv, jnp.asarray(_attn_bias_const()), sink_rows)

    row_set = lambda w: pl.BlockSpec((N1, B_PER_STEP, 1, w), lambda s: (0, s, 0, 0))
    g = pl.pallas_call(
        _dft_a_kernel,
        grid=(pl.cdiv(N2, B_PER_STEP),),
        in_specs=[row_set(F_W), _resident((2 * N1, N1)), _resident((256, 512))],
        out_specs=pl.BlockSpec((N1 // 2, B_PER_STEP, 2, 2 * F_W), lambda s: (0, s, 0, 0)),
        out_shape=jax.ShapeDtypeStruct((N1 // 2, N2, 2, 2 * F_W), BF16),
        scratch_shapes=[pltpu.VMEM((N1, F_W), F32), pltpu.VMEM((N1, 2 * F_W), F32)],
        compiler_params=_params(),
        name="dft_a",
    )(f.reshape(N1, N2, 1, F_W), bf(jnp.asarray(_stage_a_dft())), bf(jnp.asarray(_channel_dft())))

    ec, es = _stage_c_dft()
    y = pl.pallas_call(
        _dft_c_kernel,
        grid=(N1 // (2 * TCP),),
        in_specs=[pl.BlockSpec((TCP, N2, 2, 2 * F_W), lambda i: (i, 0, 0, 0)),
                  pl.BlockSpec((TCP, K2P, K2P), lambda i: (i, 0, 0)),
                  pl.BlockSpec((TCP, K2P, K2P), lambda i: (i, 0, 0))],
        out_specs=pl.BlockSpec((N2, TCP, 2, F_W), lambda i: (0, i, 0, 0)),
        out_shape=jax.ShapeDtypeStruct((N2, N1 // 2, 2, F_W), BF16),
        scratch_shapes=[pltpu.VMEM((TCP, K2P, 2 * F_W), BF16)],
        compiler_params=_params(),
        name="dft_c",
    )(g, bf(jnp.asarray(ec)), bf(jnp.asarray(es)))

    out = pl.pallas_call(
        _post_kernel,
        grid=(N_CAST + N_TILES,),
        in_specs=[row_tile(D_MODEL), row_tile(ATTN_W),
                  pl.BlockSpec((TM // 2, 2, F_W), lambda s: (tile_idx(s), 0, 0)), gain_rows(2),
                  cast_rows(D_MODEL, D_MODEL), gain_rows(3), *ffn_in, _resident((1, D_MODEL))],
        out_specs=row_tile(D_MODEL),
        out_shape=jax.ShapeDtypeStruct((SEQ, D_MODEL), F32),
        scratch_shapes=ffn_scratch + [bf16_copy(D_MODEL, D_MODEL)],
        compiler_params=_params(),
        name="post",
    )(h1, a_out, y.reshape(L // 2, 2, F_W), gains, f32(w_out), gains,
      f32(ffn2_w_gate), f32(ffn2_w_up), f32(ffn2_w_down), _row(final_norm))
    return out.reshape(1, SEQ, D_MODEL)
```

```python
import numpy as np
import jax
import jax.numpy as jnp
from jax import lax
from jax.experimental import pallas as pl
from jax.experimental.pallas import tpu as pltpu

F32 = jnp.float32
BF16 = jnp.bfloat16

D_MODEL = 1024
SEQ = 16384
N_META = 16
L = SEQ + N_META
HEAD_DIM = 64
N_HEADS = 8
N_KV = 2
Q_PER_KV = N_HEADS // N_KV
ATTN_W = N_HEADS * HEAD_DIM
KV_W = N_KV * HEAD_DIM
F_W = 512
F_GROUP = 64
IN_W = ATTN_W + 2 * KV_W + F_W
QK_W = ATTN_W + KV_W
WINDOW = 128
D_FF = 2816
N_CAST = 8
EPS = 1e-6
NEG = -1e30
LOG2E = 1.4426950408889634

TM = 656
N_TILES = L // TM
META_ROW0 = SEQ - (N_TILES - 1) * TM

BQ = 128
N_QBLK = SEQ // BQ
QBLK_PER_STEP = 12
BAND = BQ + 2 * WINDOW
N_KEYS = BAND + N_META

N1 = 200
N2 = 82
K2P = 176
B_PER_STEP = 7
TCP = 25

VMEM_LIMIT = 56 * 1024 * 1024


def _ones_blockdiag():
    m = np.kron(np.eye(4), np.ones((HEAD_DIM, HEAD_DIM)))
    return jnp.asarray(m, BF16)


def _channel_dft():
    c = np.arange(F_GROUP)
    ang = 2.0 * np.pi * np.outer(c, c) / F_GROUP
    cos = np.kron(np.eye(4), np.cos(ang) / 8.0)
    sin = np.kron(np.eye(4), np.sin(ang) / 8.0)
    return np.concatenate([cos, sin], axis=1).astype(np.float32)


def _stage_a_dft():
    a = np.arange(N1, dtype=np.int64)
    m = (N2 * np.outer(a + N_META, a)) % L
    ang = 2.0 * np.pi * m / L
    return np.concatenate([np.cos(ang), -np.sin(ang)], axis=0).astype(np.float32)


def _stage_c_dft():
    c = np.arange(N1, dtype=np.int64)[:, None, None]
    d = np.arange(N2, dtype=np.int64)[None, :, None]
    b = np.arange(N2, dtype=np.int64)[None, None, :]
    m = ((c + N1 * d + N_META) * (b + N_META)) % L
    ang = 2.0 * np.pi * m / L
    ec = np.zeros((N1 // 2, K2P, K2P), np.float32)
    es = np.zeros((N1 // 2, K2P, K2P), np.float32)
    for e in range(2):
        ec[:, e:2 * N2:2, e:2 * N2:2] = (np.cos(ang) / np.sqrt(L))[e::2]
        es[:, e:2 * N2:2, e:2 * N2:2] = (np.sin(ang) / np.sqrt(L))[e::2]
    return ec, es


def _attn_bias_const():
    slopes = 2.0 ** -(np.arange(N_HEADS) + 1.0)
    i = np.arange(BQ)[:, None]
    j = np.arange(BAND)[None, :]
    m = np.arange(N_META)[None, :]
    out = np.full((4, N_KV, Q_PER_KV * BQ, N_KEYS), NEG, np.float64)
    for t in range(4):
        if t < 3:
            dist = np.abs(t * WINDOW + i - j)
            band_ok = dist <= WINDOW
            dist_m = np.minimum(t * WINDOW + N_META + i - m, WINDOW) if t == 0 else np.full((BQ, N_META), WINDOW)
        else:
            dist = N_META + j - i
            band_ok = (dist <= WINDOW) & (i < N_META)
            dist_m = np.abs(i - m)
        for h in range(N_HEADS):
            g, hl = divmod(h, Q_PER_KV)
            rows = slice(hl * BQ, (hl + 1) * BQ)
            band = np.where(band_ok, -slopes[h] * dist, NEG)
            if t == 3:
                band = np.where(i < N_META, band, 0.0)
            out[t, g, rows, :BAND] = band
            out[t, g, rows, BAND:BAND + N_META] = -slopes[h] * dist_m
    return np.ascontiguousarray(LOG2E * out.transpose(0, 1, 3, 2)).astype(np.float32)


def _rms_scale(x):
    return lax.rsqrt(jnp.mean(x * x, axis=-1, keepdims=True) + EPS)


def _unit_rms(x):
    return x * _rms_scale(x)


def _rms(x, g):
    return _unit_rms(x) * g


def _swiglu_half(x, wg_s, wu_s, wd_s):
    r = _rms_scale(x)
    xb = x.astype(BF16)
    gate = jnp.dot(xb, wg_s[...], preferred_element_type=F32) * r
    up = jnp.dot(xb, wu_s[...], preferred_element_type=F32) * r
    act = (gate * jax.nn.sigmoid(gate) * up).astype(BF16)
    return x + 0.5 * jnp.dot(act, wd_s[...], preferred_element_type=F32)


def _cast_weights(s, f32_refs, gain_refs, bf16_scratch):
    for w_ref, g_ref, w_s in zip(f32_refs, gain_refs, bf16_scratch):
        rows = w_ref.shape[0]
        w = w_ref[...]
        if g_ref is not None:
            w = w * jnp.tile(g_ref[...], (1, w.shape[1] // g_ref.shape[1]))
        w_s[pl.ds(pl.multiple_of(s * rows, rows), rows), :] = w.astype(BF16)


def _pre_kernel(x_ref, meta_ref, g1_ref, wg_ref, wu_ref, wd_ref, gm_ref, win_ref, gqk_ref,
                ones_ref, h1_ref, q_ref, kv_ref, f_ref, wg_s, wu_s, wd_s, win_s):
    s = pl.program_id(0)

    @pl.when(s < N_CAST)
    def _():
        _cast_weights(s, (wg_ref, wu_ref, wd_ref, win_ref), (g1_ref, g1_ref, None, gm_ref),
                      (wg_s, wu_s, wd_s, win_s))

    @pl.when(s >= N_CAST)
    def _():
        _pre_tile(s - N_CAST, x_ref, meta_ref, gqk_ref, ones_ref,
                  h1_ref, q_ref, kv_ref, f_ref, wg_s, wu_s, wd_s, win_s)


def _pre_tile(i, x_ref, meta_ref, gqk_ref, ones_ref,
              h1_ref, q_ref, kv_ref, f_ref, wg_s, wu_s, wd_s, win_s):
    tail = jnp.where(i == N_TILES - 1, meta_ref[...], x_ref[META_ROW0:, :])
    x = jnp.concatenate([x_ref[:META_ROW0, :], tail], axis=0)

    h1 = _swiglu_half(x, wg_s, wu_s, wd_s)
    h1_ref[...] = h1

    u = jnp.dot(h1.astype(BF16), win_s[...], preferred_element_type=F32) * _rms_scale(h1)

    qk = u[:, :QK_W]
    sq = (qk * qk).astype(BF16)
    ones = ones_ref[...]
    ss = jnp.concatenate([
        jnp.dot(sq[:, 0:256], ones, preferred_element_type=F32),
        jnp.dot(sq[:, 256:512], ones, preferred_element_type=F32),
        jnp.dot(sq[:, 512:640], ones[:KV_W, :KV_W], preferred_element_type=F32)], axis=1)
    qkn = qk * lax.rsqrt(ss * (1.0 / HEAD_DIM) + EPS) * gqk_ref[...]
    q_ref[...] = qkn[:, :ATTN_W].astype(BF16)
    kv_ref[...] = jnp.concatenate([qkn[:, ATTN_W:], u[:, QK_W:QK_W + KV_W]], axis=1).astype(BF16)

    f_ref[...] = u[:, QK_W + KV_W:]


def _attn_kernel(q_ref, kv_ref, bias_ref, sink_ref, o_ref):
    step = pl.program_id(0)
    kv_meta = kv_ref[SEQ:L, :]
    ones_col = (lax.broadcasted_iota(jnp.int32, (N_KEYS, 16), 1) == 0).astype(BF16)

    units = []
    for j in range(QBLK_PER_STEP):
        n = step * QBLK_PER_STEP + j
        btype = jnp.where(n == 0, 0, jnp.where(n == N_QBLK - 1, 2, jnp.where(n == N_QBLK, 3, 1)))
        start = jnp.where(n == N_QBLK, 0, jnp.clip((n - 1) * BQ, 0, SEQ - BAND))
        start = pl.multiple_of(start, BQ)
        kv_cat = jnp.concatenate([kv_ref[pl.ds(start, BAND), :], kv_meta], axis=0)
        q_ok = lax.broadcasted_iota(jnp.int32, (BQ, 1), 0) < L - n * BQ
        units.extend((j, g, btype, kv_cat, q_ok) for g in range(N_KV))

    def scores(unit):
        j, g, _, kv_cat, q_ok = unit
        q = jnp.where(q_ok, q_ref[j * BQ:(j + 1) * BQ, :], 0)
        qg = jnp.concatenate(
            [q[:, (Q_PER_KV * g + hl) * HEAD_DIM:(Q_PER_KV * g + hl + 1) * HEAD_DIM]
             for hl in range(Q_PER_KV)], axis=0)
        k = kv_cat[:, g * HEAD_DIM:(g + 1) * HEAD_DIM]
        return lax.dot_general(k, qg, (((1,), (1,)), ((), ())), preferred_element_type=F32)

    st_next = scores(units[0])
    for u, (j, g, btype, kv_cat, _) in enumerate(units):
        st = st_next + bias_ref[btype, g]
        sink = sink_ref[g]
        m = jnp.maximum(jnp.max(st, axis=0, keepdims=True), sink)
        p = jnp.exp2(st - m)
        if u + 1 < len(units):
            st_next = scores(units[u + 1])
            p = jnp.concatenate([jnp.maximum(p[:16], st_next[:16] * 0.0 - 1.0), p[16:]], axis=0)
        v1 = jnp.concatenate([kv_cat[:, KV_W + g * HEAD_DIM:KV_W + (g + 1) * HEAD_DIM], ones_col], axis=1)
        ot = lax.dot_general(v1, p.astype(BF16), (((0,), (0,)), ((), ())), preferred_element_type=F32)
        denom = ot[HEAD_DIM:HEAD_DIM + 1] + jnp.exp2(sink - m)
        ot = ot[:HEAD_DIM] / denom
        for hp in range(Q_PER_KV // 2):
            h0 = Q_PER_KV * g + 2 * hp
            two = jnp.concatenate([ot[:, 2 * hp * BQ:(2 * hp + 1) * BQ],
                                   ot[:, (2 * hp + 1) * BQ:(2 * hp + 2) * BQ]], axis=0)
            o_ref[j * BQ:(j + 1) * BQ, h0 * HEAD_DIM:(h0 + 2) * HEAD_DIM] = two.T.astype(BF16)


def _dft_a_kernel(f_ref, wa_ref, cdft_ref, g_ref, fs_ref, gs_ref):
    cdft = cdft_ref[...]
    for k in range(B_PER_STEP):
        fs_ref[...] = f_ref[:, k, 0, :]
        h = jnp.dot(wa_ref[...], fs_ref[...].astype(BF16), preferred_element_type=F32).astype(BF16)
        for half in range(2):
            p = jnp.dot(h[:, half * 256:(half + 1) * 256], cdft, preferred_element_type=F32)
            gs_ref[:, half * 256:(half + 1) * 256] = p[:N1, :256] + p[N1:, 256:]
            gs_ref[:, F_W + half * 256:F_W + (half + 1) * 256] = p[N1:, :256] - p[:N1, 256:]
        g_ref[:, k, :, :] = gs_ref[...].astype(BF16).reshape(N1 // 2, 2, 2 * F_W)


def _dft_c_kernel(g_ref, ec_ref, es_ref, y_ref, gs_ref):
    gs_ref[:, 2 * N2:, :] = jnp.zeros((TCP, K2P - 2 * N2, 2 * F_W), BF16)
    for i in range(TCP):
        gs_ref[i, :2 * N2, :] = g_ref[i].reshape(2 * N2, 2 * F_W)
        g = gs_ref[i]
        y = (jnp.dot(ec_ref[i], g[:, :F_W], preferred_element_type=F32)
             + jnp.dot(es_ref[i], g[:, F_W:], preferred_element_type=F32))
        y_ref[:, i, :, :] = _unit_rms(y[:2 * N2]).astype(BF16).reshape(N2, 2, F_W)


def _post_kernel(h1_ref, a_ref, y_ref, gmix_ref, wout_ref, g2_ref, wg_ref, wu_ref, wd_ref,
                 gfin_ref, o_ref, wg_s, wu_s, wd_s, wout_s):
    s = pl.program_id(0)

    @pl.when(s < N_CAST)
    def _():
        _cast_weights(s, (wg_ref, wu_ref, wd_ref, wout_ref), (g2_ref, g2_ref, None, gmix_ref),
                      (wg_s, wu_s, wd_s, wout_s))

    @pl.when(s >= N_CAST)
    def _():
        y = y_ref[...].reshape(TM, F_W)
        a = a_ref[...]
        h2 = (h1_ref[...]
              + jnp.dot(a, wout_s[:ATTN_W, :], preferred_element_type=F32) * _rms_scale(a.astype(F32))
              + jnp.dot(y, wout_s[ATTN_W:, :], preferred_element_type=F32))
        h3 = _swiglu_half(h2, wg_s, wu_s, wd_s)
        o_ref[...] = _rms(h3, gfin_ref[...])


def _resident(shape):
    zeros = (0,) * len(shape)
    return pl.BlockSpec(shape, lambda *_: zeros, pipeline_mode=pl.Buffered(1))


def _params():
    return pltpu.CompilerParams(dimension_semantics=("arbitrary",), vmem_limit_bytes=VMEM_LIMIT)


def _row(g):
    return g.astype(F32).reshape(1, -1)


def kernel(x, meta_tokens, ffn1_norm, ffn1_w_gate, ffn1_w_up, ffn1_w_down, mix_norm, w_in, q_norm, k_norm, sink, attn_out_norm, fourier_out_norm, w_out, ffn2_norm, ffn2_w_gate, ffn2_w_up, ffn2_w_down, final_norm):
    assert x.shape == (1, SEQ, D_MODEL) and x.dtype == F32
    x2 = x.reshape(SEQ, D_MODEL)
    gqk = jnp.concatenate([jnp.tile(q_norm.astype(F32), N_HEADS) * (HEAD_DIM ** -0.5 * LOG2E),
                           jnp.tile(k_norm.astype(F32), N_KV)]).reshape(1, QK_W)
    bf = lambda w: w.astype(BF16)

    tile_idx = lambda s: jnp.maximum(s - N_CAST, 0)
    row_tile = lambda w: pl.BlockSpec((TM, w), lambda s: (tile_idx(s), 0))
    flat_rows = lambda w: pl.BlockSpec((TM, None, w), lambda s: (tile_idx(s), 0, 0))
    cast_rows = lambda r, c: pl.BlockSpec((r // N_CAST, c), lambda s: (jnp.minimum(s, N_CAST - 1), 0))
    gains = jnp.broadcast_to(
        jnp.stack([ffn1_norm, mix_norm, jnp.concatenate([attn_out_norm, fourier_out_norm]), ffn2_norm])
        .astype(F32)[:, :, None], (4, D_MODEL, 128))
    gain_rows = lambda k: pl.BlockSpec((None, D_MODEL // N_CAST, 128), lambda s: (k, jnp.minimum(s, N_CAST - 1), 0))
    bf16_copy = lambda r, c: pltpu.VMEM((r, c), BF16)
    ffn_in = [cast_rows(D_MODEL, D_FF), cast_rows(D_MODEL, D_FF), cast_rows(D_FF, D_MODEL)]
    ffn_scratch = [bf16_copy(D_MODEL, D_FF), bf16_copy(D_MODEL, D_FF), bf16_copy(D_FF, D_MODEL)]
    f32 = lambda w: w.astype(F32)

    h1, q, kv, f = pl.pallas_call(
        _pre_kernel,
        grid=(N_CAST + N_TILES,),
        in_specs=[row_tile(D_MODEL), _resident((N_META, D_MODEL)), gain_rows(0), *ffn_in,
                  gain_rows(1), cast_rows(D_MODEL, IN_W), _resident((1, QK_W)),
                  _resident((256, 256))],
        out_specs=[row_tile(D_MODEL), row_tile(ATTN_W), row_tile(2 * KV_W), flat_rows(F_W)],
        out_shape=[jax.ShapeDtypeStruct((L, D_MODEL), F32), jax.ShapeDtypeStruct((L, ATTN_W), BF16),
                   jax.ShapeDtypeStruct((L, 2 * KV_W), BF16), jax.ShapeDtypeStruct((L, 1, F_W), F32)],
        scratch_shapes=ffn_scratch + [bf16_copy(D_MODEL, IN_W)],
        compiler_params=_params(),
        name="pre",
    )(x2, meta_tokens.astype(F32), gains, f32(ffn1_w_gate), f32(ffn1_w_up), f32(ffn1_w_down), gains,
      f32(w_in), gqk, _ones_blockdiag())

    sink_rows = (jnp.repeat(sink.astype(F32).reshape(N_KV, Q_PER_KV), BQ, axis=1) * LOG2E)[:, None, :]

    q_rows = pl.BlockSpec((QBLK_PER_STEP * BQ, ATTN_W), lambda s: (s, 0))
    a_out = pl.pallas_call(
        _attn_kernel,
        grid=(pl.cdiv(N_QBLK + 1, QBLK_PER_STEP),),
        in_specs=[q_rows, _resident((L, 2 * KV_W)), _resident((4, N_KV, N_KEYS, Q_PER_KV * BQ)),
                  _resident((N_KV, 1, Q_PER_KV * BQ))],
        out_specs=q_rows,
        out_shape=jax.ShapeDtypeStruct((L, ATTN_W), BF16),
        compiler_params=_params(),
        name="attn",
    )(q, kv, jnp.asarray(_attn_bias_const()), sink_rows)

    row_set = lambda w: pl.BlockSpec((N1, B_PER_STEP, 1, w), lambda s: (0, s, 0, 0))
    g = pl.pallas_call(
        _dft_a_kernel,
        grid=(pl.cdiv(N2, B_PER_STEP),),
        in_specs=[row_set(F_W), _resident((2 * N1, N1)), _resident((256, 512))],
        out_specs=pl.BlockSpec((N1 // 2, B_PER_STEP, 2, 2 * F_W), lambda s: (0, s, 0, 0)),
        out_shape=jax.ShapeDtypeStruct((N1 // 2, N2, 2, 2 * F_W), BF16),
        scratch_shapes=[pltpu.VMEM((N1, F_W), F32), pltpu.VMEM((N1, 2 * F_W), F32)],
        compiler_params=_params(),
        name="dft_a",
    )(f.reshape(N1, N2, 1, F_W), bf(jnp.asarray(_stage_a_dft())), bf(jnp.asarray(_channel_dft())))

    ec, es = _stage_c_dft()
    y = pl.pallas_call(
        _dft_c_kernel,
        grid=(N1 // (2 * TCP),),
        in_specs=[pl.BlockSpec((TCP, N2, 2, 2 * F_W), lambda i: (i, 0, 0, 0)),
                  pl.BlockSpec((TCP, K2P, K2P), lambda i: (i, 0, 0)),
                  pl.BlockSpec((TCP, K2P, K2P), lambda i: (i, 0, 0))],
        out_specs=pl.BlockSpec((N2, TCP, 2, F_W), lambda i: (0, i, 0, 0)),
        out_shape=jax.ShapeDtypeStruct((N2, N1 // 2, 2, F_W), BF16),
        scratch_shapes=[pltpu.VMEM((TCP, K2P, 2 * F_W), BF16)],
        compiler_params=_params(),
        name="dft_c",
    )(g, bf(jnp.asarray(ec)), bf(jnp.asarray(es)))

    out = pl.pallas_call(
        _post_kernel,
        grid=(N_CAST + N_TILES,),
        in_specs=[row_tile(D_MODEL), row_tile(ATTN_W),
                  pl.BlockSpec((TM // 2, 2, F_W), lambda s: (tile_idx(s), 0, 0)), gain_rows(2),
                  cast_rows(D_MODEL, D_MODEL), gain_rows(3), *ffn_in, _resident((1, D_MODEL))],
        out_specs=row_tile(D_MODEL),
        out_shape=jax.ShapeDtypeStruct((SEQ, D_MODEL), F32),
        scratch_shapes=ffn_scratch + [bf16_copy(D_MODEL, D_MODEL)],
        compiler_params=_params(),
        name="post",
    )(h1, a_out, y.reshape(L // 2, 2, F_W), gains, f32(w_out), gains,
      f32(ffn2_w_gate), f32(ffn2_w_up), f32(ffn2_w_down), _row(final_norm))
    return out.reshape(1, SEQ, D_MODEL)
```

```python
import numpy as np
import jax
import jax.numpy as jnp
from jax import lax
from jax.experimental import pallas as pl
from jax.experimental.pallas import tpu as pltpu

F32 = jnp.float32
BF16 = jnp.bfloat16

D_MODEL = 1024
SEQ = 16384
N_META = 16
L = SEQ + N_META
HEAD_DIM = 64
N_HEADS = 8
N_KV = 2
Q_PER_KV = N_HEADS // N_KV
ATTN_W = N_HEADS * HEAD_DIM
KV_W = N_KV * HEAD_DIM
F_W = 512
F_GROUP = 64
IN_W = ATTN_W + 2 * KV_W + F_W
QK_W = ATTN_W + KV_W
WINDOW = 128
D_FF = 2816
N_CAST = 8
EPS = 1e-6
NEG = -1e30
LOG2E = 1.4426950408889634

TM = 656
N_TILES = L // TM
META_ROW0 = SEQ - (N_TILES - 1) * TM

BQ = 128
N_QBLK = SEQ // BQ
QBLK_PER_STEP = 12
BAND = BQ + 2 * WINDOW
N_KEYS = BAND + N_META

N1 = 200
N2 = 82
K2P = 176
B_PER_STEP = 7
TCP = 25

VMEM_LIMIT = 56 * 1024 * 1024


def _ones_blockdiag():
    m = np.kron(np.eye(4), np.ones((HEAD_DIM, HEAD_DIM)))
    return jnp.asarray(m, BF16)


def _channel_dft():
    c = np.arange(F_GROUP)
    ang = 2.0 * np.pi * np.outer(c, c) / F_GROUP
    cos = np.kron(np.eye(2), np.cos(ang) / 8.0)
    sin = np.kron(np.eye(2), np.sin(ang) / 8.0)
    return np.block([[cos, -sin], [sin, cos]]).astype(np.float32)


def _stage_a_dft():
    a = np.arange(N1, dtype=np.int64)
    m = (N2 * np.outer(a + N_META, a)) % L
    ang = 2.0 * np.pi * m / L
    return np.concatenate([np.cos(ang), -np.sin(ang)], axis=0).astype(np.float32)


def _stage_c_dft():
    c = np.arange(N1, dtype=np.int64)[:, None, None]
    d = np.arange(N2, dtype=np.int64)[None, :, None]
    b = np.arange(N2, dtype=np.int64)[None, None, :]
    m = ((c + N1 * d + N_META) * (b + N_META)) % L
    ang = 2.0 * np.pi * m / L
    ec = np.zeros((N1 // 2, K2P, K2P), np.float32)
    es = np.zeros((N1 // 2, K2P, K2P), np.float32)
    for e in range(2):
        ec[:, e:2 * N2:2, e:2 * N2:2] = (np.cos(ang) / np.sqrt(L))[e::2]
        es[:, e:2 * N2:2, e:2 * N2:2] = (np.sin(ang) / np.sqrt(L))[e::2]
    return ec, es


def _attn_bias_const():
    slopes = 2.0 ** -(np.arange(N_HEADS) + 1.0)
    i = np.arange(BQ)[:, None]
    j = np.arange(BAND)[None, :]
    m = np.arange(N_META)[None, :]
    out = np.full((4, N_KV, Q_PER_KV * BQ, N_KEYS), NEG, np.float64)
    for t in range(4):
        if t < 3:
            dist = np.abs(t * WINDOW + i - j)
            band_ok = dist <= WINDOW
            dist_m = np.minimum(t * WINDOW + N_META + i - m, WINDOW) if t == 0 else np.full((BQ, N_META), WINDOW)
        else:
            dist = N_META + j - i
            band_ok = (dist <= WINDOW) & (i < N_META)
            dist_m = np.abs(i - m)
        for h in range(N_HEADS):
            g, hl = divmod(h, Q_PER_KV)
            rows = slice(hl * BQ, (hl + 1) * BQ)
            band = np.where(band_ok, -slopes[h] * dist, NEG)
            if t == 3:
                band = np.where(i < N_META, band, 0.0)
            out[t, g, rows, :BAND] = band
            out[t, g, rows, BAND:BAND + N_META] = -slopes[h] * dist_m
    return np.ascontiguousarray(LOG2E * out.transpose(0, 1, 3, 2)).astype(np.float32)


def _rms_scale(x):
    return lax.rsqrt(jnp.mean(x * x, axis=-1, keepdims=True) + EPS)


def _unit_rms(x):
    return x * _rms_scale(x)


def _rms(x, g):
    return _unit_rms(x) * g


def _swiglu_half(x, wg_s, wu_s, wd_s):
    r = _rms_scale(x)
    xb = x.astype(BF16)
    gate = jnp.dot(xb, wg_s[...], preferred_element_type=F32) * r
    up = jnp.dot(xb, wu_s[...], preferred_element_type=F32) * r
    act = (gate * jax.nn.sigmoid(gate) * up).astype(BF16)
    return x + 0.5 * jnp.dot(act, wd_s[...], preferred_element_type=F32)


def _cast_weights(s, f32_refs, gain_refs, bf16_scratch):
    for w_ref, g_ref, w_s in zip(f32_refs, gain_refs, bf16_scratch):
        rows = w_ref.shape[0]
        w = w_ref[...]
        if g_ref is not None:
            w = w * jnp.tile(g_ref[...], (1, w.shape[1] // g_ref.shape[1]))
        w_s[pl.ds(pl.multiple_of(s * rows, rows), rows), :] = w.astype(BF16)


def _pre_kernel(x_ref, meta_ref, g1_ref, wg_ref, wu_ref, wd_ref, gm_ref, win_ref, gqk_ref,
                ones_ref, h1_ref, q_ref, kv_ref, f_ref, wg_s, wu_s, wd_s, win_s):
    s = pl.program_id(0)

    @pl.when(s < N_CAST)
    def _():
        _cast_weights(s, (wg_ref, wu_ref, wd_ref, win_ref), (g1_ref, g1_ref, None, gm_ref),
                      (wg_s, wu_s, wd_s, win_s))

    @pl.when(s >= N_CAST)
    def _():
        _pre_tile(s - N_CAST, x_ref, meta_ref, gqk_ref, ones_ref,
                  h1_ref, q_ref, kv_ref, f_ref, wg_s, wu_s, wd_s, win_s)


def _pre_tile(i, x_ref, meta_ref, gqk_ref, ones_ref,
              h1_ref, q_ref, kv_ref, f_ref, wg_s, wu_s, wd_s, win_s):
    tail = jnp.where(i == N_TILES - 1, meta_ref[...], x_ref[META_ROW0:, :])
    x = jnp.concatenate([x_ref[:META_ROW0, :], tail], axis=0)

    h1 = _swiglu_half(x, wg_s, wu_s, wd_s)
    h1_ref[...] = h1

    u = jnp.dot(h1.astype(BF16), win_s[...], preferred_element_type=F32) * _rms_scale(h1)

    qk = u[:, :QK_W]
    sq = (qk * qk).astype(BF16)
    ones = ones_ref[...]
    ss = jnp.concatenate([
        jnp.dot(sq[:, 0:256], ones, preferred_element_type=F32),
        jnp.dot(sq[:, 256:512], ones, preferred_element_type=F32),
        jnp.dot(sq[:, 512:640], ones[:KV_W, :KV_W], preferred_element_type=F32)], axis=1)
    qkn = qk * lax.rsqrt(ss * (1.0 / HEAD_DIM) + EPS) * gqk_ref[...]
    q_ref[...] = qkn[:, :ATTN_W].astype(BF16)
    kv_ref[...] = jnp.concatenate([qkn[:, ATTN_W:], u[:, QK_W:QK_W + KV_W]], axis=1).astype(BF16)

    f_ref[...] = u[:, QK_W + KV_W:]


def _attn_kernel(q_ref, kv_ref, bias_ref, sink_ref, o_ref):
    step = pl.program_id(0)
    kv_meta = kv_ref[SEQ:L, :]
    ones_col = (lax.broadcasted_iota(jnp.int32, (N_KEYS, 16), 1) == 0).astype(BF16)

    units = []
    for j in range(QBLK_PER_STEP):
        n = step * QBLK_PER_STEP + j
        btype = jnp.where(n == 0, 0, jnp.where(n == N_QBLK - 1, 2, jnp.where(n == N_QBLK, 3, 1)))
        start = jnp.where(n == N_QBLK, 0, jnp.clip((n - 1) * BQ, 0, SEQ - BAND))
        start = pl.multiple_of(start, BQ)
        kv_cat = jnp.concatenate([kv_ref[pl.ds(start, BAND), :], kv_meta], axis=0)
        q_ok = lax.broadcasted_iota(jnp.int32, (BQ, 1), 0) < L - n * BQ
        units.extend((j, g, btype, kv_cat, q_ok) for g in range(N_KV))

    def scores(unit):
        j, g, _, kv_cat, q_ok = unit
        q = jnp.where(q_ok, q_ref[j * BQ:(j + 1) * BQ, :], 0)
        qg = jnp.concatenate(
            [q[:, (Q_PER_KV * g + hl) * HEAD_DIM:(Q_PER_KV * g + hl + 1) * HEAD_DIM]
             for hl in range(Q_PER_KV)], axis=0)
        k = kv_cat[:, g * HEAD_DIM:(g + 1) * HEAD_DIM]
        return lax.dot_general(k, qg, (((1,), (1,)), ((), ())), preferred_element_type=F32)

    st_next = scores(units[0])
    for u, (j, g, btype, kv_cat, _) in enumerate(units):
        st = st_next + bias_ref[btype, g]
        sink = sink_ref[g]
        m = jnp.maximum(jnp.max(st, axis=0, keepdims=True), sink)
        p = jnp.exp2(st - m)
        if u + 1 < len(units):
            st_next = scores(units[u + 1])
            p = jnp.concatenate([jnp.maximum(p[:16], st_next[:16] * 0.0 - 1.0), p[16:]], axis=0)
        v1 = jnp.concatenate([kv_cat[:, KV_W + g * HEAD_DIM:KV_W + (g + 1) * HEAD_DIM], ones_col], axis=1)
        ot = lax.dot_general(v1, p.astype(BF16), (((0,), (0,)), ((), ())), preferred_element_type=F32)
        denom = ot[HEAD_DIM:HEAD_DIM + 1] + jnp.exp2(sink - m)
        ot = ot[:HEAD_DIM] / denom
        for hp in range(Q_PER_KV // 2):
            h0 = Q_PER_KV * g + 2 * hp
            two = jnp.concatenate([ot[:, 2 * hp * BQ:(2 * hp + 1) * BQ],
                                   ot[:, (2 * hp + 1) * BQ:(2 * hp + 2) * BQ]], axis=0)
            o_ref[j * BQ:(j + 1) * BQ, h0 * HEAD_DIM:(h0 + 2) * HEAD_DIM] = two.T.astype(BF16)


def _dft_a_kernel(f_ref, wa_ref, cdft_ref, g_ref, fs_ref, gs_ref):
    cdft = cdft_ref[...]
    for k in range(B_PER_STEP):
        fs_ref[...] = f_ref[:, k, 0, :]
        h = jnp.dot(wa_ref[...], fs_ref[...].astype(BF16), preferred_element_type=F32)
        hr, hi = h[:N1].astype(BF16), h[N1:].astype(BF16)
        for c in range(F_W // 128):
            cols = slice(c * 128, (c + 1) * 128)
            gri = jnp.dot(jnp.concatenate([hr[:, cols], hi[:, cols]], axis=1), cdft,
                          preferred_element_type=F32)
            gs_ref[:, cols] = gri[:, :128]
            gs_ref[:, F_W + c * 128:F_W + (c + 1) * 128] = gri[:, 128:]
        g_ref[:, k, :, :] = gs_ref[...].astype(BF16).reshape(N1 // 2, 2, 2 * F_W)


def _dft_c_kernel(g_ref, ec_ref, es_ref, y_ref, gs_ref):
    gs_ref[:, 2 * N2:, :] = jnp.zeros((TCP, K2P - 2 * N2, 2 * F_W), BF16)
    for i in range(TCP):
        gs_ref[i, :2 * N2, :] = g_ref[i].reshape(2 * N2, 2 * F_W)
        g = gs_ref[i]
        y = (jnp.dot(ec_ref[i], g[:, :F_W], preferred_element_type=F32)
             + jnp.dot(es_ref[i], g[:, F_W:], preferred_element_type=F32))
        y_ref[:, i, :, :] = _unit_rms(y[:2 * N2]).astype(BF16).reshape(N2, 2, F_W)


def _post_kernel(h1_ref, a_ref, y_ref, gmix_ref, wout_ref, g2_ref, wg_ref, wu_ref, wd_ref,
                 gfin_ref, o_ref, wg_s, wu_s, wd_s, wout_s):
    s = pl.program_id(0)

    @pl.when(s < N_CAST)
    def _():
        _cast_weights(s, (wg_ref, wu_ref, wd_ref, wout_ref), (g2_ref, g2_ref, None, gmix_ref),
                      (wg_s, wu_s, wd_s, wout_s))

    @pl.when(s >= N_CAST)
    def _():
        y = y_ref[...].reshape(TM, F_W)
        a = a_ref[...]
        h2 = (h1_ref[...]
              + jnp.dot(a, wout_s[:ATTN_W, :], preferred_element_type=F32) * _rms_scale(a.astype(F32))
              + jnp.dot(y, wout_s[ATTN_W:, :], preferred_element_type=F32))
        h3 = _swiglu_half(h2, wg_s, wu_s, wd_s)
        o_ref[...] = _rms(h3, gfin_ref[...])


def _resident(shape):
    zeros = (0,) * len(shape)
    return pl.BlockSpec(shape, lambda *_: zeros, pipeline_mode=pl.Buffered(1))


def _params():
    return pltpu.CompilerParams(dimension_semantics=("arbitrary",), vmem_limit_bytes=VMEM_LIMIT)


def _row(g):
    return g.astype(F32).reshape(1, -1)


def kernel(x, meta_tokens, ffn1_norm, ffn1_w_gate, ffn1_w_up, ffn1_w_down, mix_norm, w_in, q_norm, k_norm, sink, attn_out_norm, fourier_out_norm, w_out, ffn2_norm, ffn2_w_gate, ffn2_w_up, ffn2_w_down, final_norm):
    assert x.shape == (1, SEQ, D_MODEL) and x.dtype == F32
    x2 = x.reshape(SEQ, D_MODEL)
    gqk = jnp.concatenate([jnp.tile(q_norm.astype(F32), N_HEADS) * (HEAD_DIM ** -0.5 * LOG2E),
                           jnp.tile(k_norm.astype(F32), N_KV)]).reshape(1, QK_W)
    bf = lambda w: w.astype(BF16)

    tile_idx = lambda s: jnp.maximum(s - N_CAST, 0)
    row_tile = lambda w: pl.BlockSpec((TM, w), lambda s: (tile_idx(s), 0))
    flat_rows = lambda w: pl.BlockSpec((TM, None, w), lambda s: (tile_idx(s), 0, 0))
    cast_rows = lambda r, c: pl.BlockSpec((r // N_CAST, c), lambda s: (jnp.minimum(s, N_CAST - 1), 0))
    gains = jnp.broadcast_to(
        jnp.stack([ffn1_norm, mix_norm, jnp.concatenate([attn_out_norm, fourier_out_norm]), ffn2_norm])
        .astype(F32)[:, :, None], (4, D_MODEL, 128))
    gain_rows = lambda k: pl.BlockSpec((None, D_MODEL // N_CAST, 128), lambda s: (k, jnp.minimum(s, N_CAST - 1), 0))
    bf16_copy = lambda r, c: pltpu.VMEM((r, c), BF16)
    ffn_in = [cast_rows(D_MODEL, D_FF), cast_rows(D_MODEL, D_FF), cast_rows(D_FF, D_MODEL)]
    ffn_scratch = [bf16_copy(D_MODEL, D_FF), bf16_copy(D_MODEL, D_FF), bf16_copy(D_FF, D_MODEL)]
    f32 = lambda w: w.astype(F32)

    h1, q, kv, f = pl.pallas_call(
        _pre_kernel,
        grid=(N_CAST + N_TILES,),
        in_specs=[row_tile(D_MODEL), _resident((N_META, D_MODEL)), gain_rows(0), *ffn_in,
                  gain_rows(1), cast_rows(D_MODEL, IN_W), _resident((1, QK_W)),
                  _resident((256, 256))],
        out_specs=[row_tile(D_MODEL), row_tile(ATTN_W), row_tile(2 * KV_W), flat_rows(F_W)],
        out_shape=[jax.ShapeDtypeStruct((L, D_MODEL), F32), jax.ShapeDtypeStruct((L, ATTN_W), BF16),
                   jax.ShapeDtypeStruct((L, 2 * KV_W), BF16), jax.ShapeDtypeStruct((L, 1, F_W), F32)],
        scratch_shapes=ffn_scratch + [bf16_copy(D_MODEL, IN_W)],
        compiler_params=_params(),
        name="pre",
    )(x2, meta_tokens.astype(F32), gains, f32(ffn1_w_gate), f32(ffn1_w_up), f32(ffn1_w_down), gains,
      f32(w_in), gqk, _ones_blockdiag())

    sink_rows = (jnp.repeat(sink.astype(F32).reshape(N_KV, Q_PER_KV), BQ, axis=1) * LOG2E)[:, None, :]

    q_rows = pl.BlockSpec((QBLK_PER_STEP * BQ, ATTN_W), lambda s: (s, 0))
    a_out = pl.pallas_call(
        _attn_kernel,
        grid=(pl.cdiv(N_QBLK + 1, QBLK_PER_STEP),),
        in_specs=[q_rows, _resident((L, 2 * KV_W)), _resident((4, N_KV, N_KEYS, Q_PER_KV * BQ)),
                  _resident((N_KV, 1, Q_PER_KV * BQ))],
        out_specs=q_rows,
        out_shape=jax.ShapeDtypeStruct((L, ATTN_W), BF16),
        compiler_params=_params(),
        name="attn",
    )(q, kv, jnp.asarray(_attn_bias_const()), sink_rows)

    row_set = lambda w: pl.BlockSpec((N1, B_PER_STEP, 1, w), lambda s: (0, s, 0, 0))
    g = pl.pallas_call(
        _dft_a_kernel,
        grid=(pl.cdiv(N2, B_PER_STEP),),
        in_specs=[row_set(F_W), _resident((2 * N1, N1)), _resident((256, 256))],
        out_specs=pl.BlockSpec((N1 // 2, B_PER_STEP, 2, 2 * F_W), lambda s: (0, s, 0, 0)),
        out_shape=jax.ShapeDtypeStruct((N1 // 2, N2, 2, 2 * F_W), BF16),
        scratch_shapes=[pltpu.VMEM((N1, F_W), F32), pltpu.VMEM((N1, 2 * F_W), F32)],
        compiler_params=_params(),
        name="dft_a",
    )(f.reshape(N1, N2, 1, F_W), bf(jnp.asarray(_stage_a_dft())), bf(jnp.asarray(_channel_dft())))

    ec, es = _stage_c_dft()
    y = pl.pallas_call(
        _dft_c_kernel,
        grid=(N1 // (2 * TCP),),
        in_specs=[pl.BlockSpec((TCP, N2, 2, 2 * F_W), lambda i: (i, 0, 0, 0)),
                  pl.BlockSpec((TCP, K2P, K2P), lambda i: (i, 0, 0)),
                  pl.BlockSpec((TCP, K2P, K2P), lambda i: (i, 0, 0))],
        out_specs=pl.BlockSpec((N2, TCP, 2, F_W), lambda i: (0, i, 0, 0)),
        out_shape=jax.ShapeDtypeStruct((N2, N1 // 2, 2, F_W), BF16),
        scratch_shapes=[pltpu.VMEM((TCP, K2P, 2 * F_W), BF16)],
        compiler_params=_params(),
        name="dft_c",
    )(g, bf(jnp.asarray(ec)), bf(jnp.asarray(es)))

    out = pl.pallas_call(
        _post_kernel,
        grid=(N_CAST + N_TILES,),
        in_specs=[row_tile(D_MODEL), row_tile(ATTN_W),
                  pl.BlockSpec((TM // 2, 2, F_W), lambda s: (tile_idx(s), 0, 0)), gain_rows(2),
                  cast_rows(D_MODEL, D_MODEL), gain_rows(3), *ffn_in, _resident((1, D_MODEL))],
        out_specs=row_tile(D_MODEL),
        out_shape=jax.ShapeDtypeStruct((SEQ, D_MODEL), F32),
        scratch_shapes=ffn_scratch + [bf16_copy(D_MODEL, D_MODEL)],
        compiler_params=_params(),
        name="post",
    )(h1, a_out, y.reshape(L // 2, 2, F_W), gains, f32(w_out), gains,
      f32(ffn2_w_gate), f32(ffn2_w_up), f32(ffn2_w_down), _row(final_norm))
    return out.reshape(1, SEQ, D_MODEL)
```

```python
import numpy as np
import jax
import jax.numpy as jnp
from jax import lax
from jax.experimental import pallas as pl
from jax.experimental.pallas import tpu as pltpu

F32 = jnp.float32
BF16 = jnp.bfloat16

D_MODEL = 1024
SEQ = 16384
N_META = 16
L = SEQ + N_META
HEAD_DIM = 64
N_HEADS = 8
N_KV = 2
Q_PER_KV = N_HEADS // N_KV
ATTN_W = N_HEADS * HEAD_DIM
KV_W = N_KV * HEAD_DIM
F_W = 512
F_GROUP = 64
IN_W = ATTN_W + 2 * KV_W + F_W
QK_W = ATTN_W + KV_W
WINDOW = 128
D_FF = 2816
N_CAST = 8
EPS = 1e-6
NEG = -1e30
LOG2E = 1.4426950408889634

TM = 656
N_TILES = L // TM
META_ROW0 = SEQ - (N_TILES - 1) * TM

BQ = 128
N_QBLK = SEQ // BQ
QBLK_PER_STEP = 12
BAND = BQ + 2 * WINDOW
N_KEYS = BAND + N_META

N1 = 200
N2 = 82
A_PER_TILE = TM // N2
K2P = 176
B_PER_STEP = 7
TCP = 25

VMEM_LIMIT = 56 * 1024 * 1024


def _ones_blockdiag():
    m = np.kron(np.eye(4), np.ones((HEAD_DIM, HEAD_DIM)))
    return jnp.asarray(m, BF16)


def _channel_dft():
    c = np.arange(F_GROUP)
    ang = 2.0 * np.pi * np.outer(c, c) / F_GROUP
    cos = np.kron(np.eye(2), np.cos(ang) / 8.0)
    sin = np.kron(np.eye(2), np.sin(ang) / 8.0)
    return np.block([[cos, -sin], [sin, cos]]).astype(np.float32)


def _stage_a_dft():
    a = np.arange(N1, dtype=np.int64)
    m = (N2 * np.outer(a + N_META, a)) % L
    ang = 2.0 * np.pi * m / L
    return np.concatenate([np.cos(ang), -np.sin(ang)], axis=0).astype(np.float32)


def _stage_c_dft():
    c = np.arange(N1, dtype=np.int64)[:, None, None]
    d = np.arange(N2, dtype=np.int64)[None, :, None]
    b = np.arange(N2, dtype=np.int64)[None, None, :]
    m = ((c + N1 * d + N_META) * (b + N_META)) % L
    ang = 2.0 * np.pi * m / L
    ec = np.zeros((N1 // 2, K2P, K2P), np.float32)
    es = np.zeros((N1 // 2, K2P, K2P), np.float32)
    for e in range(2):
        ec[:, e:2 * N2:2, e:2 * N2:2] = (np.cos(ang) / np.sqrt(L))[e::2]
        es[:, e:2 * N2:2, e:2 * N2:2] = (np.sin(ang) / np.sqrt(L))[e::2]
    return ec, es


def _attn_bias_const():
    slopes = 2.0 ** -(np.arange(N_HEADS) + 1.0)
    i = np.arange(BQ)[:, None]
    j = np.arange(BAND)[None, :]
    m = np.arange(N_META)[None, :]
    out = np.full((4, N_KV, Q_PER_KV * BQ, N_KEYS), NEG, np.float64)
    for t in range(4):
        if t < 3:
            dist = np.abs(t * WINDOW + i - j)
            band_ok = dist <= WINDOW
            dist_m = np.minimum(t * WINDOW + N_META + i - m, WINDOW) if t == 0 else np.full((BQ, N_META), WINDOW)
        else:
            dist = N_META + j - i
            band_ok = (dist <= WINDOW) & (i < N_META)
            dist_m = np.abs(i - m)
        for h in range(N_HEADS):
            g, hl = divmod(h, Q_PER_KV)
            rows = slice(hl * BQ, (hl + 1) * BQ)
            band = np.where(band_ok, -slopes[h] * dist, NEG)
            if t == 3:
                band = np.where(i < N_META, band, 0.0)
            out[t, g, rows, :BAND] = band
            out[t, g, rows, BAND:BAND + N_META] = -slopes[h] * dist_m
    return np.ascontiguousarray(LOG2E * out.transpose(0, 1, 3, 2)).astype(np.float32)


def _rms_scale(x):
    return lax.rsqrt(jnp.mean(x * x, axis=-1, keepdims=True) + EPS)


def _unit_rms(x):
    return x * _rms_scale(x)


def _rms(x, g):
    return _unit_rms(x) * g


def _swiglu_half(x, wg_s, wu_s, wd_s):
    r = _rms_scale(x)
    xb = x.astype(BF16)
    gate = jnp.dot(xb, wg_s[...], preferred_element_type=F32) * r
    up = jnp.dot(xb, wu_s[...], preferred_element_type=F32) * r
    act = (gate * jax.nn.sigmoid(gate) * up).astype(BF16)
    return x + 0.5 * jnp.dot(act, wd_s[...], preferred_element_type=F32)


def _cast_weights(s, f32_refs, gain_refs, bf16_scratch):
    for w_ref, g_ref, w_s in zip(f32_refs, gain_refs, bf16_scratch):
        rows = w_ref.shape[0]
        w = w_ref[...]
        if g_ref is not None:
            w = w * jnp.tile(g_ref[...], (1, w.shape[1] // g_ref.shape[1]))
        w_s[pl.ds(pl.multiple_of(s * rows, rows), rows), :] = w.astype(BF16)


def _pre_kernel(x_ref, meta_ref, g1_ref, wg_ref, wu_ref, wd_ref, gm_ref, win_ref, gqk_ref,
                ones_ref, h1_ref, q_ref, kv_ref, f_ref, wg_s, wu_s, wd_s, win_s, fs_s):
    s = pl.program_id(0)

    @pl.when(s < N_CAST)
    def _():
        _cast_weights(s, (wg_ref, wu_ref, wd_ref, win_ref), (g1_ref, g1_ref, None, gm_ref),
                      (wg_s, wu_s, wd_s, win_s))

    @pl.when(s >= N_CAST)
    def _():
        _pre_tile(s - N_CAST, x_ref, meta_ref, gqk_ref, ones_ref,
                  h1_ref, q_ref, kv_ref, f_ref, wg_s, wu_s, wd_s, win_s, fs_s)


def _pre_tile(i, x_ref, meta_ref, gqk_ref, ones_ref,
              h1_ref, q_ref, kv_ref, f_ref, wg_s, wu_s, wd_s, win_s, fs_s):
    tail = jnp.where(i == N_TILES - 1, meta_ref[...], x_ref[META_ROW0:, :])
    x = jnp.concatenate([x_ref[:META_ROW0, :], tail], axis=0)

    h1 = _swiglu_half(x, wg_s, wu_s, wd_s)
    h1_ref[...] = h1

    u = jnp.dot(h1.astype(BF16), win_s[...], preferred_element_type=F32) * _rms_scale(h1)

    qk = u[:, :QK_W]
    sq = (qk * qk).astype(BF16)
    ones = ones_ref[...]
    ss = jnp.concatenate([
        jnp.dot(sq[:, 0:256], ones, preferred_element_type=F32),
        jnp.dot(sq[:, 256:512], ones, preferred_element_type=F32),
        jnp.dot(sq[:, 512:640], ones[:KV_W, :KV_W], preferred_element_type=F32)], axis=1)
    qkn = qk * lax.rsqrt(ss * (1.0 / HEAD_DIM) + EPS) * gqk_ref[...]
    q_ref[...] = qkn[:, :ATTN_W].astype(BF16)
    kv_ref[...] = jnp.concatenate([qkn[:, ATTN_W:], u[:, QK_W:QK_W + KV_W]], axis=1).astype(BF16)

    for c in range(F_W // 128):
        lanes = slice(c * 128, (c + 1) * 128)
        fs_s[c] = u[:, QK_W + KV_W + c * 128:QK_W + KV_W + (c + 1) * 128]
        for b in range(N2):
            f_ref[b, :, lanes] = fs_s[c, pl.ds(b, A_PER_TILE, stride=N2), :]


def _attn_kernel(q_ref, kv_ref, bias_ref, sink_ref, o_ref):
    step = pl.program_id(0)
    kv_meta = kv_ref[SEQ:L, :]
    ones_col = (lax.broadcasted_iota(jnp.int32, (N_KEYS, 16), 1) == 0).astype(BF16)

    units = []
    for j in range(QBLK_PER_STEP):
        n = step * QBLK_PER_STEP + j
        btype = jnp.where(n == 0, 0, jnp.where(n == N_QBLK - 1, 2, jnp.where(n == N_QBLK, 3, 1)))
        start = jnp.where(n == N_QBLK, 0, jnp.clip((n - 1) * BQ, 0, SEQ - BAND))
        start = pl.multiple_of(start, BQ)
        kv_cat = jnp.concatenate([kv_ref[pl.ds(start, BAND), :], kv_meta], axis=0)
        q_ok = lax.broadcasted_iota(jnp.int32, (BQ, 1), 0) < L - n * BQ
        units.extend((j, g, btype, kv_cat, q_ok) for g in range(N_KV))

    def scores(unit):
        j, g, _, kv_cat, q_ok = unit
        q = jnp.where(q_ok, q_ref[j * BQ:(j + 1) * BQ, :], 0)
        qg = jnp.concatenate(
            [q[:, (Q_PER_KV * g + hl) * HEAD_DIM:(Q_PER_KV * g + hl + 1) * HEAD_DIM]
             for hl in range(Q_PER_KV)], axis=0)
        k = kv_cat[:, g * HEAD_DIM:(g + 1) * HEAD_DIM]
        return lax.dot_general(k, qg, (((1,), (1,)), ((), ())), preferred_element_type=F32)

    st_next = scores(units[0])
    for u, (j, g, btype, kv_cat, _) in enumerate(units):
        st = st_next + bias_ref[btype, g]
        sink = sink_ref[g]
        m = jnp.maximum(jnp.max(st, axis=0, keepdims=True), sink)
        p = jnp.exp2(st - m)
        if u + 1 < len(units):
            st_next = scores(units[u + 1])
            p = jnp.concatenate([jnp.maximum(p[:16], st_next[:16] * 0.0 - 1.0), p[16:]], axis=0)
        v1 = jnp.concatenate([kv_cat[:, KV_W + g * HEAD_DIM:KV_W + (g + 1) * HEAD_DIM], ones_col], axis=1)
        ot = lax.dot_general(v1, p.astype(BF16), (((0,), (0,)), ((), ())), preferred_element_type=F32)
        denom = ot[HEAD_DIM:HEAD_DIM + 1] + jnp.exp2(sink - m)
        ot = ot[:HEAD_DIM] / denom
        for hp in range(Q_PER_KV // 2):
            h0 = Q_PER_KV * g + 2 * hp
            two = jnp.concatenate([ot[:, 2 * hp * BQ:(2 * hp + 1) * BQ],
                                   ot[:, (2 * hp + 1) * BQ:(2 * hp + 2) * BQ]], axis=0)
            o_ref[j * BQ:(j + 1) * BQ, h0 * HEAD_DIM:(h0 + 2) * HEAD_DIM] = two.T.astype(BF16)


def _dft_a_kernel(f_ref, wa_ref, cdft_ref, g_ref, gs_ref):
    cdft = cdft_ref[...]
    for k in range(B_PER_STEP):
        h = jnp.dot(wa_ref[...], f_ref[k].astype(BF16), preferred_element_type=F32)
        hr, hi = h[:N1].astype(BF16), h[N1:].astype(BF16)
        for c in range(F_W // 128):
            cols = slice(c * 128, (c + 1) * 128)
            gri = jnp.dot(jnp.concatenate([hr[:, cols], hi[:, cols]], axis=1), cdft,
                          preferred_element_type=F32)
            gs_ref[:, cols] = gri[:, :128]
            gs_ref[:, F_W + c * 128:F_W + (c + 1) * 128] = gri[:, 128:]
        g_ref[:, k, :, :] = gs_ref[...].astype(BF16).reshape(N1 // 2, 2, 2 * F_W)


def _dft_c_kernel(g_ref, ec_ref, es_ref, y_ref, gs_ref):
    gs_ref[:, 2 * N2:, :] = jnp.zeros((TCP, K2P - 2 * N2, 2 * F_W), BF16)
    for i in range(TCP):
        gs_ref[i, :2 * N2, :] = g_ref[i].reshape(2 * N2, 2 * F_W)
        g = gs_ref[i]
        y = (jnp.dot(ec_ref[i], g[:, :F_W], preferred_element_type=F32)
             + jnp.dot(es_ref[i], g[:, F_W:], preferred_element_type=F32))
        y_ref[:, i, :, :] = _unit_rms(y[:2 * N2]).astype(BF16).reshape(N2, 2, F_W)


def _post_kernel(h1_ref, a_ref, y_ref, gmix_ref, wout_ref, g2_ref, wg_ref, wu_ref, wd_ref,
                 gfin_ref, o_ref, wg_s, wu_s, wd_s, wout_s):
    s = pl.program_id(0)

    @pl.when(s < N_CAST)
    def _():
        _cast_weights(s, (wg_ref, wu_ref, wd_ref, wout_ref), (g2_ref, g2_ref, None, gmix_ref),
                      (wg_s, wu_s, wd_s, wout_s))

    @pl.when(s >= N_CAST)
    def _():
        y = y_ref[...].reshape(TM, F_W)
        a = a_ref[...]
        h2 = (h1_ref[...]
              + jnp.dot(a, wout_s[:ATTN_W, :], preferred_element_type=F32) * _rms_scale(a.astype(F32))
              + jnp.dot(y, wout_s[ATTN_W:, :], preferred_element_type=F32))
        h3 = _swiglu_half(h2, wg_s, wu_s, wd_s)
        o_ref[...] = _rms(h3, gfin_ref[...])


def _resident(shape):
    zeros = (0,) * len(shape)
    return pl.BlockSpec(shape, lambda *_: zeros, pipeline_mode=pl.Buffered(1))


def _params():
    return pltpu.CompilerParams(dimension_semantics=("arbitrary",), vmem_limit_bytes=VMEM_LIMIT)


def _row(g):
    return g.astype(F32).reshape(1, -1)


def kernel(x, meta_tokens, ffn1_norm, ffn1_w_gate, ffn1_w_up, ffn1_w_down, mix_norm, w_in, q_norm, k_norm, sink, attn_out_norm, fourier_out_norm, w_out, ffn2_norm, ffn2_w_gate, ffn2_w_up, ffn2_w_down, final_norm):
    assert x.shape == (1, SEQ, D_MODEL) and x.dtype == F32
    assert A_PER_TILE * N2 == TM and A_PER_TILE * N_TILES == N1
    x2 = x.reshape(SEQ, D_MODEL)
    gqk = jnp.concatenate([jnp.tile(q_norm.astype(F32), N_HEADS) * (HEAD_DIM ** -0.5 * LOG2E),
                           jnp.tile(k_norm.astype(F32), N_KV)]).reshape(1, QK_W)
    bf = lambda w: w.astype(BF16)

    tile_idx = lambda s: jnp.maximum(s - N_CAST, 0)
    row_tile = lambda w: pl.BlockSpec((TM, w), lambda s: (tile_idx(s), 0))
    row_sets = lambda w: pl.BlockSpec((N2, A_PER_TILE, w), lambda s: (0, tile_idx(s), 0))
    cast_rows = lambda r, c: pl.BlockSpec((r // N_CAST, c), lambda s: (jnp.minimum(s, N_CAST - 1), 0))
    gains = jnp.broadcast_to(
        jnp.stack([ffn1_norm, mix_norm, jnp.concatenate([attn_out_norm, fourier_out_norm]), ffn2_norm])
        .astype(F32)[:, :, None], (4, D_MODEL, 128))
    gain_rows = lambda k: pl.BlockSpec((None, D_MODEL // N_CAST, 128), lambda s: (k, jnp.minimum(s, N_CAST - 1), 0))
    bf16_copy = lambda r, c: pltpu.VMEM((r, c), BF16)
    ffn_in = [cast_rows(D_MODEL, D_FF), cast_rows(D_MODEL, D_FF), cast_rows(D_FF, D_MODEL)]
    ffn_scratch = [bf16_copy(D_MODEL, D_FF), bf16_copy(D_MODEL, D_FF), bf16_copy(D_FF, D_MODEL)]
    f32 = lambda w: w.astype(F32)

    h1, q, kv, f = pl.pallas_call(
        _pre_kernel,
        grid=(N_CAST + N_TILES,),
        in_specs=[row_tile(D_MODEL), _resident((N_META, D_MODEL)), gain_rows(0), *ffn_in,
                  gain_rows(1), cast_rows(D_MODEL, IN_W), _resident((1, QK_W)),
                  _resident((256, 256))],
        out_specs=[row_tile(D_MODEL), row_tile(ATTN_W), row_tile(2 * KV_W), row_sets(F_W)],
        out_shape=[jax.ShapeDtypeStruct((L, D_MODEL), F32), jax.ShapeDtypeStruct((L, ATTN_W), BF16),
                   jax.ShapeDtypeStruct((L, 2 * KV_W), BF16), jax.ShapeDtypeStruct((N2, N1, F_W), F32)],
        scratch_shapes=ffn_scratch + [bf16_copy(D_MODEL, IN_W), pltpu.VMEM((F_W // 128, TM, 128), F32)],
        compiler_params=_params(),
        name="pre",
    )(x2, meta_tokens.astype(F32), gains, f32(ffn1_w_gate), f32(ffn1_w_up), f32(ffn1_w_down), gains,
      f32(w_in), gqk, _ones_blockdiag())

    sink_rows = (jnp.repeat(sink.astype(F32).reshape(N_KV, Q_PER_KV), BQ, axis=1) * LOG2E)[:, None, :]

    q_rows = pl.BlockSpec((QBLK_PER_STEP * BQ, ATTN_W), lambda s: (s, 0))
    a_out = pl.pallas_call(
        _attn_kernel,
        grid=(pl.cdiv(N_QBLK + 1, QBLK_PER_STEP),),
        in_specs=[q_rows, _resident((L, 2 * KV_W)), _resident((4, N_KV, N_KEYS, Q_PER_KV * BQ)),
                  _resident((N_KV, 1, Q_PER_KV * BQ))],
        out_specs=q_rows,
        out_shape=jax.ShapeDtypeStruct((L, ATTN_W), BF16),
        compiler_params=_params(),
        name="attn",
    )(q, kv, jnp.asarray(_attn_bias_const()), sink_rows)

    g = pl.pallas_call(
        _dft_a_kernel,
        grid=(pl.cdiv(N2, B_PER_STEP),),
        in_specs=[pl.BlockSpec((B_PER_STEP, N1, F_W), lambda s: (s, 0, 0)),
                  _resident((2 * N1, N1)), _resident((256, 256))],
        out_specs=pl.BlockSpec((N1 // 2, B_PER_STEP, 2, 2 * F_W), lambda s: (0, s, 0, 0)),
        out_shape=jax.ShapeDtypeStruct((N1 // 2, N2, 2, 2 * F_W), BF16),
        scratch_shapes=[pltpu.VMEM((N1, 2 * F_W), F32)],
        compiler_params=_params(),
        name="dft_a",
    )(f, bf(jnp.asarray(_stage_a_dft())), bf(jnp.asarray(_channel_dft())))

    ec, es = _stage_c_dft()
    y = pl.pallas_call(
        _dft_c_kernel,
        grid=(N1 // (2 * TCP),),
        in_specs=[pl.BlockSpec((TCP, N2, 2, 2 * F_W), lambda i: (i, 0, 0, 0)),
                  pl.BlockSpec((TCP, K2P, K2P), lambda i: (i, 0, 0)),
                  pl.BlockSpec((TCP, K2P, K2P), lambda i: (i, 0, 0))],
        out_specs=pl.BlockSpec((N2, TCP, 2, F_W), lambda i: (0, i, 0, 0)),
        out_shape=jax.ShapeDtypeStruct((N2, N1 // 2, 2, F_W), BF16),
        scratch_shapes=[pltpu.VMEM((TCP, K2P, 2 * F_W), BF16)],
        compiler_params=_params(),
        name="dft_c",
    )(g, bf(jnp.asarray(ec)), bf(jnp.asarray(es)))

    out = pl.pallas_call(
        _post_kernel,
        grid=(N_CAST + N_TILES,),
        in_specs=[row_tile(D_MODEL), row_tile(ATTN_W),
                  pl.BlockSpec((TM // 2, 2, F_W), lambda s: (tile_idx(s), 0, 0)), gain_rows(2),
                  cast_rows(D_MODEL, D_MODEL), gain_rows(3), *ffn_in, _resident((1, D_MODEL))],
        out_specs=row_tile(D_MODEL),
        out_shape=jax.ShapeDtypeStruct((SEQ, D_MODEL), F32),
        scratch_shapes=ffn_scratch + [bf16_copy(D_MODEL, D_MODEL)],
        compiler_params=_params(),
        name="post",
    )(h1, a_out, y.reshape(L // 2, 2, F_W), gains, f32(w_out), gains,
      f32(ffn2_w_gate), f32(ffn2_w_up), f32(ffn2_w_down), _row(final_norm))
    return out.reshape(1, SEQ, D_MODEL)
```

```python
import numpy as np
import jax
import jax.numpy as jnp
from jax import lax
from jax.experimental import pallas as pl
from jax.experimental.pallas import tpu as pltpu

F32 = jnp.float32
BF16 = jnp.bfloat16

D_MODEL = 1024
SEQ = 16384
N_META = 16
L = SEQ + N_META
HEAD_DIM = 64
N_HEADS = 8
N_KV = 2
Q_PER_KV = N_HEADS // N_KV
ATTN_W = N_HEADS * HEAD_DIM
KV_W = N_KV * HEAD_DIM
F_W = 512
F_GROUP = 64
IN_W = ATTN_W + 2 * KV_W + F_W
QK_W = ATTN_W + KV_W
WINDOW = 128
D_FF = 2816
N_CAST = 8
EPS = 1e-6
NEG = -1e30
LOG2E = 1.4426950408889634

TM = 656
N_TILES = L // TM
META_ROW0 = SEQ - (N_TILES - 1) * TM

BQ = 128
N_QBLK = SEQ // BQ
QBLK_PER_STEP = 12
BAND = BQ + 2 * WINDOW
N_KEYS = BAND + N_META

N1 = 200
N2 = 82
A_PER_TILE = TM // N2
K2P = 176
B_PER_STEP = 8
TCP = 25

VMEM_LIMIT = 56 * 1024 * 1024


def _ones_blockdiag():
    m = np.kron(np.eye(4), np.ones((HEAD_DIM, HEAD_DIM)))
    return jnp.asarray(m, BF16)


def _channel_dft():
    c = np.arange(F_GROUP)
    ang = 2.0 * np.pi * np.outer(c, c) / F_GROUP
    cos = np.kron(np.eye(2), np.cos(ang) / 8.0)
    sin = np.kron(np.eye(2), np.sin(ang) / 8.0)
    return np.block([[cos, -sin], [sin, cos]]).astype(np.float32)


def _stage_a_dft():
    a = np.arange(N1, dtype=np.int64)
    m = (N2 * np.outer(a + N_META, a)) % L
    ang = 2.0 * np.pi * m / L
    return np.concatenate([np.cos(ang), -np.sin(ang)], axis=0).astype(np.float32)


def _stage_c_dft():
    c = np.arange(N1, dtype=np.int64)[:, None, None]
    d = np.arange(N2, dtype=np.int64)[None, :, None]
    b = np.arange(N2, dtype=np.int64)[None, None, :]
    m = ((c + N1 * d + N_META) * (b + N_META)) % L
    ang = 2.0 * np.pi * m / L
    ec = np.zeros((N1 // 2, K2P, K2P), np.float32)
    es = np.zeros((N1 // 2, K2P, K2P), np.float32)
    for e in range(2):
        ec[:, e:2 * N2:2, e:2 * N2:2] = (np.cos(ang) / np.sqrt(L))[e::2]
        es[:, e:2 * N2:2, e:2 * N2:2] = (np.sin(ang) / np.sqrt(L))[e::2]
    return ec, es


def _attn_bias_const():
    slopes = 2.0 ** -(np.arange(N_HEADS) + 1.0)
    i = np.arange(BQ)[:, None]
    j = np.arange(BAND)[None, :]
    m = np.arange(N_META)[None, :]
    out = np.full((4, N_KV, Q_PER_KV * BQ, N_KEYS), NEG, np.float64)
    for t in range(4):
        if t < 3:
            dist = np.abs(t * WINDOW + i - j)
            band_ok = dist <= WINDOW
            dist_m = np.minimum(t * WINDOW + N_META + i - m, WINDOW) if t == 0 else np.full((BQ, N_META), WINDOW)
        else:
            dist = N_META + j - i
            band_ok = (dist <= WINDOW) & (i < N_META)
            dist_m = np.abs(i - m)
        for h in range(N_HEADS):
            g, hl = divmod(h, Q_PER_KV)
            rows = slice(hl * BQ, (hl + 1) * BQ)
            band = np.where(band_ok, -slopes[h] * dist, NEG)
            if t == 3:
                band = np.where(i < N_META, band, 0.0)
            out[t, g, rows, :BAND] = band
            out[t, g, rows, BAND:BAND + N_META] = -slopes[h] * dist_m
    return np.ascontiguousarray(LOG2E * out.transpose(0, 1, 3, 2)).astype(np.float32)


def _rms_scale(x):
    return lax.rsqrt(jnp.mean(x * x, axis=-1, keepdims=True) + EPS)


def _unit_rms(x):
    return x * _rms_scale(x)


def _rms(x, g):
    return _unit_rms(x) * g


def _swiglu_half(x, wg_s, wu_s, wd_s):
    r = _rms_scale(x)
    xb = x.astype(BF16)
    gate = jnp.dot(xb, wg_s[...], preferred_element_type=F32) * r
    up = jnp.dot(xb, wu_s[...], preferred_element_type=F32) * r
    act = (gate * jax.nn.sigmoid(gate) * up).astype(BF16)
    return x + 0.5 * jnp.dot(act, wd_s[...], preferred_element_type=F32)


def _cast_weights(s, f32_refs, gain_refs, bf16_scratch):
    for w_ref, g_ref, w_s in zip(f32_refs, gain_refs, bf16_scratch):
        rows = w_ref.shape[0]
        w = w_ref[...]
        if g_ref is not None:
            w = w * jnp.tile(g_ref[...], (1, w.shape[1] // g_ref.shape[1]))
        w_s[pl.ds(pl.multiple_of(s * rows, rows), rows), :] = w.astype(BF16)


def _pre_kernel(x_ref, meta_ref, g1_ref, wg_ref, wu_ref, wd_ref, gm_ref, win_ref, gqk_ref,
                ones_ref, h1_ref, q_ref, kv_ref, f_ref, wg_s, wu_s, wd_s, win_s, fs_s):
    s = pl.program_id(0)

    @pl.when(s < N_CAST)
    def _():
        _cast_weights(s, (wg_ref, wu_ref, wd_ref, win_ref), (g1_ref, g1_ref, None, gm_ref),
                      (wg_s, wu_s, wd_s, win_s))

    @pl.when(s >= N_CAST)
    def _():
        _pre_tile(s - N_CAST, x_ref, meta_ref, gqk_ref, ones_ref,
                  h1_ref, q_ref, kv_ref, f_ref, wg_s, wu_s, wd_s, win_s, fs_s)


def _pre_tile(i, x_ref, meta_ref, gqk_ref, ones_ref,
              h1_ref, q_ref, kv_ref, f_ref, wg_s, wu_s, wd_s, win_s, fs_s):
    tail = jnp.where(i == N_TILES - 1, meta_ref[...], x_ref[META_ROW0:, :])
    x = jnp.concatenate([x_ref[:META_ROW0, :], tail], axis=0)

    h1 = _swiglu_half(x, wg_s, wu_s, wd_s)
    h1_ref[...] = h1

    u = jnp.dot(h1.astype(BF16), win_s[...], preferred_element_type=F32) * _rms_scale(h1)

    qk = u[:, :QK_W]
    sq = (qk * qk).astype(BF16)
    ones = ones_ref[...]
    ss = jnp.concatenate([
        jnp.dot(sq[:, 0:256], ones, preferred_element_type=F32),
        jnp.dot(sq[:, 256:512], ones, preferred_element_type=F32),
        jnp.dot(sq[:, 512:640], ones[:KV_W, :KV_W], preferred_element_type=F32)], axis=1)
    qkn = qk * lax.rsqrt(ss * (1.0 / HEAD_DIM) + EPS) * gqk_ref[...]
    q_ref[...] = qkn[:, :ATTN_W].astype(BF16)
    kv_ref[...] = jnp.concatenate([qkn[:, ATTN_W:], u[:, QK_W:QK_W + KV_W]], axis=1).astype(BF16)

    for c in range(F_W // 128):
        lanes = slice(c * 128, (c + 1) * 128)
        fs_s[c] = u[:, QK_W + KV_W + c * 128:QK_W + KV_W + (c + 1) * 128]
        for b in range(N2):
            f_ref[b, :, lanes] = fs_s[c, pl.ds(b, A_PER_TILE, stride=N2), :]


def _attn_kernel(q_ref, kv_ref, bias_ref, sink_ref, o_ref):
    step = pl.program_id(0)
    kv_meta = kv_ref[SEQ:L, :]
    ones_col = (lax.broadcasted_iota(jnp.int32, (N_KEYS, 16), 1) == 0).astype(BF16)

    units = []
    for j in range(QBLK_PER_STEP):
        n = step * QBLK_PER_STEP + j
        btype = jnp.where(n == 0, 0, jnp.where(n == N_QBLK - 1, 2, jnp.where(n == N_QBLK, 3, 1)))
        start = jnp.where(n == N_QBLK, 0, jnp.clip((n - 1) * BQ, 0, SEQ - BAND))
        start = pl.multiple_of(start, BQ)
        kv_cat = jnp.concatenate([kv_ref[pl.ds(start, BAND), :], kv_meta], axis=0)
        q_ok = lax.broadcasted_iota(jnp.int32, (BQ, 1), 0) < L - n * BQ
        units.extend((j, g, btype, kv_cat, q_ok) for g in range(N_KV))

    def scores(unit):
        j, g, _, kv_cat, q_ok = unit
        q = jnp.where(q_ok, q_ref[j * BQ:(j + 1) * BQ, :], 0)
        qg = jnp.concatenate(
            [q[:, (Q_PER_KV * g + hl) * HEAD_DIM:(Q_PER_KV * g + hl + 1) * HEAD_DIM]
             for hl in range(Q_PER_KV)], axis=0)
        k = kv_cat[:, g * HEAD_DIM:(g + 1) * HEAD_DIM]
        return lax.dot_general(k, qg, (((1,), (1,)), ((), ())), preferred_element_type=F32)

    st_next = scores(units[0])
    for u, (j, g, btype, kv_cat, _) in enumerate(units):
        st = st_next + bias_ref[btype, g]
        sink = sink_ref[g]
        m = jnp.maximum(jnp.max(st, axis=0, keepdims=True), sink)
        p = jnp.exp2(st - m)
        if u + 1 < len(units):
            st_next = scores(units[u + 1])
            p = jnp.concatenate([jnp.maximum(p[:16], st_next[:16] * 0.0 - 1.0), p[16:]], axis=0)
        v1 = jnp.concatenate([kv_cat[:, KV_W + g * HEAD_DIM:KV_W + (g + 1) * HEAD_DIM], ones_col], axis=1)
        ot = lax.dot_general(v1, p.astype(BF16), (((0,), (0,)), ((), ())), preferred_element_type=F32)
        denom = ot[HEAD_DIM:HEAD_DIM + 1] + jnp.exp2(sink - m)
        ot = ot[:HEAD_DIM] / denom
        for hp in range(Q_PER_KV // 2):
            h0 = Q_PER_KV * g + 2 * hp
            two = jnp.concatenate([ot[:, 2 * hp * BQ:(2 * hp + 1) * BQ],
                                   ot[:, (2 * hp + 1) * BQ:(2 * hp + 2) * BQ]], axis=0)
            o_ref[j * BQ:(j + 1) * BQ, h0 * HEAD_DIM:(h0 + 2) * HEAD_DIM] = two.T.astype(BF16)


def _dft_a_kernel(f_ref, wa_ref, cdft_ref, g_ref, gs_ref):
    cdft = cdft_ref[...]
    for k in range(B_PER_STEP):
        h = jnp.dot(wa_ref[...], f_ref[k].astype(BF16), preferred_element_type=F32)
        hr, hi = h[:N1].astype(BF16), h[N1:].astype(BF16)
        for c in range(F_W // 128):
            cols = slice(c * 128, (c + 1) * 128)
            gri = jnp.dot(jnp.concatenate([hr[:, cols], hi[:, cols]], axis=1), cdft,
                          preferred_element_type=F32)
            gs_ref[:, cols] = gri[:, :128]
            gs_ref[:, F_W + c * 128:F_W + (c + 1) * 128] = gri[:, 128:]
        g_ref[:, k, :, :] = gs_ref[...].astype(BF16).reshape(N1 // 2, 2, 2 * F_W)


def _attn_dft_a_kernel(q_ref, kv_ref, bias_ref, sink_ref, f_ref, wa_ref, cdft_ref, o_ref, g_ref, gs_ref):
    _attn_kernel(q_ref, kv_ref, bias_ref, sink_ref, o_ref)
    _dft_a_kernel(f_ref, wa_ref, cdft_ref, g_ref, gs_ref)


def _dft_c_kernel(g_ref, ec_ref, es_ref, y_ref, gs_ref):
    gs_ref[:, 2 * N2:, :] = jnp.zeros((TCP, K2P - 2 * N2, 2 * F_W), BF16)
    for i in range(TCP):
        gs_ref[i, :2 * N2, :] = g_ref[i].reshape(2 * N2, 2 * F_W)
        g = gs_ref[i]
        y = (jnp.dot(ec_ref[i], g[:, :F_W], preferred_element_type=F32)
             + jnp.dot(es_ref[i], g[:, F_W:], preferred_element_type=F32))
        y_ref[:, i, :, :] = _unit_rms(y[:2 * N2]).astype(BF16).reshape(N2, 2, F_W)


def _post_kernel(h1_ref, a_ref, y_ref, gmix_ref, wout_ref, g2_ref, wg_ref, wu_ref, wd_ref,
                 gfin_ref, o_ref, wg_s, wu_s, wd_s, wout_s):
    s = pl.program_id(0)

    @pl.when(s < N_CAST)
    def _():
        _cast_weights(s, (wg_ref, wu_ref, wd_ref, wout_ref), (g2_ref, g2_ref, None, gmix_ref),
                      (wg_s, wu_s, wd_s, wout_s))

    @pl.when(s >= N_CAST)
    def _():
        y = y_ref[...].reshape(TM, F_W)
        a = a_ref[...]
        h2 = (h1_ref[...]
              + jnp.dot(a, wout_s[:ATTN_W, :], preferred_element_type=F32) * _rms_scale(a.astype(F32))
              + jnp.dot(y, wout_s[ATTN_W:, :], preferred_element_type=F32))
        h3 = _swiglu_half(h2, wg_s, wu_s, wd_s)
        o_ref[...] = _rms(h3, gfin_ref[...])


def _resident(shape):
    zeros = (0,) * len(shape)
    return pl.BlockSpec(shape, lambda *_: zeros, pipeline_mode=pl.Buffered(1))


def _params():
    return pltpu.CompilerParams(dimension_semantics=("arbitrary",), vmem_limit_bytes=VMEM_LIMIT)


def _row(g):
    return g.astype(F32).reshape(1, -1)


def kernel(x, meta_tokens, ffn1_norm, ffn1_w_gate, ffn1_w_up, ffn1_w_down, mix_norm, w_in, q_norm, k_norm, sink, attn_out_norm, fourier_out_norm, w_out, ffn2_norm, ffn2_w_gate, ffn2_w_up, ffn2_w_down, final_norm):
    assert x.shape == (1, SEQ, D_MODEL) and x.dtype == F32
    assert A_PER_TILE * N2 == TM and A_PER_TILE * N_TILES == N1
    x2 = x.reshape(SEQ, D_MODEL)
    gqk = jnp.concatenate([jnp.tile(q_norm.astype(F32), N_HEADS) * (HEAD_DIM ** -0.5 * LOG2E),
                           jnp.tile(k_norm.astype(F32), N_KV)]).reshape(1, QK_W)
    bf = lambda w: w.astype(BF16)

    tile_idx = lambda s: jnp.maximum(s - N_CAST, 0)
    row_tile = lambda w: pl.BlockSpec((TM, w), lambda s: (tile_idx(s), 0))
    row_sets = lambda w: pl.BlockSpec((N2, A_PER_TILE, w), lambda s: (0, tile_idx(s), 0))
    cast_rows = lambda r, c: pl.BlockSpec((r // N_CAST, c), lambda s: (jnp.minimum(s, N_CAST - 1), 0))
    gains = jnp.broadcast_to(
        jnp.stack([ffn1_norm, mix_norm, jnp.concatenate([attn_out_norm, fourier_out_norm]), ffn2_norm])
        .astype(F32)[:, :, None], (4, D_MODEL, 128))
    gain_rows = lambda k: pl.BlockSpec((None, D_MODEL // N_CAST, 128), lambda s: (k, jnp.minimum(s, N_CAST - 1), 0))
    bf16_copy = lambda r, c: pltpu.VMEM((r, c), BF16)
    ffn_in = [cast_rows(D_MODEL, D_FF), cast_rows(D_MODEL, D_FF), cast_rows(D_FF, D_MODEL)]
    ffn_scratch = [bf16_copy(D_MODEL, D_FF), bf16_copy(D_MODEL, D_FF), bf16_copy(D_FF, D_MODEL)]
    f32 = lambda w: w.astype(F32)

    h1, q, kv, f = pl.pallas_call(
        _pre_kernel,
        grid=(N_CAST + N_TILES,),
        in_specs=[row_tile(D_MODEL), _resident((N_META, D_MODEL)), gain_rows(0), *ffn_in,
                  gain_rows(1), cast_rows(D_MODEL, IN_W), _resident((1, QK_W)),
                  _resident((256, 256))],
        out_specs=[row_tile(D_MODEL), row_tile(ATTN_W), row_tile(2 * KV_W), row_sets(F_W)],
        out_shape=[jax.ShapeDtypeStruct((L, D_MODEL), F32), jax.ShapeDtypeStruct((L, ATTN_W), BF16),
                   jax.ShapeDtypeStruct((L, 2 * KV_W), BF16), jax.ShapeDtypeStruct((N2, N1, F_W), F32)],
        scratch_shapes=ffn_scratch + [bf16_copy(D_MODEL, IN_W), pltpu.VMEM((F_W // 128, TM, 128), F32)],
        compiler_params=_params(),
        name="pre",
    )(x2, meta_tokens.astype(F32), gains, f32(ffn1_w_gate), f32(ffn1_w_up), f32(ffn1_w_down), gains,
      f32(w_in), gqk, _ones_blockdiag())

    sink_rows = (jnp.repeat(sink.astype(F32).reshape(N_KV, Q_PER_KV), BQ, axis=1) * LOG2E)[:, None, :]

    q_rows = pl.BlockSpec((QBLK_PER_STEP * BQ, ATTN_W), lambda s: (s, 0))
    n_steps = pl.cdiv(N_QBLK + 1, QBLK_PER_STEP)
    assert n_steps == pl.cdiv(N2, B_PER_STEP)
    a_out, g = pl.pallas_call(
        _attn_dft_a_kernel,
        grid=(n_steps,),
        in_specs=[q_rows, _resident((L, 2 * KV_W)), _resident((4, N_KV, N_KEYS, Q_PER_KV * BQ)),
                  _resident((N_KV, 1, Q_PER_KV * BQ)),
                  pl.BlockSpec((B_PER_STEP, N1, F_W), lambda s: (s, 0, 0)),
                  _resident((2 * N1, N1)), _resident((256, 256))],
        out_specs=[q_rows, pl.BlockSpec((N1 // 2, B_PER_STEP, 2, 2 * F_W), lambda s: (0, s, 0, 0))],
        out_shape=[jax.ShapeDtypeStruct((L, ATTN_W), BF16),
                   jax.ShapeDtypeStruct((N1 // 2, N2, 2, 2 * F_W), BF16)],
        scratch_shapes=[pltpu.VMEM((N1, 2 * F_W), F32)],
        compiler_params=_params(),
        name="attn_dft_a",
    )(q, kv, jnp.asarray(_attn_bias_const()), sink_rows,
      f, bf(jnp.asarray(_stage_a_dft())), bf(jnp.asarray(_channel_dft())))

    ec, es = _stage_c_dft()
    y = pl.pallas_call(
        _dft_c_kernel,
        grid=(N1 // (2 * TCP),),
        in_specs=[pl.BlockSpec((TCP, N2, 2, 2 * F_W), lambda i: (i, 0, 0, 0)),
                  pl.BlockSpec((TCP, K2P, K2P), lambda i: (i, 0, 0)),
                  pl.BlockSpec((TCP, K2P, K2P), lambda i: (i, 0, 0))],
        out_specs=pl.BlockSpec((N2, TCP, 2, F_W), lambda i: (0, i, 0, 0)),
        out_shape=jax.ShapeDtypeStruct((N2, N1 // 2, 2, F_W), BF16),
        scratch_shapes=[pltpu.VMEM((TCP, K2P, 2 * F_W), BF16)],
        compiler_params=_params(),
        name="dft_c",
    )(g, bf(jnp.asarray(ec)), bf(jnp.asarray(es)))

    out = pl.pallas_call(
        _post_kernel,
        grid=(N_CAST + N_TILES,),
        in_specs=[row_tile(D_MODEL), row_tile(ATTN_W),
                  pl.BlockSpec((TM // 2, 2, F_W), lambda s: (tile_idx(s), 0, 0)), gain_rows(2),
                  cast_rows(D_MODEL, D_MODEL), gain_rows(3), *ffn_in, _resident((1, D_MODEL))],
        out_specs=row_tile(D_MODEL),
        out_shape=jax.ShapeDtypeStruct((SEQ, D_MODEL), F32),
        scratch_shapes=ffn_scratch + [bf16_copy(D_MODEL, D_MODEL)],
        compiler_params=_params(),
        name="post",
    )(h1, a_out, y.reshape(L // 2, 2, F_W), gains, f32(w_out), gains,
      f32(ffn2_w_gate), f32(ffn2_w_up), f32(ffn2_w_down), _row(final_norm))
    return out.reshape(1, SEQ, D_MODEL)
```

```python
import numpy as np
import jax
import jax.numpy as jnp
from jax import lax
from jax.experimental import pallas as pl
from jax.experimental.pallas import tpu as pltpu

F32 = jnp.float32
BF16 = jnp.bfloat16

D_MODEL = 1024
SEQ = 16384
N_META = 16
L = SEQ + N_META
HEAD_DIM = 64
N_HEADS = 8
N_KV = 2
Q_PER_KV = N_HEADS // N_KV
ATTN_W = N_HEADS * HEAD_DIM
KV_W = N_KV * HEAD_DIM
F_W = 512
F_GROUP = 64
IN_W = ATTN_W + 2 * KV_W + F_W
QK_W = ATTN_W + KV_W
WINDOW = 128
D_FF = 2816
N_CAST = 8
EPS = 1e-6
NEG = -1e30
LOG2E = 1.4426950408889634

TM = 656
N_TILES = L // TM
META_ROW0 = SEQ - (N_TILES - 1) * TM

BQ = 128
N_QBLK = SEQ // BQ
Q_SPLIT_TILES = 16
Q1_ROWS = Q_SPLIT_TILES * TM
N_QBLK_A = Q1_ROWS // BQ
QBLK_PER_STEP_A = 12
QBLK_PER_STEP_C = 6
BAND = BQ + 2 * WINDOW
N_KEYS = BAND + N_META

N1 = 200
N2 = 82
A_PER_TILE = TM // N2
K2P = 176
B_PER_STEP = 12
TCP = 13

VMEM_LIMIT = 56 * 1024 * 1024


def _ones_blockdiag():
    m = np.kron(np.eye(4), np.ones((HEAD_DIM, HEAD_DIM)))
    return jnp.asarray(m, BF16)


def _channel_dft():
    c = np.arange(F_GROUP)
    ang = 2.0 * np.pi * np.outer(c, c) / F_GROUP
    cos = np.kron(np.eye(2), np.cos(ang) / 8.0)
    sin = np.kron(np.eye(2), np.sin(ang) / 8.0)
    return np.block([[cos, -sin], [sin, cos]]).astype(np.float32)


def _stage_a_dft():
    a = np.arange(N1, dtype=np.int64)
    m = (N2 * np.outer(a + N_META, a)) % L
    ang = 2.0 * np.pi * m / L
    return np.concatenate([np.cos(ang), -np.sin(ang)], axis=0).astype(np.float32)


def _stage_c_dft():
    c = np.arange(N1, dtype=np.int64)[:, None, None]
    d = np.arange(N2, dtype=np.int64)[None, :, None]
    b = np.arange(N2, dtype=np.int64)[None, None, :]
    m = ((c + N1 * d + N_META) * (b + N_META)) % L
    ang = 2.0 * np.pi * m / L
    ec = np.zeros((N1 // 2, K2P, K2P), np.float32)
    es = np.zeros((N1 // 2, K2P, K2P), np.float32)
    for e in range(2):
        ec[:, e:2 * N2:2, e:2 * N2:2] = (np.cos(ang) / np.sqrt(L))[e::2]
        es[:, e:2 * N2:2, e:2 * N2:2] = (np.sin(ang) / np.sqrt(L))[e::2]
    return ec, es


def _attn_bias_const():
    slopes = 2.0 ** -(np.arange(N_HEADS) + 1.0)
    i = np.arange(BQ)[:, None]
    j = np.arange(BAND)[None, :]
    m = np.arange(N_META)[None, :]
    out = np.full((4, N_KV, Q_PER_KV * BQ, N_KEYS), NEG, np.float64)
    for t in range(4):
        if t < 3:
            dist = np.abs(t * WINDOW + i - j)
            band_ok = dist <= WINDOW
            dist_m = np.minimum(t * WINDOW + N_META + i - m, WINDOW) if t == 0 else np.full((BQ, N_META), WINDOW)
        else:
            dist = N_META + j - i
            band_ok = (dist <= WINDOW) & (i < N_META)
            dist_m = np.abs(i - m)
        for h in range(N_HEADS):
            g, hl = divmod(h, Q_PER_KV)
            rows = slice(hl * BQ, (hl + 1) * BQ)
            band = np.where(band_ok, -slopes[h] * dist, NEG)
            if t == 3:
                band = np.where(i < N_META, band, 0.0)
            out[t, g, rows, :BAND] = band
            out[t, g, rows, BAND:BAND + N_META] = -slopes[h] * dist_m
    return np.ascontiguousarray(LOG2E * out.transpose(0, 1, 3, 2)).astype(np.float32)


def _rms_scale(x):
    return lax.rsqrt(jnp.mean(x * x, axis=-1, keepdims=True) + EPS)


def _unit_rms(x):
    return x * _rms_scale(x)


def _rms(x, g):
    return _unit_rms(x) * g


def _swiglu_half(x, wg_s, wu_s, wd_s):
    r = _rms_scale(x)
    xb = x.astype(BF16)
    gate = jnp.dot(xb, wg_s[...], preferred_element_type=F32) * r
    up = jnp.dot(xb, wu_s[...], preferred_element_type=F32) * r
    act = (gate * jax.nn.sigmoid(gate) * up).astype(BF16)
    return x + 0.5 * jnp.dot(act, wd_s[...], preferred_element_type=F32)


def _cast_weights(s, f32_refs, gain_refs, bf16_scratch):
    for w_ref, g_ref, w_s in zip(f32_refs, gain_refs, bf16_scratch):
        rows = w_ref.shape[0]
        w = w_ref[...]
        if g_ref is not None:
            w = w * jnp.tile(g_ref[...], (1, w.shape[1] // g_ref.shape[1]))
        w_s[pl.ds(pl.multiple_of(s * rows, rows), rows), :] = w.astype(BF16)


def _pre_kernel(x_ref, meta_ref, g1_ref, wg_ref, wu_ref, wd_ref, gm_ref, win_ref, gqk_ref,
                ones_ref, h1_ref, q1_ref, q2_ref, kv_ref, f_ref, wg_s, wu_s, wd_s, win_s, fs_s):
    s = pl.program_id(0)

    @pl.when(s < N_CAST)
    def _():
        _cast_weights(s, (wg_ref, wu_ref, wd_ref, win_ref), (g1_ref, g1_ref, None, gm_ref),
                      (wg_s, wu_s, wd_s, win_s))

    @pl.when(s >= N_CAST)
    def _():
        _pre_tile(s - N_CAST, x_ref, meta_ref, gqk_ref, ones_ref,
                  h1_ref, q1_ref, q2_ref, kv_ref, f_ref, wg_s, wu_s, wd_s, win_s, fs_s)


def _pre_tile(i, x_ref, meta_ref, gqk_ref, ones_ref,
              h1_ref, q1_ref, q2_ref, kv_ref, f_ref, wg_s, wu_s, wd_s, win_s, fs_s):
    tail = jnp.where(i == N_TILES - 1, meta_ref[...], x_ref[META_ROW0:, :])
    x = jnp.concatenate([x_ref[:META_ROW0, :], tail], axis=0)

    h1 = _swiglu_half(x, wg_s, wu_s, wd_s)
    h1_ref[...] = h1

    u = jnp.dot(h1.astype(BF16), win_s[...], preferred_element_type=F32) * _rms_scale(h1)

    qk = u[:, :QK_W]
    sq = (qk * qk).astype(BF16)
    ones = ones_ref[...]
    ss = jnp.concatenate([
        jnp.dot(sq[:, 0:256], ones, preferred_element_type=F32),
        jnp.dot(sq[:, 256:512], ones, preferred_element_type=F32),
        jnp.dot(sq[:, 512:640], ones[:KV_W, :KV_W], preferred_element_type=F32)], axis=1)
    qkn = qk * lax.rsqrt(ss * (1.0 / HEAD_DIM) + EPS) * gqk_ref[...]
    q1_ref[...] = q2_ref[...] = qkn[:, :ATTN_W].astype(BF16)
    kv_ref[...] = jnp.concatenate([qkn[:, ATTN_W:], u[:, QK_W:QK_W + KV_W]], axis=1).astype(BF16)

    for c in range(F_W // 128):
        lanes = slice(c * 128, (c + 1) * 128)
        fs_s[c] = u[:, QK_W + KV_W + c * 128:QK_W + KV_W + (c + 1) * 128]
        for b in range(N2):
            f_ref[b, :, lanes] = fs_s[c, pl.ds(b, A_PER_TILE, stride=N2), :]


def _attn_kernel(q_ref, kv_ref, bias_ref, sink_ref, o_ref, *, first_blk, blk_per_step, q_end):
    step = pl.program_id(0)
    kv_meta = kv_ref[SEQ:L, :]
    ones_col = (lax.broadcasted_iota(jnp.int32, (N_KEYS, 16), 1) == 0).astype(BF16)

    units = []
    for j in range(blk_per_step):
        n = first_blk + step * blk_per_step + j
        btype = jnp.where(n == 0, 0, jnp.where(n == N_QBLK - 1, 2, jnp.where(n == N_QBLK, 3, 1)))
        start = jnp.where(n == N_QBLK, 0, jnp.clip((n - 1) * BQ, 0, SEQ - BAND))
        start = pl.multiple_of(start, BQ)
        kv_cat = jnp.concatenate([kv_ref[pl.ds(start, BAND), :], kv_meta], axis=0)
        q_ok = lax.broadcasted_iota(jnp.int32, (BQ, 1), 0) < q_end - n * BQ
        units.extend((j, g, btype, kv_cat, q_ok) for g in range(N_KV))

    def scores(unit):
        j, g, _, kv_cat, q_ok = unit
        q = jnp.where(q_ok, q_ref[j * BQ:(j + 1) * BQ, :], 0)
        qg = jnp.concatenate(
            [q[:, (Q_PER_KV * g + hl) * HEAD_DIM:(Q_PER_KV * g + hl + 1) * HEAD_DIM]
             for hl in range(Q_PER_KV)], axis=0)
        k = kv_cat[:, g * HEAD_DIM:(g + 1) * HEAD_DIM]
        return lax.dot_general(k, qg, (((1,), (1,)), ((), ())), preferred_element_type=F32)

    st_next = scores(units[0])
    for u, (j, g, btype, kv_cat, _) in enumerate(units):
        st = st_next + bias_ref[btype, g]
        sink = sink_ref[g]
        m = jnp.maximum(jnp.max(st, axis=0, keepdims=True), sink)
        p = jnp.exp2(st - m)
        if u + 1 < len(units):
            st_next = scores(units[u + 1])
            p = jnp.concatenate([jnp.maximum(p[:16], st_next[:16] * 0.0 - 1.0), p[16:]], axis=0)
        v1 = jnp.concatenate([kv_cat[:, KV_W + g * HEAD_DIM:KV_W + (g + 1) * HEAD_DIM], ones_col], axis=1)
        ot = lax.dot_general(v1, p.astype(BF16), (((0,), (0,)), ((), ())), preferred_element_type=F32)
        denom = ot[HEAD_DIM:HEAD_DIM + 1] + jnp.exp2(sink - m)
        ot = ot[:HEAD_DIM] / denom
        for hp in range(Q_PER_KV // 2):
            h0 = Q_PER_KV * g + 2 * hp
            two = jnp.concatenate([ot[:, 2 * hp * BQ:(2 * hp + 1) * BQ],
                                   ot[:, (2 * hp + 1) * BQ:(2 * hp + 2) * BQ]], axis=0)
            o_ref[j * BQ:(j + 1) * BQ, h0 * HEAD_DIM:(h0 + 2) * HEAD_DIM] = two.T.astype(BF16)


def _dft_a_kernel(f_ref, wa_ref, cdft_ref, g_ref, gs_ref):
    cdft = cdft_ref[...]
    for k in range(B_PER_STEP):
        h = jnp.dot(wa_ref[...], f_ref[k].astype(BF16), preferred_element_type=F32)
        hr, hi = h[:N1].astype(BF16), h[N1:].astype(BF16)
        for c in range(F_W // 128):
            cols = slice(c * 128, (c + 1) * 128)
            gri = jnp.dot(jnp.concatenate([hr[:, cols], hi[:, cols]], axis=1), cdft,
                          preferred_element_type=F32)
            gs_ref[:, cols] = gri[:, :128]
            gs_ref[:, F_W + c * 128:F_W + (c + 1) * 128] = gri[:, 128:]
        g_ref[:, k, :, :] = gs_ref[...].astype(BF16).reshape(N1 // 2, 2, 2 * F_W)


def _attn_dft_a_kernel(q_ref, kv_ref, bias_ref, sink_ref, f_ref, wa_ref, cdft_ref, o_ref, g_ref, gs_ref):
    _attn_kernel(q_ref, kv_ref, bias_ref, sink_ref, o_ref,
                 first_blk=0, blk_per_step=QBLK_PER_STEP_A, q_end=Q1_ROWS)
    _dft_a_kernel(f_ref, wa_ref, cdft_ref, g_ref, gs_ref)


def _attn_dft_c_kernel(q_ref, kv_ref, bias_ref, sink_ref, g_ref, ec_ref, es_ref, o_ref, y_ref, gs_ref):
    _attn_kernel(q_ref, kv_ref, bias_ref, sink_ref, o_ref,
                 first_blk=N_QBLK_A, blk_per_step=QBLK_PER_STEP_C, q_end=L)
    _dft_c_kernel(g_ref, ec_ref, es_ref, y_ref, gs_ref)


def _dft_c_kernel(g_ref, ec_ref, es_ref, y_ref, gs_ref):
    gs_ref[:, 2 * N2:, :] = jnp.zeros((TCP, K2P - 2 * N2, 2 * F_W), BF16)
    for i in range(TCP):
        gs_ref[i, :2 * N2, :] = g_ref[i].reshape(2 * N2, 2 * F_W)
        g = gs_ref[i]
        y = (jnp.dot(ec_ref[i], g[:, :F_W], preferred_element_type=F32)
             + jnp.dot(es_ref[i], g[:, F_W:], preferred_element_type=F32))
        y_ref[:, i, :, :] = _unit_rms(y[:2 * N2]).astype(BF16).reshape(N2, 2, F_W)


def _post_kernel(h1_ref, a1_ref, a2_ref, y_ref, gmix_ref, wout_ref, g2_ref, wg_ref, wu_ref, wd_ref,
                 gfin_ref, o_ref, wg_s, wu_s, wd_s, wout_s):
    s = pl.program_id(0)

    @pl.when(s < N_CAST)
    def _():
        _cast_weights(s, (wg_ref, wu_ref, wd_ref, wout_ref), (g2_ref, g2_ref, None, gmix_ref),
                      (wg_s, wu_s, wd_s, wout_s))

    @pl.when(s >= N_CAST)
    def _():
        y = y_ref[...].reshape(TM, F_W)
        a = jnp.where(s - N_CAST < Q_SPLIT_TILES, a1_ref[...], a2_ref[...])
        h2 = (h1_ref[...]
              + jnp.dot(a, wout_s[:ATTN_W, :], preferred_element_type=F32) * _rms_scale(a.astype(F32))
              + jnp.dot(y, wout_s[ATTN_W:, :], preferred_element_type=F32))
        h3 = _swiglu_half(h2, wg_s, wu_s, wd_s)
        o_ref[...] = _rms(h3, gfin_ref[...])


def _resident(shape):
    zeros = (0,) * len(shape)
    return pl.BlockSpec(shape, lambda *_: zeros, pipeline_mode=pl.Buffered(1))


def _params():
    return pltpu.CompilerParams(dimension_semantics=("arbitrary",), vmem_limit_bytes=VMEM_LIMIT)


def _row(g):
    return g.astype(F32).reshape(1, -1)


def kernel(x, meta_tokens, ffn1_norm, ffn1_w_gate, ffn1_w_up, ffn1_w_down, mix_norm, w_in, q_norm, k_norm, sink, attn_out_norm, fourier_out_norm, w_out, ffn2_norm, ffn2_w_gate, ffn2_w_up, ffn2_w_down, final_norm):
    assert x.shape == (1, SEQ, D_MODEL) and x.dtype == F32
    assert A_PER_TILE * N2 == TM and A_PER_TILE * N_TILES == N1
    x2 = x.reshape(SEQ, D_MODEL)
    gqk = jnp.concatenate([jnp.tile(q_norm.astype(F32), N_HEADS) * (HEAD_DIM ** -0.5 * LOG2E),
                           jnp.tile(k_norm.astype(F32), N_KV)]).reshape(1, QK_W)
    bf = lambda w: w.astype(BF16)

    tile_idx = lambda s: jnp.maximum(s - N_CAST, 0)
    row_tile = lambda w: pl.BlockSpec((TM, w), lambda s: (tile_idx(s), 0))
    row_sets = lambda w: pl.BlockSpec((N2, A_PER_TILE, w), lambda s: (0, tile_idx(s), 0))
    cast_rows = lambda r, c: pl.BlockSpec((r // N_CAST, c), lambda s: (jnp.minimum(s, N_CAST - 1), 0))
    gains = jnp.broadcast_to(
        jnp.stack([ffn1_norm, mix_norm, jnp.concatenate([attn_out_norm, fourier_out_norm]), ffn2_norm])
        .astype(F32)[:, :, None], (4, D_MODEL, 128))
    gain_rows = lambda k: pl.BlockSpec((None, D_MODEL // N_CAST, 128), lambda s: (k, jnp.minimum(s, N_CAST - 1), 0))
    bf16_copy = lambda r, c: pltpu.VMEM((r, c), BF16)
    ffn_in = [cast_rows(D_MODEL, D_FF), cast_rows(D_MODEL, D_FF), cast_rows(D_FF, D_MODEL)]
    ffn_scratch = [bf16_copy(D_MODEL, D_FF), bf16_copy(D_MODEL, D_FF), bf16_copy(D_FF, D_MODEL)]
    f32 = lambda w: w.astype(F32)

    n_tiles_2 = N_TILES - Q_SPLIT_TILES
    part_1 = lambda s: (jnp.minimum(tile_idx(s), Q_SPLIT_TILES), 0)
    part_2 = lambda s: (jnp.where(tile_idx(s) < Q_SPLIT_TILES, n_tiles_2, tile_idx(s) - Q_SPLIT_TILES), 0)
    h1, q1, q2, kv, f = pl.pallas_call(
        _pre_kernel,
        grid=(N_CAST + N_TILES,),
        in_specs=[row_tile(D_MODEL), _resident((N_META, D_MODEL)), gain_rows(0), *ffn_in,
                  gain_rows(1), cast_rows(D_MODEL, IN_W), _resident((1, QK_W)),
                  _resident((256, 256))],
        out_specs=[row_tile(D_MODEL), pl.BlockSpec((TM, ATTN_W), part_1), pl.BlockSpec((TM, ATTN_W), part_2),
                   row_tile(2 * KV_W), row_sets(F_W)],
        out_shape=[jax.ShapeDtypeStruct((L, D_MODEL), F32),
                   jax.ShapeDtypeStruct(((Q_SPLIT_TILES + 1) * TM, ATTN_W), BF16),
                   jax.ShapeDtypeStruct(((n_tiles_2 + 1) * TM, ATTN_W), BF16),
                   jax.ShapeDtypeStruct((L, 2 * KV_W), BF16), jax.ShapeDtypeStruct((N2, N1, F_W), F32)],
        scratch_shapes=ffn_scratch + [bf16_copy(D_MODEL, IN_W), pltpu.VMEM((F_W // 128, TM, 128), F32)],
        compiler_params=_params(),
        name="pre",
    )(x2, meta_tokens.astype(F32), gains, f32(ffn1_w_gate), f32(ffn1_w_up), f32(ffn1_w_down), gains,
      f32(w_in), gqk, _ones_blockdiag())

    sink_rows = (jnp.repeat(sink.astype(F32).reshape(N_KV, Q_PER_KV), BQ, axis=1) * LOG2E)[:, None, :]

    attn_in = [_resident((L, 2 * KV_W)), _resident((4, N_KV, N_KEYS, Q_PER_KV * BQ)),
               _resident((N_KV, 1, Q_PER_KV * BQ))]
    attn_args = (kv, jnp.asarray(_attn_bias_const()), sink_rows)
    q_rows = lambda blocks: pl.BlockSpec((blocks * BQ, ATTN_W), lambda s: (s, 0))

    steps_a = pl.cdiv(N_QBLK_A, QBLK_PER_STEP_A)
    assert N_QBLK_A * BQ == Q1_ROWS and steps_a == pl.cdiv(N2, B_PER_STEP)
    a1, g = pl.pallas_call(
        _attn_dft_a_kernel,
        grid=(steps_a,),
        in_specs=[q_rows(QBLK_PER_STEP_A), *attn_in,
                  pl.BlockSpec((B_PER_STEP, N1, F_W), lambda s: (s, 0, 0)),
                  _resident((2 * N1, N1)), _resident((256, 256))],
        out_specs=[q_rows(QBLK_PER_STEP_A),
                   pl.BlockSpec((N1 // 2, B_PER_STEP, 2, 2 * F_W), lambda s: (0, s, 0, 0))],
        out_shape=[jax.ShapeDtypeStruct((Q1_ROWS, ATTN_W), BF16),
                   jax.ShapeDtypeStruct((N1 // 2, N2, 2, 2 * F_W), BF16)],
        scratch_shapes=[pltpu.VMEM((N1, 2 * F_W), F32)],
        compiler_params=_params(),
        name="attn_dft_a",
    )(q1, *attn_args, f, bf(jnp.asarray(_stage_a_dft())), bf(jnp.asarray(_channel_dft())))

    ec, es = _stage_c_dft()
    steps_c = pl.cdiv(N_QBLK + 1 - N_QBLK_A, QBLK_PER_STEP_C)
    assert steps_c == pl.cdiv(N1 // 2, TCP)
    a2, y = pl.pallas_call(
        _attn_dft_c_kernel,
        grid=(steps_c,),
        in_specs=[q_rows(QBLK_PER_STEP_C), *attn_in,
                  pl.BlockSpec((TCP, N2, 2, 2 * F_W), lambda i: (i, 0, 0, 0)),
                  pl.BlockSpec((TCP, K2P, K2P), lambda i: (i, 0, 0)),
                  pl.BlockSpec((TCP, K2P, K2P), lambda i: (i, 0, 0))],
        out_specs=[q_rows(QBLK_PER_STEP_C), pl.BlockSpec((N2, TCP, 2, F_W), lambda i: (0, i, 0, 0))],
        out_shape=[jax.ShapeDtypeStruct((L - Q1_ROWS, ATTN_W), BF16),
                   jax.ShapeDtypeStruct((N2, N1 // 2, 2, F_W), BF16)],
        scratch_shapes=[pltpu.VMEM((TCP, K2P, 2 * F_W), BF16)],
        compiler_params=_params(),
        name="attn_dft_c",
    )(q2, *attn_args, g, bf(jnp.asarray(ec)), bf(jnp.asarray(es)))

    out = pl.pallas_call(
        _post_kernel,
        grid=(N_CAST + N_TILES,),
        in_specs=[row_tile(D_MODEL),
                  pl.BlockSpec((TM, ATTN_W), lambda s: (jnp.minimum(tile_idx(s), Q_SPLIT_TILES - 1), 0)),
                  pl.BlockSpec((TM, ATTN_W), lambda s: (jnp.maximum(tile_idx(s) - Q_SPLIT_TILES, 0), 0)),
                  pl.BlockSpec((TM // 2, 2, F_W), lambda s: (tile_idx(s), 0, 0)), gain_rows(2),
                  cast_rows(D_MODEL, D_MODEL), gain_rows(3), *ffn_in, _resident((1, D_MODEL))],
        out_specs=row_tile(D_MODEL),
        out_shape=jax.ShapeDtypeStruct((SEQ, D_MODEL), F32),
        scratch_shapes=ffn_scratch + [bf16_copy(D_MODEL, D_MODEL)],
        compiler_params=_params(),
        name="post",
    )(h1, a1, a2, y.reshape(L // 2, 2, F_W), gains, f32(w_out), gains,
      f32(ffn2_w_gate), f32(ffn2_w_up), f32(ffn2_w_down), _row(final_norm))
    return out.reshape(1, SEQ, D_MODEL)
```

```python
import numpy as np
import jax
import jax.numpy as jnp
from jax import lax
from jax.experimental import pallas as pl
from jax.experimental.pallas import tpu as pltpu

F32 = jnp.float32
BF16 = jnp.bfloat16

D_MODEL = 1024
SEQ = 16384
N_META = 16
L = SEQ + N_META
HEAD_DIM = 64
N_HEADS = 8
N_KV = 2
Q_PER_KV = N_HEADS // N_KV
ATTN_W = N_HEADS * HEAD_DIM
KV_W = N_KV * HEAD_DIM
F_W = 512
F_GROUP = 64
IN_W = ATTN_W + 2 * KV_W + F_W
QK_W = ATTN_W + KV_W
WINDOW = 128
D_FF = 2816
N_CAST = 8
EPS = 1e-6
NEG = -1e30
LOG2E = 1.4426950408889634

TM = 656
N_TILES = L // TM
META_ROW0 = SEQ - (N_TILES - 1) * TM

BQ = 128
N_QBLK = SEQ // BQ
QBLK_PER_STEP = 12
BAND = BQ + 2 * WINDOW
N_KEYS = BAND + N_META

N1 = 200
N2 = 82
A_PER_TILE = TM // N2
K2P = 176
B_PER_STEP = 8
TCP = 25

VMEM_LIMIT = 56 * 1024 * 1024


def _ones_blockdiag():
    m = np.kron(np.eye(4), np.ones((HEAD_DIM, HEAD_DIM)))
    return jnp.asarray(m, BF16)


def _channel_dft():
    c = np.arange(F_GROUP)
    ang = 2.0 * np.pi * np.outer(c, c) / F_GROUP
    cos = np.kron(np.eye(2), np.cos(ang) / 8.0)
    sin = np.kron(np.eye(2), np.sin(ang) / 8.0)
    return np.block([[cos, -sin], [sin, cos]]).astype(np.float32)


def _stage_a_dft():
    a = np.arange(N1, dtype=np.int64)
    m = (N2 * np.outer(a + N_META, a)) % L
    ang = 2.0 * np.pi * m / L
    return np.concatenate([np.cos(ang), -np.sin(ang)], axis=0).astype(np.float32)


def _stage_c_dft():
    c = np.arange(N1, dtype=np.int64)[:, None, None]
    d = np.arange(N2, dtype=np.int64)[None, :, None]
    b = np.arange(N2, dtype=np.int64)[None, None, :]
    m = ((c + N1 * d + N_META) * (b + N_META)) % L
    ang = 2.0 * np.pi * m / L
    ec = np.zeros((N1 // 2, K2P, K2P), np.float32)
    es = np.zeros((N1 // 2, K2P, K2P), np.float32)
    for e in range(2):
        ec[:, e:2 * N2:2, e:2 * N2:2] = (np.cos(ang) / np.sqrt(L))[e::2]
        es[:, e:2 * N2:2, e:2 * N2:2] = (np.sin(ang) / np.sqrt(L))[e::2]
    return ec, es


def _attn_bias_const():
    slopes = 2.0 ** -(np.arange(N_HEADS) + 1.0)
    i = np.arange(BQ)[:, None]
    j = np.arange(BAND)[None, :]
    m = np.arange(N_META)[None, :]
    out = np.full((4, N_KV, Q_PER_KV * BQ, N_KEYS), NEG, np.float64)
    for t in range(4):
        if t < 3:
            dist = np.abs(t * WINDOW + i - j)
            band_ok = dist <= WINDOW
            dist_m = np.minimum(t * WINDOW + N_META + i - m, WINDOW) if t == 0 else np.full((BQ, N_META), WINDOW)
        else:
            dist = N_META + j - i
            band_ok = (dist <= WINDOW) & (i < N_META)
            dist_m = np.abs(i - m)
        for h in range(N_HEADS):
            g, hl = divmod(h, Q_PER_KV)
            rows = slice(hl * BQ, (hl + 1) * BQ)
            band = np.where(band_ok, -slopes[h] * dist, NEG)
            if t == 3:
                band = np.where(i < N_META, band, 0.0)
            out[t, g, rows, :BAND] = band
            out[t, g, rows, BAND:BAND + N_META] = -slopes[h] * dist_m
    return np.ascontiguousarray(LOG2E * out.transpose(0, 1, 3, 2)).astype(np.float32)


def _rms_scale(x):
    return lax.rsqrt(jnp.mean(x * x, axis=-1, keepdims=True) + EPS)


def _unit_rms(x):
    return x * _rms_scale(x)


def _rms(x, g):
    return _unit_rms(x) * g


def _swiglu_half(x, wg_s, wu_s, wd_s):
    r = _rms_scale(x)
    xb = x.astype(BF16)
    gate = jnp.dot(xb, wg_s[...], preferred_element_type=F32) * r
    up = jnp.dot(xb, wu_s[...], preferred_element_type=F32) * r
    act = (gate * jax.nn.sigmoid(gate) * up).astype(BF16)
    return x + 0.5 * jnp.dot(act, wd_s[...], preferred_element_type=F32)


def _cast_weights(s, f32_refs, gain_refs, bf16_scratch):
    for w_ref, g_ref, w_s in zip(f32_refs, gain_refs, bf16_scratch):
        rows = w_ref.shape[0]
        w_s[pl.ds(pl.multiple_of(s * rows, rows), rows), :] = _fold_gain_bf16(w_ref, g_ref)


def _fold_gain_bf16(w_ref, g_ref):
    w = w_ref[...]
    if g_ref is not None:
        w = w * jnp.tile(g_ref[...], (1, w.shape[1] // g_ref.shape[1]))
    return w.astype(BF16)


def _pre_kernel(x_ref, meta_ref, g1_ref, wg_ref, wu_ref, wd_ref, gm_ref, win_ref, gqk_ref,
                ones_ref, h1_ref, q_ref, kv_ref, f_ref, wg_s, wu_s, wd_s, win_s, fs_s):
    s = pl.program_id(0)

    @pl.when(s < N_CAST)
    def _():
        _cast_weights(s, (wg_ref, wu_ref, wd_ref, win_ref), (g1_ref, g1_ref, None, gm_ref),
                      (wg_s, wu_s, wd_s, win_s))

    @pl.when(s >= N_CAST)
    def _():
        _pre_tile(s - N_CAST, x_ref, meta_ref, gqk_ref, ones_ref,
                  h1_ref, q_ref, kv_ref, f_ref, wg_s, wu_s, wd_s, win_s, fs_s)


def _pre_tile(i, x_ref, meta_ref, gqk_ref, ones_ref,
              h1_ref, q_ref, kv_ref, f_ref, wg_s, wu_s, wd_s, win_s, fs_s):
    tail = jnp.where(i == N_TILES - 1, meta_ref[...], x_ref[META_ROW0:, :])
    x = jnp.concatenate([x_ref[:META_ROW0, :], tail], axis=0)

    h1 = _swiglu_half(x, wg_s, wu_s, wd_s)
    h1_ref[...] = h1

    u = jnp.dot(h1.astype(BF16), win_s[...], preferred_element_type=F32) * _rms_scale(h1)

    qk = u[:, :QK_W]
    sq = (qk * qk).astype(BF16)
    ones = ones_ref[...]
    ss = jnp.concatenate([
        jnp.dot(sq[:, 0:256], ones, preferred_element_type=F32),
        jnp.dot(sq[:, 256:512], ones, preferred_element_type=F32),
        jnp.dot(sq[:, 512:640], ones[:KV_W, :KV_W], preferred_element_type=F32)], axis=1)
    qkn = qk * lax.rsqrt(ss * (1.0 / HEAD_DIM) + EPS) * gqk_ref[...]
    q_ref[...] = qkn[:, :ATTN_W].astype(BF16)
    kv_ref[...] = jnp.concatenate([qkn[:, ATTN_W:], u[:, QK_W:QK_W + KV_W]], axis=1).astype(BF16)

    for c in range(F_W // 128):
        lanes = slice(c * 128, (c + 1) * 128)
        fs_s[c] = u[:, QK_W + KV_W + c * 128:QK_W + KV_W + (c + 1) * 128]
        for b in range(N2):
            f_ref[b, :, lanes] = fs_s[c, pl.ds(b, A_PER_TILE, stride=N2), :]


def _attn_kernel(q_ref, kv_ref, bias_ref, sink_ref, o_ref):
    step = pl.program_id(0)
    kv_meta = kv_ref[SEQ:L, :]
    ones_col = (lax.broadcasted_iota(jnp.int32, (N_KEYS, 16), 1) == 0).astype(BF16)

    units = []
    for j in range(QBLK_PER_STEP):
        n = step * QBLK_PER_STEP + j
        btype = jnp.where(n == 0, 0, jnp.where(n == N_QBLK - 1, 2, jnp.where(n == N_QBLK, 3, 1)))
        start = jnp.where(n == N_QBLK, 0, jnp.clip((n - 1) * BQ, 0, SEQ - BAND))
        start = pl.multiple_of(start, BQ)
        kv_cat = jnp.concatenate([kv_ref[pl.ds(start, BAND), :], kv_meta], axis=0)
        q_ok = lax.broadcasted_iota(jnp.int32, (BQ, 1), 0) < L - n * BQ
        units.extend((j, g, btype, kv_cat, q_ok) for g in range(N_KV))

    def scores(unit):
        j, g, _, kv_cat, q_ok = unit
        q = jnp.where(q_ok, q_ref[j * BQ:(j + 1) * BQ, :], 0)
        qg = jnp.concatenate(
            [q[:, (Q_PER_KV * g + hl) * HEAD_DIM:(Q_PER_KV * g + hl + 1) * HEAD_DIM]
             for hl in range(Q_PER_KV)], axis=0)
        k = kv_cat[:, g * HEAD_DIM:(g + 1) * HEAD_DIM]
        return lax.dot_general(k, qg, (((1,), (1,)), ((), ())), preferred_element_type=F32)

    st_next = scores(units[0])
    for u, (j, g, btype, kv_cat, _) in enumerate(units):
        st = st_next + bias_ref[btype, g]
        sink = sink_ref[g]
        m = jnp.maximum(jnp.max(st, axis=0, keepdims=True), sink)
        p = jnp.exp2(st - m)
        if u + 1 < len(units):
            st_next = scores(units[u + 1])
            p = jnp.concatenate([jnp.maximum(p[:16], st_next[:16] * 0.0 - 1.0), p[16:]], axis=0)
        v1 = jnp.concatenate([kv_cat[:, KV_W + g * HEAD_DIM:KV_W + (g + 1) * HEAD_DIM], ones_col], axis=1)
        ot = lax.dot_general(v1, p.astype(BF16), (((0,), (0,)), ((), ())), preferred_element_type=F32)
        denom = ot[HEAD_DIM:HEAD_DIM + 1] + jnp.exp2(sink - m)
        ot = ot[:HEAD_DIM] / denom
        for hp in range(Q_PER_KV // 2):
            h0 = Q_PER_KV * g + 2 * hp
            two = jnp.concatenate([ot[:, 2 * hp * BQ:(2 * hp + 1) * BQ],
                                   ot[:, (2 * hp + 1) * BQ:(2 * hp + 2) * BQ]], axis=0)
            o_ref[j * BQ:(j + 1) * BQ, h0 * HEAD_DIM:(h0 + 2) * HEAD_DIM] = two.T.astype(BF16)


def _dft_a_kernel(f_ref, wa_ref, cdft_ref, g_ref, gs_ref):
    cdft = cdft_ref[...]
    for k in range(B_PER_STEP):
        h = jnp.dot(wa_ref[...], f_ref[k].astype(BF16), preferred_element_type=F32)
        hr, hi = h[:N1].astype(BF16), h[N1:].astype(BF16)
        for c in range(F_W // 128):
            cols = slice(c * 128, (c + 1) * 128)
            gri = jnp.dot(jnp.concatenate([hr[:, cols], hi[:, cols]], axis=1), cdft,
                          preferred_element_type=F32)
            gs_ref[:, cols] = gri[:, :128]
            gs_ref[:, F_W + c * 128:F_W + (c + 1) * 128] = gri[:, 128:]
        g_ref[:, k, :, :] = gs_ref[...].astype(BF16).reshape(N1 // 2, 2, 2 * F_W)


def _attn_dft_a_kernel(q_ref, kv_ref, bias_ref, sink_ref, f_ref, wa_ref, cdft_ref,
                       gmix_ref, wout_ref, g2_ref, wg_ref, wu_ref, wd_ref,
                       o_ref, g_ref, wout_b, wg_b, wu_b, wd_b, gs_ref):
    _attn_kernel(q_ref, kv_ref, bias_ref, sink_ref, o_ref)
    _dft_a_kernel(f_ref, wa_ref, cdft_ref, g_ref, gs_ref)

    @pl.when(pl.program_id(0) < N_CAST)
    def _():
        for w_ref, gain_ref, w_b in ((wout_ref, gmix_ref, wout_b), (wg_ref, g2_ref, wg_b),
                                     (wu_ref, g2_ref, wu_b), (wd_ref, None, wd_b)):
            w_b[...] = _fold_gain_bf16(w_ref, gain_ref)


def _dft_c_kernel(g_ref, ec_ref, es_ref, y_ref, gs_ref):
    gs_ref[:, 2 * N2:, :] = jnp.zeros((TCP, K2P - 2 * N2, 2 * F_W), BF16)
    for i in range(TCP):
        gs_ref[i, :2 * N2, :] = g_ref[i].reshape(2 * N2, 2 * F_W)
        g = gs_ref[i]
        y = (jnp.dot(ec_ref[i], g[:, :F_W], preferred_element_type=F32)
             + jnp.dot(es_ref[i], g[:, F_W:], preferred_element_type=F32))
        y_ref[:, i, :, :] = _unit_rms(y[:2 * N2]).astype(BF16).reshape(N2, 2, F_W)


def _post_kernel(h1_ref, a_ref, y_ref, wout_ref, wg_ref, wu_ref, wd_ref, gfin_ref, o_ref):
    y = y_ref[...].reshape(TM, F_W)
    a = a_ref[...]
    h2 = (h1_ref[...]
          + jnp.dot(a, wout_ref[:ATTN_W, :], preferred_element_type=F32) * _rms_scale(a.astype(F32))
          + jnp.dot(y, wout_ref[ATTN_W:, :], preferred_element_type=F32))
    h3 = _swiglu_half(h2, wg_ref, wu_ref, wd_ref)
    o_ref[...] = _rms(h3, gfin_ref[...])


def _resident(shape):
    zeros = (0,) * len(shape)
    return pl.BlockSpec(shape, lambda *_: zeros, pipeline_mode=pl.Buffered(1))


def _params():
    return pltpu.CompilerParams(dimension_semantics=("arbitrary",), vmem_limit_bytes=VMEM_LIMIT)


def _row(g):
    return g.astype(F32).reshape(1, -1)


def kernel(x, meta_tokens, ffn1_norm, ffn1_w_gate, ffn1_w_up, ffn1_w_down, mix_norm, w_in, q_norm, k_norm, sink, attn_out_norm, fourier_out_norm, w_out, ffn2_norm, ffn2_w_gate, ffn2_w_up, ffn2_w_down, final_norm):
    assert x.shape == (1, SEQ, D_MODEL) and x.dtype == F32
    assert A_PER_TILE * N2 == TM and A_PER_TILE * N_TILES == N1
    x2 = x.reshape(SEQ, D_MODEL)
    gqk = jnp.concatenate([jnp.tile(q_norm.astype(F32), N_HEADS) * (HEAD_DIM ** -0.5 * LOG2E),
                           jnp.tile(k_norm.astype(F32), N_KV)]).reshape(1, QK_W)
    bf = lambda w: w.astype(BF16)

    tile_idx = lambda s: jnp.maximum(s - N_CAST, 0)
    row_tile = lambda w: pl.BlockSpec((TM, w), lambda s: (tile_idx(s), 0))
    tile = lambda w: pl.BlockSpec((TM, w), lambda s: (s, 0))
    row_sets = lambda w: pl.BlockSpec((N2, A_PER_TILE, w), lambda s: (0, tile_idx(s), 0))
    cast_rows = lambda r, c: pl.BlockSpec((r // N_CAST, c), lambda s: (jnp.minimum(s, N_CAST - 1), 0))
    gains = jnp.broadcast_to(
        jnp.stack([ffn1_norm, mix_norm, jnp.concatenate([attn_out_norm, fourier_out_norm]), ffn2_norm])
        .astype(F32)[:, :, None], (4, D_MODEL, 128))
    gain_rows = lambda k: pl.BlockSpec((None, D_MODEL // N_CAST, 128), lambda s: (k, jnp.minimum(s, N_CAST - 1), 0))
    bf16_copy = lambda r, c: pltpu.VMEM((r, c), BF16)
    ffn_in = [cast_rows(D_MODEL, D_FF), cast_rows(D_MODEL, D_FF), cast_rows(D_FF, D_MODEL)]
    ffn_scratch = [bf16_copy(D_MODEL, D_FF), bf16_copy(D_MODEL, D_FF), bf16_copy(D_FF, D_MODEL)]
    f32 = lambda w: w.astype(F32)

    h1, q, kv, f = pl.pallas_call(
        _pre_kernel,
        grid=(N_CAST + N_TILES,),
        in_specs=[row_tile(D_MODEL), _resident((N_META, D_MODEL)), gain_rows(0), *ffn_in,
                  gain_rows(1), cast_rows(D_MODEL, IN_W), _resident((1, QK_W)),
                  _resident((256, 256))],
        out_specs=[row_tile(D_MODEL), row_tile(ATTN_W), row_tile(2 * KV_W), row_sets(F_W)],
        out_shape=[jax.ShapeDtypeStruct((L, D_MODEL), F32), jax.ShapeDtypeStruct((L, ATTN_W), BF16),
                   jax.ShapeDtypeStruct((L, 2 * KV_W), BF16), jax.ShapeDtypeStruct((N2, N1, F_W), F32)],
        scratch_shapes=ffn_scratch + [bf16_copy(D_MODEL, IN_W), pltpu.VMEM((F_W // 128, TM, 128), F32)],
        compiler_params=_params(),
        name="pre",
    )(x2, meta_tokens.astype(F32), gains, f32(ffn1_w_gate), f32(ffn1_w_up), f32(ffn1_w_down), gains,
      f32(w_in), gqk, _ones_blockdiag())

    sink_rows = (jnp.repeat(sink.astype(F32).reshape(N_KV, Q_PER_KV), BQ, axis=1) * LOG2E)[:, None, :]

    q_rows = pl.BlockSpec((QBLK_PER_STEP * BQ, ATTN_W), lambda s: (s, 0))
    n_steps = pl.cdiv(N_QBLK + 1, QBLK_PER_STEP)
    assert n_steps == pl.cdiv(N2, B_PER_STEP) and n_steps >= N_CAST
    post_w_blocks = [cast_rows(D_MODEL, D_MODEL), *ffn_in]
    a_out, g, wout_b, wg2_b, wu2_b, wd2_b = pl.pallas_call(
        _attn_dft_a_kernel,
        grid=(n_steps,),
        in_specs=[q_rows, _resident((L, 2 * KV_W)), _resident((4, N_KV, N_KEYS, Q_PER_KV * BQ)),
                  _resident((N_KV, 1, Q_PER_KV * BQ)),
                  pl.BlockSpec((B_PER_STEP, N1, F_W), lambda s: (s, 0, 0)),
                  _resident((2 * N1, N1)), _resident((256, 256)),
                  gain_rows(2), post_w_blocks[0], gain_rows(3), *post_w_blocks[1:]],
        out_specs=[q_rows, pl.BlockSpec((N1 // 2, B_PER_STEP, 2, 2 * F_W), lambda s: (0, s, 0, 0)),
                   *post_w_blocks],
        out_shape=[jax.ShapeDtypeStruct((L, ATTN_W), BF16),
                   jax.ShapeDtypeStruct((N1 // 2, N2, 2, 2 * F_W), BF16),
                   jax.ShapeDtypeStruct((D_MODEL, D_MODEL), BF16), jax.ShapeDtypeStruct((D_MODEL, D_FF), BF16),
                   jax.ShapeDtypeStruct((D_MODEL, D_FF), BF16), jax.ShapeDtypeStruct((D_FF, D_MODEL), BF16)],
        scratch_shapes=[pltpu.VMEM((N1, 2 * F_W), F32)],
        compiler_params=_params(),
        name="attn_dft_a",
    )(q, kv, jnp.asarray(_attn_bias_const()), sink_rows,
      f, bf(jnp.asarray(_stage_a_dft())), bf(jnp.asarray(_channel_dft())),
      gains, f32(w_out), gains, f32(ffn2_w_gate), f32(ffn2_w_up), f32(ffn2_w_down))

    ec, es = _stage_c_dft()
    y = pl.pallas_call(
        _dft_c_kernel,
        grid=(N1 // (2 * TCP),),
        in_specs=[pl.BlockSpec((TCP, N2, 2, 2 * F_W), lambda i: (i, 0, 0, 0)),
                  pl.BlockSpec((TCP, K2P, K2P), lambda i: (i, 0, 0)),
                  pl.BlockSpec((TCP, K2P, K2P), lambda i: (i, 0, 0))],
        out_specs=pl.BlockSpec((N2, TCP, 2, F_W), lambda i: (0, i, 0, 0)),
        out_shape=jax.ShapeDtypeStruct((N2, N1 // 2, 2, F_W), BF16),
        scratch_shapes=[pltpu.VMEM((TCP, K2P, 2 * F_W), BF16)],
        compiler_params=_params(),
        name="dft_c",
    )(g, bf(jnp.asarray(ec)), bf(jnp.asarray(es)))

    out = pl.pallas_call(
        _post_kernel,
        grid=(N_TILES,),
        in_specs=[tile(D_MODEL), tile(ATTN_W), pl.BlockSpec((TM // 2, 2, F_W), lambda s: (s, 0, 0)),
                  _resident((D_MODEL, D_MODEL)), _resident((D_MODEL, D_FF)), _resident((D_MODEL, D_FF)),
                  _resident((D_FF, D_MODEL)), _resident((1, D_MODEL))],
        out_specs=tile(D_MODEL),
        out_shape=jax.ShapeDtypeStruct((SEQ, D_MODEL), F32),
        compiler_params=_params(),
        name="post",
    )(h1, a_out, y.reshape(L // 2, 2, F_W), wout_b, wg2_b, wu2_b, wd2_b, _row(final_norm))
    return out.reshape(1, SEQ, D_MODEL)
```

```python
import numpy as np
import jax
import jax.numpy as jnp
from jax import lax
from jax.experimental import pallas as pl
from jax.experimental.pallas import tpu as pltpu

F32 = jnp.float32
BF16 = jnp.bfloat16

D_MODEL = 1024
SEQ = 16384
N_META = 16
L = SEQ + N_META
HEAD_DIM = 64
N_HEADS = 8
N_KV = 2
Q_PER_KV = N_HEADS // N_KV
ATTN_W = N_HEADS * HEAD_DIM
KV_W = N_KV * HEAD_DIM
F_W = 512
F_GROUP = 64
IN_W = ATTN_W + 2 * KV_W + F_W
QK_W = ATTN_W + KV_W
WINDOW = 128
D_FF = 2816
N_CAST = 8
EPS = 1e-6
NEG = -1e30
LOG2E = 1.4426950408889634

TM = 656
N_TILES = L // TM
META_ROW0 = SEQ - (N_TILES - 1) * TM

BQ = 128
N_QBLK = SEQ // BQ
QBLK_PER_STEP = 12
BAND = BQ + 2 * WINDOW
N_KEYS = BAND + N_META

N1 = 200
N2 = 82
A_PER_TILE = TM // N2
K2P = 176
B_PER_STEP = 8
TCP = 25

VMEM_LIMIT = 56 * 1024 * 1024


def _ones_blockdiag():
    m = np.kron(np.eye(4), np.ones((HEAD_DIM, HEAD_DIM)))
    return jnp.asarray(m, BF16)


def _channel_dft():
    c = np.arange(F_GROUP)
    ang = 2.0 * np.pi * np.outer(c, c) / F_GROUP
    cos = np.kron(np.eye(2), np.cos(ang) / 8.0)
    sin = np.kron(np.eye(2), np.sin(ang) / 8.0)
    return np.block([[cos, -sin], [sin, cos]]).astype(np.float32)


def _stage_a_dft():
    a = np.arange(N1, dtype=np.int64)
    m = (N2 * np.outer(a + N_META, a)) % L
    ang = 2.0 * np.pi * m / L
    return np.concatenate([np.cos(ang), -np.sin(ang)], axis=0).astype(np.float32)


def _stage_c_dft():
    c = np.arange(N1, dtype=np.int64)[:, None, None]
    d = np.arange(N2, dtype=np.int64)[None, :, None]
    b = np.arange(N2, dtype=np.int64)[None, None, :]
    m = ((c + N1 * d + N_META) * (b + N_META)) % L
    ang = 2.0 * np.pi * m / L
    ec = np.zeros((N1 // 2, K2P, K2P), np.float32)
    es = np.zeros((N1 // 2, K2P, K2P), np.float32)
    for e in range(2):
        ec[:, e:2 * N2:2, e:2 * N2:2] = (np.cos(ang) / np.sqrt(L))[e::2]
        es[:, e:2 * N2:2, e:2 * N2:2] = (np.sin(ang) / np.sqrt(L))[e::2]
    return ec, es


def _attn_bias_const():
    slopes = 2.0 ** -(np.arange(N_HEADS) + 1.0)
    i = np.arange(BQ)[:, None]
    j = np.arange(BAND)[None, :]
    m = np.arange(N_META)[None, :]
    out = np.full((4, N_KV, Q_PER_KV * BQ, N_KEYS), NEG, np.float64)
    for t in range(4):
        if t < 3:
            dist = np.abs(t * WINDOW + i - j)
            band_ok = dist <= WINDOW
            dist_m = np.minimum(t * WINDOW + N_META + i - m, WINDOW) if t == 0 else np.full((BQ, N_META), WINDOW)
        else:
            dist = N_META + j - i
            band_ok = (dist <= WINDOW) & (i < N_META)
            dist_m = np.abs(i - m)
        for h in range(N_HEADS):
            g, hl = divmod(h, Q_PER_KV)
            rows = slice(hl * BQ, (hl + 1) * BQ)
            band = np.where(band_ok, -slopes[h] * dist, NEG)
            if t == 3:
                band = np.where(i < N_META, band, 0.0)
            out[t, g, rows, :BAND] = band
            out[t, g, rows, BAND:BAND + N_META] = -slopes[h] * dist_m
    return np.ascontiguousarray(LOG2E * out.transpose(0, 1, 3, 2)).astype(np.float32)


def _rms_scale(x):
    return lax.rsqrt(jnp.mean(x * x, axis=-1, keepdims=True) + EPS)


def _unit_rms(x):
    return x * _rms_scale(x)


def _rms(x, g):
    return _unit_rms(x) * g


def _swiglu_half(x, wg_s, wu_s, wd_s, ready=lambda w_s: None):
    r = _rms_scale(x)
    xb = x.astype(BF16)
    ready(wg_s)
    gate = jnp.dot(xb, wg_s[...], preferred_element_type=F32) * r
    ready(wu_s)
    up = jnp.dot(xb, wu_s[...], preferred_element_type=F32) * r
    act = (gate * jax.nn.sigmoid(gate) * up).astype(BF16)
    ready(wd_s)
    return x + 0.5 * jnp.dot(act, wd_s[...], preferred_element_type=F32)


def _cast_weights(s, f32_refs, gain_refs, bf16_scratch):
    for w_ref, g_ref, w_s in zip(f32_refs, gain_refs, bf16_scratch):
        rows = w_ref.shape[0]
        w_s[pl.ds(pl.multiple_of(s * rows, rows), rows), :] = _fold_gain_bf16(w_ref, g_ref)


def _fold_gain_bf16(w_ref, g_ref):
    w = w_ref[...]
    if g_ref is not None:
        w = w * jnp.tile(g_ref[...], (1, w.shape[1] // g_ref.shape[1]))
    return w.astype(BF16)


def _pre_kernel(x_ref, meta_ref, g1_ref, wg_ref, wu_ref, wd_ref, gm_ref, win_ref, gqk_ref,
                ones_ref, h1_ref, q_ref, kv_ref, f_ref, wg_s, wu_s, wd_s, win_s, fs_s):
    s = pl.program_id(0)

    @pl.when(s < N_CAST)
    def _():
        _cast_weights(s, (wg_ref, wu_ref, wd_ref, win_ref), (g1_ref, g1_ref, None, gm_ref),
                      (wg_s, wu_s, wd_s, win_s))

    @pl.when(s >= N_CAST)
    def _():
        _pre_tile(s - N_CAST, x_ref, meta_ref, gqk_ref, ones_ref,
                  h1_ref, q_ref, kv_ref, f_ref, wg_s, wu_s, wd_s, win_s, fs_s)


def _pre_tile(i, x_ref, meta_ref, gqk_ref, ones_ref,
              h1_ref, q_ref, kv_ref, f_ref, wg_s, wu_s, wd_s, win_s, fs_s):
    tail = jnp.where(i == N_TILES - 1, meta_ref[...], x_ref[META_ROW0:, :])
    x = jnp.concatenate([x_ref[:META_ROW0, :], tail], axis=0)

    h1 = _swiglu_half(x, wg_s, wu_s, wd_s)
    h1_ref[...] = h1

    u = jnp.dot(h1.astype(BF16), win_s[...], preferred_element_type=F32) * _rms_scale(h1)

    qk = u[:, :QK_W]
    sq = (qk * qk).astype(BF16)
    ones = ones_ref[...]
    ss = jnp.concatenate([
        jnp.dot(sq[:, 0:256], ones, preferred_element_type=F32),
        jnp.dot(sq[:, 256:512], ones, preferred_element_type=F32),
        jnp.dot(sq[:, 512:640], ones[:KV_W, :KV_W], preferred_element_type=F32)], axis=1)
    qkn = qk * lax.rsqrt(ss * (1.0 / HEAD_DIM) + EPS) * gqk_ref[...]
    q_ref[...] = qkn[:, :ATTN_W].astype(BF16)
    kv_ref[...] = jnp.concatenate([qkn[:, ATTN_W:], u[:, QK_W:QK_W + KV_W]], axis=1).astype(BF16)

    for c in range(F_W // 128):
        lanes = slice(c * 128, (c + 1) * 128)
        fs_s[c] = u[:, QK_W + KV_W + c * 128:QK_W + KV_W + (c + 1) * 128]
        for b in range(N2):
            f_ref[b, :, lanes] = fs_s[c, pl.ds(b, A_PER_TILE, stride=N2), :]


def _attn_kernel(q_ref, kv_ref, bias_ref, sink_ref, o_ref):
    step = pl.program_id(0)
    kv_meta = kv_ref[SEQ:L, :]
    ones_col = (lax.broadcasted_iota(jnp.int32, (N_KEYS, 16), 1) == 0).astype(BF16)

    units = []
    for j in range(QBLK_PER_STEP):
        n = step * QBLK_PER_STEP + j
        btype = jnp.where(n == 0, 0, jnp.where(n == N_QBLK - 1, 2, jnp.where(n == N_QBLK, 3, 1)))
        start = jnp.where(n == N_QBLK, 0, jnp.clip((n - 1) * BQ, 0, SEQ - BAND))
        start = pl.multiple_of(start, BQ)
        kv_cat = jnp.concatenate([kv_ref[pl.ds(start, BAND), :], kv_meta], axis=0)
        q_ok = lax.broadcasted_iota(jnp.int32, (BQ, 1), 0) < L - n * BQ
        units.extend((j, g, btype, kv_cat, q_ok) for g in range(N_KV))

    def scores(unit):
        j, g, _, kv_cat, q_ok = unit
        q = jnp.where(q_ok, q_ref[j * BQ:(j + 1) * BQ, :], 0)
        qg = jnp.concatenate(
            [q[:, (Q_PER_KV * g + hl) * HEAD_DIM:(Q_PER_KV * g + hl + 1) * HEAD_DIM]
             for hl in range(Q_PER_KV)], axis=0)
        k = kv_cat[:, g * HEAD_DIM:(g + 1) * HEAD_DIM]
        return lax.dot_general(k, qg, (((1,), (1,)), ((), ())), preferred_element_type=F32)

    st_next = scores(units[0])
    for u, (j, g, btype, kv_cat, _) in enumerate(units):
        st = st_next + bias_ref[btype, g]
        sink = sink_ref[g]
        m = jnp.maximum(jnp.max(st, axis=0, keepdims=True), sink)
        p = jnp.exp2(st - m)
        if u + 1 < len(units):
            st_next = scores(units[u + 1])
            p = jnp.concatenate([jnp.maximum(p[:16], st_next[:16] * 0.0 - 1.0), p[16:]], axis=0)
        v1 = jnp.concatenate([kv_cat[:, KV_W + g * HEAD_DIM:KV_W + (g + 1) * HEAD_DIM], ones_col], axis=1)
        ot = lax.dot_general(v1, p.astype(BF16), (((0,), (0,)), ((), ())), preferred_element_type=F32)
        denom = ot[HEAD_DIM:HEAD_DIM + 1] + jnp.exp2(sink - m)
        ot = ot[:HEAD_DIM] / denom
        for hp in range(Q_PER_KV // 2):
            h0 = Q_PER_KV * g + 2 * hp
            two = jnp.concatenate([ot[:, 2 * hp * BQ:(2 * hp + 1) * BQ],
                                   ot[:, (2 * hp + 1) * BQ:(2 * hp + 2) * BQ]], axis=0)
            o_ref[j * BQ:(j + 1) * BQ, h0 * HEAD_DIM:(h0 + 2) * HEAD_DIM] = two.T.astype(BF16)


def _dft_a_kernel(f_ref, wa_ref, cdft_ref, g_ref, gs_ref):
    cdft = cdft_ref[...]
    for k in range(B_PER_STEP):
        h = jnp.dot(wa_ref[...], f_ref[k].astype(BF16), preferred_element_type=F32)
        hr, hi = h[:N1].astype(BF16), h[N1:].astype(BF16)
        for c in range(F_W // 128):
            cols = slice(c * 128, (c + 1) * 128)
            gri = jnp.dot(jnp.concatenate([hr[:, cols], hi[:, cols]], axis=1), cdft,
                          preferred_element_type=F32)
            gs_ref[:, cols] = gri[:, :128]
            gs_ref[:, F_W + c * 128:F_W + (c + 1) * 128] = gri[:, 128:]
        g_ref[:, k, :, :] = gs_ref[...].astype(BF16).reshape(N1 // 2, 2, 2 * F_W)


def _attn_dft_a_kernel(q_ref, kv_ref, bias_ref, sink_ref, f_ref, wa_ref, cdft_ref,
                       gmix_ref, wout_ref, g2_ref, wg_ref, wu_ref, wd_ref,
                       o_ref, g_ref, wout_b, wg_b, wu_b, wd_b, gs_ref):
    _attn_kernel(q_ref, kv_ref, bias_ref, sink_ref, o_ref)
    _dft_a_kernel(f_ref, wa_ref, cdft_ref, g_ref, gs_ref)

    @pl.when(pl.program_id(0) < N_CAST)
    def _():
        for w_ref, gain_ref, w_b in ((wout_ref, gmix_ref, wout_b), (wg_ref, g2_ref, wg_b),
                                     (wu_ref, g2_ref, wu_b), (wd_ref, None, wd_b)):
            w_b[...] = _fold_gain_bf16(w_ref, gain_ref)


def _dft_c_kernel(g_ref, ec_ref, es_ref, y_ref, gs_ref):
    gs_ref[:, 2 * N2:, :] = jnp.zeros((TCP, K2P - 2 * N2, 2 * F_W), BF16)
    for i in range(TCP):
        gs_ref[i, :2 * N2, :] = g_ref[i].reshape(2 * N2, 2 * F_W)
        g = gs_ref[i]
        y = (jnp.dot(ec_ref[i], g[:, :F_W], preferred_element_type=F32)
             + jnp.dot(es_ref[i], g[:, F_W:], preferred_element_type=F32))
        y_ref[:, i, :, :] = _unit_rms(y[:2 * N2]).astype(BF16).reshape(N2, 2, F_W)


def _post_kernel(h1_ref, a_ref, y_ref, wout_hbm, wg_hbm, wu_hbm, wd_hbm, gfin_ref, o_ref,
                 wout_s, wg_s, wu_s, wd_s, sems):
    scratch = (wout_s, wg_s, wu_s, wd_s)
    copies = {id(w_s): pltpu.make_async_copy(w_hbm, w_s, sems.at[k])
              for k, (w_hbm, w_s) in enumerate(zip((wout_hbm, wg_hbm, wu_hbm, wd_hbm), scratch))}

    def tile(ready):
        y = y_ref[...].reshape(TM, F_W)
        a = a_ref[...]
        ready(wout_s)
        h2 = (h1_ref[...]
              + jnp.dot(a, wout_s[:ATTN_W, :], preferred_element_type=F32) * _rms_scale(a.astype(F32))
              + jnp.dot(y, wout_s[ATTN_W:, :], preferred_element_type=F32))
        h3 = _swiglu_half(h2, wg_s, wu_s, wd_s, ready)
        o_ref[...] = _rms(h3, gfin_ref[...])

    @pl.when(pl.program_id(0) == 0)
    def _():
        for w_s in scratch:
            copies[id(w_s)].start()
        tile(lambda w_s: copies[id(w_s)].wait())

    @pl.when(pl.program_id(0) > 0)
    def _():
        tile(lambda w_s: None)


def _resident(shape):
    zeros = (0,) * len(shape)
    return pl.BlockSpec(shape, lambda *_: zeros, pipeline_mode=pl.Buffered(1))


def _params():
    return pltpu.CompilerParams(dimension_semantics=("arbitrary",), vmem_limit_bytes=VMEM_LIMIT)


def _row(g):
    return g.astype(F32).reshape(1, -1)


def kernel(x, meta_tokens, ffn1_norm, ffn1_w_gate, ffn1_w_up, ffn1_w_down, mix_norm, w_in, q_norm, k_norm, sink, attn_out_norm, fourier_out_norm, w_out, ffn2_norm, ffn2_w_gate, ffn2_w_up, ffn2_w_down, final_norm):
    assert x.shape == (1, SEQ, D_MODEL) and x.dtype == F32
    assert A_PER_TILE * N2 == TM and A_PER_TILE * N_TILES == N1
    x2 = x.reshape(SEQ, D_MODEL)
    gqk = jnp.concatenate([jnp.tile(q_norm.astype(F32), N_HEADS) * (HEAD_DIM ** -0.5 * LOG2E),
                           jnp.tile(k_norm.astype(F32), N_KV)]).reshape(1, QK_W)
    bf = lambda w: w.astype(BF16)

    tile_idx = lambda s: jnp.maximum(s - N_CAST, 0)
    row_tile = lambda w: pl.BlockSpec((TM, w), lambda s: (tile_idx(s), 0))
    tile = lambda w: pl.BlockSpec((TM, w), lambda s: (s, 0))
    row_sets = lambda w: pl.BlockSpec((N2, A_PER_TILE, w), lambda s: (0, tile_idx(s), 0))
    cast_rows = lambda r, c: pl.BlockSpec((r // N_CAST, c), lambda s: (jnp.minimum(s, N_CAST - 1), 0))
    gains = jnp.broadcast_to(
        jnp.stack([ffn1_norm, mix_norm, jnp.concatenate([attn_out_norm, fourier_out_norm]), ffn2_norm])
        .astype(F32)[:, :, None], (4, D_MODEL, 128))
    gain_rows = lambda k: pl.BlockSpec((None, D_MODEL // N_CAST, 128), lambda s: (k, jnp.minimum(s, N_CAST - 1), 0))
    bf16_copy = lambda r, c: pltpu.VMEM((r, c), BF16)
    ffn_in = [cast_rows(D_MODEL, D_FF), cast_rows(D_MODEL, D_FF), cast_rows(D_FF, D_MODEL)]
    ffn_scratch = [bf16_copy(D_MODEL, D_FF), bf16_copy(D_MODEL, D_FF), bf16_copy(D_FF, D_MODEL)]
    f32 = lambda w: w.astype(F32)

    h1, q, kv, f = pl.pallas_call(
        _pre_kernel,
        grid=(N_CAST + N_TILES,),
        in_specs=[row_tile(D_MODEL), _resident((N_META, D_MODEL)), gain_rows(0), *ffn_in,
                  gain_rows(1), cast_rows(D_MODEL, IN_W), _resident((1, QK_W)),
                  _resident((256, 256))],
        out_specs=[row_tile(D_MODEL), row_tile(ATTN_W), row_tile(2 * KV_W), row_sets(F_W)],
        out_shape=[jax.ShapeDtypeStruct((L, D_MODEL), F32), jax.ShapeDtypeStruct((L, ATTN_W), BF16),
                   jax.ShapeDtypeStruct((L, 2 * KV_W), BF16), jax.ShapeDtypeStruct((N2, N1, F_W), F32)],
        scratch_shapes=ffn_scratch + [bf16_copy(D_MODEL, IN_W), pltpu.VMEM((F_W // 128, TM, 128), F32)],
        compiler_params=_params(),
        name="pre",
    )(x2, meta_tokens.astype(F32), gains, f32(ffn1_w_gate), f32(ffn1_w_up), f32(ffn1_w_down), gains,
      f32(w_in), gqk, _ones_blockdiag())

    sink_rows = (jnp.repeat(sink.astype(F32).reshape(N_KV, Q_PER_KV), BQ, axis=1) * LOG2E)[:, None, :]

    q_rows = pl.BlockSpec((QBLK_PER_STEP * BQ, ATTN_W), lambda s: (s, 0))
    n_steps = pl.cdiv(N_QBLK + 1, QBLK_PER_STEP)
    assert n_steps == pl.cdiv(N2, B_PER_STEP) and n_steps >= N_CAST
    post_w_blocks = [cast_rows(D_MODEL, D_MODEL), *ffn_in]
    a_out, g, wout_b, wg2_b, wu2_b, wd2_b = pl.pallas_call(
        _attn_dft_a_kernel,
        grid=(n_steps,),
        in_specs=[q_rows, _resident((L, 2 * KV_W)), _resident((4, N_KV, N_KEYS, Q_PER_KV * BQ)),
                  _resident((N_KV, 1, Q_PER_KV * BQ)),
                  pl.BlockSpec((B_PER_STEP, N1, F_W), lambda s: (s, 0, 0)),
                  _resident((2 * N1, N1)), _resident((256, 256)),
                  gain_rows(2), post_w_blocks[0], gain_rows(3), *post_w_blocks[1:]],
        out_specs=[q_rows, pl.BlockSpec((N1 // 2, B_PER_STEP, 2, 2 * F_W), lambda s: (0, s, 0, 0)),
                   *post_w_blocks],
        out_shape=[jax.ShapeDtypeStruct((L, ATTN_W), BF16),
                   jax.ShapeDtypeStruct((N1 // 2, N2, 2, 2 * F_W), BF16),
                   jax.ShapeDtypeStruct((D_MODEL, D_MODEL), BF16), jax.ShapeDtypeStruct((D_MODEL, D_FF), BF16),
                   jax.ShapeDtypeStruct((D_MODEL, D_FF), BF16), jax.ShapeDtypeStruct((D_FF, D_MODEL), BF16)],
        scratch_shapes=[pltpu.VMEM((N1, 2 * F_W), F32)],
        compiler_params=_params(),
        name="attn_dft_a",
    )(q, kv, jnp.asarray(_attn_bias_const()), sink_rows,
      f, bf(jnp.asarray(_stage_a_dft())), bf(jnp.asarray(_channel_dft())),
      gains, f32(w_out), gains, f32(ffn2_w_gate), f32(ffn2_w_up), f32(ffn2_w_down))

    ec, es = _stage_c_dft()
    y = pl.pallas_call(
        _dft_c_kernel,
        grid=(N1 // (2 * TCP),),
        in_specs=[pl.BlockSpec((TCP, N2, 2, 2 * F_W), lambda i: (i, 0, 0, 0)),
                  pl.BlockSpec((TCP, K2P, K2P), lambda i: (i, 0, 0)),
                  pl.BlockSpec((TCP, K2P, K2P), lambda i: (i, 0, 0))],
        out_specs=pl.BlockSpec((N2, TCP, 2, F_W), lambda i: (0, i, 0, 0)),
        out_shape=jax.ShapeDtypeStruct((N2, N1 // 2, 2, F_W), BF16),
        scratch_shapes=[pltpu.VMEM((TCP, K2P, 2 * F_W), BF16)],
        compiler_params=_params(),
        name="dft_c",
    )(g, bf(jnp.asarray(ec)), bf(jnp.asarray(es)))

    out = pl.pallas_call(
        _post_kernel,
        grid=(N_TILES,),
        in_specs=[tile(D_MODEL), tile(ATTN_W), pl.BlockSpec((TM // 2, 2, F_W), lambda s: (s, 0, 0)),
                  *[pl.BlockSpec(memory_space=pl.ANY)] * 4, _resident((1, D_MODEL))],
        out_specs=tile(D_MODEL),
        out_shape=jax.ShapeDtypeStruct((SEQ, D_MODEL), F32),
        scratch_shapes=[bf16_copy(D_MODEL, D_MODEL), *ffn_scratch, pltpu.SemaphoreType.DMA((4,))],
        compiler_params=_params(),
        name="post",
    )(h1, a_out, y.reshape(L // 2, 2, F_W), wout_b, wg2_b, wu2_b, wd2_b, _row(final_norm))
    return out.reshape(1, SEQ, D_MODEL)
```

```python
import numpy as np
import jax
import jax.numpy as jnp
from jax import lax
from jax.experimental import pallas as pl
from jax.experimental.pallas import tpu as pltpu

F32 = jnp.float32
BF16 = jnp.bfloat16

D_MODEL = 1024
SEQ = 16384
N_META = 16
L = SEQ + N_META
HEAD_DIM = 64
N_HEADS = 8
N_KV = 2
Q_PER_KV = N_HEADS // N_KV
ATTN_W = N_HEADS * HEAD_DIM
KV_W = N_KV * HEAD_DIM
F_W = 512
F_GROUP = 64
IN_W = ATTN_W + 2 * KV_W + F_W
QK_W = ATTN_W + KV_W
WINDOW = 128
D_FF = 2816
N_CAST = 8
EPS = 1e-6
NEG = -1e30
LOG2E = 1.4426950408889634

TM = 656
N_TILES = L // TM
META_ROW0 = SEQ - (N_TILES - 1) * TM

BQ = 128
N_QBLK = SEQ // BQ
QBLK_PER_STEP = 13
BAND = BQ + 2 * WINDOW
N_KEYS = BAND + N_META

N1 = 200
N2 = 82
A_PER_TILE = TM // N2
K2P = 176
B_PER_STEP = 9
TCP = 25

VMEM_LIMIT = 56 * 1024 * 1024


def _ones_blockdiag():
    m = np.kron(np.eye(4), np.ones((HEAD_DIM, HEAD_DIM)))
    return jnp.asarray(m, BF16)


def _channel_dft():
    c = np.arange(F_GROUP)
    ang = 2.0 * np.pi * np.outer(c, c) / F_GROUP
    cos = np.kron(np.eye(2), np.cos(ang) / 8.0)
    sin = np.kron(np.eye(2), np.sin(ang) / 8.0)
    return np.block([[cos, -sin], [sin, cos]]).astype(np.float32)


def _stage_a_dft():
    a = np.arange(N1, dtype=np.int64)
    m = (N2 * np.outer(a + N_META, a)) % L
    ang = 2.0 * np.pi * m / L
    return np.concatenate([np.cos(ang), -np.sin(ang)], axis=0).astype(np.float32)


def _stage_c_dft():
    c = np.arange(N1, dtype=np.int64)[:, None, None]
    d = np.arange(N2, dtype=np.int64)[None, :, None]
    b = np.arange(N2, dtype=np.int64)[None, None, :]
    m = ((c + N1 * d + N_META) * (b + N_META)) % L
    ang = 2.0 * np.pi * m / L
    ec = np.zeros((N1 // 2, K2P, K2P), np.float32)
    es = np.zeros((N1 // 2, K2P, K2P), np.float32)
    for e in range(2):
        ec[:, e:2 * N2:2, e:2 * N2:2] = (np.cos(ang) / np.sqrt(L))[e::2]
        es[:, e:2 * N2:2, e:2 * N2:2] = (np.sin(ang) / np.sqrt(L))[e::2]
    return ec, es


def _attn_bias_const():
    slopes = 2.0 ** -(np.arange(N_HEADS) + 1.0)
    i = np.arange(BQ)[:, None]
    j = np.arange(BAND)[None, :]
    m = np.arange(N_META)[None, :]
    out = np.full((4, N_KV, Q_PER_KV * BQ, N_KEYS), NEG, np.float64)
    for t in range(4):
        if t < 3:
            dist = np.abs(t * WINDOW + i - j)
            band_ok = dist <= WINDOW
            dist_m = np.minimum(t * WINDOW + N_META + i - m, WINDOW) if t == 0 else np.full((BQ, N_META), WINDOW)
        else:
            dist = N_META + j - i
            band_ok = (dist <= WINDOW) & (i < N_META)
            dist_m = np.abs(i - m)
        for h in range(N_HEADS):
            g, hl = divmod(h, Q_PER_KV)
            rows = slice(hl * BQ, (hl + 1) * BQ)
            band = np.where(band_ok, -slopes[h] * dist, NEG)
            if t == 3:
                band = np.where(i < N_META, band, 0.0)
            out[t, g, rows, :BAND] = band
            out[t, g, rows, BAND:BAND + N_META] = -slopes[h] * dist_m
    return np.ascontiguousarray(LOG2E * out.transpose(0, 1, 3, 2)).astype(np.float32)


def _rms_scale(x):
    return lax.rsqrt(jnp.mean(x * x, axis=-1, keepdims=True) + EPS)


def _unit_rms(x):
    return x * _rms_scale(x)


def _rms(x, g):
    return _unit_rms(x) * g


def _swiglu_half(x, wg_s, wu_s, wd_s):
    r = _rms_scale(x)
    xb = x.astype(BF16)
    gate = jnp.dot(xb, wg_s[...], preferred_element_type=F32) * r
    up = jnp.dot(xb, wu_s[...], preferred_element_type=F32) * r
    act = (gate * jax.nn.sigmoid(gate) * up).astype(BF16)
    return x + 0.5 * jnp.dot(act, wd_s[...], preferred_element_type=F32)


def _cast_weights(s, f32_refs, gain_refs, bf16_scratch):
    for w_ref, g_ref, w_s in zip(f32_refs, gain_refs, bf16_scratch):
        rows = w_ref.shape[0]
        w_s[pl.ds(pl.multiple_of(s * rows, rows), rows), :] = _fold_gain_bf16(w_ref, g_ref)


def _fold_gain_bf16(w_ref, g_ref):
    w = w_ref[...]
    if g_ref is not None:
        w = w * jnp.tile(g_ref[...], (1, w.shape[1] // g_ref.shape[1]))
    return w.astype(BF16)


def _pre_kernel(x_ref, meta_ref, g1_ref, wg_ref, wu_ref, wd_ref, gm_ref, win_ref, gqk_ref,
                ones_ref, h1_ref, q_ref, kv_ref, f_ref, wg_s, wu_s, wd_s, win_s, fs_s):
    s = pl.program_id(0)

    @pl.when(s < N_CAST)
    def _():
        _cast_weights(s, (wg_ref, wu_ref, wd_ref, win_ref), (g1_ref, g1_ref, None, gm_ref),
                      (wg_s, wu_s, wd_s, win_s))

    @pl.when(s >= N_CAST)
    def _():
        _pre_tile(s - N_CAST, x_ref, meta_ref, gqk_ref, ones_ref,
                  h1_ref, q_ref, kv_ref, f_ref, wg_s, wu_s, wd_s, win_s, fs_s)


def _pre_tile(i, x_ref, meta_ref, gqk_ref, ones_ref,
              h1_ref, q_ref, kv_ref, f_ref, wg_s, wu_s, wd_s, win_s, fs_s):
    tail = jnp.where(i == N_TILES - 1, meta_ref[...], x_ref[META_ROW0:, :])
    x = jnp.concatenate([x_ref[:META_ROW0, :], tail], axis=0)

    h1 = _swiglu_half(x, wg_s, wu_s, wd_s)
    h1_ref[...] = h1

    u = jnp.dot(h1.astype(BF16), win_s[...], preferred_element_type=F32) * _rms_scale(h1)

    qk = u[:, :QK_W]
    sq = (qk * qk).astype(BF16)
    ones = ones_ref[...]
    ss = jnp.concatenate([
        jnp.dot(sq[:, 0:256], ones, preferred_element_type=F32),
        jnp.dot(sq[:, 256:512], ones, preferred_element_type=F32),
        jnp.dot(sq[:, 512:640], ones[:KV_W, :KV_W], preferred_element_type=F32)], axis=1)
    qkn = qk * lax.rsqrt(ss * (1.0 / HEAD_DIM) + EPS) * gqk_ref[...]
    q_ref[...] = qkn[:, :ATTN_W].astype(BF16)
    kv_ref[...] = jnp.concatenate([qkn[:, ATTN_W:], u[:, QK_W:QK_W + KV_W]], axis=1).astype(BF16)

    for c in range(F_W // 128):
        lanes = slice(c * 128, (c + 1) * 128)
        fs_s[c] = u[:, QK_W + KV_W + c * 128:QK_W + KV_W + (c + 1) * 128]
        for b in range(N2):
            f_ref[b, :, lanes] = fs_s[c, pl.ds(b, A_PER_TILE, stride=N2), :]


def _attn_kernel(q_ref, kv_ref, bias_ref, sink_ref, o_ref):
    step = pl.program_id(0)
    kv_meta = kv_ref[SEQ:L, :]
    ones_col = (lax.broadcasted_iota(jnp.int32, (N_KEYS, 16), 1) == 0).astype(BF16)

    units = []
    for j in range(QBLK_PER_STEP):
        n = step * QBLK_PER_STEP + j
        btype = jnp.where(n == 0, 0, jnp.where(n == N_QBLK - 1, 2, jnp.where(n == N_QBLK, 3, 1)))
        start = jnp.where(n == N_QBLK, 0, jnp.clip((n - 1) * BQ, 0, SEQ - BAND))
        start = pl.multiple_of(start, BQ)
        kv_cat = jnp.concatenate([kv_ref[pl.ds(start, BAND), :], kv_meta], axis=0)
        q_ok = lax.broadcasted_iota(jnp.int32, (BQ, 1), 0) < L - n * BQ
        units.extend((j, g, btype, kv_cat, q_ok) for g in range(N_KV))

    def scores(unit):
        j, g, _, kv_cat, q_ok = unit
        q = jnp.where(q_ok, q_ref[j * BQ:(j + 1) * BQ, :], 0)
        qg = jnp.concatenate(
            [q[:, (Q_PER_KV * g + hl) * HEAD_DIM:(Q_PER_KV * g + hl + 1) * HEAD_DIM]
             for hl in range(Q_PER_KV)], axis=0)
        k = kv_cat[:, g * HEAD_DIM:(g + 1) * HEAD_DIM]
        return lax.dot_general(k, qg, (((1,), (1,)), ((), ())), preferred_element_type=F32)

    st_next = scores(units[0])
    for u, (j, g, btype, kv_cat, _) in enumerate(units):
        st = st_next + bias_ref[btype, g]
        sink = sink_ref[g]
        m = jnp.maximum(jnp.max(st, axis=0, keepdims=True), sink)
        p = jnp.exp2(st - m)
        if u + 1 < len(units):
            st_next = scores(units[u + 1])
            p = jnp.concatenate([jnp.maximum(p[:16], st_next[:16] * 0.0 - 1.0), p[16:]], axis=0)
        v1 = jnp.concatenate([kv_cat[:, KV_W + g * HEAD_DIM:KV_W + (g + 1) * HEAD_DIM], ones_col], axis=1)
        ot = lax.dot_general(v1, p.astype(BF16), (((0,), (0,)), ((), ())), preferred_element_type=F32)
        denom = ot[HEAD_DIM:HEAD_DIM + 1] + jnp.exp2(sink - m)
        ot = ot[:HEAD_DIM] / denom
        for hp in range(Q_PER_KV // 2):
            h0 = Q_PER_KV * g + 2 * hp
            two = jnp.concatenate([ot[:, 2 * hp * BQ:(2 * hp + 1) * BQ],
                                   ot[:, (2 * hp + 1) * BQ:(2 * hp + 2) * BQ]], axis=0)
            o_ref[j * BQ:(j + 1) * BQ, h0 * HEAD_DIM:(h0 + 2) * HEAD_DIM] = two.T.astype(BF16)


def _dft_a_kernel(f_ref, wa_ref, cdft_ref, g_ref, gs_ref):
    cdft = cdft_ref[...]
    for k in range(B_PER_STEP):
        h = jnp.dot(wa_ref[...], f_ref[k].astype(BF16), preferred_element_type=F32)
        hr, hi = h[:N1].astype(BF16), h[N1:].astype(BF16)
        for c in range(F_W // 128):
            cols = slice(c * 128, (c + 1) * 128)
            gri = jnp.dot(jnp.concatenate([hr[:, cols], hi[:, cols]], axis=1), cdft,
                          preferred_element_type=F32)
            gs_ref[:, cols] = gri[:, :128]
            gs_ref[:, F_W + c * 128:F_W + (c + 1) * 128] = gri[:, 128:]
        g_ref[:, k, :, :] = gs_ref[...].astype(BF16).reshape(N1 // 2, 2, 2 * F_W)


def _attn_dft_a_kernel(q_ref, kv_ref, bias_ref, sink_ref, f_ref, wa_ref, cdft_ref,
                       gmix_ref, wout_ref, g2_ref, wg_ref, wu_ref, wd_ref,
                       o_ref, g_ref, wout_b, wg_b, wu_b, wd_b, gs_ref):
    _attn_kernel(q_ref, kv_ref, bias_ref, sink_ref, o_ref)
    _dft_a_kernel(f_ref, wa_ref, cdft_ref, g_ref, gs_ref)

    @pl.when(pl.program_id(0) < N_CAST)
    def _():
        for w_ref, gain_ref, w_b in ((wout_ref, gmix_ref, wout_b), (wg_ref, g2_ref, wg_b),
                                     (wu_ref, g2_ref, wu_b), (wd_ref, None, wd_b)):
            w_b[...] = _fold_gain_bf16(w_ref, gain_ref)


def _dft_c_kernel(g_ref, ec_ref, es_ref, y_ref, gs_ref):
    gs_ref[:, 2 * N2:, :] = jnp.zeros((TCP, K2P - 2 * N2, 2 * F_W), BF16)
    for i in range(TCP):
        gs_ref[i, :2 * N2, :] = g_ref[i].reshape(2 * N2, 2 * F_W)
        g = gs_ref[i]
        y = (jnp.dot(ec_ref[i], g[:, :F_W], preferred_element_type=F32)
             + jnp.dot(es_ref[i], g[:, F_W:], preferred_element_type=F32))
        y_ref[:, i, :, :] = _unit_rms(y[:2 * N2]).astype(BF16).reshape(N2, 2, F_W)


def _post_kernel(h1_ref, a_ref, y_ref, wout_ref, wg_ref, wu_ref, wd_ref, gfin_ref, o_ref):
    y = y_ref[...].reshape(TM, F_W)
    a = a_ref[...]
    h2 = (h1_ref[...]
          + jnp.dot(a, wout_ref[:ATTN_W, :], preferred_element_type=F32) * _rms_scale(a.astype(F32))
          + jnp.dot(y, wout_ref[ATTN_W:, :], preferred_element_type=F32))
    h3 = _swiglu_half(h2, wg_ref, wu_ref, wd_ref)
    o_ref[...] = _rms(h3, gfin_ref[...])


def _resident(shape):
    zeros = (0,) * len(shape)
    return pl.BlockSpec(shape, lambda *_: zeros, pipeline_mode=pl.Buffered(1))


def _params():
    return pltpu.CompilerParams(dimension_semantics=("arbitrary",), vmem_limit_bytes=VMEM_LIMIT)


def _row(g):
    return g.astype(F32).reshape(1, -1)


def kernel(x, meta_tokens, ffn1_norm, ffn1_w_gate, ffn1_w_up, ffn1_w_down, mix_norm, w_in, q_norm, k_norm, sink, attn_out_norm, fourier_out_norm, w_out, ffn2_norm, ffn2_w_gate, ffn2_w_up, ffn2_w_down, final_norm):
    assert x.shape == (1, SEQ, D_MODEL) and x.dtype == F32
    assert A_PER_TILE * N2 == TM and A_PER_TILE * N_TILES == N1
    x2 = x.reshape(SEQ, D_MODEL)
    gqk = jnp.concatenate([jnp.tile(q_norm.astype(F32), N_HEADS) * (HEAD_DIM ** -0.5 * LOG2E),
                           jnp.tile(k_norm.astype(F32), N_KV)]).reshape(1, QK_W)
    bf = lambda w: w.astype(BF16)

    tile_idx = lambda s: jnp.maximum(s - N_CAST, 0)
    row_tile = lambda w: pl.BlockSpec((TM, w), lambda s: (tile_idx(s), 0))
    tile = lambda w: pl.BlockSpec((TM, w), lambda s: (s, 0))
    row_sets = lambda w: pl.BlockSpec((N2, A_PER_TILE, w), lambda s: (0, tile_idx(s), 0))
    cast_rows = lambda r, c: pl.BlockSpec((r // N_CAST, c), lambda s: (jnp.minimum(s, N_CAST - 1), 0))
    gains = jnp.broadcast_to(
        jnp.stack([ffn1_norm, mix_norm, jnp.concatenate([attn_out_norm, fourier_out_norm]), ffn2_norm])
        .astype(F32)[:, :, None], (4, D_MODEL, 128))
    gain_rows = lambda k: pl.BlockSpec((None, D_MODEL // N_CAST, 128), lambda s: (k, jnp.minimum(s, N_CAST - 1), 0))
    bf16_copy = lambda r, c: pltpu.VMEM((r, c), BF16)
    ffn_in = [cast_rows(D_MODEL, D_FF), cast_rows(D_MODEL, D_FF), cast_rows(D_FF, D_MODEL)]
    ffn_scratch = [bf16_copy(D_MODEL, D_FF), bf16_copy(D_MODEL, D_FF), bf16_copy(D_FF, D_MODEL)]
    f32 = lambda w: w.astype(F32)

    h1, q, kv, f = pl.pallas_call(
        _pre_kernel,
        grid=(N_CAST + N_TILES,),
        in_specs=[row_tile(D_MODEL), _resident((N_META, D_MODEL)), gain_rows(0), *ffn_in,
                  gain_rows(1), cast_rows(D_MODEL, IN_W), _resident((1, QK_W)),
                  _resident((256, 256))],
        out_specs=[row_tile(D_MODEL), row_tile(ATTN_W), row_tile(2 * KV_W), row_sets(F_W)],
        out_shape=[jax.ShapeDtypeStruct((L, D_MODEL), F32), jax.ShapeDtypeStruct((L, ATTN_W), BF16),
                   jax.ShapeDtypeStruct((L, 2 * KV_W), BF16), jax.ShapeDtypeStruct((N2, N1, F_W), F32)],
        scratch_shapes=ffn_scratch + [bf16_copy(D_MODEL, IN_W), pltpu.VMEM((F_W // 128, TM, 128), F32)],
        compiler_params=_params(),
        name="pre",
    )(x2, meta_tokens.astype(F32), gains, f32(ffn1_w_gate), f32(ffn1_w_up), f32(ffn1_w_down), gains,
      f32(w_in), gqk, _ones_blockdiag())

    sink_rows = (jnp.repeat(sink.astype(F32).reshape(N_KV, Q_PER_KV), BQ, axis=1) * LOG2E)[:, None, :]

    q_rows = pl.BlockSpec((QBLK_PER_STEP * BQ, ATTN_W), lambda s: (s, 0))
    n_steps = pl.cdiv(N_QBLK + 1, QBLK_PER_STEP)
    assert n_steps == pl.cdiv(N2, B_PER_STEP) and n_steps >= N_CAST
    post_w_blocks = [cast_rows(D_MODEL, D_MODEL), *ffn_in]
    a_out, g, wout_b, wg2_b, wu2_b, wd2_b = pl.pallas_call(
        _attn_dft_a_kernel,
        grid=(n_steps,),
        in_specs=[q_rows, _resident((L, 2 * KV_W)), _resident((4, N_KV, N_KEYS, Q_PER_KV * BQ)),
                  _resident((N_KV, 1, Q_PER_KV * BQ)),
                  pl.BlockSpec((B_PER_STEP, N1, F_W), lambda s: (s, 0, 0)),
                  _resident((2 * N1, N1)), _resident((256, 256)),
                  gain_rows(2), post_w_blocks[0], gain_rows(3), *post_w_blocks[1:]],
        out_specs=[q_rows, pl.BlockSpec((N1 // 2, B_PER_STEP, 2, 2 * F_W), lambda s: (0, s, 0, 0)),
                   *post_w_blocks],
        out_shape=[jax.ShapeDtypeStruct((L, ATTN_W), BF16),
                   jax.ShapeDtypeStruct((N1 // 2, N2, 2, 2 * F_W), BF16),
                   jax.ShapeDtypeStruct((D_MODEL, D_MODEL), BF16), jax.ShapeDtypeStruct((D_MODEL, D_FF), BF16),
                   jax.ShapeDtypeStruct((D_MODEL, D_FF), BF16), jax.ShapeDtypeStruct((D_FF, D_MODEL), BF16)],
        scratch_shapes=[pltpu.VMEM((N1, 2 * F_W), F32)],
        compiler_params=_params(),
        name="attn_dft_a",
    )(q, kv, jnp.asarray(_attn_bias_const()), sink_rows,
      f, bf(jnp.asarray(_stage_a_dft())), bf(jnp.asarray(_channel_dft())),
      gains, f32(w_out), gains, f32(ffn2_w_gate), f32(ffn2_w_up), f32(ffn2_w_down))

    ec, es = _stage_c_dft()
    y = pl.pallas_call(
        _dft_c_kernel,
        grid=(N1 // (2 * TCP),),
        in_specs=[pl.BlockSpec((TCP, N2, 2, 2 * F_W), lambda i: (i, 0, 0, 0)),
                  pl.BlockSpec((TCP, K2P, K2P), lambda i: (i, 0, 0)),
                  pl.BlockSpec((TCP, K2P, K2P), lambda i: (i, 0, 0))],
        out_specs=pl.BlockSpec((N2, TCP, 2, F_W), lambda i: (0, i, 0, 0)),
        out_shape=jax.ShapeDtypeStruct((N2, N1 // 2, 2, F_W), BF16),
        scratch_shapes=[pltpu.VMEM((TCP, K2P, 2 * F_W), BF16)],
        compiler_params=_params(),
        name="dft_c",
    )(g, bf(jnp.asarray(ec)), bf(jnp.asarray(es)))

    out = pl.pallas_call(
        _post_kernel,
        grid=(N_TILES,),
        in_specs=[tile(D_MODEL), tile(ATTN_W), pl.BlockSpec((TM // 2, 2, F_W), lambda s: (s, 0, 0)),
                  _resident((D_MODEL, D_MODEL)), _resident((D_MODEL, D_FF)), _resident((D_MODEL, D_FF)),
                  _resident((D_FF, D_MODEL)), _resident((1, D_MODEL))],
        out_specs=tile(D_MODEL),
        out_shape=jax.ShapeDtypeStruct((SEQ, D_MODEL), F32),
        compiler_params=_params(),
        name="post",
    )(h1, a_out, y.reshape(L // 2, 2, F_W), wout_b, wg2_b, wu2_b, wd2_b, _row(final_norm))
    return out.reshape(1, SEQ, D_MODEL)
```

```python
import numpy as np
import jax
import jax.numpy as jnp
from jax import lax
from jax.experimental import pallas as pl
from jax.experimental.pallas import tpu as pltpu

F32 = jnp.float32
BF16 = jnp.bfloat16

D_MODEL = 1024
SEQ = 16384
N_META = 16
L = SEQ + N_META
HEAD_DIM = 64
N_HEADS = 8
N_KV = 2
Q_PER_KV = N_HEADS // N_KV
ATTN_W = N_HEADS * HEAD_DIM
KV_W = N_KV * HEAD_DIM
F_W = 512
F_GROUP = 64
IN_W = ATTN_W + 2 * KV_W + F_W
QK_W = ATTN_W + KV_W
WINDOW = 128
D_FF = 2816
N_CAST = 8
CAST_RING = 3
EPS = 1e-6
NEG = -1e30
LOG2E = 1.4426950408889634

TM = 656
N_TILES = L // TM
META_ROW0 = SEQ - (N_TILES - 1) * TM

BQ = 128
N_QBLK = SEQ // BQ
QBLK_PER_STEP = 12
BAND = BQ + 2 * WINDOW
N_KEYS = BAND + N_META

N1 = 200
N2 = 82
A_PER_TILE = TM // N2
K2P = 176
B_PER_STEP = 8
TCP = 25

VMEM_LIMIT = 60 * 1024 * 1024


def _ones_blockdiag():
    m = np.kron(np.eye(4), np.ones((HEAD_DIM, HEAD_DIM)))
    return jnp.asarray(m, BF16)


def _channel_dft():
    c = np.arange(F_GROUP)
    ang = 2.0 * np.pi * np.outer(c, c) / F_GROUP
    cos = np.kron(np.eye(2), np.cos(ang) / 8.0)
    sin = np.kron(np.eye(2), np.sin(ang) / 8.0)
    return np.block([[cos, -sin], [sin, cos]]).astype(np.float32)


def _stage_a_dft():
    a = np.arange(N1, dtype=np.int64)
    m = (N2 * np.outer(a + N_META, a)) % L
    ang = 2.0 * np.pi * m / L
    return np.concatenate([np.cos(ang), -np.sin(ang)], axis=0).astype(np.float32)


def _stage_c_dft():
    c = np.arange(N1, dtype=np.int64)[:, None, None]
    d = np.arange(N2, dtype=np.int64)[None, :, None]
    b = np.arange(N2, dtype=np.int64)[None, None, :]
    m = ((c + N1 * d + N_META) * (b + N_META)) % L
    ang = 2.0 * np.pi * m / L
    ec = np.zeros((N1 // 2, K2P, K2P), np.float32)
    es = np.zeros((N1 // 2, K2P, K2P), np.float32)
    for e in range(2):
        ec[:, e:2 * N2:2, e:2 * N2:2] = (np.cos(ang) / np.sqrt(L))[e::2]
        es[:, e:2 * N2:2, e:2 * N2:2] = (np.sin(ang) / np.sqrt(L))[e::2]
    return ec, es


def _attn_bias_const():
    slopes = 2.0 ** -(np.arange(N_HEADS) + 1.0)
    i = np.arange(BQ)[:, None]
    j = np.arange(BAND)[None, :]
    m = np.arange(N_META)[None, :]
    out = np.full((4, N_KV, Q_PER_KV * BQ, N_KEYS), NEG, np.float64)
    for t in range(4):
        if t < 3:
            dist = np.abs(t * WINDOW + i - j)
            band_ok = dist <= WINDOW
            dist_m = np.minimum(t * WINDOW + N_META + i - m, WINDOW) if t == 0 else np.full((BQ, N_META), WINDOW)
        else:
            dist = N_META + j - i
            band_ok = (dist <= WINDOW) & (i < N_META)
            dist_m = np.abs(i - m)
        for h in range(N_HEADS):
            g, hl = divmod(h, Q_PER_KV)
            rows = slice(hl * BQ, (hl + 1) * BQ)
            band = np.where(band_ok, -slopes[h] * dist, NEG)
            if t == 3:
                band = np.where(i < N_META, band, 0.0)
            out[t, g, rows, :BAND] = band
            out[t, g, rows, BAND:BAND + N_META] = -slopes[h] * dist_m
    return np.ascontiguousarray(LOG2E * out.transpose(0, 1, 3, 2)).astype(np.float32)


def _rms_scale(x):
    return lax.rsqrt(jnp.mean(x * x, axis=-1, keepdims=True) + EPS)


def _unit_rms(x):
    return x * _rms_scale(x)


def _rms(x, g):
    return _unit_rms(x) * g


def _swiglu_half(x, wg_s, wu_s, wd_s):
    r = _rms_scale(x)
    xb = x.astype(BF16)
    gate = jnp.dot(xb, wg_s[...], preferred_element_type=F32) * r
    up = jnp.dot(xb, wu_s[...], preferred_element_type=F32) * r
    act = (gate * jax.nn.sigmoid(gate) * up).astype(BF16)
    return x + 0.5 * jnp.dot(act, wd_s[...], preferred_element_type=F32)


def _cast_weights(s, f32_refs, gain_refs, bf16_scratch):
    for w_ref, g_ref, w_s in zip(f32_refs, gain_refs, bf16_scratch):
        rows = w_ref.shape[0]
        w_s[pl.ds(pl.multiple_of(s * rows, rows), rows), :] = _fold_gain_bf16(w_ref, g_ref)


def _fold_gain_bf16(w_ref, g_ref):
    w = w_ref[...]
    if g_ref is not None:
        w = w * jnp.tile(g_ref[...], (1, w.shape[1] // g_ref.shape[1]))
    return w.astype(BF16)


def _pre_kernel(x_ref, meta_ref, g1_ref, wg_hbm, wu_hbm, wd_hbm, gm_ref, win_hbm, gqk_ref,
                ones_ref, h1_ref, q_ref, kv_ref, f_ref, wg_s, wu_s, wd_s, win_s, fs_s,
                wg_r, wu_r, wd_r, win_r, sems):
    s = pl.program_id(0)
    hbm = (wg_hbm, wu_hbm, wd_hbm, win_hbm)
    rings = (wg_r, wu_r, wd_r, win_r)

    def block_copy(k, blk):
        rows = rings[k].shape[1]
        slot = blk % CAST_RING
        src = hbm[k].at[pl.ds(pl.multiple_of(blk * rows, rows), rows), :]
        return pltpu.make_async_copy(src, rings[k].at[slot], sems.at[k, slot])

    @pl.when(s == 0)
    def _():
        for blk in range(CAST_RING - 1):
            for k in range(len(hbm)):
                block_copy(k, blk).start()

    @pl.when(s < N_CAST)
    def _():
        @pl.when(s + CAST_RING - 1 < N_CAST)
        def _():
            for k in range(len(hbm)):
                block_copy(k, s + CAST_RING - 1).start()

        for k in range(len(hbm)):
            block_copy(k, s).wait()
        slot = s % CAST_RING
        _cast_weights(s, [r.at[slot] for r in rings], (g1_ref, g1_ref, None, gm_ref),
                      (wg_s, wu_s, wd_s, win_s))

    @pl.when(s >= N_CAST)
    def _():
        _pre_tile(s - N_CAST, x_ref, meta_ref, gqk_ref, ones_ref,
                  h1_ref, q_ref, kv_ref, f_ref, wg_s, wu_s, wd_s, win_s, fs_s)


def _pre_tile(i, x_ref, meta_ref, gqk_ref, ones_ref,
              h1_ref, q_ref, kv_ref, f_ref, wg_s, wu_s, wd_s, win_s, fs_s):
    tail = jnp.where(i == N_TILES - 1, meta_ref[...], x_ref[META_ROW0:, :])
    x = jnp.concatenate([x_ref[:META_ROW0, :], tail], axis=0)

    h1 = _swiglu_half(x, wg_s, wu_s, wd_s)
    h1_ref[...] = h1

    u = jnp.dot(h1.astype(BF16), win_s[...], preferred_element_type=F32) * _rms_scale(h1)

    qk = u[:, :QK_W]
    sq = (qk * qk).astype(BF16)
    ones = ones_ref[...]
    ss = jnp.concatenate([
        jnp.dot(sq[:, 0:256], ones, preferred_element_type=F32),
        jnp.dot(sq[:, 256:512], ones, preferred_element_type=F32),
        jnp.dot(sq[:, 512:640], ones[:KV_W, :KV_W], preferred_element_type=F32)], axis=1)
    qkn = qk * lax.rsqrt(ss * (1.0 / HEAD_DIM) + EPS) * gqk_ref[...]
    q_ref[...] = qkn[:, :ATTN_W].astype(BF16)
    kv_ref[...] = jnp.concatenate([qkn[:, ATTN_W:], u[:, QK_W:QK_W + KV_W]], axis=1).astype(BF16)

    for c in range(F_W // 128):
        lanes = slice(c * 128, (c + 1) * 128)
        fs_s[c] = u[:, QK_W + KV_W + c * 128:QK_W + KV_W + (c + 1) * 128]
        for b in range(N2):
            f_ref[b, :, lanes] = fs_s[c, pl.ds(b, A_PER_TILE, stride=N2), :]


def _attn_kernel(q_ref, kv_ref, bias_ref, sink_ref, o_ref):
    step = pl.program_id(0)
    kv_meta = kv_ref[SEQ:L, :]
    ones_col = (lax.broadcasted_iota(jnp.int32, (N_KEYS, 16), 1) == 0).astype(BF16)

    units = []
    for j in range(QBLK_PER_STEP):
        n = step * QBLK_PER_STEP + j
        btype = jnp.where(n == 0, 0, jnp.where(n == N_QBLK - 1, 2, jnp.where(n == N_QBLK, 3, 1)))
        start = jnp.where(n == N_QBLK, 0, jnp.clip((n - 1) * BQ, 0, SEQ - BAND))
        start = pl.multiple_of(start, BQ)
        kv_cat = jnp.concatenate([kv_ref[pl.ds(start, BAND), :], kv_meta], axis=0)
        q_ok = lax.broadcasted_iota(jnp.int32, (BQ, 1), 0) < L - n * BQ
        units.extend((j, g, btype, kv_cat, q_ok) for g in range(N_KV))

    def scores(unit):
        j, g, _, kv_cat, q_ok = unit
        q = jnp.where(q_ok, q_ref[j * BQ:(j + 1) * BQ, :], 0)
        qg = jnp.concatenate(
            [q[:, (Q_PER_KV * g + hl) * HEAD_DIM:(Q_PER_KV * g + hl + 1) * HEAD_DIM]
             for hl in range(Q_PER_KV)], axis=0)
        k = kv_cat[:, g * HEAD_DIM:(g + 1) * HEAD_DIM]
        return lax.dot_general(k, qg, (((1,), (1,)), ((), ())), preferred_element_type=F32)

    st_next = scores(units[0])
    for u, (j, g, btype, kv_cat, _) in enumerate(units):
        st = st_next + bias_ref[btype, g]
        sink = sink_ref[g]
        m = jnp.maximum(jnp.max(st, axis=0, keepdims=True), sink)
        p = jnp.exp2(st - m)
        if u + 1 < len(units):
            st_next = scores(units[u + 1])
            p = jnp.concatenate([jnp.maximum(p[:16], st_next[:16] * 0.0 - 1.0), p[16:]], axis=0)
        v1 = jnp.concatenate([kv_cat[:, KV_W + g * HEAD_DIM:KV_W + (g + 1) * HEAD_DIM], ones_col], axis=1)
        ot = lax.dot_general(v1, p.astype(BF16), (((0,), (0,)), ((), ())), preferred_element_type=F32)
        denom = ot[HEAD_DIM:HEAD_DIM + 1] + jnp.exp2(sink - m)
        ot = ot[:HEAD_DIM] / denom
        for hp in range(Q_PER_KV // 2):
            h0 = Q_PER_KV * g + 2 * hp
            two = jnp.concatenate([ot[:, 2 * hp * BQ:(2 * hp + 1) * BQ],
                                   ot[:, (2 * hp + 1) * BQ:(2 * hp + 2) * BQ]], axis=0)
            o_ref[j * BQ:(j + 1) * BQ, h0 * HEAD_DIM:(h0 + 2) * HEAD_DIM] = two.T.astype(BF16)


def _dft_a_kernel(f_ref, wa_ref, cdft_ref, g_ref, gs_ref):
    cdft = cdft_ref[...]
    for k in range(B_PER_STEP):
        h = jnp.dot(wa_ref[...], f_ref[k].astype(BF16), preferred_element_type=F32)
        hr, hi = h[:N1].astype(BF16), h[N1:].astype(BF16)
        for c in range(F_W // 128):
            cols = slice(c * 128, (c + 1) * 128)
            gri = jnp.dot(jnp.concatenate([hr[:, cols], hi[:, cols]], axis=1), cdft,
                          preferred_element_type=F32)
            gs_ref[:, cols] = gri[:, :128]
            gs_ref[:, F_W + c * 128:F_W + (c + 1) * 128] = gri[:, 128:]
        g_ref[:, k, :, :] = gs_ref[...].astype(BF16).reshape(N1 // 2, 2, 2 * F_W)


def _attn_dft_a_kernel(q_ref, kv_ref, bias_ref, sink_ref, f_ref, wa_ref, cdft_ref,
                       gmix_ref, wout_ref, g2_ref, wg_ref, wu_ref, wd_ref,
                       o_ref, g_ref, wout_b, wg_b, wu_b, wd_b, gs_ref):
    _attn_kernel(q_ref, kv_ref, bias_ref, sink_ref, o_ref)
    _dft_a_kernel(f_ref, wa_ref, cdft_ref, g_ref, gs_ref)

    @pl.when(pl.program_id(0) < N_CAST)
    def _():
        for w_ref, gain_ref, w_b in ((wout_ref, gmix_ref, wout_b), (wg_ref, g2_ref, wg_b),
                                     (wu_ref, g2_ref, wu_b), (wd_ref, None, wd_b)):
            w_b[...] = _fold_gain_bf16(w_ref, gain_ref)


def _dft_c_kernel(g_ref, ec_ref, es_ref, y_ref, gs_ref):
    gs_ref[:, 2 * N2:, :] = jnp.zeros((TCP, K2P - 2 * N2, 2 * F_W), BF16)
    for i in range(TCP):
        gs_ref[i, :2 * N2, :] = g_ref[i].reshape(2 * N2, 2 * F_W)
        g = gs_ref[i]
        y = (jnp.dot(ec_ref[i], g[:, :F_W], preferred_element_type=F32)
             + jnp.dot(es_ref[i], g[:, F_W:], preferred_element_type=F32))
        y_ref[:, i, :, :] = _unit_rms(y[:2 * N2]).astype(BF16).reshape(N2, 2, F_W)


def _post_kernel(h1_ref, a_ref, y_ref, wout_ref, wg_ref, wu_ref, wd_ref, gfin_ref, o_ref):
    y = y_ref[...].reshape(TM, F_W)
    a = a_ref[...]
    h2 = (h1_ref[...]
          + jnp.dot(a, wout_ref[:ATTN_W, :], preferred_element_type=F32) * _rms_scale(a.astype(F32))
          + jnp.dot(y, wout_ref[ATTN_W:, :], preferred_element_type=F32))
    h3 = _swiglu_half(h2, wg_ref, wu_ref, wd_ref)
    o_ref[...] = _rms(h3, gfin_ref[...])


def _resident(shape):
    zeros = (0,) * len(shape)
    return pl.BlockSpec(shape, lambda *_: zeros, pipeline_mode=pl.Buffered(1))


def _params():
    return pltpu.CompilerParams(dimension_semantics=("arbitrary",), vmem_limit_bytes=VMEM_LIMIT)


def _row(g):
    return g.astype(F32).reshape(1, -1)


def kernel(x, meta_tokens, ffn1_norm, ffn1_w_gate, ffn1_w_up, ffn1_w_down, mix_norm, w_in, q_norm, k_norm, sink, attn_out_norm, fourier_out_norm, w_out, ffn2_norm, ffn2_w_gate, ffn2_w_up, ffn2_w_down, final_norm):
    assert x.shape == (1, SEQ, D_MODEL) and x.dtype == F32
    assert A_PER_TILE * N2 == TM and A_PER_TILE * N_TILES == N1
    x2 = x.reshape(SEQ, D_MODEL)
    gqk = jnp.concatenate([jnp.tile(q_norm.astype(F32), N_HEADS) * (HEAD_DIM ** -0.5 * LOG2E),
                           jnp.tile(k_norm.astype(F32), N_KV)]).reshape(1, QK_W)
    bf = lambda w: w.astype(BF16)

    tile_idx = lambda s: jnp.maximum(s - N_CAST, 0)
    row_tile = lambda w: pl.BlockSpec((TM, w), lambda s: (tile_idx(s), 0))
    tile = lambda w: pl.BlockSpec((TM, w), lambda s: (s, 0))
    row_sets = lambda w: pl.BlockSpec((N2, A_PER_TILE, w), lambda s: (0, tile_idx(s), 0))
    cast_rows = lambda r, c: pl.BlockSpec((r // N_CAST, c), lambda s: (jnp.minimum(s, N_CAST - 1), 0))
    gains = jnp.broadcast_to(
        jnp.stack([ffn1_norm, mix_norm, jnp.concatenate([attn_out_norm, fourier_out_norm]), ffn2_norm])
        .astype(F32)[:, :, None], (4, D_MODEL, 128))
    gain_rows = lambda k: pl.BlockSpec((None, D_MODEL // N_CAST, 128), lambda s: (k, jnp.minimum(s, N_CAST - 1), 0))
    bf16_copy = lambda r, c: pltpu.VMEM((r, c), BF16)
    ffn_in = [cast_rows(D_MODEL, D_FF), cast_rows(D_MODEL, D_FF), cast_rows(D_FF, D_MODEL)]
    in_hbm = pl.BlockSpec(memory_space=pl.ANY)
    ring = lambda r, c: pltpu.VMEM((CAST_RING, r // N_CAST, c), F32)
    ffn_scratch = [bf16_copy(D_MODEL, D_FF), bf16_copy(D_MODEL, D_FF), bf16_copy(D_FF, D_MODEL)]
    f32 = lambda w: w.astype(F32)

    h1, q, kv, f = pl.pallas_call(
        _pre_kernel,
        grid=(N_CAST + N_TILES,),
        in_specs=[row_tile(D_MODEL), _resident((N_META, D_MODEL)), gain_rows(0), *[in_hbm] * 3,
                  gain_rows(1), in_hbm, _resident((1, QK_W)),
                  _resident((256, 256))],
        out_specs=[row_tile(D_MODEL), row_tile(ATTN_W), row_tile(2 * KV_W), row_sets(F_W)],
        out_shape=[jax.ShapeDtypeStruct((L, D_MODEL), F32), jax.ShapeDtypeStruct((L, ATTN_W), BF16),
                   jax.ShapeDtypeStruct((L, 2 * KV_W), BF16), jax.ShapeDtypeStruct((N2, N1, F_W), F32)],
        scratch_shapes=ffn_scratch + [bf16_copy(D_MODEL, IN_W), pltpu.VMEM((F_W // 128, TM, 128), F32),
                                      ring(D_MODEL, D_FF), ring(D_MODEL, D_FF), ring(D_FF, D_MODEL),
                                      ring(D_MODEL, IN_W), pltpu.SemaphoreType.DMA((4, CAST_RING))],
        compiler_params=_params(),
        name="pre",
    )(x2, meta_tokens.astype(F32), gains, f32(ffn1_w_gate), f32(ffn1_w_up), f32(ffn1_w_down), gains,
      f32(w_in), gqk, _ones_blockdiag())

    sink_rows = (jnp.repeat(sink.astype(F32).reshape(N_KV, Q_PER_KV), BQ, axis=1) * LOG2E)[:, None, :]

    q_rows = pl.BlockSpec((QBLK_PER_STEP * BQ, ATTN_W), lambda s: (s, 0))
    n_steps = pl.cdiv(N_QBLK + 1, QBLK_PER_STEP)
    assert n_steps == pl.cdiv(N2, B_PER_STEP) and n_steps >= N_CAST
    post_w_blocks = [cast_rows(D_MODEL, D_MODEL), *ffn_in]
    a_out, g, wout_b, wg2_b, wu2_b, wd2_b = pl.pallas_call(
        _attn_dft_a_kernel,
        grid=(n_steps,),
        in_specs=[q_rows, _resident((L, 2 * KV_W)), _resident((4, N_KV, N_KEYS, Q_PER_KV * BQ)),
                  _resident((N_KV, 1, Q_PER_KV * BQ)),
                  pl.BlockSpec((B_PER_STEP, N1, F_W), lambda s: (s, 0, 0)),
                  _resident((2 * N1, N1)), _resident((256, 256)),
                  gain_rows(2), post_w_blocks[0], gain_rows(3), *post_w_blocks[1:]],
        out_specs=[q_rows, pl.BlockSpec((N1 // 2, B_PER_STEP, 2, 2 * F_W), lambda s: (0, s, 0, 0)),
                   *post_w_blocks],
        out_shape=[jax.ShapeDtypeStruct((L, ATTN_W), BF16),
                   jax.ShapeDtypeStruct((N1 // 2, N2, 2, 2 * F_W), BF16),
                   jax.ShapeDtypeStruct((D_MODEL, D_MODEL), BF16), jax.ShapeDtypeStruct((D_MODEL, D_FF), BF16),
                   jax.ShapeDtypeStruct((D_MODEL, D_FF), BF16), jax.ShapeDtypeStruct((D_FF, D_MODEL), BF16)],
        scratch_shapes=[pltpu.VMEM((N1, 2 * F_W), F32)],
        compiler_params=_params(),
        name="attn_dft_a",
    )(q, kv, jnp.asarray(_attn_bias_const()), sink_rows,
      f, bf(jnp.asarray(_stage_a_dft())), bf(jnp.asarray(_channel_dft())),
      gains, f32(w_out), gains, f32(ffn2_w_gate), f32(ffn2_w_up), f32(ffn2_w_down))

    ec, es = _stage_c_dft()
    y = pl.pallas_call(
        _dft_c_kernel,
        grid=(N1 // (2 * TCP),),
        in_specs=[pl.BlockSpec((TCP, N2, 2, 2 * F_W), lambda i: (i, 0, 0, 0)),
                  pl.BlockSpec((TCP, K2P, K2P), lambda i: (i, 0, 0)),
                  pl.BlockSpec((TCP, K2P, K2P), lambda i: (i, 0, 0))],
        out_specs=pl.BlockSpec((N2, TCP, 2, F_W), lambda i: (0, i, 0, 0)),
        out_shape=jax.ShapeDtypeStruct((N2, N1 // 2, 2, F_W), BF16),
        scratch_shapes=[pltpu.VMEM((TCP, K2P, 2 * F_W), BF16)],
        compiler_params=_params(),
        name="dft_c",
    )(g, bf(jnp.asarray(ec)), bf(jnp.asarray(es)))

    out = pl.pallas_call(
        _post_kernel,
        grid=(N_TILES,),
        in_specs=[tile(D_MODEL), tile(ATTN_W), pl.BlockSpec((TM // 2, 2, F_W), lambda s: (s, 0, 0)),
                  _resident((D_MODEL, D_MODEL)), _resident((D_MODEL, D_FF)), _resident((D_MODEL, D_FF)),
                  _resident((D_FF, D_MODEL)), _resident((1, D_MODEL))],
        out_specs=tile(D_MODEL),
        out_shape=jax.ShapeDtypeStruct((SEQ, D_MODEL), F32),
        compiler_params=_params(),
        name="post",
    )(h1, a_out, y.reshape(L // 2, 2, F_W), wout_b, wg2_b, wu2_b, wd2_b, _row(final_norm))
    return out.reshape(1, SEQ, D_MODEL)
```

```python
import numpy as np
import jax
import jax.numpy as jnp
from jax import lax
from jax.experimental import pallas as pl
from jax.experimental.pallas import tpu as pltpu

F32 = jnp.float32
BF16 = jnp.bfloat16

D_MODEL = 1024
SEQ = 16384
N_META = 16
L = SEQ + N_META
HEAD_DIM = 64
N_HEADS = 8
N_KV = 2
Q_PER_KV = N_HEADS // N_KV
ATTN_W = N_HEADS * HEAD_DIM
KV_W = N_KV * HEAD_DIM
F_W = 512
F_GROUP = 64
IN_W = ATTN_W + 2 * KV_W + F_W
QK_W = ATTN_W + KV_W
WINDOW = 128
D_FF = 2816
FF_CHUNK = 256
N_CAST = 8
EPS = 1e-6
NEG = -1e30
LOG2E = 1.4426950408889634

TM = 656
N_TILES = L // TM
META_ROW0 = SEQ - (N_TILES - 1) * TM

BQ = 128
N_QBLK = SEQ // BQ
QBLK_PER_STEP = 12
BAND = BQ + 2 * WINDOW
N_KEYS = BAND + N_META

N1 = 200
N2 = 82
A_PER_TILE = TM // N2
K2P = 176
B_PER_STEP = 8
TCP = 25

VMEM_LIMIT = 56 * 1024 * 1024


def _ones_blockdiag():
    m = np.kron(np.eye(4), np.ones((HEAD_DIM, HEAD_DIM)))
    return jnp.asarray(m, BF16)


def _channel_dft():
    c = np.arange(F_GROUP)
    ang = 2.0 * np.pi * np.outer(c, c) / F_GROUP
    cos = np.kron(np.eye(2), np.cos(ang) / 8.0)
    sin = np.kron(np.eye(2), np.sin(ang) / 8.0)
    return np.block([[cos, -sin], [sin, cos]]).astype(np.float32)


def _stage_a_dft():
    a = np.arange(N1, dtype=np.int64)
    m = (N2 * np.outer(a + N_META, a)) % L
    ang = 2.0 * np.pi * m / L
    return np.concatenate([np.cos(ang), -np.sin(ang)], axis=0).astype(np.float32)


def _stage_c_dft():
    c = np.arange(N1, dtype=np.int64)[:, None, None]
    d = np.arange(N2, dtype=np.int64)[None, :, None]
    b = np.arange(N2, dtype=np.int64)[None, None, :]
    m = ((c + N1 * d + N_META) * (b + N_META)) % L
    ang = 2.0 * np.pi * m / L
    ec = np.zeros((N1 // 2, K2P, K2P), np.float32)
    es = np.zeros((N1 // 2, K2P, K2P), np.float32)
    for e in range(2):
        ec[:, e:2 * N2:2, e:2 * N2:2] = (np.cos(ang) / np.sqrt(L))[e::2]
        es[:, e:2 * N2:2, e:2 * N2:2] = (np.sin(ang) / np.sqrt(L))[e::2]
    return ec, es


def _attn_bias_const():
    slopes = 2.0 ** -(np.arange(N_HEADS) + 1.0)
    i = np.arange(BQ)[:, None]
    j = np.arange(BAND)[None, :]
    m = np.arange(N_META)[None, :]
    out = np.full((4, N_KV, Q_PER_KV * BQ, N_KEYS), NEG, np.float64)
    for t in range(4):
        if t < 3:
            dist = np.abs(t * WINDOW + i - j)
            band_ok = dist <= WINDOW
            dist_m = np.minimum(t * WINDOW + N_META + i - m, WINDOW) if t == 0 else np.full((BQ, N_META), WINDOW)
        else:
            dist = N_META + j - i
            band_ok = (dist <= WINDOW) & (i < N_META)
            dist_m = np.abs(i - m)
        for h in range(N_HEADS):
            g, hl = divmod(h, Q_PER_KV)
            rows = slice(hl * BQ, (hl + 1) * BQ)
            band = np.where(band_ok, -slopes[h] * dist, NEG)
            if t == 3:
                band = np.where(i < N_META, band, 0.0)
            out[t, g, rows, :BAND] = band
            out[t, g, rows, BAND:BAND + N_META] = -slopes[h] * dist_m
    return np.ascontiguousarray(LOG2E * out.transpose(0, 1, 3, 2)).astype(np.float32)


def _rms_scale(x):
    return lax.rsqrt(jnp.mean(x * x, axis=-1, keepdims=True) + EPS)


def _unit_rms(x):
    return x * _rms_scale(x)


def _rms(x, g):
    return _unit_rms(x) * g


def _swiglu_half(x, wg_s, wu_s, wd_s):
    r = _rms_scale(x)
    xb = x.astype(BF16)
    acts = []
    for c in range(0, D_FF, FF_CHUNK):
        gate = jnp.dot(xb, wg_s[:, c:c + FF_CHUNK], preferred_element_type=F32) * r
        up = jnp.dot(xb, wu_s[:, c:c + FF_CHUNK], preferred_element_type=F32) * r
        acts.append((gate * jax.nn.sigmoid(gate) * up).astype(BF16))
    act = jnp.concatenate(acts, axis=1)
    return x + 0.5 * jnp.dot(act, wd_s[...], preferred_element_type=F32)


def _cast_weights(s, f32_refs, gain_refs, bf16_scratch):
    for w_ref, g_ref, w_s in zip(f32_refs, gain_refs, bf16_scratch):
        rows = w_ref.shape[0]
        w_s[pl.ds(pl.multiple_of(s * rows, rows), rows), :] = _fold_gain_bf16(w_ref, g_ref)


def _fold_gain_bf16(w_ref, g_ref):
    w = w_ref[...]
    if g_ref is not None:
        w = w * jnp.tile(g_ref[...], (1, w.shape[1] // g_ref.shape[1]))
    return w.astype(BF16)


def _pre_kernel(x_ref, meta_ref, g1_ref, wg_ref, wu_ref, wd_ref, gm_ref, win_ref, gqk_ref,
                ones_ref, h1_ref, q_ref, kv_ref, f_ref, wg_s, wu_s, wd_s, win_s, fs_s):
    s = pl.program_id(0)

    @pl.when(s < N_CAST)
    def _():
        _cast_weights(s, (wg_ref, wu_ref, wd_ref, win_ref), (g1_ref, g1_ref, None, gm_ref),
                      (wg_s, wu_s, wd_s, win_s))

    @pl.when(s >= N_CAST)
    def _():
        _pre_tile(s - N_CAST, x_ref, meta_ref, gqk_ref, ones_ref,
                  h1_ref, q_ref, kv_ref, f_ref, wg_s, wu_s, wd_s, win_s, fs_s)


def _pre_tile(i, x_ref, meta_ref, gqk_ref, ones_ref,
              h1_ref, q_ref, kv_ref, f_ref, wg_s, wu_s, wd_s, win_s, fs_s):
    tail = jnp.where(i == N_TILES - 1, meta_ref[...], x_ref[META_ROW0:, :])
    x = jnp.concatenate([x_ref[:META_ROW0, :], tail], axis=0)

    h1 = _swiglu_half(x, wg_s, wu_s, wd_s)
    h1_ref[...] = h1

    u = jnp.dot(h1.astype(BF16), win_s[...], preferred_element_type=F32) * _rms_scale(h1)

    qk = u[:, :QK_W]
    sq = (qk * qk).astype(BF16)
    ones = ones_ref[...]
    ss = jnp.concatenate([
        jnp.dot(sq[:, 0:256], ones, preferred_element_type=F32),
        jnp.dot(sq[:, 256:512], ones, preferred_element_type=F32),
        jnp.dot(sq[:, 512:640], ones[:KV_W, :KV_W], preferred_element_type=F32)], axis=1)
    qkn = qk * lax.rsqrt(ss * (1.0 / HEAD_DIM) + EPS) * gqk_ref[...]
    q_ref[...] = qkn[:, :ATTN_W].astype(BF16)
    kv_ref[...] = jnp.concatenate([qkn[:, ATTN_W:], u[:, QK_W:QK_W + KV_W]], axis=1).astype(BF16)

    for c in range(F_W // 128):
        lanes = slice(c * 128, (c + 1) * 128)
        fs_s[c] = u[:, QK_W + KV_W + c * 128:QK_W + KV_W + (c + 1) * 128]
        for b in range(N2):
            f_ref[b, :, lanes] = fs_s[c, pl.ds(b, A_PER_TILE, stride=N2), :]


def _attn_kernel(q_ref, kv_ref, bias_ref, sink_ref, o_ref):
    step = pl.program_id(0)
    kv_meta = kv_ref[SEQ:L, :]
    ones_col = (lax.broadcasted_iota(jnp.int32, (N_KEYS, 16), 1) == 0).astype(BF16)

    units = []
    for j in range(QBLK_PER_STEP):
        n = step * QBLK_PER_STEP + j
        btype = jnp.where(n == 0, 0, jnp.where(n == N_QBLK - 1, 2, jnp.where(n == N_QBLK, 3, 1)))
        start = jnp.where(n == N_QBLK, 0, jnp.clip((n - 1) * BQ, 0, SEQ - BAND))
        start = pl.multiple_of(start, BQ)
        kv_cat = jnp.concatenate([kv_ref[pl.ds(start, BAND), :], kv_meta], axis=0)
        q_ok = lax.broadcasted_iota(jnp.int32, (BQ, 1), 0) < L - n * BQ
        units.extend((j, g, btype, kv_cat, q_ok) for g in range(N_KV))

    def scores(unit):
        j, g, _, kv_cat, q_ok = unit
        q = jnp.where(q_ok, q_ref[j * BQ:(j + 1) * BQ, :], 0)
        qg = jnp.concatenate(
            [q[:, (Q_PER_KV * g + hl) * HEAD_DIM:(Q_PER_KV * g + hl + 1) * HEAD_DIM]
             for hl in range(Q_PER_KV)], axis=0)
        k = kv_cat[:, g * HEAD_DIM:(g + 1) * HEAD_DIM]
        return lax.dot_general(k, qg, (((1,), (1,)), ((), ())), preferred_element_type=F32)

    st_next = scores(units[0])
    for u, (j, g, btype, kv_cat, _) in enumerate(units):
        st = st_next + bias_ref[btype, g]
        sink = sink_ref[g]
        m = jnp.maximum(jnp.max(st, axis=0, keepdims=True), sink)
        p = jnp.exp2(st - m)
        if u + 1 < len(units):
            st_next = scores(units[u + 1])
            p = jnp.concatenate([jnp.maximum(p[:16], st_next[:16] * 0.0 - 1.0), p[16:]], axis=0)
        v1 = jnp.concatenate([kv_cat[:, KV_W + g * HEAD_DIM:KV_W + (g + 1) * HEAD_DIM], ones_col], axis=1)
        ot = lax.dot_general(v1, p.astype(BF16), (((0,), (0,)), ((), ())), preferred_element_type=F32)
        denom = ot[HEAD_DIM:HEAD_DIM + 1] + jnp.exp2(sink - m)
        ot = ot[:HEAD_DIM] / denom
        for hp in range(Q_PER_KV // 2):
            h0 = Q_PER_KV * g + 2 * hp
            two = jnp.concatenate([ot[:, 2 * hp * BQ:(2 * hp + 1) * BQ],
                                   ot[:, (2 * hp + 1) * BQ:(2 * hp + 2) * BQ]], axis=0)
            o_ref[j * BQ:(j + 1) * BQ, h0 * HEAD_DIM:(h0 + 2) * HEAD_DIM] = two.T.astype(BF16)


def _dft_a_kernel(f_ref, wa_ref, cdft_ref, g_ref, gs_ref):
    cdft = cdft_ref[...]
    for k in range(B_PER_STEP):
        h = jnp.dot(wa_ref[...], f_ref[k].astype(BF16), preferred_element_type=F32)
        hr, hi = h[:N1].astype(BF16), h[N1:].astype(BF16)
        for c in range(F_W // 128):
            cols = slice(c * 128, (c + 1) * 128)
            gri = jnp.dot(jnp.concatenate([hr[:, cols], hi[:, cols]], axis=1), cdft,
                          preferred_element_type=F32)
            gs_ref[:, cols] = gri[:, :128]
            gs_ref[:, F_W + c * 128:F_W + (c + 1) * 128] = gri[:, 128:]
        g_ref[:, k, :, :] = gs_ref[...].astype(BF16).reshape(N1 // 2, 2, 2 * F_W)


def _attn_dft_a_kernel(q_ref, kv_ref, bias_ref, sink_ref, f_ref, wa_ref, cdft_ref,
                       gmix_ref, wout_ref, g2_ref, wg_ref, wu_ref, wd_ref,
                       o_ref, g_ref, wout_b, wg_b, wu_b, wd_b, gs_ref):
    _attn_kernel(q_ref, kv_ref, bias_ref, sink_ref, o_ref)
    _dft_a_kernel(f_ref, wa_ref, cdft_ref, g_ref, gs_ref)

    @pl.when(pl.program_id(0) < N_CAST)
    def _():
        for w_ref, gain_ref, w_b in ((wout_ref, gmix_ref, wout_b), (wg_ref, g2_ref, wg_b),
                                     (wu_ref, g2_ref, wu_b), (wd_ref, None, wd_b)):
            w_b[...] = _fold_gain_bf16(w_ref, gain_ref)


def _dft_c_kernel(g_ref, ec_ref, es_ref, y_ref, gs_ref):
    gs_ref[:, 2 * N2:, :] = jnp.zeros((TCP, K2P - 2 * N2, 2 * F_W), BF16)
    for i in range(TCP):
        gs_ref[i, :2 * N2, :] = g_ref[i].reshape(2 * N2, 2 * F_W)
        g = gs_ref[i]
        y = (jnp.dot(ec_ref[i], g[:, :F_W], preferred_element_type=F32)
             + jnp.dot(es_ref[i], g[:, F_W:], preferred_element_type=F32))
        y_ref[:, i, :, :] = _unit_rms(y[:2 * N2]).astype(BF16).reshape(N2, 2, F_W)


def _post_kernel(h1_ref, a_ref, y_ref, wout_ref, wg_ref, wu_ref, wd_ref, gfin_ref, o_ref):
    y = y_ref[...].reshape(TM, F_W)
    a = a_ref[...]
    h2 = (h1_ref[...]
          + jnp.dot(a, wout_ref[:ATTN_W, :], preferred_element_type=F32) * _rms_scale(a.astype(F32))
          + jnp.dot(y, wout_ref[ATTN_W:, :], preferred_element_type=F32))
    h3 = _swiglu_half(h2, wg_ref, wu_ref, wd_ref)
    o_ref[...] = _rms(h3, gfin_ref[...])


def _resident(shape):
    zeros = (0,) * len(shape)
    return pl.BlockSpec(shape, lambda *_: zeros, pipeline_mode=pl.Buffered(1))


def _params():
    return pltpu.CompilerParams(dimension_semantics=("arbitrary",), vmem_limit_bytes=VMEM_LIMIT)


def _row(g):
    return g.astype(F32).reshape(1, -1)


def kernel(x, meta_tokens, ffn1_norm, ffn1_w_gate, ffn1_w_up, ffn1_w_down, mix_norm, w_in, q_norm, k_norm, sink, attn_out_norm, fourier_out_norm, w_out, ffn2_norm, ffn2_w_gate, ffn2_w_up, ffn2_w_down, final_norm):
    assert x.shape == (1, SEQ, D_MODEL) and x.dtype == F32
    assert A_PER_TILE * N2 == TM and A_PER_TILE * N_TILES == N1
    x2 = x.reshape(SEQ, D_MODEL)
    gqk = jnp.concatenate([jnp.tile(q_norm.astype(F32), N_HEADS) * (HEAD_DIM ** -0.5 * LOG2E),
                           jnp.tile(k_norm.astype(F32), N_KV)]).reshape(1, QK_W)
    bf = lambda w: w.astype(BF16)

    tile_idx = lambda s: jnp.maximum(s - N_CAST, 0)
    row_tile = lambda w: pl.BlockSpec((TM, w), lambda s: (tile_idx(s), 0))
    tile = lambda w: pl.BlockSpec((TM, w), lambda s: (s, 0))
    row_sets = lambda w: pl.BlockSpec((N2, A_PER_TILE, w), lambda s: (0, tile_idx(s), 0))
    cast_rows = lambda r, c: pl.BlockSpec((r // N_CAST, c), lambda s: (jnp.minimum(s, N_CAST - 1), 0))
    gains = jnp.broadcast_to(
        jnp.stack([ffn1_norm, mix_norm, jnp.concatenate([attn_out_norm, fourier_out_norm]), ffn2_norm])
        .astype(F32)[:, :, None], (4, D_MODEL, 128))
    gain_rows = lambda k: pl.BlockSpec((None, D_MODEL // N_CAST, 128), lambda s: (k, jnp.minimum(s, N_CAST - 1), 0))
    bf16_copy = lambda r, c: pltpu.VMEM((r, c), BF16)
    ffn_in = [cast_rows(D_MODEL, D_FF), cast_rows(D_MODEL, D_FF), cast_rows(D_FF, D_MODEL)]
    ffn_scratch = [bf16_copy(D_MODEL, D_FF), bf16_copy(D_MODEL, D_FF), bf16_copy(D_FF, D_MODEL)]
    f32 = lambda w: w.astype(F32)

    h1, q, kv, f = pl.pallas_call(
        _pre_kernel,
        grid=(N_CAST + N_TILES,),
        in_specs=[row_tile(D_MODEL), _resident((N_META, D_MODEL)), gain_rows(0), *ffn_in,
                  gain_rows(1), cast_rows(D_MODEL, IN_W), _resident((1, QK_W)),
                  _resident((256, 256))],
        out_specs=[row_tile(D_MODEL), row_tile(ATTN_W), row_tile(2 * KV_W), row_sets(F_W)],
        out_shape=[jax.ShapeDtypeStruct((L, D_MODEL), F32), jax.ShapeDtypeStruct((L, ATTN_W), BF16),
                   jax.ShapeDtypeStruct((L, 2 * KV_W), BF16), jax.ShapeDtypeStruct((N2, N1, F_W), F32)],
        scratch_shapes=ffn_scratch + [bf16_copy(D_MODEL, IN_W), pltpu.VMEM((F_W // 128, TM, 128), F32)],
        compiler_params=_params(),
        name="pre",
    )(x2, meta_tokens.astype(F32), gains, f32(ffn1_w_gate), f32(ffn1_w_up), f32(ffn1_w_down), gains,
      f32(w_in), gqk, _ones_blockdiag())

    sink_rows = (jnp.repeat(sink.astype(F32).reshape(N_KV, Q_PER_KV), BQ, axis=1) * LOG2E)[:, None, :]

    q_rows = pl.BlockSpec((QBLK_PER_STEP * BQ, ATTN_W), lambda s: (s, 0))
    n_steps = pl.cdiv(N_QBLK + 1, QBLK_PER_STEP)
    assert n_steps == pl.cdiv(N2, B_PER_STEP) and n_steps >= N_CAST
    post_w_blocks = [cast_rows(D_MODEL, D_MODEL), *ffn_in]
    a_out, g, wout_b, wg2_b, wu2_b, wd2_b = pl.pallas_call(
        _attn_dft_a_kernel,
        grid=(n_steps,),
        in_specs=[q_rows, _resident((L, 2 * KV_W)), _resident((4, N_KV, N_KEYS, Q_PER_KV * BQ)),
                  _resident((N_KV, 1, Q_PER_KV * BQ)),
                  pl.BlockSpec((B_PER_STEP, N1, F_W), lambda s: (s, 0, 0)),
                  _resident((2 * N1, N1)), _resident((256, 256)),
                  gain_rows(2), post_w_blocks[0], gain_rows(3), *post_w_blocks[1:]],
        out_specs=[q_rows, pl.BlockSpec((N1 // 2, B_PER_STEP, 2, 2 * F_W), lambda s: (0, s, 0, 0)),
                   *post_w_blocks],
        out_shape=[jax.ShapeDtypeStruct((L, ATTN_W), BF16),
                   jax.ShapeDtypeStruct((N1 // 2, N2, 2, 2 * F_W), BF16),
                   jax.ShapeDtypeStruct((D_MODEL, D_MODEL), BF16), jax.ShapeDtypeStruct((D_MODEL, D_FF), BF16),
                   jax.ShapeDtypeStruct((D_MODEL, D_FF), BF16), jax.ShapeDtypeStruct((D_FF, D_MODEL), BF16)],
        scratch_shapes=[pltpu.VMEM((N1, 2 * F_W), F32)],
        compiler_params=_params(),
        name="attn_dft_a",
    )(q, kv, jnp.asarray(_attn_bias_const()), sink_rows,
      f, bf(jnp.asarray(_stage_a_dft())), bf(jnp.asarray(_channel_dft())),
      gains, f32(w_out), gains, f32(ffn2_w_gate), f32(ffn2_w_up), f32(ffn2_w_down))

    ec, es = _stage_c_dft()
    y = pl.pallas_call(
        _dft_c_kernel,
        grid=(N1 // (2 * TCP),),
        in_specs=[pl.BlockSpec((TCP, N2, 2, 2 * F_W), lambda i: (i, 0, 0, 0)),
                  pl.BlockSpec((TCP, K2P, K2P), lambda i: (i, 0, 0)),
                  pl.BlockSpec((TCP, K2P, K2P), lambda i: (i, 0, 0))],
        out_specs=pl.BlockSpec((N2, TCP, 2, F_W), lambda i: (0, i, 0, 0)),
        out_shape=jax.ShapeDtypeStruct((N2, N1 // 2, 2, F_W), BF16),
        scratch_shapes=[pltpu.VMEM((TCP, K2P, 2 * F_W), BF16)],
        compiler_params=_params(),
        name="dft_c",
    )(g, bf(jnp.asarray(ec)), bf(jnp.asarray(es)))

    out = pl.pallas_call(
        _post_kernel,
        grid=(N_TILES,),
        in_specs=[tile(D_MODEL), tile(ATTN_W), pl.BlockSpec((TM // 2, 2, F_W), lambda s: (s, 0, 0)),
                  _resident((D_MODEL, D_MODEL)), _resident((D_MODEL, D_FF)), _resident((D_MODEL, D_FF)),
                  _resident((D_FF, D_MODEL)), _resident((1, D_MODEL))],
        out_specs=tile(D_MODEL),
        out_shape=jax.ShapeDtypeStruct((SEQ, D_MODEL), F32),
        compiler_params=_params(),
        name="post",
    )(h1, a_out, y.reshape(L // 2, 2, F_W), wout_b, wg2_b, wu2_b, wd2_b, _row(final_norm))
    return out.reshape(1, SEQ, D_MODEL)
```
